```python
import math
import jax, jax.numpy as jnp
from jax import lax
import numpy as np

D_MODEL = 1024
BATCH = 4
SEQ = 4096
DEPTH = 4

GRID_W = 64
CTX_LEN = 256
EPS = 1e-6

ATT_HEADS = 8
ATT_KV_HEADS = 2
ATT_HEAD_DIM = 64
ATT_WINDOW = 128
ATT_BLOCK = 128
ROPE_THETA = 10000.0
ML_HEADS = 4
ML_HEAD_DIM = 128
ML_CHUNK = 128
HY_WIDTH = 512
HY_ORDER = 2
HY_EMB_BANDS = 16
HY_EMB_DIM = 1 + 2 * HY_EMB_BANDS
HY_FILTER_WIDTH = 64
HY_SHORT = 3
HY_DECAY_TARGET = 1e-2
HY_DECAY_FAST = 0.3
HY_DECAY_SLOW = 1.5
D_FF = 2816
FFN_CONV = 3

ATT_Q = ATT_HEADS * ATT_HEAD_DIM
ATT_KV = ATT_KV_HEADS * ATT_HEAD_DIM
ML_W = ML_HEADS * ML_HEAD_DIM
ML_GATES = 2 * 2 * ML_HEADS
N_BRANCH = 3
BRANCH_W = 512
IN_SIZES = (ATT_Q, ATT_KV, ATT_KV, ML_W, ML_W, ML_W, ML_W, ML_GATES, 3 * HY_WIDTH, N_BRANCH * D_MODEL)
IN_WIDTH = sum(IN_SIZES)
IN_OFFSETS = tuple(int(o) for o in np.cumsum(IN_SIZES)[:-1])

kernel_name = 'hybrid_attn_mlstm_hyena_dit'

F32 = jnp.float32


def rms_norm(x, g):
    xf = x.astype(F32)
    y = xf * lax.rsqrt(jnp.mean(xf * xf, axis=-1, keepdims=True) + EPS)
    return (y * g.astype(F32)).astype(x.dtype)


def dwconv_centered(x, w, b):
    K = w.shape[0]
    pad = K // 2
    L = x.shape[1]
    xp = jnp.pad(x, ((0, 0), (pad, pad), (0, 0)))
    out = b + xp[:, 0:L] * w[0]
    for j in range(1, K):
        out = out + xp[:, j:j + L] * w[j]
    return out


def split_heads(t, n):
    return t.reshape(*t.shape[:-1], n, -1)


def axial_rope_angles(L):
    rows = L // GRID_W
    row = jnp.repeat(jnp.arange(rows, dtype=F32), GRID_W)
    col = jnp.tile(jnp.arange(GRID_W, dtype=F32), rows)
    nf = ATT_HEAD_DIM // 4
    inv = ROPE_THETA ** (-jnp.arange(nf, dtype=F32) / nf)
    ang = jnp.concatenate([row[:, None] * inv, col[:, None] * inv], axis=-1)
    return jnp.cos(ang), jnp.sin(ang)


def apply_rope(t, cos, sin):
    half = t.shape[-1] // 2
    t1 = t[..., :half].astype(F32)
    t2 = t[..., half:].astype(F32)
    c = cos[:, None, :]
    s = sin[:, None, :]
    return jnp.concatenate([t1 * c - t2 * s, t2 * c + t1 * s], axis=-1).astype(t.dtype)


def window_attention(q, k, v, kc, vc, sink):
    B, L, Hq, hd = q.shape
    Hkv = k.shape[2]
    G = Hq // Hkv
    T = ATT_BLOCK
    nb = L // T
    Lc = kc.shape[1]
    scale = hd ** -0.5
    qb = q.reshape(B, nb, T, Hkv, G, hd)

    def band(t):
        tp = jnp.pad(t, ((0, 0), (T, T), (0, 0), (0, 0))).reshape(B, nb + 2, T, Hkv, hd)
        return jnp.concatenate([tp[:, :-2], tp[:, 1:-1], tp[:, 2:]], axis=2)

    kb, vb = band(k), band(v)
    blk = jnp.arange(nb)[:, None, None]
    qpos = blk * T + jnp.arange(T)[None, :, None]
    kpos = (blk - 1) * T + jnp.arange(3 * T)[None, None, :]
    valid = (jnp.abs(kpos - qpos) <= ATT_WINDOW) & (kpos >= 0) & (kpos < L)
    s_loc = jnp.einsum('bntkgd,bnskd->bnkgts', qb, kb).astype(F32) * scale
    s_loc = jnp.where(valid[None, :, None, None], s_loc, -jnp.inf)
    s_ctx = jnp.einsum('bntkgd,bckd->bnkgtc', qb, kc).astype(F32) * scale
    s_sink = jnp.broadcast_to(sink.astype(F32).reshape(Hkv, G, 1, 1), s_ctx.shape[:-1] + (1,))
    p = jax.nn.softmax(jnp.concatenate([s_sink, s_ctx, s_loc], axis=-1), axis=-1)
    p_ctx = p[..., 1:1 + Lc].astype(v.dtype)
    p_loc = p[..., 1 + Lc:].astype(v.dtype)
    o = jnp.einsum('bnkgts,bnskd->bntkgd', p_loc, vb) + jnp.einsum('bnkgtc,bckd->bntkgd', p_ctx, vc)
    return o.reshape(B, L, Hq * hd).astype(v.dtype)


def context_attention(q, k, v, sink):
    B, Lc, Hq, hd = q.shape
    Hkv = k.shape[2]
    G = Hq // Hkv
    qg = q.reshape(B, Lc, Hkv, G, hd)
    s = jnp.einsum('btkgd,bskd->bkgts', qg, k).astype(F32) * hd ** -0.5
    s_sink = jnp.broadcast_to(sink.astype(F32).reshape(Hkv, G, 1, 1), s.shape[:-1] + (1,))
    p = jax.nn.softmax(jnp.concatenate([s_sink, s], axis=-1), axis=-1)[..., 1:].astype(v.dtype)
    o = jnp.einsum('bkgts,bskd->btkgd', p, v)
    return o.reshape(B, Lc, Hq * hd).astype(v.dtype)


def mlstm_scan(q, k, v, i_pre, logf, state):
    B, H, L, d = q.shape
    T = ML_CHUNK
    nc = L // T

    def chunks(a):
        return jnp.moveaxis(a.reshape(B, H, nc, T, *a.shape[3:]), 2, 0)

    tril = jnp.tril(jnp.ones((T, T), dtype=bool))

    def step(carry, inp):
        C, n, m = carry
        qc, kc, vc, ic, fc = inp
        b = jnp.cumsum(fc, axis=-1)
        a = b + m[..., None]
        dmat = jnp.where(tril, b[..., :, None] - b[..., None, :] + ic[..., None, :], -jnp.inf)
        mt = jnp.maximum(a, jnp.max(dmat, axis=-1))
        w_inter = jnp.exp(a - mt)
        s = jnp.einsum('bhtd,bhsd->bhts', qc, kc) * jnp.exp(dmat - mt[..., None])
        num = w_inter[..., None] * jnp.einsum('bhvk,bhtk->bhtv', C, qc) + jnp.einsum('bhts,bhsv->bhtv', s, vc)
        den = w_inter * jnp.einsum('bhk,bhtk->bht', n, qc) + jnp.sum(s, axis=-1)
        h = num / jnp.maximum(jnp.abs(den), jnp.exp(-mt))[..., None]
        bl = b[..., -1]
        src = bl[..., None] - b + ic
        m_new = jnp.maximum(bl + m, jnp.max(src, axis=-1))
        g = jnp.exp(src - m_new[..., None])
        decay = jnp.exp(bl + m - m_new)
        C_new = decay[..., None, None] * C + jnp.einsum('bhs,bhsv,bhsk->bhvk', g, vc, kc)
        n_new = decay[..., None] * n + jnp.einsum('bhs,bhsk->bhk', g, kc)
        return (C_new, n_new, m_new), h

    state, hs = lax.scan(step, state, (chunks(q), chunks(k), chunks(v), chunks(i_pre), chunks(logf)))
    return jnp.moveaxis(hs, 0, 2).reshape(B, H, L, d), state


def mlstm_inputs(p, gate_b):
    B, L, _ = p[3].shape

    def to_bhld(t):
        return split_heads(t, ML_HEADS).astype(F32).transpose(0, 2, 1, 3)

    q = to_bhld(p[3])
    k = to_bhld(p[4]) * ML_HEAD_DIM ** -0.5
    v = to_bhld(p[5])
    g = p[7].astype(F32).reshape(B, L, 2, 2, ML_HEADS) + gate_b.astype(F32)
    g = jnp.moveaxis(g, 1, -1)
    return q, k, v, g[:, :, 0], jax.nn.log_sigmoid(g[:, :, 1])


def mlstm_bidir(ctx_in, lat_in, need_ctx):
    qc, kc, vc, ic, fc = ctx_in
    ql, kl, vl, il, fl = lat_in
    B, H, _, d = ql.shape
    zero = (jnp.zeros((B, H, d, d), F32), jnp.zeros((B, H, d), F32), jnp.zeros((B, H), F32))

    def flip(t):
        return jnp.flip(t, axis=2)

    hcf, sf = mlstm_scan(qc, kc, vc, ic[:, 0], fc[:, 0], zero)
    hlf, _ = mlstm_scan(ql, kl, vl, il[:, 0], fl[:, 0], sf)
    hcb, sb = mlstm_scan(flip(qc), flip(kc), flip(vc), flip(ic[:, 1]), flip(fc[:, 1]), zero)
    hlb, _ = mlstm_scan(flip(ql), flip(kl), flip(vl), flip(il[:, 1]), flip(fl[:, 1]), sb)
    h_lat = hlf + flip(hlb)
    h_ctx = hcf + flip(hcb) if need_ctx else None
    return h_lat, h_ctx


def mlstm_output(h, o_pre, g):
    B, H, L, d = h.shape
    h = h.transpose(0, 2, 1, 3)
    y = h * lax.rsqrt(jnp.mean(h * h, axis=-1, keepdims=True) + EPS) * g.astype(F32).reshape(H, d)
    y = y * jax.nn.sigmoid(split_heads(o_pre, ML_HEADS).astype(F32))
    return y.reshape(B, L, H * d).astype(o_pre.dtype)


def hyena_filters(L, w1, b1, w2, b2, w3, freq):
    t = jnp.arange(L, dtype=F32)
    t_norm = t / max(L - 1, 1)
    w = 2.0 * math.pi * t / L
    bands = jnp.linspace(1e-4, HY_EMB_BANDS - 1, HY_EMB_BANDS, dtype=F32)
    z = jnp.concatenate([t_norm[:, None], jnp.cos(w[:, None] * bands), -jnp.sin(w[:, None] * bands)], axis=-1)
    z = z.astype(w1.dtype)
    h = jnp.sin(freq[0] * (z @ w1 + b1))
    h = jnp.sin(freq[1] * (h @ w2 + b2))
    h = (h @ w3).astype(F32).reshape(L, HY_ORDER, 2, HY_WIDTH)
    deltas = jnp.abs(jnp.linspace(math.log(HY_DECAY_TARGET) / HY_DECAY_SLOW,
                                  math.log(HY_DECAY_TARGET) / HY_DECAY_FAST, HY_WIDTH, dtype=F32))
    h = h * jnp.exp(-t_norm[:, None] * deltas)[:, None, None, :]
    return h / jnp.sum(jnp.abs(h), axis=0, keepdims=True)


def bidir_long_conv(u, hf, hb, skip):
    L = u.shape[1]
    n = 2 * L
    U = jnp.fft.rfft(u.astype(F32), n=n, axis=1)
    Hf = jnp.fft.rfft(hf, n=n, axis=0) + jnp.conj(jnp.fft.rfft(hb, n=n, axis=0))
    y = jnp.fft.irfft(U * Hf[None], n=n, axis=1)[:, :L]
    return (y + u.astype(F32) * skip.astype(F32)).astype(u.dtype)


def hyena_operator(u, filt, skip):
    x1, x2, v = jnp.split(u, 3, axis=-1)
    z = v
    for o, gate in enumerate((x1, x2)):
        z = gate * bidir_long_conv(z, filt[:, o, 0], filt[:, o, 1], skip[o])
    return z


def merge_branches(branches, gate_pre, w_branch, w_out):
    gates = jax.nn.sigmoid(gate_pre).reshape(*gate_pre.shape[:-1], N_BRANCH, -1)
    y = gates[..., 0, :] * (branches[0] @ w_branch[0])
    for r in range(1, N_BRANCH):
        y = y + gates[..., r, :] * (branches[r] @ w_branch[r])
    return y @ w_out


def token_mixers(hl, hc, cos, sin, w_in, sink, gate_b, ml_g, hy_sw, hy_sb, hy_fp, hy_skip, w_branch, w_out, need_ctx):
    L = hl.shape[1]
    Lc = hc.shape[1]
    pl = jnp.split(hl @ w_in, IN_OFFSETS, axis=-1)
    pc = jnp.split(hc @ w_in, IN_OFFSETS, axis=-1)
    ql = apply_rope(split_heads(pl[0], ATT_HEADS), cos, sin)
    kl = apply_rope(split_heads(pl[1], ATT_KV_HEADS), cos, sin)
    vl = split_heads(pl[2], ATT_KV_HEADS)
    kc = split_heads(pc[1], ATT_KV_HEADS)
    vc = split_heads(pc[2], ATT_KV_HEADS)
    att_l = window_attention(ql, kl, vl, kc, vc, sink)
    h_lat, h_ctx = mlstm_bidir(mlstm_inputs(pc, gate_b), mlstm_inputs(pl, gate_b), need_ctx)
    mls_l = mlstm_output(h_lat, pl[6], ml_g)
    hy_l = hyena_operator(dwconv_centered(pl[8], hy_sw, hy_sb), hyena_filters(L, *hy_fp), hy_skip)
    y_l = merge_branches((att_l, mls_l, hy_l), pl[9], w_branch, w_out)
    if not need_ctx:
        return y_l, None
    att_c = context_attention(split_heads(pc[0], ATT_HEADS), kc, vc, sink)
    mls_c = mlstm_output(h_ctx, pc[6], ml_g)
    hy_c = hyena_operator(dwconv_centered(pc[8], hy_sw, hy_sb), hyena_filters(Lc, *hy_fp), hy_skip)
    y_c = merge_branches((att_c, mls_c, hy_c), pc[9], w_branch, w_out)
    return y_l, y_c


def conv_ffn(h, w_up, cw, cb, w_down):
    u = dwconv_centered(h @ w_up, cw, cb)
    gate, val = jnp.split(u, 2, axis=-1)
    return (jax.nn.silu(gate) * val) @ w_down


def setup_inputs(seed: int = 0) -> dict:
    key = jax.random.key(seed)
    kit = iter(jax.random.split(key, 40))
    D = D_MODEL

    def nrm(shape, scale):
        return jax.random.normal(next(kit), shape, F32) * scale

    i_b = nrm((DEPTH, 2, ML_HEADS), 0.1)
    f_b = jnp.linspace(3.0, 6.0, ML_HEADS, dtype=F32) + nrm((DEPTH, 2, ML_HEADS), 0.1)
    return {
        'x': nrm((BATCH, SEQ, D), 1.0),
        'c': nrm((BATCH, D), 1.0),
        'ctx': nrm((BATCH, CTX_LEN, D), 1.0),
        'c_ctx': nrm((D,), 1.0),
        'ada_w': nrm((DEPTH, D, 6 * D), 0.5 * D ** -0.5),
        'ada_b': nrm((DEPTH, 6 * D), 0.02),
        'norm1_g': 1.0 + nrm((DEPTH, D), 0.05),
        'norm2_g': 1.0 + nrm((DEPTH, D), 0.05),
        'w_in': nrm((DEPTH, D, IN_WIDTH), D ** -0.5),
        'att_sink': nrm((DEPTH, ATT_HEADS), 0.5),
        'ml_gate_b': jnp.stack([i_b, f_b], axis=2),
        'ml_norm_g': 1.0 + nrm((DEPTH, ML_W), 0.05),
        'hy_short_w': nrm((DEPTH, HY_SHORT, 3 * HY_WIDTH), HY_SHORT ** -0.5),
        'hy_short_b': nrm((DEPTH, 3 * HY_WIDTH), 0.02),
        'hy_w1': nrm((DEPTH, HY_EMB_DIM, HY_FILTER_WIDTH), HY_EMB_DIM ** -0.5),
        'hy_b1': nrm((DEPTH, HY_FILTER_WIDTH), 0.02),
        'hy_w2': nrm((DEPTH, HY_FILTER_WIDTH, HY_FILTER_WIDTH), HY_FILTER_WIDTH ** -0.5),
        'hy_b2': nrm((DEPTH, HY_FILTER_WIDTH), 0.02),
        'hy_w3': nrm((DEPTH, HY_FILTER_WIDTH, HY_ORDER * 2 * HY_WIDTH), HY_FILTER_WIDTH ** -0.5),
        'hy_freq': 1.0 + nrm((DEPTH, 2, HY_FILTER_WIDTH), 0.1),
        'hy_skip': nrm((DEPTH, HY_ORDER, HY_WIDTH), 0.5),
        'w_branch': nrm((DEPTH, N_BRANCH, BRANCH_W, D), BRANCH_W ** -0.5),
        'w_out': nrm((DEPTH, D, D), D ** -0.5),
        'w_up': nrm((DEPTH, D, 2 * D_FF), D ** -0.5),
        'ffn_conv_w': nrm((DEPTH, FFN_CONV, 2 * D_FF), FFN_CONV ** -0.5),
        'ffn_conv_b': nrm((DEPTH, 2 * D_FF), 0.02),
        'w_down': nrm((DEPTH, D_FF, D), D_FF ** -0.5),
        'final_g': 1.0 + nrm((D,), 0.05),
    }


def reference(x, c, ctx, c_ctx, ada_w, ada_b, norm1_g, norm2_g, w_in, att_sink, ml_gate_b, ml_norm_g,
              hy_short_w, hy_short_b, hy_w1, hy_b1, hy_w2, hy_b2, hy_w3, hy_freq, hy_skip,
              w_branch, w_out, w_up, ffn_conv_w, ffn_conv_b, w_down, final_g):
    L = x.shape[1]
    cos, sin = axial_rope_angles(L)
    sc = jax.nn.silu(c)
    scc = jax.nn.silu(c_ctx)
    xl, xc = x, ctx
    for l in range(DEPTH):
        need_ctx = l < DEPTH - 1
        sh1, sc1, g1, sh2, sc2, g2 = [m[:, None, :] for m in jnp.split(sc @ ada_w[l] + ada_b[l], 6, axis=-1)]
        csh1, csc1, cg1, csh2, csc2, cg2 = jnp.split(scc @ ada_w[l] + ada_b[l], 6, axis=-1)
        hl = rms_norm(xl, norm1_g[l]) * (1.0 + sc1) + sh1
        hc = rms_norm(xc, norm1_g[l]) * (1.0 + csc1) + csh1
        hy_fp = (hy_w1[l], hy_b1[l], hy_w2[l], hy_b2[l], hy_w3[l], hy_freq[l])
        yl, yc = token_mixers(hl, hc, cos, sin, w_in[l], att_sink[l], ml_gate_b[l], ml_norm_g[l],
                              hy_short_w[l], hy_short_b[l], hy_fp, hy_skip[l], w_branch[l], w_out[l], need_ctx)
        xl = xl + g1 * yl
        hl = rms_norm(xl, norm2_g[l]) * (1.0 + sc2) + sh2
        xl = xl + g2 * conv_ffn(hl, w_up[l], ffn_conv_w[l], ffn_conv_b[l], w_down[l])
        if need_ctx:
            xc = xc + cg1 * yc
            hc = rms_norm(xc, norm2_g[l]) * (1.0 + csc2) + csh2
            xc = xc + cg2 * conv_ffn(hc, w_up[l], ffn_conv_w[l], ffn_conv_b[l], w_down[l])
    return rms_norm(xl, final_g)
```

```python
import functools
import math

import numpy as np
import jax
import jax.numpy as jnp
from jax import lax
from jax.experimental import pallas as pl
from jax.experimental.pallas import tpu as pltpu

F32 = jnp.float32
BF16 = jnp.bfloat16

D_MODEL = 1024
DEPTH = 4
GRID_W = 64
EPS = 1e-6
ATT_HEADS = 8
ATT_KV_HEADS = 2
ATT_HEAD_DIM = 64
ATT_WINDOW = 128
ATT_BLOCK = 128
ROPE_THETA = 10000.0
ML_HEADS = 4
ML_HEAD_DIM = 128
ML_CHUNK = 128
HY_WIDTH = 512
HY_ORDER = 2
HY_EMB_BANDS = 16
HY_DECAY_TARGET = 1e-2
HY_DECAY_FAST = 0.3
HY_DECAY_SLOW = 1.5
D_FF = 2816
N_BRANCH = 3

ATT_Q = ATT_HEADS * ATT_HEAD_DIM
ATT_KV = ATT_KV_HEADS * ATT_HEAD_DIM
ML_W = ML_HEADS * ML_HEAD_DIM
ML_GATES = 2 * 2 * ML_HEADS
IN_SIZES = (ATT_Q, ATT_KV, ATT_KV, ML_W, ML_W, ML_W, ML_W, ML_GATES, 3 * HY_WIDTH, N_BRANCH * D_MODEL)
IN_OFFSETS = tuple(int(o) for o in np.cumsum(IN_SIZES)[:-1])
IN_WIDTH = sum(IN_SIZES)
IN_PAD = 7680

VMEM_LIMIT = 48 * 1024 * 1024


def _cparams(sem):
    return pltpu.CompilerParams(dimension_semantics=sem, vmem_limit_bytes=VMEM_LIMIT)


def _norm_proj_kernel(x_ref, gs_ref, sh_ref, w_ref, o_ref, h_ref):
    @pl.when(pl.program_id(2) == 0)
    def _():
        x = x_ref[0]
        ms = jnp.mean(x * x, axis=-1, keepdims=True)
        h = x * lax.rsqrt(ms + EPS) * gs_ref[0] + sh_ref[0]
        h_ref[...] = h.astype(BF16)

    o_ref[0] = jnp.dot(h_ref[...], w_ref[...], preferred_element_type=F32).astype(o_ref.dtype)


def _norm_proj(x, gs, sh, w, out_dtype, tn=512):
    B, T, D = x.shape
    N = w.shape[1]
    tm = min(T, 512)
    return pl.pallas_call(
        _norm_proj_kernel,
        grid=(B, T // tm, N // tn),
        in_specs=[
            pl.BlockSpec((1, tm, D), lambda b, i, j: (b, i, 0)),
            pl.BlockSpec((1, 1, D), lambda b, i, j: (b, 0, 0)),
            pl.BlockSpec((1, 1, D), lambda b, i, j: (b, 0, 0)),
            pl.BlockSpec((D, tn), lambda b, i, j: (0, j)),
        ],
        out_specs=pl.BlockSpec((1, tm, tn), lambda b, i, j: (b, i, j)),
        out_shape=jax.ShapeDtypeStruct((B, T, N), out_dtype),
        scratch_shapes=[pltpu.VMEM((tm, D), BF16)],
        compiler_params=_cparams(("parallel", "parallel", "arbitrary")),
        name="norm_proj",
    )(x, gs, sh, w)


def _proj_kernel(a_ref, w_ref, o_ref):
    o_ref[0] = jnp.dot(a_ref[0], w_ref[...], preferred_element_type=F32).astype(o_ref.dtype)


def _proj(a, w, out_dtype=F32):
    B, T, K = a.shape
    N = w.shape[1]
    tm = min(T, 512)
    return pl.pallas_call(
        _proj_kernel,
        grid=(B, T // tm),
        in_specs=[
            pl.BlockSpec((1, tm, K), lambda b, i: (b, i, 0)),
            pl.BlockSpec((K, N), lambda b, i: (0, 0)),
        ],
        out_specs=pl.BlockSpec((1, tm, N), lambda b, i: (b, i, 0)),
        out_shape=jax.ShapeDtypeStruct((B, T, N), out_dtype),
        compiler_params=_cparams(("parallel", "parallel")),
        name="proj",
    )(a, w)


def _proj_res_kernel(a_ref, w_ref, res_ref, g_ref, o_ref):
    acc = jnp.dot(a_ref[0], w_ref[...], preferred_element_type=F32)
    o_ref[0] = res_ref[0] + g_ref[0] * acc


def _proj_res(a, w, res, g):
    B, T, K = a.shape
    N = w.shape[1]
    tm = min(T, 512)
    return pl.pallas_call(
        _proj_res_kernel,
        grid=(B, T // tm),
        in_specs=[
            pl.BlockSpec((1, tm, K), lambda b, i: (b, i, 0)),
            pl.BlockSpec((K, N), lambda b, i: (0, 0)),
            pl.BlockSpec((1, tm, N), lambda b, i: (b, i, 0)),
            pl.BlockSpec((1, 1, N), lambda b, i: (b, 0, 0)),
        ],
        out_specs=pl.BlockSpec((1, tm, N), lambda b, i: (b, i, 0)),
        out_shape=jax.ShapeDtypeStruct((B, T, N), F32),
        compiler_params=_cparams(("parallel", "parallel")),
        name="proj_res",
    )(a, w, res, g)


def _final_norm_kernel(x_ref, g_ref, o_ref):
    x = x_ref[0]
    ms = jnp.mean(x * x, axis=-1, keepdims=True)
    o_ref[0] = x * lax.rsqrt(ms + EPS) * g_ref[...]


def _final_norm(x, g):
    B, T, D = x.shape
    tm = 512
    return pl.pallas_call(
        _final_norm_kernel,
        grid=(B, T // tm),
        in_specs=[
            pl.BlockSpec((1, tm, D), lambda b, i: (b, i, 0)),
            pl.BlockSpec((1, D), lambda b, i: (0, 0)),
        ],
        out_specs=pl.BlockSpec((1, tm, D), lambda b, i: (b, i, 0)),
        out_shape=jax.ShapeDtypeStruct((B, T, D), F32),
        compiler_params=_cparams(("parallel", "parallel")),
        name="final_norm",
    )(x, g.reshape(1, D))


def _dwconv_centered(x, w, b):
    K = w.shape[0]
    pad = K // 2
    L = x.shape[1]
    xp = jnp.pad(x, ((0, 0), (pad, pad), (0, 0)))
    out = b + xp[:, 0:L] * w[0]
    for j in range(1, K):
        out = out + xp[:, j:j + L] * w[j]
    return out


def _split_heads(t, n):
    return t.reshape(*t.shape[:-1], n, -1)


def _axial_rope_angles(L):
    rows = L // GRID_W
    row = jnp.repeat(jnp.arange(rows, dtype=F32), GRID_W)
    col = jnp.tile(jnp.arange(GRID_W, dtype=F32), rows)
    nf = ATT_HEAD_DIM // 4
    inv = ROPE_THETA ** (-jnp.arange(nf, dtype=F32) / nf)
    ang = jnp.concatenate([row[:, None] * inv, col[:, None] * inv], axis=-1)
    return jnp.cos(ang), jnp.sin(ang)


def _apply_rope(t, cos, sin):
    half = t.shape[-1] // 2
    t1 = t[..., :half].astype(F32)
    t2 = t[..., half:].astype(F32)
    c = cos[:, None, :]
    s = sin[:, None, :]
    return jnp.concatenate([t1 * c - t2 * s, t2 * c + t1 * s], axis=-1).astype(t.dtype)


def _window_attention(q, k, v, kc, vc, sink):
    B, L, Hq, hd = q.shape
    Hkv = k.shape[2]
    G = Hq // Hkv
    T = ATT_BLOCK
    nb = L // T
    Lc = kc.shape[1]
    scale = hd ** -0.5
    qb = q.reshape(B, nb, T, Hkv, G, hd)

    def band(t):
        tp = jnp.pad(t, ((0, 0), (T, T), (0, 0), (0, 0))).reshape(B, nb + 2, T, Hkv, hd)
        return jnp.concatenate([tp[:, :-2], tp[:, 1:-1], tp[:, 2:]], axis=2)

    kb, vb = band(k), band(v)
    blk = jnp.arange(nb)[:, None, None]
    qpos = blk * T + jnp.arange(T)[None, :, None]
    kpos = (blk - 1) * T + jnp.arange(3 * T)[None, None, :]
    valid = (jnp.abs(kpos - qpos) <= ATT_WINDOW) & (kpos >= 0) & (kpos < L)
    s_loc = jnp.einsum('bntkgd,bnskd->bnkgts', qb, kb).astype(F32) * scale
    s_loc = jnp.where(valid[None, :, None, None], s_loc, -jnp.inf)
    s_ctx = jnp.einsum('bntkgd,bckd->bnkgtc', qb, kc).astype(F32) * scale
    s_sink = jnp.broadcast_to(sink.astype(F32).reshape(Hkv, G, 1, 1), s_ctx.shape[:-1] + (1,))
    p = jax.nn.softmax(jnp.concatenate([s_sink, s_ctx, s_loc], axis=-1), axis=-1)
    p_ctx = p[..., 1:1 + Lc].astype(v.dtype)
    p_loc = p[..., 1 + Lc:].astype(v.dtype)
    o = jnp.einsum('bnkgts,bnskd->bntkgd', p_loc, vb) + jnp.einsum('bnkgtc,bckd->bntkgd', p_ctx, vc)
    return o.reshape(B, L, Hq * hd).astype(v.dtype)


def _context_attention(q, k, v, sink):
    B, Lc, Hq, hd = q.shape
    Hkv = k.shape[2]
    G = Hq // Hkv
    qg = q.reshape(B, Lc, Hkv, G, hd)
    s = jnp.einsum('btkgd,bskd->bkgts', qg, k).astype(F32) * hd ** -0.5
    s_sink = jnp.broadcast_to(sink.astype(F32).reshape(Hkv, G, 1, 1), s.shape[:-1] + (1,))
    p = jax.nn.softmax(jnp.concatenate([s_sink, s], axis=-1), axis=-1)[..., 1:].astype(v.dtype)
    o = jnp.einsum('bkgts,bskd->btkgd', p, v)
    return o.reshape(B, Lc, Hq * hd).astype(v.dtype)


def _mlstm_scan(q, k, v, i_pre, logf, state):
    B, H, L, d = q.shape
    T = ML_CHUNK
    nc = L // T

    def chunks(a):
        return jnp.moveaxis(a.reshape(B, H, nc, T, *a.shape[3:]), 2, 0)

    tril = jnp.tril(jnp.ones((T, T), dtype=bool))

    def step(carry, inp):
        C, n, m = carry
        qc, kc, vc, ic, fc = inp
        b = jnp.cumsum(fc, axis=-1)
        a = b + m[..., None]
        dmat = jnp.where(tril, b[..., :, None] - b[..., None, :] + ic[..., None, :], -jnp.inf)
        mt = jnp.maximum(a, jnp.max(dmat, axis=-1))
        w_inter = jnp.exp(a - mt)
        s = jnp.einsum('bhtd,bhsd->bhts', qc, kc) * jnp.exp(dmat - mt[..., None])
        num = w_inter[..., None] * jnp.einsum('bhvk,bhtk->bhtv', C, qc) + jnp.einsum('bhts,bhsv->bhtv', s, vc)
        den = w_inter * jnp.einsum('bhk,bhtk->bht', n, qc) + jnp.sum(s, axis=-1)
        h = num / jnp.maximum(jnp.abs(den), jnp.exp(-mt))[..., None]
        bl = b[..., -1]
        src = bl[..., None] - b + ic
        m_new = jnp.maximum(bl + m, jnp.max(src, axis=-1))
        g = jnp.exp(src - m_new[..., None])
        decay = jnp.exp(bl + m - m_new)
        C_new = decay[..., None, None] * C + jnp.einsum('bhs,bhsv,bhsk->bhvk', g, vc, kc)
        n_new = decay[..., None] * n + jnp.einsum('bhs,bhsk->bhk', g, kc)
        return (C_new, n_new, m_new), h

    state, hs = lax.scan(step, state, (chunks(q), chunks(k), chunks(v), chunks(i_pre), chunks(logf)))
    return jnp.moveaxis(hs, 0, 2).reshape(B, H, L, d), state


def _mlstm_inputs(p, gate_b):
    B, L, _ = p[3].shape

    def to_bhld(t):
        return _split_heads(t, ML_HEADS).astype(F32).transpose(0, 2, 1, 3)

    q = to_bhld(p[3])
    k = to_bhld(p[4]) * ML_HEAD_DIM ** -0.5
    v = to_bhld(p[5])
    g = p[7].astype(F32).reshape(B, L, 2, 2, ML_HEADS) + gate_b.astype(F32)
    g = jnp.moveaxis(g, 1, -1)
    return q, k, v, g[:, :, 0], jax.nn.log_sigmoid(g[:, :, 1])


def _mlstm_bidir(ctx_in, lat_in, need_ctx):
    qc, kc, vc, ic, fc = ctx_in
    ql, kl, vl, il, fl = lat_in
    B, H, _, d = ql.shape
    zero = (jnp.zeros((B, H, d, d), F32), jnp.zeros((B, H, d), F32), jnp.zeros((B, H), F32))

    def flip(t):
        return jnp.flip(t, axis=2)

    hcf, sf = _mlstm_scan(qc, kc, vc, ic[:, 0], fc[:, 0], zero)
    hlf, _ = _mlstm_scan(ql, kl, vl, il[:, 0], fl[:, 0], sf)
    hcb, sb = _mlstm_scan(flip(qc), flip(kc), flip(vc), flip(ic[:, 1]), flip(fc[:, 1]), zero)
    hlb, _ = _mlstm_scan(flip(ql), flip(kl), flip(vl), flip(il[:, 1]), flip(fl[:, 1]), sb)
    h_lat = hlf + flip(hlb)
    h_ctx = hcf + flip(hcb) if need_ctx else None
    return h_lat, h_ctx


def _mlstm_output(h, o_pre, g):
    B, H, L, d = h.shape
    h = h.transpose(0, 2, 1, 3)
    y = h * lax.rsqrt(jnp.mean(h * h, axis=-1, keepdims=True) + EPS) * g.astype(F32).reshape(H, d)
    y = y * jax.nn.sigmoid(_split_heads(o_pre, ML_HEADS).astype(F32))
    return y.reshape(B, L, H * d).astype(o_pre.dtype)


def _hyena_filters(L, w1, b1, w2, b2, w3, freq):
    t = jnp.arange(L, dtype=F32)
    t_norm = t / max(L - 1, 1)
    w = 2.0 * math.pi * t / L
    bands = jnp.linspace(1e-4, HY_EMB_BANDS - 1, HY_EMB_BANDS, dtype=F32)
    z = jnp.concatenate([t_norm[:, None], jnp.cos(w[:, None] * bands), -jnp.sin(w[:, None] * bands)], axis=-1)
    hp = lax.Precision.HIGHEST
    h = jnp.sin(freq[0] * (jnp.dot(z, w1, precision=hp) + b1))
    h = jnp.sin(freq[1] * (jnp.dot(h, w2, precision=hp) + b2))
    h = jnp.dot(h, w3, precision=hp).astype(F32).reshape(L, HY_ORDER, 2, HY_WIDTH)
    deltas = jnp.abs(jnp.linspace(math.log(HY_DECAY_TARGET) / HY_DECAY_SLOW,
                                  math.log(HY_DECAY_TARGET) / HY_DECAY_FAST, HY_WIDTH, dtype=F32))
    h = h * jnp.exp(-t_norm[:, None] * deltas)[:, None, None, :]
    return h / jnp.sum(jnp.abs(h), axis=0, keepdims=True)


def _bidir_long_conv(u, hf, hb, skip):
    L = u.shape[1]
    n = 2 * L
    U = jnp.fft.rfft(u.astype(F32), n=n, axis=1)
    Hf = jnp.fft.rfft(hf, n=n, axis=0) + jnp.conj(jnp.fft.rfft(hb, n=n, axis=0))
    y = jnp.fft.irfft(U * Hf[None], n=n, axis=1)[:, :L]
    return (y + u.astype(F32) * skip.astype(F32)).astype(u.dtype)


def _hyena_operator(u, filt, skip):
    x1, x2, v = jnp.split(u, 3, axis=-1)
    z = v
    for o, gate in enumerate((x1, x2)):
        z = gate * _bidir_long_conv(z, filt[:, o, 0], filt[:, o, 1], skip[o])
    return z


def _merge(branches, gate_pre, w_branch, w_out, res, g):
    gates = jax.nn.sigmoid(gate_pre).reshape(*gate_pre.shape[:-1], N_BRANCH, -1)
    y = gates[..., 0, :] * _proj(branches[0].astype(BF16), w_branch[0])
    for r in range(1, N_BRANCH):
        y = y + gates[..., r, :] * _proj(branches[r].astype(BF16), w_branch[r])
    return _proj_res(y.astype(BF16), w_out, res, g)


def _conv_ffn(x, gs, sh, w_up, cw, cb, w_down, g):
    u = _norm_proj(x, gs, sh, w_up, F32)
    u = _dwconv_centered(u, cw, cb)
    gate, val = jnp.split(u, 2, axis=-1)
    act = (jax.nn.silu(gate) * val).astype(BF16)
    return _proj_res(act, w_down, x, g)


def _pad_cols(w, n):
    return jnp.pad(w, ((0, 0), (0, n - w.shape[1])))


def kernel(x, c, ctx, c_ctx, ada_w, ada_b, norm1_g, norm2_g, w_in, att_sink, ml_gate_b, ml_norm_g,
           hy_short_w, hy_short_b, hy_w1, hy_b1, hy_w2, hy_b2, hy_w3, hy_freq, hy_skip,
           w_branch, w_out, w_up, ffn_conv_w, ffn_conv_b, w_down, final_g):
    B, L, D = x.shape
    Lc = ctx.shape[1]
    cos, sin = _axial_rope_angles(L)
    sc = jax.nn.silu(c)
    scc = jax.nn.silu(c_ctx)
    xl, xc = x, ctx
    hp = lax.Precision.HIGHEST
    for l in range(DEPTH):
        need_ctx = l < DEPTH - 1
        mod_l = jnp.dot(sc, ada_w[l], precision=hp) + ada_b[l]
        mod_c = jnp.dot(scc, ada_w[l], precision=hp) + ada_b[l]
        sh1, sc1, g1, sh2, sc2, g2 = [m[:, None, :] for m in jnp.split(mod_l, 6, axis=-1)]
        csh1, csc1, cg1, csh2, csc2, cg2 = [jnp.broadcast_to(m[None, None, :], (B, 1, D))
                                            for m in jnp.split(mod_c, 6, axis=-1)]
        w_in_b = _pad_cols(w_in[l], IN_PAD).astype(BF16)
        n1 = norm1_g[l][None, None, :]
        n2 = norm2_g[l][None, None, :]
        pl_all = _norm_proj(xl, n1 * (1.0 + sc1), sh1, w_in_b, F32)
        pc_all = _norm_proj(xc, n1 * (1.0 + csc1), csh1, w_in_b, F32)
        pl_ = jnp.split(pl_all[..., :IN_WIDTH], IN_OFFSETS, axis=-1)
        pc_ = jnp.split(pc_all[..., :IN_WIDTH], IN_OFFSETS, axis=-1)
        ql = _apply_rope(_split_heads(pl_[0], ATT_HEADS), cos, sin)
        kl = _apply_rope(_split_heads(pl_[1], ATT_KV_HEADS), cos, sin)
        vl = _split_heads(pl_[2], ATT_KV_HEADS)
        kc = _split_heads(pc_[1], ATT_KV_HEADS)
        vc = _split_heads(pc_[2], ATT_KV_HEADS)
        att_l = _window_attention(ql, kl, vl, kc, vc, att_sink[l])
        h_lat, h_ctx = _mlstm_bidir(_mlstm_inputs(pc_, ml_gate_b[l]), _mlstm_inputs(pl_, ml_gate_b[l]), need_ctx)
        mls_l = _mlstm_output(h_lat, pl_[6], ml_norm_g[l])
        hy_fp = (hy_w1[l], hy_b1[l], hy_w2[l], hy_b2[l], hy_w3[l], hy_freq[l])
        hy_l = _hyena_operator(_dwconv_centered(pl_[8], hy_short_w[l], hy_short_b[l]),
                               _hyena_filters(L, *hy_fp), hy_skip[l])
        wb = w_branch[l].astype(BF16)
        wo = w_out[l].astype(BF16)
        wu = w_up[l].astype(BF16)
        wd = w_down[l].astype(BF16)
        xl = _merge((att_l, mls_l, hy_l), pl_[9], wb, wo, xl, g1)
        xl = _conv_ffn(xl, n2 * (1.0 + sc2), sh2, wu, ffn_conv_w[l], ffn_conv_b[l], wd, g2)
        if need_ctx:
            att_c = _context_attention(_split_heads(pc_[0], ATT_HEADS), kc, vc, att_sink[l])
            mls_c = _mlstm_output(h_ctx, pc_[6], ml_norm_g[l])
            hy_c = _hyena_operator(_dwconv_centered(pc_[8], hy_short_w[l], hy_short_b[l]),
                                   _hyena_filters(Lc, *hy_fp), hy_skip[l])
            xc = _merge((att_c, mls_c, hy_c), pc_[9], wb, wo, xc, cg1)
            xc = _conv_ffn(xc, n2 * (1.0 + csc2), csh2, wu, ffn_conv_w[l], ffn_conv_b[l], wd, cg2)
    return _final_norm(xl, final_g)
```

```python
import functools
import math

import numpy as np
import jax
import jax.numpy as jnp
from jax import lax
from jax.experimental import pallas as pl
from jax.experimental.pallas import tpu as pltpu

F32 = jnp.float32
BF16 = jnp.bfloat16

D_MODEL = 1024
DEPTH = 4
GRID_W = 64
EPS = 1e-6
ATT_HEADS = 8
ATT_KV_HEADS = 2
ATT_HEAD_DIM = 64
ATT_BLOCK = 128
ROPE_THETA = 10000.0
ML_HEADS = 4
ML_HEAD_DIM = 128
ML_CHUNK = 128
HY_WIDTH = 512
HY_ORDER = 2
HY_EMB_BANDS = 16
HY_DECAY_TARGET = 1e-2
HY_DECAY_FAST = 0.3
HY_DECAY_SLOW = 1.5
D_FF = 2816
N_BRANCH = 3
LANES = 128

ATT_Q = ATT_HEADS * ATT_HEAD_DIM
ATT_KV = ATT_KV_HEADS * ATT_HEAD_DIM
ML_W = ML_HEADS * ML_HEAD_DIM
ML_GATES = 2 * 2 * ML_HEADS
IN_SIZES = (ATT_Q, ATT_KV, ATT_KV, ML_W, ML_W, ML_W, ML_W, ML_GATES, 3 * HY_WIDTH, N_BRANCH * D_MODEL)
IN_OFFSETS = tuple(int(o) for o in np.cumsum(IN_SIZES)[:-1])

COL_Q = 0
COL_ML = COL_Q + ATT_Q
COL_HY = COL_ML + 4 * ML_W
COL_BG = COL_HY + 3 * HY_WIDTH
COL_K = COL_BG + N_BRANCH * D_MODEL
COL_V = COL_K + ATT_KV
COL_G = COL_V + ATT_KV
IN_PAD = 7680

HY_N2 = 256
HY_CT = 128
HY_PASSES = 3

VMEM_LIMIT = 56 * 1024 * 1024


def _cparams(sem):
    return pltpu.CompilerParams(dimension_semantics=sem, vmem_limit_bytes=VMEM_LIMIT)


def _sigmoid(x):
    return 1.0 / (1.0 + jnp.exp(-x))


def _log_sigmoid(x):
    return jnp.minimum(x, 0.0) - jnp.log(1.0 + jnp.exp(-jnp.abs(x)))


def _norm_proj_kernel(x_ref, gs_ref, sh_ref, w_ref, o_ref, h_ref):
    @pl.when(pl.program_id(2) == 0)
    def _():
        x = x_ref[0]
        ms = jnp.mean(x * x, axis=-1, keepdims=True)
        h = x * lax.rsqrt(ms + EPS) * gs_ref[0] + sh_ref[0]
        h_ref[...] = h.astype(BF16)

    o_ref[0] = jnp.dot(h_ref[...], w_ref[...], preferred_element_type=F32).astype(o_ref.dtype)


def _norm_proj(x, gs, sh, w, out_dtype, tn=512):
    B, T, D = x.shape
    N = w.shape[1]
    tm = min(T, 512)
    return pl.pallas_call(
        _norm_proj_kernel,
        grid=(B, T // tm, N // tn),
        in_specs=[
            pl.BlockSpec((1, tm, D), lambda b, i, j: (b, i, 0)),
            pl.BlockSpec((1, 1, D), lambda b, i, j: (b, 0, 0)),
            pl.BlockSpec((1, 1, D), lambda b, i, j: (b, 0, 0)),
            pl.BlockSpec((D, tn), lambda b, i, j: (0, j)),
        ],
        out_specs=pl.BlockSpec((1, tm, tn), lambda b, i, j: (b, i, j)),
        out_shape=jax.ShapeDtypeStruct((B, T, N), out_dtype),
        scratch_shapes=[pltpu.VMEM((tm, D), BF16)],
        compiler_params=_cparams(("parallel", "parallel", "arbitrary")),
        name="norm_proj",
    )(x, gs, sh, w)


def _small_proj_kernel(a_ref, w_ref, b_ref, o_ref):
    a = a_ref[...]
    a_hi = a.astype(BF16)
    a_lo = (a - a_hi.astype(F32)).astype(BF16)
    w = w_ref[...]
    w_hi = w.astype(BF16)
    w_lo = (w - w_hi.astype(F32)).astype(BF16)
    acc = jnp.dot(a_hi, w_hi, preferred_element_type=F32)
    acc = acc + (jnp.dot(a_hi, w_lo, preferred_element_type=F32) + jnp.dot(a_lo, w_hi, preferred_element_type=F32))
    o_ref[...] = acc + b_ref[...]


def _ada_proj(a, w, b):
    M, D = a.shape
    N = w.shape[1]
    tn = 1024
    return pl.pallas_call(
        _small_proj_kernel,
        grid=(N // tn,),
        in_specs=[
            pl.BlockSpec((M, D), lambda j: (0, 0)),
            pl.BlockSpec((D, tn), lambda j: (0, j)),
            pl.BlockSpec((1, tn), lambda j: (0, j)),
        ],
        out_specs=pl.BlockSpec((M, tn), lambda j: (0, j)),
        out_shape=jax.ShapeDtypeStruct((M, N), F32),
        compiler_params=_cparams(("parallel",)),
        name="ada_proj",
    )(a, w, b.reshape(1, N))


def _final_norm_kernel(x_ref, g_ref, o_ref):
    x = x_ref[0]
    ms = jnp.mean(x * x, axis=-1, keepdims=True)
    o_ref[0] = x * lax.rsqrt(ms + EPS) * g_ref[...]


def _final_norm(x, g):
    B, T, D = x.shape
    tm = 512
    return pl.pallas_call(
        _final_norm_kernel,
        grid=(B, T // tm),
        in_specs=[
            pl.BlockSpec((1, tm, D), lambda b, i: (b, i, 0)),
            pl.BlockSpec((1, D), lambda b, i: (0, 0)),
        ],
        out_specs=pl.BlockSpec((1, tm, D), lambda b, i: (b, i, 0)),
        out_shape=jax.ShapeDtypeStruct((B, T, D), F32),
        compiler_params=_cparams(("parallel", "parallel")),
        name="final_norm",
    )(x, g.reshape(1, D))


def _lane_lo_mask(shape):
    lane = lax.broadcasted_iota(jnp.int32, shape, len(shape) - 1)
    return (lane % 64) < 32


def _rope(x, tab):
    c = tab[:, :LANES]
    s = tab[:, LANES:]
    outs = []
    for g in range(x.shape[1] // LANES):
        xg = x[:, g * LANES:(g + 1) * LANES]
        outs.append(xg * c + pltpu.roll(xg, 64, axis=1) * s)
    return outs[0] if len(outs) == 1 else jnp.concatenate(outs, axis=1)


def _attend(qs, kcat, vcat, bias, sink_lo, sink_hi):
    lo = _lane_lo_mask(kcat.shape)
    outs = []
    for msk, sink in ((lo, sink_lo), (jnp.logical_not(lo), sink_hi)):
        kh = jnp.where(msk, kcat, 0.0).astype(BF16)
        s = lax.dot_general(qs, kh, (((1,), (1,)), ((), ())), preferred_element_type=F32)
        if bias is not None:
            s = s + bias
        m = jnp.maximum(jnp.max(s, axis=-1, keepdims=True), sink)
        p = jnp.exp(s - m)
        den = jnp.sum(p, axis=-1, keepdims=True) + jnp.exp(sink - m)
        p = (p / den).astype(BF16)
        outs.append(jnp.dot(p, vcat, preferred_element_type=F32))
    return jnp.where(_lane_lo_mask(outs[0].shape), outs[0], outs[1])


def _win_attn_kernel(q_ref, kp_ref, kc_ref, kn_ref, vp_ref, vc_ref, vn_ref, kx_ref, vx_ref,
                     tp_ref, tc_ref, tn_ref, sink_ref, o_ref):
    i = pl.program_id(1)
    nb = pl.num_programs(1)
    T = ATT_BLOCK
    q = _rope(q_ref[0], tc_ref[...]) * (ATT_HEAD_DIM ** -0.5)
    n_pair = ATT_Q // LANES
    qs = jnp.concatenate([q[:, g * LANES:(g + 1) * LANES] for g in range(n_pair)], axis=0).astype(BF16)
    kcat = jnp.concatenate([kx_ref[0], _rope(kp_ref[0], tp_ref[...]), _rope(kc_ref[0], tc_ref[...]),
                            _rope(kn_ref[0], tn_ref[...])], axis=0)
    vcat = jnp.concatenate([vx_ref[0], vp_ref[0], vc_ref[0], vn_ref[0]], axis=0).astype(BF16)
    Lc = kx_ref.shape[1]
    S = Lc + 3 * T
    t = lax.broadcasted_iota(jnp.int32, (T, S), 0)
    s = lax.broadcasted_iota(jnp.int32, (T, S), 1) - Lc
    valid = (s < 0) | ((s < T) & (s >= t) & (i > 0)) | ((s >= T) & (s < 2 * T)) | \
            ((s >= 2 * T) & (s - 2 * T <= t) & (i < nb - 1))
    bias1 = jnp.where(valid, 0.0, -1e30).astype(F32)
    bias = jnp.concatenate([bias1] * n_pair, axis=0)
    o = _attend(qs, kcat, vcat, bias, sink_ref[0], sink_ref[1])
    o_ref[0] = jnp.concatenate([o[g * T:(g + 1) * T] for g in range(n_pair)], axis=1)


def _win_attention(pl_all, pc_all, tabs, sinks):
    B, L, _ = pl_all.shape
    Lc = pc_all.shape[1] // B
    T = ATT_BLOCK
    nb = L // T
    kb, vb = COL_K // LANES, COL_V // LANES

    def blk(col, d):
        return pl.BlockSpec((1, T, LANES), lambda b, i: (b, jnp.clip(i + d, 0, nb - 1), col))

    def tab(d):
        return pl.BlockSpec((T, 2 * LANES), lambda b, i: (jnp.clip(i + d, 0, nb - 1), 0))

    return pl.pallas_call(
        _win_attn_kernel,
        grid=(B, nb),
        in_specs=[
            pl.BlockSpec((1, T, ATT_Q), lambda b, i: (b, i, 0)),
            blk(kb, -1), blk(kb, 0), blk(kb, 1),
            blk(vb, -1), blk(vb, 0), blk(vb, 1),
            pl.BlockSpec((1, Lc, LANES), lambda b, i: (0, b, kb)),
            pl.BlockSpec((1, Lc, LANES), lambda b, i: (0, b, vb)),
            tab(-1), tab(0), tab(1),
            pl.BlockSpec((2, ATT_Q, 1), lambda b, i: (0, 0, 0)),
        ],
        out_specs=pl.BlockSpec((1, T, ATT_Q), lambda b, i: (b, i, 0)),
        out_shape=jax.ShapeDtypeStruct((B, L, ATT_Q), F32),
        compiler_params=_cparams(("parallel", "parallel")),
        name="win_attention",
    )(pl_all, pl_all, pl_all, pl_all, pl_all, pl_all, pl_all, pc_all, pc_all, tabs, tabs, tabs, sinks)


def _ctx_attn_kernel(q_ref, kx_ref, vx_ref, sink_ref, o_ref):
    Lc = q_ref.shape[1]
    n_pair = ATT_Q // LANES
    q = q_ref[0] * (ATT_HEAD_DIM ** -0.5)
    qs = jnp.concatenate([q[:, g * LANES:(g + 1) * LANES] for g in range(n_pair)], axis=0).astype(BF16)
    o = _attend(qs, kx_ref[0], vx_ref[0].astype(BF16), None, sink_ref[0], sink_ref[1])
    o_ref[0] = jnp.concatenate([o[g * Lc:(g + 1) * Lc] for g in range(n_pair)], axis=1)


def _ctx_attention(pc_all, sinks, B):
    Lc = pc_all.shape[1] // B
    kb, vb = COL_K // LANES, COL_V // LANES
    return pl.pallas_call(
        _ctx_attn_kernel,
        grid=(B,),
        in_specs=[
            pl.BlockSpec((1, Lc, ATT_Q), lambda b: (0, b, 0)),
            pl.BlockSpec((1, Lc, LANES), lambda b: (0, b, kb)),
            pl.BlockSpec((1, Lc, LANES), lambda b: (0, b, vb)),
            pl.BlockSpec((2, ATT_Q // LANES * Lc, 1), lambda b: (0, 0, 0)),
        ],
        out_specs=pl.BlockSpec((1, Lc, ATT_Q), lambda b: (0, b, 0)),
        out_shape=jax.ShapeDtypeStruct((1, B * Lc, ATT_Q), F32),
        compiler_params=_cparams(("parallel",)),
        name="ctx_attention",
    )(pc_all, pc_all, pc_all, sinks)


def _dot_hl(a_exact, x):
    x_hi = x.astype(BF16)
    x_lo = (x - x_hi.astype(F32)).astype(BF16)
    return jnp.dot(a_exact, x_hi, preferred_element_type=F32) + jnp.dot(a_exact, x_lo, preferred_element_type=F32)


def _mlstm_kernel(qf_l, kf_l, vf_l, gf_l, qb_l, kb_l, vb_l, gb_l,
                  qf_c, kf_c, vf_c, gf_c, qb_c, kb_c, vb_c, gb_c, bias_ref,
                  hf_ref, hb_ref, ct_ref, m_ref, *, n_ctx_chunks):
    j = pl.program_id(1)
    T = ML_CHUNK
    d = ML_HEAD_DIM
    is_ctx = j < n_ctx_chunks

    @pl.when(j == 0)
    def _():
        ct_ref[...] = jnp.zeros_like(ct_ref)
        m_ref[...] = jnp.zeros_like(m_ref)

    row = lax.broadcasted_iota(jnp.int32, (T, T), 0)
    col = lax.broadcasted_iota(jnp.int32, (T, T), 1)
    ones_td = jnp.ones((T, d), F32)

    for di, (q_l, k_l, v_l, g_l, q_c, k_c, v_c, g_c, h_ref) in enumerate(
            ((qf_l, kf_l, vf_l, gf_l, qf_c, kf_c, vf_c, gf_c, hf_ref),
             (qb_l, kb_l, vb_l, gb_l, qb_c, kb_c, vb_c, gb_c, hb_ref))):
        keep = (col <= row) if di == 0 else (col >= row)
        keep_b = keep.astype(BF16)
        g = jnp.where(is_ctx, g_c[0], g_l[0]) + bias_ref[...]
        lane = lax.broadcasted_iota(jnp.int32, g.shape, 1)
        is_f = (lane % 8) >= 4
        gv = jnp.where(is_f, _log_sigmoid(g), g)
        gt = gv.T
        cum_c = _dot_hl(keep_b, gv)
        gt_hi = gt.astype(BF16)
        gt_lo = (gt - gt_hi.astype(F32)).astype(BF16)
        nt = (((1,), (1,)), ((), ()))
        cum_r = lax.dot_general(gt_hi, keep_b, nt, preferred_element_type=F32) + \
            lax.dot_general(gt_lo, keep_b, nt, preferred_element_type=F32)
        q_all = jnp.where(is_ctx, q_c[0], q_l[0])
        k_all = jnp.where(is_ctx, k_c[0], k_l[0]) * (d ** -0.5)
        v_all = jnp.where(is_ctx, v_c[0], v_l[0])
        h_out = []
        for h in range(ML_HEADS):
            ci = di * 8 + h
            cf = di * 8 + 4 + h
            sl = slice(h * d, (h + 1) * d)
            qh = q_all[:, sl].astype(BF16)
            kh = k_all[:, sl]
            vh = v_all[:, sl]
            b_c = cum_c[:, cf:cf + 1]
            b_r = cum_r[cf:cf + 1, :]
            i_c = gv[:, ci:ci + 1]
            i_r = gt[ci:ci + 1, :]
            m_old = m_ref[di * ML_HEADS + h]
            bl = b_c[T - 1:T, :] if di == 0 else b_c[0:1, :]
            a = b_c + m_old
            dmat = jnp.where(keep, b_c - b_r + i_r, -1e30)
            mt = jnp.maximum(a, jnp.max(dmat, axis=-1, keepdims=True))
            w_inter = jnp.exp(a - mt)
            s = lax.dot_general(qh, kh.astype(BF16), (((1,), (1,)), ((), ())), preferred_element_type=F32)
            s = (s * jnp.exp(dmat - mt)).astype(BF16)
            v_aug = jnp.concatenate([vh, ones_td], axis=1).astype(BF16)
            ct = ct_ref[di * ML_HEADS + h]
            r = w_inter * jnp.dot(qh, ct.astype(BF16), preferred_element_type=F32) + \
                jnp.dot(s, v_aug, preferred_element_type=F32)
            num = r[:, :d]
            den = r[:, d:]
            h_out.append(num / jnp.maximum(jnp.abs(den), jnp.exp(-mt)))
            src = bl - b_c + i_c
            m_new = jnp.maximum(bl + m_old, jnp.max(src, axis=0, keepdims=True))
            gk = (jnp.exp(src - m_new) * kh).T.astype(BF16)
            decay = jnp.exp(bl + m_old - m_new)
            ct_ref[di * ML_HEADS + h] = decay * ct + jnp.dot(gk, v_aug, preferred_element_type=F32)
            m_ref[di * ML_HEADS + h] = m_new
        h_ref[0] = jnp.concatenate(h_out, axis=1)


def _mlstm(pl_all, pc_all, gate_bias):
    B, L, _ = pl_all.shape
    Lc = pc_all.shape[1] // B
    T = ML_CHUNK
    nl, ncx = L // T, Lc // T
    nsteps = nl + ncx

    def lat_f(j):
        return jnp.clip(j - ncx, 0, nl - 1)

    def lat_b(j):
        return jnp.clip(nsteps - 1 - j, 0, nl - 1)

    def ctx_f(j):
        return jnp.clip(j, 0, ncx - 1)

    def ctx_b(j):
        return jnp.clip(ncx - 1 - j, 0, ncx - 1)

    def lat_specs(fn):
        base = COL_ML // ML_W
        return [pl.BlockSpec((1, T, ML_W), lambda b, j, o=o: (b, fn(j), base + o)) for o in range(3)] + \
               [pl.BlockSpec((1, T, LANES), lambda b, j: (b, fn(j), COL_G // LANES))]

    def ctx_specs(fn):
        base = COL_ML // ML_W
        return [pl.BlockSpec((1, T, ML_W), lambda b, j, o=o: (0, b * ncx + fn(j), base + o)) for o in range(3)] + \
               [pl.BlockSpec((1, T, LANES), lambda b, j: (0, b * ncx + fn(j), COL_G // LANES))]

    def out_f(b, j):
        return (b, jnp.where(j < ncx, nl + j, j - ncx), 0)

    def out_b(b, j):
        return (b, jnp.where(j < ncx, nl + ncx - 1 - j, nsteps - 1 - j), 0)

    return pl.pallas_call(
        functools.partial(_mlstm_kernel, n_ctx_chunks=ncx),
        grid=(B, nsteps),
        in_specs=lat_specs(lat_f) + lat_specs(lat_b) + ctx_specs(ctx_f) + ctx_specs(ctx_b) +
        [pl.BlockSpec((1, LANES), lambda b, j: (0, 0))],
        out_specs=[pl.BlockSpec((1, T, ML_W), out_f), pl.BlockSpec((1, T, ML_W), out_b)],
        out_shape=[jax.ShapeDtypeStruct((B, L + Lc, ML_W), F32)] * 2,
        scratch_shapes=[pltpu.VMEM((2 * ML_HEADS, ML_HEAD_DIM, 2 * ML_HEAD_DIM), F32),
                        pltpu.VMEM((2 * ML_HEADS, 1, 1), F32)],
        compiler_params=_cparams(("parallel", "arbitrary")),
        name="mlstm",
    )(*([pl_all] * 8 + [pc_all] * 8 + [gate_bias]))


def _short_conv_kernel(x_ref, w_ref, b_ref, o_ref):
    x = x_ref[0]
    L = x.shape[0]
    row = lax.broadcasted_iota(jnp.int32, x.shape, 0)
    prev = jnp.where(row == 0, 0.0, pltpu.roll(x, 1, axis=0))
    nxt = jnp.where(row == L - 1, 0.0, pltpu.roll(x, L - 1, axis=0))
    w = w_ref[...]
    o_ref[0] = b_ref[...] + prev * w[0:1] + x * w[1:2] + nxt * w[2:3]


def _short_conv(p_all, w, b, seq_len):
    Bx, T, _ = p_all.shape
    nseq = T // seq_len
    C = w.shape[1]
    cb = COL_HY // LANES
    return pl.pallas_call(
        _short_conv_kernel,
        grid=(Bx, nseq, C // LANES),
        in_specs=[
            pl.BlockSpec((1, seq_len, LANES), lambda b, s, j: (b, s, cb + j)),
            pl.BlockSpec((3, LANES), lambda b, s, j: (0, j)),
            pl.BlockSpec((1, LANES), lambda b, s, j: (0, j)),
        ],
        out_specs=pl.BlockSpec((1, seq_len, LANES), lambda b, s, j: (b, s, j)),
        out_shape=jax.ShapeDtypeStruct((Bx, T, C), F32),
        compiler_params=_cparams(("parallel", "parallel", "parallel")),
        name="hy_short_conv",
    )(p_all, w, b.reshape(1, C))


@functools.lru_cache(maxsize=None)
def _dft_consts(L):
    N2 = HY_N2
    N = 2 * L
    N1 = N // N2
    S = N1 // 2 + 1
    k1 = np.arange(S)[:, None]
    n2 = np.arange(N2)
    tw = np.exp(-2j * np.pi * k1 * n2[None, :] / N)
    tw_tab = np.stack([np.repeat(tw.real[:, :, None], LANES, 2), np.repeat(tw.imag[:, :, None], LANES, 2)], 1)
    F = np.exp(-2j * np.pi * np.outer(n2, n2) / N2)
    M2 = np.block([[F.real, -F.imag], [F.imag, F.real]])

    def split(m):
        m32 = m.astype(np.float32)
        hi = m32.astype(BF16)
        lo = (m32 - hi.astype(np.float32)).astype(BF16)
        return np.stack([hi, lo])

    def stage1(nb):
        n1 = np.arange(nb)[None, :]
        ang = 2 * np.pi * k1 * n1 / N1
        return np.cos(ang), -np.sin(ang)

    n1o = np.arange(L // N2)[:, None]
    k1o = np.arange(S)[None, :]
    ang = 2 * np.pi * n1o * k1o / N1
    wgt = np.where((k1o == 0) | (k1o == N1 // 2), 1.0, 2.0) / N
    return dict(N=N, N1=N1, S=S, stage1=stage1, tw=tw_tab.astype(np.float32),
                m2=split(M2), m2t=split(M2.T), icr=np.cos(ang) * wgt, ici=-np.sin(ang) * wgt)


def _dft_mm(m_ref, x):
    x_hi = x.astype(BF16)
    acc = jnp.dot(m_ref[0], x_hi, preferred_element_type=F32)
    if HY_PASSES >= 2:
        x_lo = (x - x_hi.astype(F32)).astype(BF16)
        acc = acc + jnp.dot(m_ref[0], x_lo, preferred_element_type=F32)
    if HY_PASSES >= 3:
        acc = acc + jnp.dot(m_ref[1], x_hi, preferred_element_type=F32)
    return acc


def _coef_acc(acc, c, x):
    if abs(c) < 1e-12:
        return acc
    if abs(c - 1.0) < 1e-12:
        return x if acc is None else acc + x
    if abs(c + 1.0) < 1e-12:
        return -x if acc is None else acc - x
    return c * x if acc is None else acc + c * x


HY_SUB = 64


def _stage1_slot(load, nb, cr_k, ci_k, k, tw_ref, a_ref):
    N2 = HY_N2

    def body(t, carry):
        r0 = pl.multiple_of(t * HY_SUB, HY_SUB)
        ar = ai = None
        for n1 in range(nb):
            xb = load(n1, r0)
            ar = _coef_acc(ar, cr_k[n1], xb)
            ai = _coef_acc(ai, ci_k[n1], xb)
        if ai is None:
            ai = jnp.zeros_like(ar)
        if k > 0:
            twr = tw_ref[k, 0, pl.ds(r0, HY_SUB), :]
            twi = tw_ref[k, 1, pl.ds(r0, HY_SUB), :]
            ar, ai = ar * twr - ai * twi, ar * twi + ai * twr
        a_ref[pl.ds(r0, HY_SUB), :] = ar
        a_ref[pl.ds(N2 + r0, HY_SUB), :] = ai
        return carry

    lax.fori_loop(0, N2 // HY_SUB, body, 0)


def _stage1_inv_slot(b_ref, nb, icr_k, ici_k, k, tw_ref, o_ref):
    N2 = HY_N2

    def body(t, carry):
        r0 = pl.multiple_of(t * HY_SUB, HY_SUB)
        br = b_ref[pl.ds(r0, HY_SUB), :]
        bi = b_ref[pl.ds(N2 + r0, HY_SUB), :]
        if k > 0:
            twr = tw_ref[k, 0, pl.ds(r0, HY_SUB), :]
            twi = tw_ref[k, 1, pl.ds(r0, HY_SUB), :]
            br, bi = br * twr + bi * twi, bi * twr - br * twi
        for n1 in range(nb):
            contrib = _coef_acc(_coef_acc(None, icr_k[n1], br), ici_k[n1], bi)
            rows = pl.ds(n1 * N2 + r0, HY_SUB)
            if k == 0:
                o_ref[0, rows, :] = contrib
            elif contrib is not None:
                o_ref[0, rows, :] += contrib
        return carry

    lax.fori_loop(0, N2 // HY_SUB, body, 0)


def _filter_spec_kernel(g_ref, tw_ref, m2_ref, o_ref, a_ref, *, L):
    c = _dft_consts(L)
    cr, ci = c["stage1"](c["N1"])
    for k in range(c["S"]):
        _stage1_slot(lambda n1, r0: g_ref[pl.ds(n1 * HY_N2 + r0, HY_SUB), :], c["N1"], cr[k], ci[k], k, tw_ref, a_ref)
        o_ref[k] = _dft_mm(m2_ref, a_ref[...])


def _filter_spectrum(g, L):
    c = _dft_consts(L)
    C = g.shape[1]
    S, R = c["S"], 2 * HY_N2
    return pl.pallas_call(
        functools.partial(_filter_spec_kernel, L=L),
        grid=(C // HY_CT,),
        in_specs=[
            pl.BlockSpec((2 * L, HY_CT), lambda j: (0, j)),
            pl.BlockSpec((S, 2, HY_N2, LANES), lambda j: (0, 0, 0, 0)),
            pl.BlockSpec((2, R, R), lambda j: (0, 0, 0)),
        ],
        out_specs=pl.BlockSpec((S, R, HY_CT), lambda j: (0, 0, j)),
        out_shape=jax.ShapeDtypeStruct((S, R, C), F32),
        scratch_shapes=[pltpu.VMEM((R, HY_CT), F32)],
        compiler_params=_cparams(("parallel",)),
        name="hy_filter_spectrum",
    )(g, jnp.asarray(c["tw"]), jnp.asarray(c["m2"]))


def _long_conv_kernel(u_ref, gate_ref, spec_ref, skip_ref, tw_ref, m2_ref, m2t_ref, o_ref, a_ref, b_ref, *, L):
    c = _dft_consts(L)
    N2 = HY_N2
    nb = L // N2
    cr, ci = c["stage1"](nb)
    icr, ici = c["icr"] * c["N"], c["ici"] * c["N"]
    for k in range(c["S"]):
        _stage1_slot(lambda n1, r0: u_ref[0, pl.ds(n1 * N2 + r0, HY_SUB), :], nb, cr[k], ci[k], k, tw_ref, a_ref)
        x = _dft_mm(m2_ref, a_ref[...])
        g = spec_ref[k]
        xr, xi, gr, gi = x[:N2], x[N2:], g[:N2], g[N2:]
        y = jnp.concatenate([xr * gr - xi * gi, xr * gi + xi * gr], axis=0)
        b_ref[...] = _dft_mm(m2t_ref, y)
        _stage1_inv_slot(b_ref, nb, icr[:, k], ici[:, k], k, tw_ref, o_ref)
    o_ref[0] = gate_ref[0] * (o_ref[0] * (1.0 / c["N"]) + skip_ref[...] * u_ref[0])


def _long_conv(u_arr, u_col, gate_arr, gate_col, spec, skip, L):
    Bx, T, _ = u_arr.shape
    nseq = T // L
    c = _dft_consts(L)
    C = spec.shape[2]
    S, R = c["S"], 2 * HY_N2
    ub, gb = u_col // HY_CT, gate_col // HY_CT
    return pl.pallas_call(
        functools.partial(_long_conv_kernel, L=L),
        grid=(C // HY_CT, Bx, nseq),
        in_specs=[
            pl.BlockSpec((1, L, HY_CT), lambda j, b, s: (b, s, ub + j)),
            pl.BlockSpec((1, L, HY_CT), lambda j, b, s: (b, s, gb + j)),
            pl.BlockSpec((S, R, HY_CT), lambda j, b, s: (0, 0, j)),
            pl.BlockSpec((1, HY_CT), lambda j, b, s: (0, j)),
            pl.BlockSpec((S, 2, HY_N2, LANES), lambda j, b, s: (0, 0, 0, 0)),
            pl.BlockSpec((2, R, R), lambda j, b, s: (0, 0, 0)),
            pl.BlockSpec((2, R, R), lambda j, b, s: (0, 0, 0)),
        ],
        out_specs=pl.BlockSpec((1, L, HY_CT), lambda j, b, s: (b, s, j)),
        out_shape=jax.ShapeDtypeStruct((Bx, T, C), F32),
        scratch_shapes=[pltpu.VMEM((R, HY_CT), F32), pltpu.VMEM((R, HY_CT), F32)],
        compiler_params=_cparams(("parallel", "parallel", "parallel")),
        name="hy_long_conv",
    )(u_arr, gate_arr, spec, skip.reshape(1, C), jnp.asarray(c["tw"]), jnp.asarray(c["m2"]), jnp.asarray(c["m2t"]))


def _hyena_filters(L, w1, b1, w2, b2, w3, freq):
    t = jnp.arange(L, dtype=F32)
    t_norm = t / max(L - 1, 1)
    w = 2.0 * math.pi * t / L
    bands = jnp.linspace(1e-4, HY_EMB_BANDS - 1, HY_EMB_BANDS, dtype=F32)
    z = jnp.concatenate([t_norm[:, None], jnp.cos(w[:, None] * bands), -jnp.sin(w[:, None] * bands)], axis=-1)
    hp = lax.Precision.HIGHEST
    h = jnp.sin(freq[0] * (jnp.dot(z, w1, precision=hp) + b1))
    h = jnp.sin(freq[1] * (jnp.dot(h, w2, precision=hp) + b2))
    h = jnp.dot(h, w3, precision=hp).astype(F32).reshape(L, HY_ORDER, 2, HY_WIDTH)
    deltas = jnp.abs(jnp.linspace(math.log(HY_DECAY_TARGET) / HY_DECAY_SLOW,
                                  math.log(HY_DECAY_TARGET) / HY_DECAY_FAST, HY_WIDTH, dtype=F32))
    h = h * jnp.exp(-t_norm[:, None] * deltas)[:, None, None, :]
    return h / jnp.sum(jnp.abs(h), axis=0, keepdims=True)


def _hyena_spectra(L, hy_fp):
    filt = _hyena_filters(L, *hy_fp)
    hf = filt[:, :, 0].reshape(L, HY_ORDER * HY_WIDTH)
    hb = filt[:, :, 1].reshape(L, HY_ORDER * HY_WIDTH)
    zero = jnp.zeros((1, hf.shape[1]), F32)
    g = jnp.concatenate([hf[:1] + hb[:1], hf[1:], zero, jnp.flip(hb[1:], axis=0)], axis=0)
    return _filter_spectrum(g, L)


def _hyena(p_all, sw, sb, spec, skip, L):
    u3 = _short_conv(p_all, sw, sb, L)
    W = HY_WIDTH
    z1 = _long_conv(u3, 2 * W, u3, 0, spec[:, :, :W], skip[0], L)
    return _long_conv(z1, 0, u3, W, spec[:, :, W:], skip[1], L)


def _merge_kernel(att_ref, hf_ref, hb_ref, op_ref, hy_ref, gp0_ref, gp1_ref, gp2_ref, mg_ref, wb_ref, wo_ref,
                  res_ref, g_ref, o_ref):
    d = ML_HEAD_DIM
    h = hf_ref[0] + hb_ref[0]
    parts = []
    for hh in range(ML_HEADS):
        hs = h[:, hh * d:(hh + 1) * d]
        parts.append(hs * lax.rsqrt(jnp.mean(hs * hs, axis=-1, keepdims=True) + EPS))
    mls = jnp.concatenate(parts, axis=1) * mg_ref[...] * _sigmoid(op_ref[0])
    y = _sigmoid(gp0_ref[0]) * jnp.dot(att_ref[0].astype(BF16), wb_ref[0], preferred_element_type=F32)
    y = y + _sigmoid(gp1_ref[0]) * jnp.dot(mls.astype(BF16), wb_ref[1], preferred_element_type=F32)
    y = y + _sigmoid(gp2_ref[0]) * jnp.dot(hy_ref[0].astype(BF16), wb_ref[2], preferred_element_type=F32)
    o_ref[0] = res_ref[0] + g_ref[0] * jnp.dot(y.astype(BF16), wo_ref[...], preferred_element_type=F32)


def _merge(att, hf, hb, p_all, hy, ml_g, wb, wo, res, g, tm):
    Bx, T, D = res.shape
    W = ML_W
    row = lambda b, i: (b, i, 0)
    return pl.pallas_call(
        _merge_kernel,
        grid=(Bx, T // tm),
        in_specs=[
            pl.BlockSpec((1, tm, W), row),
            pl.BlockSpec((1, tm, W), row),
            pl.BlockSpec((1, tm, W), row),
            pl.BlockSpec((1, tm, W), lambda b, i: (b, i, COL_ML // W + 3)),
            pl.BlockSpec((1, tm, W), row),
            pl.BlockSpec((1, tm, D), lambda b, i: (b, i, COL_BG // D)),
            pl.BlockSpec((1, tm, D), lambda b, i: (b, i, COL_BG // D + 1)),
            pl.BlockSpec((1, tm, D), lambda b, i: (b, i, COL_BG // D + 2)),
            pl.BlockSpec((1, W), lambda b, i: (0, 0)),
            pl.BlockSpec((N_BRANCH, W, D), lambda b, i: (0, 0, 0)),
            pl.BlockSpec((D, D), lambda b, i: (0, 0)),
            pl.BlockSpec((1, tm, D), row),
            pl.BlockSpec((1, 1, D), lambda b, i: (b, 0, 0)),
        ],
        out_specs=pl.BlockSpec((1, tm, D), row),
        out_shape=jax.ShapeDtypeStruct((Bx, T, D), F32),
        compiler_params=_cparams(("parallel", "parallel")),
        name="merge",
    )(att, hf, hb, p_all, hy, p_all, p_all, p_all, ml_g.reshape(1, W), wb, wo, res, g)


FFN_CHUNK = 256
FFN_HALO = 16


def _ffn_down_kernel(u_ref, up_ref, un_ref, cw_ref, cb_ref, wd_ref, res_ref, g_ref, o_ref, act_ref, *, tiles_per_seq):
    i = pl.program_id(1)
    tm = u_ref.shape[1]
    first = (i % tiles_per_seq) == 0
    last = (i % tiles_per_seq) == tiles_per_seq - 1
    row = lax.broadcasted_iota(jnp.int32, (tm, FFN_CHUNK), 0)

    def conv(c0):
        cs = slice(c0, c0 + FFN_CHUNK)
        x = u_ref[0, :, cs].astype(F32)
        pr = jnp.where(first, 0.0, up_ref[0, FFN_HALO - 1:FFN_HALO, cs].astype(F32))
        nx = jnp.where(last, 0.0, un_ref[0, 0:1, cs].astype(F32))
        prev = jnp.where(row == 0, pr, pltpu.roll(x, 1, axis=0))
        nxt = jnp.where(row == tm - 1, nx, pltpu.roll(x, tm - 1, axis=0))
        w = cw_ref[:, cs]
        return cb_ref[:, cs] + prev * w[0:1] + x * w[1:2] + nxt * w[2:3]

    for c in range(D_FF // FFN_CHUNK):
        gate = conv(c * FFN_CHUNK)
        val = conv(D_FF + c * FFN_CHUNK)
        act_ref[:, c * FFN_CHUNK:(c + 1) * FFN_CHUNK] = (gate * _sigmoid(gate) * val).astype(BF16)
    o_ref[0] = res_ref[0] + g_ref[0] * jnp.dot(act_ref[...], wd_ref[...], preferred_element_type=F32)


def _ffn_down(u, cw, cb, wd, res, g, seq_len):
    Bx, T, D = res.shape
    tm = min(seq_len, 512)
    nt = T // tm
    hb = tm // FFN_HALO
    nh = T // FFN_HALO
    return pl.pallas_call(
        functools.partial(_ffn_down_kernel, tiles_per_seq=seq_len // tm),
        grid=(Bx, nt),
        in_specs=[
            pl.BlockSpec((1, tm, 2 * D_FF), lambda b, i: (b, i, 0)),
            pl.BlockSpec((1, FFN_HALO, 2 * D_FF), lambda b, i: (b, jnp.maximum(i * hb - 1, 0), 0)),
            pl.BlockSpec((1, FFN_HALO, 2 * D_FF), lambda b, i: (b, jnp.minimum((i + 1) * hb, nh - 1), 0)),
            pl.BlockSpec((3, 2 * D_FF), lambda b, i: (0, 0)),
            pl.BlockSpec((1, 2 * D_FF), lambda b, i: (0, 0)),
            pl.BlockSpec((D_FF, D), lambda b, i: (0, 0)),
            pl.BlockSpec((1, tm, D), lambda b, i: (b, i, 0)),
            pl.BlockSpec((1, 1, D), lambda b, i: (b, 0, 0)),
        ],
        out_specs=pl.BlockSpec((1, tm, D), lambda b, i: (b, i, 0)),
        out_shape=jax.ShapeDtypeStruct((Bx, T, D), F32),
        scratch_shapes=[pltpu.VMEM((tm, D_FF), BF16)],
        compiler_params=_cparams(("parallel", "parallel")),
        name="ffn_down",
    )(u, u, u, cw, cb.reshape(1, -1), wd, res, g)


def _pair_perm():
    hd, half = ATT_HEAD_DIM, ATT_HEAD_DIM // 2
    n_pair = ATT_Q // LANES
    qperm = []
    for p in range(n_pair):
        for sub in range(4):
            head = p if sub % 2 == 0 else n_pair + p
            qperm += [head * hd + (sub // 2) * half + dd for dd in range(half)]
    kperm = []
    for sub in range(4):
        kperm += [(sub % 2) * hd + (sub // 2) * half + dd for dd in range(half)]
    return np.asarray(qperm), np.asarray(kperm)


def _rope_tables(L):
    rows = L // GRID_W
    row = jnp.repeat(jnp.arange(rows, dtype=F32), GRID_W)
    col = jnp.tile(jnp.arange(GRID_W, dtype=F32), rows)
    nf = ATT_HEAD_DIM // 4
    inv = ROPE_THETA ** (-jnp.arange(nf, dtype=F32) / nf)
    ang = jnp.concatenate([row[:, None] * inv, col[:, None] * inv], axis=-1)
    cos, sin = jnp.cos(ang), jnp.sin(ang)
    return jnp.concatenate([cos, cos, cos, cos, -sin, -sin, sin, sin], axis=1)


def _prep_w_in(w):
    qperm, kperm = _pair_perm()
    o = IN_OFFSETS
    cols = [w[:, :ATT_Q][:, qperm], w[:, o[2]:o[6]], w[:, o[7]:o[8]], w[:, o[8]:],
            w[:, o[0]:o[1]][:, kperm], w[:, o[1]:o[2]][:, kperm], w[:, o[6]:o[7]]]
    wc = jnp.concatenate(cols, axis=1)
    return jnp.pad(wc, ((0, 0), (0, IN_PAD - wc.shape[1]))).astype(BF16)


def _sink_cols(sink, rows_per_pair):
    n_pair = ATT_Q // LANES
    lo = jnp.repeat(sink[:n_pair], rows_per_pair)
    hi = jnp.repeat(sink[n_pair:], rows_per_pair)
    return jnp.stack([lo, hi])[:, :, None]


def kernel(x, c, ctx, c_ctx, ada_w, ada_b, norm1_g, norm2_g, w_in, att_sink, ml_gate_b, ml_norm_g,
           hy_short_w, hy_short_b, hy_w1, hy_b1, hy_w2, hy_b2, hy_w3, hy_freq, hy_skip,
           w_branch, w_out, w_up, ffn_conv_w, ffn_conv_b, w_down, final_g):
    B, L, D = x.shape
    Lc = ctx.shape[1]
    qperm, _ = _pair_perm()
    tabs = _rope_tables(L)
    sc_rows = jnp.concatenate([jax.nn.silu(c), jax.nn.silu(c_ctx)[None], jnp.zeros((8 - B - 1, D), F32)], axis=0)
    xl = x
    xc = ctx.reshape(1, B * Lc, D)
    for l in range(DEPTH):
        need_ctx = l < DEPTH - 1
        mod = _ada_proj(sc_rows, ada_w[l], ada_b[l])
        sh1, sc1, g1, sh2, sc2, g2 = [m[:B, None, :] for m in jnp.split(mod, 6, axis=-1)]
        csh1, csc1, cg1, csh2, csc2, cg2 = [m[B:B + 1, None, :] for m in jnp.split(mod, 6, axis=-1)]
        n1 = norm1_g[l][None, None, :]
        n2 = norm2_g[l][None, None, :]
        w_in_b = _prep_w_in(w_in[l])
        wb = jnp.concatenate([w_branch[l][:1][:, qperm], w_branch[l][1:]], axis=0).astype(BF16)
        wo = w_out[l].astype(BF16)
        wu = w_up[l].astype(BF16)
        wd = w_down[l].astype(BF16)
        gate_bias = jnp.pad(ml_gate_b[l].reshape(1, ML_GATES), ((0, 0), (0, LANES - ML_GATES)))
        hy_fp = (hy_w1[l], hy_b1[l], hy_w2[l], hy_b2[l], hy_w3[l], hy_freq[l])

        pl_all = _norm_proj(xl, n1 * (1.0 + sc1), sh1, w_in_b, F32)
        pc_all = _norm_proj(xc, n1 * (1.0 + csc1), csh1, w_in_b, F32)
        att_l = _win_attention(pl_all, pc_all, tabs, _sink_cols(att_sink[l], ATT_BLOCK))
        hf, hb = _mlstm(pl_all, pc_all, gate_bias)
        hy_l = _hyena(pl_all, hy_short_w[l], hy_short_b[l], _hyena_spectra(L, hy_fp), hy_skip[l], L)
        xl = _merge(att_l, hf, hb, pl_all, hy_l, ml_norm_g[l], wb, wo, xl, g1, 512)
        u = _norm_proj(xl, n2 * (1.0 + sc2), sh2, wu, BF16)
        xl = _ffn_down(u, ffn_conv_w[l], ffn_conv_b[l], wd, xl, g2, L)
        if need_ctx:
            att_c = _ctx_attention(pc_all, _sink_cols(att_sink[l], Lc), B)
            hy_c = _hyena(pc_all, hy_short_w[l], hy_short_b[l], _hyena_spectra(Lc, hy_fp), hy_skip[l], Lc)
            hfc = hf[:, L:].reshape(1, B * Lc, ML_W)
            hbc = hb[:, L:].reshape(1, B * Lc, ML_W)
            xc = _merge(att_c, hfc, hbc, pc_all, hy_c, ml_norm_g[l], wb, wo, xc, cg1, Lc)
            uc = _norm_proj(xc, n2 * (1.0 + csc2), csh2, wu, BF16)
            xc = _ffn_down(uc, ffn_conv_w[l], ffn_conv_b[l], wd, xc, cg2, Lc)
    return _final_norm(xl, final_g)
```

```python
import functools
import math

import numpy as np
import jax
import jax.numpy as jnp
from jax import lax
from jax.experimental import pallas as pl
from jax.experimental.pallas import tpu as pltpu

F32 = jnp.float32
BF16 = jnp.bfloat16

D_MODEL = 1024
DEPTH = 4
GRID_W = 64
EPS = 1e-6
ATT_HEADS = 8
ATT_KV_HEADS = 2
ATT_HEAD_DIM = 64
ATT_BLOCK = 128
ROPE_THETA = 10000.0
ML_HEADS = 4
ML_HEAD_DIM = 128
ML_CHUNK = 128
HY_WIDTH = 512
HY_ORDER = 2
HY_EMB_BANDS = 16
HY_DECAY_TARGET = 1e-2
HY_DECAY_FAST = 0.3
HY_DECAY_SLOW = 1.5
D_FF = 2816
N_BRANCH = 3
LANES = 128

ATT_Q = ATT_HEADS * ATT_HEAD_DIM
ATT_KV = ATT_KV_HEADS * ATT_HEAD_DIM
ML_W = ML_HEADS * ML_HEAD_DIM
ML_GATES = 2 * 2 * ML_HEADS
IN_SIZES = (ATT_Q, ATT_KV, ATT_KV, ML_W, ML_W, ML_W, ML_W, ML_GATES, 3 * HY_WIDTH, N_BRANCH * D_MODEL)
IN_OFFSETS = tuple(int(o) for o in np.cumsum(IN_SIZES)[:-1])

COL_Q = 0
COL_ML = COL_Q + ATT_Q
COL_HY = COL_ML + 4 * ML_W
COL_BG = COL_HY + 3 * HY_WIDTH
COL_K = COL_BG + N_BRANCH * D_MODEL
COL_V = COL_K + ATT_KV
IN_PAD = 7680

HY_N2 = 256
HY_CT = 128
HY_SUB = 32
HY_PASSES = 1

VMEM_LIMIT = 56 * 1024 * 1024


def _cparams(sem):
    return pltpu.CompilerParams(dimension_semantics=sem, vmem_limit_bytes=VMEM_LIMIT)


def _sigmoid(x):
    return 1.0 / (1.0 + jnp.exp(-x))


def _log_sigmoid(x):
    return jnp.minimum(x, 0.0) - jnp.log(1.0 + jnp.exp(-jnp.abs(x)))


def _norm_proj_kernel(x_ref, gs_ref, sh_ref, w_ref, o_ref, h_ref):
    @pl.when(pl.program_id(2) == 0)
    def _():
        x = x_ref[0]
        ms = jnp.mean(x * x, axis=-1, keepdims=True)
        h = x * lax.rsqrt(ms + EPS) * gs_ref[0] + sh_ref[0]
        h_ref[...] = h.astype(BF16)

    o_ref[0] = jnp.dot(h_ref[...], w_ref[...], preferred_element_type=F32).astype(o_ref.dtype)


def _norm_proj(x, gs, sh, w, out_dtype, tn=512):
    B, T, D = x.shape
    N = w.shape[1]
    tm = min(T, 1024)
    tn = min(tn, N)
    return pl.pallas_call(
        _norm_proj_kernel,
        grid=(B, T // tm, N // tn),
        in_specs=[
            pl.BlockSpec((1, tm, D), lambda b, i, j: (b, i, 0)),
            pl.BlockSpec((1, 1, D), lambda b, i, j: (b, 0, 0)),
            pl.BlockSpec((1, 1, D), lambda b, i, j: (b, 0, 0)),
            pl.BlockSpec((D, tn), lambda b, i, j: (0, j)),
        ],
        out_specs=pl.BlockSpec((1, tm, tn), lambda b, i, j: (b, i, j)),
        out_shape=jax.ShapeDtypeStruct((B, T, N), out_dtype),
        scratch_shapes=[pltpu.VMEM((tm, D), BF16)],
        compiler_params=_cparams(("parallel", "parallel", "arbitrary")),
        name="norm_proj",
    )(x, gs, sh, w)


def _small_proj_kernel(a_ref, w_ref, b_ref, o_ref):
    a = a_ref[...]
    a_hi = a.astype(BF16)
    a_lo = (a - a_hi.astype(F32)).astype(BF16)
    w = w_ref[...]
    w_hi = w.astype(BF16)
    w_lo = (w - w_hi.astype(F32)).astype(BF16)
    acc = jnp.dot(a_hi, w_hi, preferred_element_type=F32)
    acc = acc + (jnp.dot(a_hi, w_lo, preferred_element_type=F32) + jnp.dot(a_lo, w_hi, preferred_element_type=F32))
    o_ref[...] = acc + b_ref[...]


def _ada_proj(a, w, b):
    M, D = a.shape
    N = w.shape[1]
    tn = 1024
    return pl.pallas_call(
        _small_proj_kernel,
        grid=(N // tn,),
        in_specs=[
            pl.BlockSpec((M, D), lambda j: (0, 0)),
            pl.BlockSpec((D, tn), lambda j: (0, j)),
            pl.BlockSpec((1, tn), lambda j: (0, j)),
        ],
        out_specs=pl.BlockSpec((M, tn), lambda j: (0, j)),
        out_shape=jax.ShapeDtypeStruct((M, N), F32),
        compiler_params=_cparams(("parallel",)),
        name="ada_proj",
    )(a, w, b.reshape(1, N))


def _final_norm_kernel(x_ref, g_ref, o_ref):
    x = x_ref[0]
    ms = jnp.mean(x * x, axis=-1, keepdims=True)
    o_ref[0] = x * lax.rsqrt(ms + EPS) * g_ref[...]


def _final_norm(x, g):
    B, T, D = x.shape
    tm = 512
    return pl.pallas_call(
        _final_norm_kernel,
        grid=(B, T // tm),
        in_specs=[
            pl.BlockSpec((1, tm, D), lambda b, i: (b, i, 0)),
            pl.BlockSpec((1, D), lambda b, i: (0, 0)),
        ],
        out_specs=pl.BlockSpec((1, tm, D), lambda b, i: (b, i, 0)),
        out_shape=jax.ShapeDtypeStruct((B, T, D), F32),
        compiler_params=_cparams(("parallel", "parallel")),
        name="final_norm",
    )(x, g.reshape(1, D))


def _lane_lo_mask(shape):
    lane = lax.broadcasted_iota(jnp.int32, shape, len(shape) - 1)
    return (lane % 64) < 32


def _rope(x, tab):
    c = tab[:, :LANES]
    s = tab[:, LANES:]
    outs = []
    for g in range(x.shape[1] // LANES):
        xg = x[:, g * LANES:(g + 1) * LANES]
        outs.append(xg * c + pltpu.roll(xg, 64, axis=1) * s)
    return outs[0] if len(outs) == 1 else jnp.concatenate(outs, axis=1)


def _attend(qs, kcat, vcat, bias, sink_lo, sink_hi):
    lo = _lane_lo_mask(kcat.shape)
    outs = []
    for msk, sink in ((lo, sink_lo), (jnp.logical_not(lo), sink_hi)):
        kh = jnp.where(msk, kcat, 0.0).astype(BF16)
        s = lax.dot_general(qs, kh, (((1,), (1,)), ((), ())), preferred_element_type=F32)
        if bias is not None:
            s = s + bias
        m = jnp.maximum(jnp.max(s, axis=-1, keepdims=True), sink)
        p = jnp.exp(s - m)
        den = jnp.sum(p, axis=-1, keepdims=True) + jnp.exp(sink - m)
        p = (p / den).astype(BF16)
        outs.append(jnp.dot(p, vcat, preferred_element_type=F32))
    return jnp.where(_lane_lo_mask(outs[0].shape), outs[0], outs[1])


def _win_attn_kernel(q_ref, kp_ref, kc_ref, kn_ref, vp_ref, vc_ref, vn_ref, kx_ref, vx_ref,
                     tp_ref, tc_ref, tn_ref, sink_ref, o_ref):
    i = pl.program_id(1)
    nb = pl.num_programs(1)
    T = ATT_BLOCK
    q = _rope(q_ref[0].astype(F32), tc_ref[...]) * (ATT_HEAD_DIM ** -0.5)
    n_pair = ATT_Q // LANES
    qs = jnp.concatenate([q[:, g * LANES:(g + 1) * LANES] for g in range(n_pair)], axis=0).astype(BF16)
    kcat = jnp.concatenate([kx_ref[0].astype(F32), _rope(kp_ref[0].astype(F32), tp_ref[...]),
                            _rope(kc_ref[0].astype(F32), tc_ref[...]),
                            _rope(kn_ref[0].astype(F32), tn_ref[...])], axis=0)
    vcat = jnp.concatenate([vx_ref[0], vp_ref[0], vc_ref[0], vn_ref[0]], axis=0)
    Lc = kx_ref.shape[1]
    S = Lc + 3 * T
    t = lax.broadcasted_iota(jnp.int32, (T, S), 0)
    s = lax.broadcasted_iota(jnp.int32, (T, S), 1) - Lc
    valid = (s < 0) | ((s < T) & (s >= t) & (i > 0)) | ((s >= T) & (s < 2 * T)) | \
            ((s >= 2 * T) & (s - 2 * T <= t) & (i < nb - 1))
    bias1 = jnp.where(valid, 0.0, -1e30).astype(F32)
    bias = jnp.concatenate([bias1] * n_pair, axis=0)
    o = _attend(qs, kcat, vcat, bias, sink_ref[0], sink_ref[1])
    o_ref[0] = jnp.concatenate([o[g * T:(g + 1) * T] for g in range(n_pair)], axis=1).astype(o_ref.dtype)


def _win_attention(pl_all, pc_all, tabs, sinks):
    B, L, _ = pl_all.shape
    Lc = pc_all.shape[1] // B
    T = ATT_BLOCK
    nb = L // T
    kb, vb = COL_K // LANES, COL_V // LANES

    def blk(col, d):
        return pl.BlockSpec((1, T, LANES), lambda b, i: (b, jnp.clip(i + d, 0, nb - 1), col))

    def tab(d):
        return pl.BlockSpec((T, 2 * LANES), lambda b, i: (jnp.clip(i + d, 0, nb - 1), 0))

    return pl.pallas_call(
        _win_attn_kernel,
        grid=(B, nb),
        in_specs=[
            pl.BlockSpec((1, T, ATT_Q), lambda b, i: (b, i, 0)),
            blk(kb, -1), blk(kb, 0), blk(kb, 1),
            blk(vb, -1), blk(vb, 0), blk(vb, 1),
            pl.BlockSpec((1, Lc, LANES), lambda b, i: (0, b, kb)),
            pl.BlockSpec((1, Lc, LANES), lambda b, i: (0, b, vb)),
            tab(-1), tab(0), tab(1),
            pl.BlockSpec((2, ATT_Q, 1), lambda b, i: (0, 0, 0)),
        ],
        out_specs=pl.BlockSpec((1, T, ATT_Q), lambda b, i: (b, i, 0)),
        out_shape=jax.ShapeDtypeStruct((B, L, ATT_Q), BF16),
        compiler_params=_cparams(("parallel", "parallel")),
        name="win_attention",
    )(pl_all, pl_all, pl_all, pl_all, pl_all, pl_all, pl_all, pc_all, pc_all, tabs, tabs, tabs, sinks)


def _ctx_attn_kernel(q_ref, kx_ref, vx_ref, sink_ref, o_ref):
    Lc = q_ref.shape[1]
    n_pair = ATT_Q // LANES
    q = q_ref[0].astype(F32) * (ATT_HEAD_DIM ** -0.5)
    qs = jnp.concatenate([q[:, g * LANES:(g + 1) * LANES] for g in range(n_pair)], axis=0).astype(BF16)
    o = _attend(qs, kx_ref[0].astype(F32), vx_ref[0], None, sink_ref[0], sink_ref[1])
    o_ref[0] = jnp.concatenate([o[g * Lc:(g + 1) * Lc] for g in range(n_pair)], axis=1).astype(o_ref.dtype)


def _ctx_attention(pc_all, sinks, B):
    Lc = pc_all.shape[1] // B
    kb, vb = COL_K // LANES, COL_V // LANES
    return pl.pallas_call(
        _ctx_attn_kernel,
        grid=(B,),
        in_specs=[
            pl.BlockSpec((1, Lc, ATT_Q), lambda b: (0, b, 0)),
            pl.BlockSpec((1, Lc, LANES), lambda b: (0, b, kb)),
            pl.BlockSpec((1, Lc, LANES), lambda b: (0, b, vb)),
            pl.BlockSpec((2, ATT_Q // LANES * Lc, 1), lambda b: (0, 0, 0)),
        ],
        out_specs=pl.BlockSpec((1, Lc, ATT_Q), lambda b: (0, b, 0)),
        out_shape=jax.ShapeDtypeStruct((1, B * Lc, ATT_Q), BF16),
        compiler_params=_cparams(("parallel",)),
        name="ctx_attention",
    )(pc_all, pc_all, pc_all, sinks)


def _dot_hl(a_exact, x):
    x_hi = x.astype(BF16)
    x_lo = (x - x_hi.astype(F32)).astype(BF16)
    return jnp.dot(a_exact, x_hi, preferred_element_type=F32) + jnp.dot(a_exact, x_lo, preferred_element_type=F32)


def _mlstm_kernel(qf_l, kf_l, vf_l, gf_l, qb_l, kb_l, vb_l, gb_l,
                  qf_c, kf_c, vf_c, gf_c, qb_c, kb_c, vb_c, gb_c, bias_ref,
                  hf_ref, hb_ref, ct_ref, m_ref, *, n_ctx_chunks):
    j = pl.program_id(1)
    T = ML_CHUNK
    d = ML_HEAD_DIM
    is_ctx = j < n_ctx_chunks

    @pl.when(j == 0)
    def _():
        ct_ref[...] = jnp.zeros_like(ct_ref)
        m_ref[...] = jnp.zeros_like(m_ref)

    row = lax.broadcasted_iota(jnp.int32, (T, T), 0)
    col = lax.broadcasted_iota(jnp.int32, (T, T), 1)
    ones_td = jnp.ones((T, d), BF16)

    for di, (q_l, k_l, v_l, g_l, q_c, k_c, v_c, g_c, h_ref) in enumerate(
            ((qf_l, kf_l, vf_l, gf_l, qf_c, kf_c, vf_c, gf_c, hf_ref),
             (qb_l, kb_l, vb_l, gb_l, qb_c, kb_c, vb_c, gb_c, hb_ref))):
        keep = (col <= row) if di == 0 else (col >= row)
        keep_b = keep.astype(BF16)
        g = jnp.where(is_ctx, g_c[0], g_l[0]) + bias_ref[...]
        lane = lax.broadcasted_iota(jnp.int32, g.shape, 1)
        is_f = (lane % 8) >= 4
        gv = jnp.where(is_f, _log_sigmoid(g), g)
        gt = gv.T
        cum_c = _dot_hl(keep_b, gv)
        gt_hi = gt.astype(BF16)
        gt_lo = (gt - gt_hi.astype(F32)).astype(BF16)
        nt = (((1,), (1,)), ((), ()))
        cum_r = lax.dot_general(gt_hi, keep_b, nt, preferred_element_type=F32) + \
            lax.dot_general(gt_lo, keep_b, nt, preferred_element_type=F32)
        q_all = jnp.where(is_ctx, q_c[0], q_l[0])
        k_all = jnp.where(is_ctx, k_c[0], k_l[0]).astype(F32) * (d ** -0.5)
        v_all = jnp.where(is_ctx, v_c[0], v_l[0])
        h_out = []
        for h in range(ML_HEADS):
            ci = di * 8 + h
            cf = di * 8 + 4 + h
            sl = slice(h * d, (h + 1) * d)
            qh = q_all[:, sl]
            kh = k_all[:, sl]
            vh = v_all[:, sl]
            b_c = cum_c[:, cf:cf + 1]
            b_r = cum_r[cf:cf + 1, :]
            i_c = gv[:, ci:ci + 1]
            i_r = gt[ci:ci + 1, :]
            m_old = m_ref[di * ML_HEADS + h]
            bl = b_c[T - 1:T, :] if di == 0 else b_c[0:1, :]
            a = b_c + m_old
            dmat = jnp.where(keep, b_c - b_r + i_r, -1e30)
            mt = jnp.maximum(a, jnp.max(dmat, axis=-1, keepdims=True))
            w_inter = jnp.exp(a - mt)
            s = lax.dot_general(qh, kh.astype(BF16), (((1,), (1,)), ((), ())), preferred_element_type=F32)
            s = (s * jnp.exp(dmat - mt)).astype(BF16)
            v_aug = jnp.concatenate([vh, ones_td], axis=1)
            ct = ct_ref[di * ML_HEADS + h]
            r = w_inter * jnp.dot(qh, ct.astype(BF16), preferred_element_type=F32) + \
                jnp.dot(s, v_aug, preferred_element_type=F32)
            num = r[:, :d]
            den = r[:, d:]
            h_out.append(num / jnp.maximum(jnp.abs(den), jnp.exp(-mt)))
            src = bl - b_c + i_c
            m_new = jnp.maximum(bl + m_old, jnp.max(src, axis=0, keepdims=True))
            gk = (jnp.exp(src - m_new) * kh).T.astype(BF16)
            decay = jnp.exp(bl + m_old - m_new)
            ct_ref[di * ML_HEADS + h] = decay * ct + jnp.dot(gk, v_aug, preferred_element_type=F32)
            m_ref[di * ML_HEADS + h] = m_new
        h_ref[0] = jnp.concatenate(h_out, axis=1)


def _mlstm(pl_all, pc_all, gl, gc, gate_bias):
    B, L, _ = pl_all.shape
    Lc = pc_all.shape[1] // B
    T = ML_CHUNK
    nl, ncx = L // T, Lc // T
    nsteps = nl + ncx

    def lat_f(j):
        return jnp.clip(j - ncx, 0, nl - 1)

    def lat_b(j):
        return jnp.clip(nsteps - 1 - j, 0, nl - 1)

    def ctx_f(j):
        return jnp.clip(j, 0, ncx - 1)

    def ctx_b(j):
        return jnp.clip(ncx - 1 - j, 0, ncx - 1)

    def lat_specs(fn):
        base = COL_ML // ML_W
        return [pl.BlockSpec((1, T, ML_W), lambda b, j, o=o: (b, fn(j), base + o)) for o in range(3)] + \
               [pl.BlockSpec((1, T, LANES), lambda b, j: (b, fn(j), 0))]

    def ctx_specs(fn):
        base = COL_ML // ML_W
        return [pl.BlockSpec((1, T, ML_W), lambda b, j, o=o: (0, b * ncx + fn(j), base + o)) for o in range(3)] + \
               [pl.BlockSpec((1, T, LANES), lambda b, j: (0, b * ncx + fn(j), 0))]

    def out_f(b, j):
        return (b, jnp.where(j < ncx, nl + j, j - ncx), 0)

    def out_b(b, j):
        return (b, jnp.where(j < ncx, nl + ncx - 1 - j, nsteps - 1 - j), 0)

    return pl.pallas_call(
        functools.partial(_mlstm_kernel, n_ctx_chunks=ncx),
        grid=(B, nsteps),
        in_specs=lat_specs(lat_f) + lat_specs(lat_b) + ctx_specs(ctx_f) + ctx_specs(ctx_b) +
        [pl.BlockSpec((1, LANES), lambda b, j: (0, 0))],
        out_specs=[pl.BlockSpec((1, T, ML_W), out_f), pl.BlockSpec((1, T, ML_W), out_b)],
        out_shape=[jax.ShapeDtypeStruct((B, L + Lc, ML_W), F32)] * 2,
        scratch_shapes=[pltpu.VMEM((2 * ML_HEADS, ML_HEAD_DIM, 2 * ML_HEAD_DIM), F32),
                        pltpu.VMEM((2 * ML_HEADS, 1, 1), F32)],
        compiler_params=_cparams(("parallel", "arbitrary")),
        name="mlstm",
    )(*(([pl_all] * 3 + [gl]) * 2 + ([pc_all] * 3 + [gc]) * 2 + [gate_bias]))


def _short_conv_kernel(x_ref, w_ref, b_ref, o_ref):
    x = x_ref[0].astype(F32)
    L = x.shape[0]
    row = lax.broadcasted_iota(jnp.int32, x.shape, 0)
    prev = jnp.where(row == 0, 0.0, pltpu.roll(x, 1, axis=0))
    nxt = jnp.where(row == L - 1, 0.0, pltpu.roll(x, L - 1, axis=0))
    w = w_ref[...]
    o_ref[0] = b_ref[...] + prev * w[0:1] + x * w[1:2] + nxt * w[2:3]


def _short_conv(p_all, w, b, seq_len):
    Bx, T, _ = p_all.shape
    nseq = T // seq_len
    C = w.shape[1]
    cb = COL_HY // LANES
    return pl.pallas_call(
        _short_conv_kernel,
        grid=(Bx, nseq, C // LANES),
        in_specs=[
            pl.BlockSpec((1, seq_len, LANES), lambda b, s, j: (b, s, cb + j)),
            pl.BlockSpec((3, LANES), lambda b, s, j: (0, j)),
            pl.BlockSpec((1, LANES), lambda b, s, j: (0, j)),
        ],
        out_specs=pl.BlockSpec((1, seq_len, LANES), lambda b, s, j: (b, s, j)),
        out_shape=jax.ShapeDtypeStruct((Bx, T, C), F32),
        compiler_params=_cparams(("parallel", "parallel", "parallel")),
        name="hy_short_conv",
    )(p_all, w, b.reshape(1, C))


@functools.lru_cache(maxsize=None)
def _dft_consts(L):
    N2 = HY_N2
    N = 2 * L
    N1 = N // N2
    S = N1 // 2 + 1
    k1 = np.arange(S)[:, None]
    n2 = np.arange(N2)
    tw = np.exp(-2j * np.pi * k1 * n2[None, :] / N)
    tw_tab = np.stack([np.repeat(tw.real[:, :, None], LANES, 2), np.repeat(tw.imag[:, :, None], LANES, 2)], 1)
    F = np.exp(-2j * np.pi * np.outer(n2, n2) / N2)
    M2 = np.block([[F.real, -F.imag], [F.imag, F.real]])

    def split(m):
        m32 = m.astype(np.float32)
        hi = m32.astype(BF16)
        lo = (m32 - hi.astype(np.float32)).astype(BF16)
        return np.stack([hi, lo])

    def stage1(nb):
        n1 = np.arange(nb)[None, :]
        ang = 2 * np.pi * k1 * n1 / N1
        return np.cos(ang), -np.sin(ang)

    n1o = np.arange(L // N2)[:, None]
    k1o = np.arange(S)[None, :]
    ang = 2 * np.pi * n1o * k1o / N1
    wgt = np.where((k1o == 0) | (k1o == N1 // 2), 1.0, 2.0) / N
    return dict(N=N, N1=N1, S=S, stage1=stage1, tw=tw_tab.astype(np.float32),
                m2=split(M2), m2t=split(M2.T), icr=np.cos(ang) * wgt, ici=-np.sin(ang) * wgt)


def _dft_mm(m_ref, x, passes):
    x_hi = x.astype(BF16)
    acc = jnp.dot(m_ref[0], x_hi, preferred_element_type=F32)
    if passes >= 2:
        x_lo = (x - x_hi.astype(F32)).astype(BF16)
        acc = acc + jnp.dot(m_ref[0], x_lo, preferred_element_type=F32)
    if passes >= 3:
        acc = acc + jnp.dot(m_ref[1], x_hi, preferred_element_type=F32)
    return acc


def _coef_acc(acc, c, x):
    if x is None or abs(c) < 1e-12:
        return acc
    if abs(c - 1.0) < 1e-12:
        return x if acc is None else acc + x
    if abs(c + 1.0) < 1e-12:
        return -x if acc is None else acc - x
    return c * x if acc is None else acc + c * x


def _vadd(a, b):
    return b if a is None else a if b is None else a + b


def _vsub(a, b):
    return (None if b is None else -b) if a is None else a if b is None else a - b


def _slot_groups(N1):
    half = N1 // 2
    return [(k, half - k if half - k != k else None) for k in range(half // 2 + 1)]


def _stage1_group(load, nb, cr, ci, k, kp, tw_ref, a_ref):
    N2, CT = HY_N2, HY_CT
    for r0 in range(0, N2, HY_SUB):
        if kp is None:
            ar = ai = None
            for n1 in range(nb):
                xb = load(n1, r0)
                ar = _coef_acc(ar, cr[k, n1], xb)
                ai = _coef_acc(ai, ci[k, n1], xb)
            slots = [(k, ar, ai)]
        else:
            even = [None, None]
            odd = [None, None]
            for n1 in range(nb):
                xb = load(n1, r0)
                tgt = even if n1 % 2 == 0 else odd
                tgt[0] = _coef_acc(tgt[0], cr[k, n1], xb)
                tgt[1] = _coef_acc(tgt[1], ci[k, n1], xb)
            slots = [(k, _vadd(even[0], odd[0]), _vadd(even[1], odd[1])),
                     (kp, _vsub(even[0], odd[0]), _vsub(odd[1], even[1]))]
        for idx, (kk, ar, ai) in enumerate(slots):
            zero = jnp.zeros((HY_SUB, CT), F32)
            ar = zero if ar is None else ar
            if kk > 0:
                twr = tw_ref[kk, 0, r0:r0 + HY_SUB, :]
                twi = tw_ref[kk, 1, r0:r0 + HY_SUB, :]
                if ai is None:
                    ar, ai = ar * twr, ar * twi
                else:
                    ar, ai = ar * twr - ai * twi, ar * twi + ai * twr
            ai = zero if ai is None else ai
            a_ref[r0:r0 + HY_SUB, idx * CT:(idx + 1) * CT] = ar
            a_ref[N2 + r0:N2 + r0 + HY_SUB, idx * CT:(idx + 1) * CT] = ai


def _stage1_inv_group(b_ref, nb, icr, ici, k, kp, tw_ref, acc_ref, first):
    N2, CT = HY_N2, HY_CT

    def load(idx, kk, r0):
        br = b_ref[r0:r0 + HY_SUB, idx * CT:(idx + 1) * CT]
        bi = b_ref[N2 + r0:N2 + r0 + HY_SUB, idx * CT:(idx + 1) * CT]
        if kk > 0:
            twr = tw_ref[kk, 0, r0:r0 + HY_SUB, :]
            twi = tw_ref[kk, 1, r0:r0 + HY_SUB, :]
            br, bi = br * twr + bi * twi, bi * twr - br * twi
        return br, bi

    for r0 in range(0, N2, HY_SUB):
        br, bi = load(0, k, r0)
        if kp is None:
            q_even = q_odd = (br, bi)
        else:
            br2, bi2 = load(1, kp, r0)
            q_even = (br + br2, bi - bi2)
            q_odd = (br - br2, bi + bi2)
        for n1 in range(nb):
            qr, qi = q_even if n1 % 2 == 0 else q_odd
            contrib = _coef_acc(_coef_acc(None, icr[n1, k], qr), ici[n1, k], qi)
            rows = slice(n1 * N2 + r0, n1 * N2 + r0 + HY_SUB)
            if first:
                acc_ref[rows, :] = contrib
            elif contrib is not None:
                acc_ref[rows, :] += contrib


def _filter_spec_kernel(hf_ref, hb_ref, tw_ref, m2_ref, o_ref, a_ref, *, L):
    c = _dft_consts(L)
    N2, CT = HY_N2, HY_CT
    nb = L // N2
    cr, ci = c["stage1"](nb)
    for k, kp in _slot_groups(c["N1"]):
        w = CT if kp is None else 2 * CT
        specs = []
        for h_ref in (hf_ref, hb_ref):
            _stage1_group(lambda n1, r0: h_ref[n1 * N2 + r0:n1 * N2 + r0 + HY_SUB, :], nb, cr, ci, k, kp, tw_ref, a_ref)
            specs.append(_dft_mm(m2_ref, a_ref[:, :w], 3))
        xf, xb = specs
        g = jnp.concatenate([xf[:N2] + xb[:N2], xf[N2:] - xb[N2:]], axis=0)
        o_ref[k] = g[:, :CT]
        if kp is not None:
            o_ref[kp] = g[:, CT:]


def _filter_spectrum(hf, hb, L):
    c = _dft_consts(L)
    C = hf.shape[1]
    S, R = c["S"], 2 * HY_N2
    return pl.pallas_call(
        functools.partial(_filter_spec_kernel, L=L),
        grid=(C // HY_CT,),
        in_specs=[
            pl.BlockSpec((L, HY_CT), lambda j: (0, j)),
            pl.BlockSpec((L, HY_CT), lambda j: (0, j)),
            pl.BlockSpec((S, 2, HY_N2, LANES), lambda j: (0, 0, 0, 0)),
            pl.BlockSpec((2, R, R), lambda j: (0, 0, 0)),
        ],
        out_specs=pl.BlockSpec((S, R, HY_CT), lambda j: (0, 0, j)),
        out_shape=jax.ShapeDtypeStruct((S, R, C), F32),
        scratch_shapes=[pltpu.VMEM((R, 2 * HY_CT), F32)],
        compiler_params=_cparams(("parallel",)),
        name="hy_filter_spectrum",
    )(hf, hb, jnp.asarray(c["tw"]), jnp.asarray(c["m2"]))


def _long_conv_kernel(u_ref, gate_ref, spec_ref, skip_ref, tw_ref, m2_ref, m2t_ref, o_ref, a_ref, b_ref, acc_ref,
                      *, L):
    c = _dft_consts(L)
    N2, CT = HY_N2, HY_CT
    nb = L // N2
    cr, ci = c["stage1"](nb)
    icr, ici = c["icr"] * c["N"], c["ici"] * c["N"]
    for gi_, (k, kp) in enumerate(_slot_groups(c["N1"])):
        w = CT if kp is None else 2 * CT
        _stage1_group(lambda n1, r0: u_ref[0, n1 * N2 + r0:n1 * N2 + r0 + HY_SUB, :], nb, cr, ci, k, kp, tw_ref, a_ref)
        x = _dft_mm(m2_ref, a_ref[:, :w], HY_PASSES)
        g = spec_ref[k] if kp is None else jnp.concatenate([spec_ref[k], spec_ref[kp]], axis=1)
        xr, xi, gr, gi = x[:N2], x[N2:], g[:N2], g[N2:]
        y = jnp.concatenate([xr * gr - xi * gi, xr * gi + xi * gr], axis=0)
        b_ref[:, :w] = _dft_mm(m2t_ref, y, HY_PASSES)
        _stage1_inv_group(b_ref, nb, icr, ici, k, kp, tw_ref, acc_ref, gi_ == 0)
    o_ref[0] = (gate_ref[0] * (acc_ref[...] * (1.0 / c["N"]) + skip_ref[...] * u_ref[0])).astype(o_ref.dtype)


def _long_conv(u_arr, u_col, gate_arr, gate_col, spec, spec_col, skip, L, out_dtype):
    Bx, T, _ = u_arr.shape
    nseq = T // L
    c = _dft_consts(L)
    C = skip.shape[0]
    S, R = c["S"], 2 * HY_N2
    ub, gb, sb = u_col // HY_CT, gate_col // HY_CT, spec_col // HY_CT
    return pl.pallas_call(
        functools.partial(_long_conv_kernel, L=L),
        grid=(C // HY_CT, Bx, nseq),
        in_specs=[
            pl.BlockSpec((1, L, HY_CT), lambda j, b, s: (b, s, ub + j)),
            pl.BlockSpec((1, L, HY_CT), lambda j, b, s: (b, s, gb + j)),
            pl.BlockSpec((S, R, HY_CT), lambda j, b, s: (0, 0, sb + j)),
            pl.BlockSpec((1, HY_CT), lambda j, b, s: (0, j)),
            pl.BlockSpec((S, 2, HY_N2, LANES), lambda j, b, s: (0, 0, 0, 0)),
            pl.BlockSpec((2, R, R), lambda j, b, s: (0, 0, 0)),
            pl.BlockSpec((2, R, R), lambda j, b, s: (0, 0, 0)),
        ],
        out_specs=pl.BlockSpec((1, L, HY_CT), lambda j, b, s: (b, s, j)),
        out_shape=jax.ShapeDtypeStruct((Bx, T, C), out_dtype),
        scratch_shapes=[pltpu.VMEM((R, 2 * HY_CT), F32), pltpu.VMEM((R, 2 * HY_CT), F32),
                        pltpu.VMEM((L, HY_CT), F32)],
        compiler_params=_cparams(("parallel", "parallel", "parallel")),
        name="hy_long_conv",
    )(u_arr, gate_arr, spec, skip.reshape(1, C), jnp.asarray(c["tw"]), jnp.asarray(c["m2"]), jnp.asarray(c["m2t"]))


def _hyena_filters(L, w1, b1, w2, b2, w3, freq):
    t = jnp.arange(L, dtype=F32)
    t_norm = t / max(L - 1, 1)
    w = 2.0 * math.pi * t / L
    bands = jnp.linspace(1e-4, HY_EMB_BANDS - 1, HY_EMB_BANDS, dtype=F32)
    z = jnp.concatenate([t_norm[:, None], jnp.cos(w[:, None] * bands), -jnp.sin(w[:, None] * bands)], axis=-1)
    hp = lax.Precision.HIGHEST
    h = jnp.sin(freq[0] * (jnp.dot(z, w1, precision=hp) + b1))
    h = jnp.sin(freq[1] * (jnp.dot(h, w2, precision=hp) + b2))
    h = jnp.dot(h, w3, precision=hp).astype(F32).reshape(L, HY_ORDER, 2, HY_WIDTH)
    deltas = jnp.abs(jnp.linspace(math.log(HY_DECAY_TARGET) / HY_DECAY_SLOW,
                                  math.log(HY_DECAY_TARGET) / HY_DECAY_FAST, HY_WIDTH, dtype=F32))
    h = h * jnp.exp(-t_norm[:, None] * deltas)[:, None, None, :]
    return h / jnp.sum(jnp.abs(h), axis=0, keepdims=True)


def _hyena_spectra(L, hy_fp):
    filt = _hyena_filters(L, *hy_fp)
    hf = filt[:, :, 0].reshape(L, HY_ORDER * HY_WIDTH)
    hb = filt[:, :, 1].reshape(L, HY_ORDER * HY_WIDTH)
    return _filter_spectrum(hf, hb, L)


def _hyena(p_all, sw, sb, spec, skip, L):
    u3 = _short_conv(p_all, sw, sb, L)
    W = HY_WIDTH
    z1 = _long_conv(u3, 2 * W, u3, 0, spec, 0, skip[0], L, F32)
    return _long_conv(z1, 0, u3, W, spec, W, skip[1], L, BF16)


def _merge_kernel(att_ref, hf_ref, hb_ref, op_ref, hy_ref, gp0_ref, gp1_ref, gp2_ref, mg_ref, wb_ref, wo_ref,
                  res_ref, g_ref, o_ref):
    d = ML_HEAD_DIM
    h = hf_ref[0] + hb_ref[0]
    parts = []
    for hh in range(ML_HEADS):
        hs = h[:, hh * d:(hh + 1) * d]
        parts.append(hs * lax.rsqrt(jnp.mean(hs * hs, axis=-1, keepdims=True) + EPS))
    mls = jnp.concatenate(parts, axis=1) * mg_ref[...] * _sigmoid(op_ref[0].astype(F32))
    y = _sigmoid(gp0_ref[0].astype(F32)) * jnp.dot(att_ref[0], wb_ref[0], preferred_element_type=F32)
    y = y + _sigmoid(gp1_ref[0].astype(F32)) * jnp.dot(mls.astype(BF16), wb_ref[1], preferred_element_type=F32)
    y = y + _sigmoid(gp2_ref[0].astype(F32)) * jnp.dot(hy_ref[0], wb_ref[2], preferred_element_type=F32)
    o_ref[0] = res_ref[0] + g_ref[0] * jnp.dot(y.astype(BF16), wo_ref[...], preferred_element_type=F32)


def _merge(att, hf, hb, p_all, hy, ml_g, wb, wo, res, g, tm):
    Bx, T, D = res.shape
    W = ML_W
    row = lambda b, i: (b, i, 0)
    return pl.pallas_call(
        _merge_kernel,
        grid=(Bx, T // tm),
        in_specs=[
            pl.BlockSpec((1, tm, W), row),
            pl.BlockSpec((1, tm, W), row),
            pl.BlockSpec((1, tm, W), row),
            pl.BlockSpec((1, tm, W), lambda b, i: (b, i, COL_ML // W + 3)),
            pl.BlockSpec((1, tm, W), row),
            pl.BlockSpec((1, tm, D), lambda b, i: (b, i, COL_BG // D)),
            pl.BlockSpec((1, tm, D), lambda b, i: (b, i, COL_BG // D + 1)),
            pl.BlockSpec((1, tm, D), lambda b, i: (b, i, COL_BG // D + 2)),
            pl.BlockSpec((1, W), lambda b, i: (0, 0)),
            pl.BlockSpec((N_BRANCH, W, D), lambda b, i: (0, 0, 0)),
            pl.BlockSpec((D, D), lambda b, i: (0, 0)),
            pl.BlockSpec((1, tm, D), row),
            pl.BlockSpec((1, 1, D), lambda b, i: (b, 0, 0)),
        ],
        out_specs=pl.BlockSpec((1, tm, D), row),
        out_shape=jax.ShapeDtypeStruct((Bx, T, D), F32),
        compiler_params=_cparams(("parallel", "parallel")),
        name="merge",
    )(att, hf, hb, p_all, hy, p_all, p_all, p_all, ml_g.reshape(1, W), wb, wo, res, g)


FFN_CHUNK = 256
FFN_HALO = 16


def _ffn_down_kernel(u_ref, up_ref, un_ref, cw_ref, cb_ref, wd_ref, res_ref, g_ref, o_ref, act_ref, *, tiles_per_seq):
    i = pl.program_id(1)
    tm = u_ref.shape[1]
    first = (i % tiles_per_seq) == 0
    last = (i % tiles_per_seq) == tiles_per_seq - 1
    row8 = lax.broadcasted_iota(jnp.int32, (8, FFN_CHUNK), 0)

    def conv(c0):
        cs = slice(c0, c0 + FFN_CHUNK)
        x = u_ref[0, :, cs].astype(F32)
        pr = jnp.where(first, 0.0, up_ref[0, FFN_HALO - 1:FFN_HALO, cs].astype(F32))
        nx = jnp.where(last, 0.0, un_ref[0, 0:1, cs].astype(F32))
        prev = pltpu.roll(x, 1, axis=0)
        nxt = pltpu.roll(x, tm - 1, axis=0)
        prev = jnp.concatenate([jnp.where(row8 == 0, pr, prev[:8]), prev[8:]], axis=0)
        nxt = jnp.concatenate([nxt[:tm - 8], jnp.where(row8 == 7, nx, nxt[tm - 8:])], axis=0)
        w = cw_ref[:, cs]
        return cb_ref[:, cs] + prev * w[0:1] + x * w[1:2] + nxt * w[2:3]

    for c in range(D_FF // FFN_CHUNK):
        gate = conv(c * FFN_CHUNK)
        val = conv(D_FF + c * FFN_CHUNK)
        act_ref[:, c * FFN_CHUNK:(c + 1) * FFN_CHUNK] = (gate * _sigmoid(gate) * val).astype(BF16)
    o_ref[0] = res_ref[0] + g_ref[0] * jnp.dot(act_ref[...], wd_ref[...], preferred_element_type=F32)


def _ffn_down(u, cw, cb, wd, res, g, seq_len):
    Bx, T, D = res.shape
    tm = min(seq_len, 512)
    nt = T // tm
    hb = tm // FFN_HALO
    nh = T // FFN_HALO
    return pl.pallas_call(
        functools.partial(_ffn_down_kernel, tiles_per_seq=seq_len // tm),
        grid=(Bx, nt),
        in_specs=[
            pl.BlockSpec((1, tm, 2 * D_FF), lambda b, i: (b, i, 0)),
            pl.BlockSpec((1, FFN_HALO, 2 * D_FF), lambda b, i: (b, jnp.maximum(i * hb - 1, 0), 0)),
            pl.BlockSpec((1, FFN_HALO, 2 * D_FF), lambda b, i: (b, jnp.minimum((i + 1) * hb, nh - 1), 0)),
            pl.BlockSpec((3, 2 * D_FF), lambda b, i: (0, 0)),
            pl.BlockSpec((1, 2 * D_FF), lambda b, i: (0, 0)),
            pl.BlockSpec((D_FF, D), lambda b, i: (0, 0)),
            pl.BlockSpec((1, tm, D), lambda b, i: (b, i, 0)),
            pl.BlockSpec((1, 1, D), lambda b, i: (b, 0, 0)),
        ],
        out_specs=pl.BlockSpec((1, tm, D), lambda b, i: (b, i, 0)),
        out_shape=jax.ShapeDtypeStruct((Bx, T, D), F32),
        scratch_shapes=[pltpu.VMEM((tm, D_FF), BF16)],
        compiler_params=_cparams(("parallel", "parallel")),
        name="ffn_down",
    )(u, u, u, cw, cb.reshape(1, -1), wd, res, g)


def _pair_perm():
    hd, half = ATT_HEAD_DIM, ATT_HEAD_DIM // 2
    n_pair = ATT_Q // LANES
    qperm = []
    for p in range(n_pair):
        for sub in range(4):
            head = p if sub % 2 == 0 else n_pair + p
            qperm += [head * hd + (sub // 2) * half + dd for dd in range(half)]
    kperm = []
    for sub in range(4):
        kperm += [(sub % 2) * hd + (sub // 2) * half + dd for dd in range(half)]
    return np.asarray(qperm), np.asarray(kperm)


def _rope_tables(L):
    rows = L // GRID_W
    row = jnp.repeat(jnp.arange(rows, dtype=F32), GRID_W)
    col = jnp.tile(jnp.arange(GRID_W, dtype=F32), rows)
    nf = ATT_HEAD_DIM // 4
    inv = ROPE_THETA ** (-jnp.arange(nf, dtype=F32) / nf)
    ang = jnp.concatenate([row[:, None] * inv, col[:, None] * inv], axis=-1)
    cos, sin = jnp.cos(ang), jnp.sin(ang)
    return jnp.concatenate([cos, cos, cos, cos, -sin, -sin, sin, sin], axis=1)


def _prep_w_in(w):
    qperm, kperm = _pair_perm()
    o = IN_OFFSETS
    cols = [w[:, :ATT_Q][:, qperm], w[:, o[2]:o[6]], w[:, o[7]:o[8]], w[:, o[8]:],
            w[:, o[0]:o[1]][:, kperm], w[:, o[1]:o[2]][:, kperm]]
    wc = jnp.concatenate(cols, axis=1)
    w_gate = jnp.pad(w[:, o[6]:o[7]], ((0, 0), (0, LANES - ML_GATES))).astype(BF16)
    return jnp.pad(wc, ((0, 0), (0, IN_PAD - wc.shape[1]))).astype(BF16), w_gate


def _sink_cols(sink, rows_per_pair):
    n_pair = ATT_Q // LANES
    lo = jnp.repeat(sink[:n_pair], rows_per_pair)
    hi = jnp.repeat(sink[n_pair:], rows_per_pair)
    return jnp.stack([lo, hi])[:, :, None]


def kernel(x, c, ctx, c_ctx, ada_w, ada_b, norm1_g, norm2_g, w_in, att_sink, ml_gate_b, ml_norm_g,
           hy_short_w, hy_short_b, hy_w1, hy_b1, hy_w2, hy_b2, hy_w3, hy_freq, hy_skip,
           w_branch, w_out, w_up, ffn_conv_w, ffn_conv_b, w_down, final_g):
    B, L, D = x.shape
    Lc = ctx.shape[1]
    qperm, _ = _pair_perm()
    tabs = _rope_tables(L)
    sc_rows = jnp.concatenate([jax.nn.silu(c), jax.nn.silu(c_ctx)[None], jnp.zeros((8 - B - 1, D), F32)], axis=0)
    xl = x
    xc = ctx.reshape(1, B * Lc, D)
    for l in range(DEPTH):
        need_ctx = l < DEPTH - 1
        mod = _ada_proj(sc_rows, ada_w[l], ada_b[l])
        sh1, sc1, g1, sh2, sc2, g2 = [m[:B, None, :] for m in jnp.split(mod, 6, axis=-1)]
        csh1, csc1, cg1, csh2, csc2, cg2 = [m[B:B + 1, None, :] for m in jnp.split(mod, 6, axis=-1)]
        n1 = norm1_g[l][None, None, :]
        n2 = norm2_g[l][None, None, :]
        w_in_b, w_gate = _prep_w_in(w_in[l])
        wb = jnp.concatenate([w_branch[l][:1][:, qperm], w_branch[l][1:]], axis=0).astype(BF16)
        wo = w_out[l].astype(BF16)
        wu = w_up[l].astype(BF16)
        wd = w_down[l].astype(BF16)
        gate_bias = jnp.pad(ml_gate_b[l].reshape(1, ML_GATES), ((0, 0), (0, LANES - ML_GATES)))
        hy_fp = (hy_w1[l], hy_b1[l], hy_w2[l], hy_b2[l], hy_w3[l], hy_freq[l])

        pl_all = _norm_proj(xl, n1 * (1.0 + sc1), sh1, w_in_b, BF16)
        pc_all = _norm_proj(xc, n1 * (1.0 + csc1), csh1, w_in_b, BF16)
        gl = _norm_proj(xl, n1 * (1.0 + sc1), sh1, w_gate, F32)
        gc = _norm_proj(xc, n1 * (1.0 + csc1), csh1, w_gate, F32)
        att_l = _win_attention(pl_all, pc_all, tabs, _sink_cols(att_sink[l], ATT_BLOCK))
        hf, hb = _mlstm(pl_all, pc_all, gl, gc, gate_bias)
        hy_l = _hyena(pl_all, hy_short_w[l], hy_short_b[l], _hyena_spectra(L, hy_fp), hy_skip[l], L)
        xl = _merge(att_l, hf, hb, pl_all, hy_l, ml_norm_g[l], wb, wo, xl, g1, 512)
        u = _norm_proj(xl, n2 * (1.0 + sc2), sh2, wu, BF16)
        xl = _ffn_down(u, ffn_conv_w[l], ffn_conv_b[l], wd, xl, g2, L)
        if need_ctx:
            att_c = _ctx_attention(pc_all, _sink_cols(att_sink[l], Lc), B)
            hy_c = _hyena(pc_all, hy_short_w[l], hy_short_b[l], _hyena_spectra(Lc, hy_fp), hy_skip[l], Lc)
            hfc = hf[:, L:].reshape(1, B * Lc, ML_W)
            hbc = hb[:, L:].reshape(1, B * Lc, ML_W)
            xc = _merge(att_c, hfc, hbc, pc_all, hy_c, ml_norm_g[l], wb, wo, xc, cg1, Lc)
            uc = _norm_proj(xc, n2 * (1.0 + csc2), csh2, wu, BF16)
            xc = _ffn_down(uc, ffn_conv_w[l], ffn_conv_b[l], wd, xc, cg2, Lc)
    return _final_norm(xl, final_g)
```

```python
import functools
import math

import numpy as np
import jax
import jax.numpy as jnp
from jax import lax
from jax.experimental import pallas as pl
from jax.experimental.pallas import tpu as pltpu

F32 = jnp.float32
BF16 = jnp.bfloat16

D_MODEL = 1024
DEPTH = 4
GRID_W = 64
EPS = 1e-6
ATT_HEADS = 8
ATT_KV_HEADS = 2
ATT_HEAD_DIM = 64
ATT_BLOCK = 128
ROPE_THETA = 10000.0
ML_HEADS = 4
ML_HEAD_DIM = 128
ML_CHUNK = 128
ML_NB = 2
HY_WIDTH = 512
HY_ORDER = 2
HY_EMB_BANDS = 16
HY_DECAY_TARGET = 1e-2
HY_DECAY_FAST = 0.3
HY_DECAY_SLOW = 1.5
D_FF = 2816
N_BRANCH = 3
LANES = 128

ATT_Q = ATT_HEADS * ATT_HEAD_DIM
ATT_KV = ATT_KV_HEADS * ATT_HEAD_DIM
ML_W = ML_HEADS * ML_HEAD_DIM
ML_GATES = 2 * 2 * ML_HEADS
IN_SIZES = (ATT_Q, ATT_KV, ATT_KV, ML_W, ML_W, ML_W, ML_W, ML_GATES, 3 * HY_WIDTH, N_BRANCH * D_MODEL)
IN_OFFSETS = tuple(int(o) for o in np.cumsum(IN_SIZES)[:-1])

COL_Q = 0
COL_ML = COL_Q + ATT_Q
COL_HY = COL_ML + 4 * ML_W
COL_BG = COL_HY + 3 * HY_WIDTH
COL_K = COL_BG + N_BRANCH * D_MODEL
COL_V = COL_K + ATT_KV
IN_PAD = 7680
IN_TN = 1536

HY_N2 = 256
HY_CT = 128
HY_SUB = 32
HY_PASSES = 1

VMEM_LIMIT = 56 * 1024 * 1024


def _cparams(sem):
    return pltpu.CompilerParams(dimension_semantics=sem, vmem_limit_bytes=VMEM_LIMIT)


def _sigmoid(x):
    return 1.0 / (1.0 + jnp.exp(-x))


def _log_sigmoid(x):
    return jnp.minimum(x, 0.0) - jnp.log(1.0 + jnp.exp(-jnp.abs(x)))


def _rms_mod(x, gs, sh):
    ms = jnp.mean(x * x, axis=-1, keepdims=True)
    return (x * lax.rsqrt(ms + EPS) * gs + sh).astype(BF16)


def _norm_proj_kernel(x_ref, gs_ref, sh_ref, w_ref, *rest, with_aux):
    if with_aux:
        wa_ref, o_ref, oa_ref, h_ref = rest
    else:
        o_ref, h_ref = rest

    @pl.when(pl.program_id(2) == 0)
    def _():
        h = _rms_mod(x_ref[0], gs_ref[0], sh_ref[0])
        h_ref[...] = h
        if with_aux:
            oa_ref[0] = jnp.dot(h, wa_ref[...], preferred_element_type=F32)

    o_ref[0] = jnp.dot(h_ref[...], w_ref[...], preferred_element_type=F32).astype(o_ref.dtype)


def _norm_proj(x, gs, sh, w, tn, w_aux=None):
    B, T, D = x.shape
    N = w.shape[1]
    tm = min(T, 1024)
    in_specs = [
        pl.BlockSpec((1, tm, D), lambda b, i, j: (b, i, 0)),
        pl.BlockSpec((1, 1, D), lambda b, i, j: (b, 0, 0)),
        pl.BlockSpec((1, 1, D), lambda b, i, j: (b, 0, 0)),
        pl.BlockSpec((D, tn), lambda b, i, j: (0, j)),
    ]
    out_specs = [pl.BlockSpec((1, tm, tn), lambda b, i, j: (b, i, j))]
    out_shape = [jax.ShapeDtypeStruct((B, T, N), BF16)]
    args = [x, gs, sh, w]
    if w_aux is not None:
        na = w_aux.shape[1]
        in_specs.append(pl.BlockSpec((D, na), lambda b, i, j: (0, 0)))
        out_specs.append(pl.BlockSpec((1, tm, na), lambda b, i, j: (b, i, 0)))
        out_shape.append(jax.ShapeDtypeStruct((B, T, na), F32))
        args.append(w_aux)
    outs = pl.pallas_call(
        functools.partial(_norm_proj_kernel, with_aux=w_aux is not None),
        grid=(B, T // tm, N // tn),
        in_specs=in_specs,
        out_specs=out_specs,
        out_shape=out_shape,
        scratch_shapes=[pltpu.VMEM((tm, D), BF16)],
        compiler_params=_cparams(("parallel", "parallel", "arbitrary")),
        name="norm_proj",
    )(*args)
    return outs if w_aux is not None else outs[0]


def _small_proj_kernel(a_ref, w_ref, b_ref, o_ref):
    a = a_ref[...]
    a_hi = a.astype(BF16)
    a_lo = (a - a_hi.astype(F32)).astype(BF16)
    w = w_ref[...]
    w_hi = w.astype(BF16)
    w_lo = (w - w_hi.astype(F32)).astype(BF16)
    acc = jnp.dot(a_hi, w_hi, preferred_element_type=F32)
    acc = acc + (jnp.dot(a_hi, w_lo, preferred_element_type=F32) + jnp.dot(a_lo, w_hi, preferred_element_type=F32))
    o_ref[...] = acc + b_ref[...]


def _ada_proj(a, w, b):
    M, D = a.shape
    N = w.shape[1]
    tn = 1024
    return pl.pallas_call(
        _small_proj_kernel,
        grid=(N // tn,),
        in_specs=[
            pl.BlockSpec((M, D), lambda j: (0, 0)),
            pl.BlockSpec((D, tn), lambda j: (0, j)),
            pl.BlockSpec((1, tn), lambda j: (0, j)),
        ],
        out_specs=pl.BlockSpec((M, tn), lambda j: (0, j)),
        out_shape=jax.ShapeDtypeStruct((M, N), F32),
        compiler_params=_cparams(("parallel",)),
        name="ada_proj",
    )(a, w, b.reshape(1, N))


def _final_norm_kernel(x_ref, g_ref, o_ref):
    x = x_ref[0]
    ms = jnp.mean(x * x, axis=-1, keepdims=True)
    o_ref[0] = x * lax.rsqrt(ms + EPS) * g_ref[...]


def _final_norm(x, g):
    B, T, D = x.shape
    tm = 512
    return pl.pallas_call(
        _final_norm_kernel,
        grid=(B, T // tm),
        in_specs=[
            pl.BlockSpec((1, tm, D), lambda b, i: (b, i, 0)),
            pl.BlockSpec((1, D), lambda b, i: (0, 0)),
        ],
        out_specs=pl.BlockSpec((1, tm, D), lambda b, i: (b, i, 0)),
        out_shape=jax.ShapeDtypeStruct((B, T, D), F32),
        compiler_params=_cparams(("parallel", "parallel")),
        name="final_norm",
    )(x, g.reshape(1, D))


def _lane_lo_mask(shape):
    lane = lax.broadcasted_iota(jnp.int32, shape, len(shape) - 1)
    return (lane % 64) < 32


def _rope(x, tab):
    c = tab[:, :LANES]
    s = tab[:, LANES:]
    outs = []
    for g in range(x.shape[1] // LANES):
        xg = x[:, g * LANES:(g + 1) * LANES]
        outs.append(xg * c + pltpu.roll(xg, 64, axis=1) * s)
    return outs[0] if len(outs) == 1 else jnp.concatenate(outs, axis=1)


def _attend(qs, kcat, vcat, bias, sink_lo, sink_hi):
    lo = _lane_lo_mask(kcat.shape)
    T = ATT_BLOCK
    outs = []
    for msk, sink in ((lo, sink_lo), (jnp.logical_not(lo), sink_hi)):
        kh = jnp.where(msk, kcat, 0.0).astype(BF16)
        s = lax.dot_general(qs, kh, (((1,), (1,)), ((), ())), preferred_element_type=F32)
        if bias is not None:
            c0, bias_prev, bias_next = bias
            s = jnp.concatenate([s[:, :c0], s[:, c0:c0 + T] + bias_prev, s[:, c0 + T:c0 + 2 * T],
                                 s[:, c0 + 2 * T:] + bias_next], axis=1)
        m = jnp.maximum(jnp.max(s, axis=-1, keepdims=True), sink)
        p = jnp.exp(s - m)
        den = jnp.sum(p, axis=-1, keepdims=True) + jnp.exp(sink - m)
        outs.append(jnp.dot(p.astype(BF16), vcat, preferred_element_type=F32) * (1.0 / den))
    return jnp.where(_lane_lo_mask(outs[0].shape), outs[0], outs[1])


def _win_attn_kernel(q_ref, kp_ref, kc_ref, kn_ref, vp_ref, vc_ref, vn_ref, kx_ref, vx_ref,
                     tp_ref, tc_ref, tn_ref, sink_ref, o_ref):
    i = pl.program_id(1)
    nb = pl.num_programs(1)
    T = ATT_BLOCK
    q = _rope(q_ref[0].astype(F32), tc_ref[...]) * (ATT_HEAD_DIM ** -0.5)
    n_pair = ATT_Q // LANES
    qs = jnp.concatenate([q[:, g * LANES:(g + 1) * LANES] for g in range(n_pair)], axis=0).astype(BF16)
    kcat = jnp.concatenate([kx_ref[0].astype(F32), _rope(kp_ref[0].astype(F32), tp_ref[...]),
                            _rope(kc_ref[0].astype(F32), tc_ref[...]),
                            _rope(kn_ref[0].astype(F32), tn_ref[...])], axis=0)
    vcat = jnp.concatenate([vx_ref[0], vp_ref[0], vc_ref[0], vn_ref[0]], axis=0)
    Lc = kx_ref.shape[1]
    t = lax.broadcasted_iota(jnp.int32, (T, T), 0)
    s = lax.broadcasted_iota(jnp.int32, (T, T), 1)
    bias_prev = jnp.where((s >= t) & (i > 0), 0.0, -1e30).astype(F32)
    bias_next = jnp.where((s <= t) & (i < nb - 1), 0.0, -1e30).astype(F32)
    bias = (Lc, jnp.concatenate([bias_prev] * n_pair, axis=0), jnp.concatenate([bias_next] * n_pair, axis=0))
    o = _attend(qs, kcat, vcat, bias, sink_ref[0], sink_ref[1])
    o_ref[0] = jnp.concatenate([o[g * T:(g + 1) * T] for g in range(n_pair)], axis=1).astype(o_ref.dtype)


def _win_attention(pl_all, pc_all, tabs, sinks):
    B, L, _ = pl_all.shape
    Lc = pc_all.shape[1] // B
    T = ATT_BLOCK
    nb = L // T
    kb, vb = COL_K // LANES, COL_V // LANES

    def blk(col, d):
        return pl.BlockSpec((1, T, LANES), lambda b, i: (b, jnp.clip(i + d, 0, nb - 1), col))

    def tab(d):
        return pl.BlockSpec((T, 2 * LANES), lambda b, i: (jnp.clip(i + d, 0, nb - 1), 0))

    return pl.pallas_call(
        _win_attn_kernel,
        grid=(B, nb),
        in_specs=[
            pl.BlockSpec((1, T, ATT_Q), lambda b, i: (b, i, 0)),
            blk(kb, -1), blk(kb, 0), blk(kb, 1),
            blk(vb, -1), blk(vb, 0), blk(vb, 1),
            pl.BlockSpec((1, Lc, LANES), lambda b, i: (0, b, kb)),
            pl.BlockSpec((1, Lc, LANES), lambda b, i: (0, b, vb)),
            tab(-1), tab(0), tab(1),
            pl.BlockSpec((2, ATT_Q, 1), lambda b, i: (0, 0, 0)),
        ],
        out_specs=pl.BlockSpec((1, T, ATT_Q), lambda b, i: (b, i, 0)),
        out_shape=jax.ShapeDtypeStruct((B, L, ATT_Q), BF16),
        compiler_params=_cparams(("parallel", "parallel")),
        name="win_attention",
    )(pl_all, pl_all, pl_all, pl_all, pl_all, pl_all, pl_all, pc_all, pc_all, tabs, tabs, tabs, sinks)


def _ctx_attn_kernel(q_ref, kx_ref, vx_ref, sink_ref, o_ref):
    Lc = q_ref.shape[1]
    n_pair = ATT_Q // LANES
    q = q_ref[0].astype(F32) * (ATT_HEAD_DIM ** -0.5)
    qs = jnp.concatenate([q[:, g * LANES:(g + 1) * LANES] for g in range(n_pair)], axis=0).astype(BF16)
    o = _attend(qs, kx_ref[0].astype(F32), vx_ref[0], None, sink_ref[0], sink_ref[1])
    o_ref[0] = jnp.concatenate([o[g * Lc:(g + 1) * Lc] for g in range(n_pair)], axis=1).astype(o_ref.dtype)


def _ctx_attention(pc_all, sinks, B):
    Lc = pc_all.shape[1] // B
    kb, vb = COL_K // LANES, COL_V // LANES
    return pl.pallas_call(
        _ctx_attn_kernel,
        grid=(B,),
        in_specs=[
            pl.BlockSpec((1, Lc, ATT_Q), lambda b: (0, b, 0)),
            pl.BlockSpec((1, Lc, LANES), lambda b: (0, b, kb)),
            pl.BlockSpec((1, Lc, LANES), lambda b: (0, b, vb)),
            pl.BlockSpec((2, ATT_Q // LANES * Lc, 1), lambda b: (0, 0, 0)),
        ],
        out_specs=pl.BlockSpec((1, Lc, ATT_Q), lambda b: (0, b, 0)),
        out_shape=jax.ShapeDtypeStruct((1, B * Lc, ATT_Q), BF16),
        compiler_params=_cparams(("parallel",)),
        name="ctx_attention",
    )(pc_all, pc_all, pc_all, sinks)


def _dot_hl(a_exact, x):
    x_hi = x.astype(BF16)
    x_lo = (x - x_hi.astype(F32)).astype(BF16)
    return jnp.dot(a_exact, x_hi, preferred_element_type=F32) + jnp.dot(a_exact, x_lo, preferred_element_type=F32)


def _mlstm_kernel(qf_l, kf_l, vf_l, gf_l, qb_l, kb_l, vb_l, gb_l,
                  qf_c, kf_c, vf_c, gf_c, qb_c, kb_c, vb_c, gb_c, bias_ref,
                  hf_ref, hb_ref, ct_ref, m_ref, *, n_ctx_chunks):
    j = pl.program_id(1)
    T = ML_CHUNK
    d = ML_HEAD_DIM
    is_ctx = j < n_ctx_chunks

    @pl.when(j == 0)
    def _():
        ct_ref[...] = jnp.zeros_like(ct_ref)
        m_ref[...] = jnp.zeros_like(m_ref)

    row = lax.broadcasted_iota(jnp.int32, (T, T), 0)
    col = lax.broadcasted_iota(jnp.int32, (T, T), 1)
    ones_td = jnp.ones((T, d), BF16)

    for bi in range(ML_NB):
        for di, (q_l, k_l, v_l, g_l, q_c, k_c, v_c, g_c, h_ref) in enumerate(
                ((qf_l, kf_l, vf_l, gf_l, qf_c, kf_c, vf_c, gf_c, hf_ref),
                 (qb_l, kb_l, vb_l, gb_l, qb_c, kb_c, vb_c, gb_c, hb_ref))):
            keep = (col <= row) if di == 0 else (col >= row)
            keep_b = keep.astype(BF16)
            g = jnp.where(is_ctx, g_c[bi], g_l[bi]) + bias_ref[...]
            lane = lax.broadcasted_iota(jnp.int32, g.shape, 1)
            is_f = (lane % 8) >= 4
            gv = jnp.where(is_f, _log_sigmoid(g), g)
            gt = gv.T
            cum_c = _dot_hl(keep_b, gv)
            gt_hi = gt.astype(BF16)
            gt_lo = (gt - gt_hi.astype(F32)).astype(BF16)
            nt = (((1,), (1,)), ((), ()))
            cum_r = lax.dot_general(gt_hi, keep_b, nt, preferred_element_type=F32) + \
                lax.dot_general(gt_lo, keep_b, nt, preferred_element_type=F32)
            q_all = jnp.where(is_ctx, q_c[bi], q_l[bi])
            k_all = jnp.where(is_ctx, k_c[bi], k_l[bi]).astype(F32) * (d ** -0.5)
            v_all = jnp.where(is_ctx, v_c[bi], v_l[bi])
            h_out = []
            for h in range(ML_HEADS):
                ci = di * 8 + h
                cf = di * 8 + 4 + h
                sl = slice(h * d, (h + 1) * d)
                qh = q_all[:, sl]
                kh = k_all[:, sl]
                vh = v_all[:, sl]
                b_c = cum_c[:, cf:cf + 1]
                b_r = cum_r[cf:cf + 1, :]
                i_c = gv[:, ci:ci + 1]
                i_r = gt[ci:ci + 1, :]
                m_old = m_ref[(bi * 2 + di) * ML_HEADS + h]
                bl = b_c[T - 1:T, :] if di == 0 else b_c[0:1, :]
                a = b_c + m_old
                dmat = jnp.where(keep, b_c - b_r + i_r, -1e30)
                mt = jnp.maximum(a, jnp.max(dmat, axis=-1, keepdims=True))
                w_inter = jnp.exp(a - mt)
                s = lax.dot_general(qh, kh.astype(BF16), (((1,), (1,)), ((), ())), preferred_element_type=F32)
                s = (s * jnp.exp(dmat - mt)).astype(BF16)
                v_aug = jnp.concatenate([vh, ones_td], axis=1)
                ct = ct_ref[(bi * 2 + di) * ML_HEADS + h]
                r = w_inter * jnp.dot(qh, ct.astype(BF16), preferred_element_type=F32) + \
                    jnp.dot(s, v_aug, preferred_element_type=F32)
                num = r[:, :d]
                den = r[:, d:]
                h_out.append(num / jnp.maximum(jnp.abs(den), jnp.exp(-mt)))
                src = bl - b_c + i_c
                m_new = jnp.maximum(bl + m_old, jnp.max(src, axis=0, keepdims=True))
                gk = (jnp.exp(src - m_new) * kh).T.astype(BF16)
                decay = jnp.exp(bl + m_old - m_new)
                ct_ref[(bi * 2 + di) * ML_HEADS + h] = decay * ct + jnp.dot(gk, v_aug, preferred_element_type=F32)
                m_ref[(bi * 2 + di) * ML_HEADS + h] = m_new
            h_ref[bi] = jnp.concatenate(h_out, axis=1)


def _mlstm(pl_all, pc_all, gl, gc, gate_bias):
    B, L, _ = pl_all.shape
    Lc = pc_all.shape[1] // B
    T = ML_CHUNK
    nl, ncx = L // T, Lc // T
    nsteps = nl + ncx
    pc3 = pc_all.reshape(B, Lc, pc_all.shape[2])
    gc3 = gc.reshape(B, Lc, gc.shape[2])

    def lat_f(j):
        return jnp.clip(j - ncx, 0, nl - 1)

    def lat_b(j):
        return jnp.clip(nsteps - 1 - j, 0, nl - 1)

    def ctx_f(j):
        return jnp.clip(j, 0, ncx - 1)

    def ctx_b(j):
        return jnp.clip(ncx - 1 - j, 0, ncx - 1)

    def lat_specs(fn):
        base = COL_ML // ML_W
        return [pl.BlockSpec((ML_NB, T, ML_W), lambda b, j, o=o: (b, fn(j), base + o)) for o in range(3)] + \
               [pl.BlockSpec((ML_NB, T, LANES), lambda b, j: (b, fn(j), 0))]

    def ctx_specs(fn):
        base = COL_ML // ML_W
        return [pl.BlockSpec((ML_NB, T, ML_W), lambda b, j, o=o: (b, fn(j), base + o)) for o in range(3)] + \
               [pl.BlockSpec((ML_NB, T, LANES), lambda b, j: (b, fn(j), 0))]

    def out_f(b, j):
        return (b, jnp.where(j < ncx, nl + j, j - ncx), 0)

    def out_b(b, j):
        return (b, jnp.where(j < ncx, nl + ncx - 1 - j, nsteps - 1 - j), 0)

    return pl.pallas_call(
        functools.partial(_mlstm_kernel, n_ctx_chunks=ncx),
        grid=(B // ML_NB, nsteps),
        in_specs=lat_specs(lat_f) + lat_specs(lat_b) + ctx_specs(ctx_f) + ctx_specs(ctx_b) +
        [pl.BlockSpec((1, LANES), lambda b, j: (0, 0))],
        out_specs=[pl.BlockSpec((ML_NB, T, ML_W), out_f), pl.BlockSpec((ML_NB, T, ML_W), out_b)],
        out_shape=[jax.ShapeDtypeStruct((B, L + Lc, ML_W), F32)] * 2,
        scratch_shapes=[pltpu.VMEM((ML_NB * 2 * ML_HEADS, ML_HEAD_DIM, 2 * ML_HEAD_DIM), F32),
                        pltpu.VMEM((ML_NB * 2 * ML_HEADS, 1, 1), F32)],
        compiler_params=_cparams(("parallel", "arbitrary")),
        name="mlstm",
    )(*(([pl_all] * 3 + [gl]) * 2 + ([pc3] * 3 + [gc3]) * 2 + [gate_bias]))


def _short_conv_kernel(x_ref, w_ref, b_ref, o_ref):
    x = x_ref[0].astype(F32)
    L = x.shape[0]
    row = lax.broadcasted_iota(jnp.int32, x.shape, 0)
    prev = jnp.where(row == 0, 0.0, pltpu.roll(x, 1, axis=0))
    nxt = jnp.where(row == L - 1, 0.0, pltpu.roll(x, L - 1, axis=0))
    w = w_ref[...]
    o_ref[0] = b_ref[...] + prev * w[0:1] + x * w[1:2] + nxt * w[2:3]


def _short_conv(p_all, w, b, seq_len):
    Bx, T, _ = p_all.shape
    nseq = T // seq_len
    C = w.shape[1]
    cb = COL_HY // LANES
    return pl.pallas_call(
        _short_conv_kernel,
        grid=(Bx, nseq, C // LANES),
        in_specs=[
            pl.BlockSpec((1, seq_len, LANES), lambda b, s, j: (b, s, cb + j)),
            pl.BlockSpec((3, LANES), lambda b, s, j: (0, j)),
            pl.BlockSpec((1, LANES), lambda b, s, j: (0, j)),
        ],
        out_specs=pl.BlockSpec((1, seq_len, LANES), lambda b, s, j: (b, s, j)),
        out_shape=jax.ShapeDtypeStruct((Bx, T, C), F32),
        compiler_params=_cparams(("parallel", "parallel", "parallel")),
        name="hy_short_conv",
    )(p_all, w, b.reshape(1, C))


@functools.lru_cache(maxsize=None)
def _dft_consts(L):
    N2 = HY_N2
    N = 2 * L
    N1 = N // N2
    S = N1 // 2 + 1
    k1 = np.arange(S)[:, None]
    n2 = np.arange(N2)
    tw = np.exp(-2j * np.pi * k1 * n2[None, :] / N)
    tw_tab = np.stack([np.repeat(tw.real[:, :, None], LANES, 2), np.repeat(tw.imag[:, :, None], LANES, 2)], 1)
    F = np.exp(-2j * np.pi * np.outer(n2, n2) / N2)
    M2 = np.block([[F.real, -F.imag], [F.imag, F.real]])

    def split(m):
        m32 = m.astype(np.float32)
        hi = m32.astype(BF16)
        lo = (m32 - hi.astype(np.float32)).astype(BF16)
        return np.stack([hi, lo])

    def stage1(nb):
        n1 = np.arange(nb)[None, :]
        ang = 2 * np.pi * k1 * n1 / N1
        return np.cos(ang), -np.sin(ang)

    n1o = np.arange(L // N2)[:, None]
    k1o = np.arange(S)[None, :]
    ang = 2 * np.pi * n1o * k1o / N1
    wgt = np.where((k1o == 0) | (k1o == N1 // 2), 1.0, 2.0) / N
    return dict(N=N, N1=N1, S=S, stage1=stage1, tw=tw_tab.astype(np.float32),
                m2=split(M2), m2t=split(M2.T), icr=np.cos(ang) * wgt, ici=-np.sin(ang) * wgt)


def _dft_mm(m_ref, x, passes):
    x_hi = x.astype(BF16)
    acc = jnp.dot(m_ref[0], x_hi, preferred_element_type=F32)
    if passes >= 2:
        x_lo = (x - x_hi.astype(F32)).astype(BF16)
        acc = acc + jnp.dot(m_ref[0], x_lo, preferred_element_type=F32)
    if passes >= 3:
        acc = acc + jnp.dot(m_ref[1], x_hi, preferred_element_type=F32)
    return acc


def _coef_acc(acc, c, x):
    if x is None or abs(c) < 1e-12:
        return acc
    if abs(c - 1.0) < 1e-12:
        return x if acc is None else acc + x
    if abs(c + 1.0) < 1e-12:
        return -x if acc is None else acc - x
    return c * x if acc is None else acc + c * x


def _vadd(a, b):
    return b if a is None else a if b is None else a + b


def _vsub(a, b):
    return (None if b is None else -b) if a is None else a if b is None else a - b


def _slot_groups(N1):
    half = N1 // 2
    return [(k, half - k if half - k != k else None) for k in range(half // 2 + 1)]


def _stage1_group(load, nb, cr, ci, k, kp, tw_ref, a_ref):
    N2, CT = HY_N2, HY_CT
    for r0 in range(0, N2, HY_SUB):
        if kp is None:
            ar = ai = None
            for n1 in range(nb):
                xb = load(n1, r0)
                ar = _coef_acc(ar, cr[k, n1], xb)
                ai = _coef_acc(ai, ci[k, n1], xb)
            slots = [(k, ar, ai)]
        else:
            even = [None, None]
            odd = [None, None]
            for n1 in range(nb):
                xb = load(n1, r0)
                tgt = even if n1 % 2 == 0 else odd
                tgt[0] = _coef_acc(tgt[0], cr[k, n1], xb)
                tgt[1] = _coef_acc(tgt[1], ci[k, n1], xb)
            slots = [(k, _vadd(even[0], odd[0]), _vadd(even[1], odd[1])),
                     (kp, _vsub(even[0], odd[0]), _vsub(odd[1], even[1]))]
        for idx, (kk, ar, ai) in enumerate(slots):
            zero = jnp.zeros((HY_SUB, CT), F32)
            ar = zero if ar is None else ar
            if kk > 0:
                twr = tw_ref[kk, 0, r0:r0 + HY_SUB, :]
                twi = tw_ref[kk, 1, r0:r0 + HY_SUB, :]
                if ai is None:
                    ar, ai = ar * twr, ar * twi
                else:
                    ar, ai = ar * twr - ai * twi, ar * twi + ai * twr
            ai = zero if ai is None else ai
            a_ref[r0:r0 + HY_SUB, idx * CT:(idx + 1) * CT] = ar
            a_ref[N2 + r0:N2 + r0 + HY_SUB, idx * CT:(idx + 1) * CT] = ai


def _stage1_inv_group(b_ref, nb, icr, ici, k, kp, tw_ref, acc_ref, first):
    N2, CT = HY_N2, HY_CT

    def load(idx, kk, r0):
        br = b_ref[r0:r0 + HY_SUB, idx * CT:(idx + 1) * CT]
        bi = b_ref[N2 + r0:N2 + r0 + HY_SUB, idx * CT:(idx + 1) * CT]
        if kk > 0:
            twr = tw_ref[kk, 0, r0:r0 + HY_SUB, :]
            twi = tw_ref[kk, 1, r0:r0 + HY_SUB, :]
            br, bi = br * twr + bi * twi, bi * twr - br * twi
        return br, bi

    for r0 in range(0, N2, HY_SUB):
        br, bi = load(0, k, r0)
        if kp is None:
            q_even = q_odd = (br, bi)
        else:
            br2, bi2 = load(1, kp, r0)
            q_even = (br + br2, bi - bi2)
            q_odd = (br - br2, bi + bi2)
        for n1 in range(nb):
            qr, qi = q_even if n1 % 2 == 0 else q_odd
            contrib = _coef_acc(_coef_acc(None, icr[n1, k], qr), ici[n1, k], qi)
            rows = slice(n1 * N2 + r0, n1 * N2 + r0 + HY_SUB)
            if first:
                acc_ref[rows, :] = contrib
            elif contrib is not None:
                acc_ref[rows, :] += contrib


def _filter_spec_kernel(hf_ref, hb_ref, tw_ref, m2_ref, o_ref, a_ref, *, L):
    c = _dft_consts(L)
    N2, CT = HY_N2, HY_CT
    nb = L // N2
    cr, ci = c["stage1"](nb)
    for k, kp in _slot_groups(c["N1"]):
        w = CT if kp is None else 2 * CT
        specs = []
        for h_ref in (hf_ref, hb_ref):
            _stage1_group(lambda n1, r0: h_ref[n1 * N2 + r0:n1 * N2 + r0 + HY_SUB, :], nb, cr, ci, k, kp, tw_ref, a_ref)
            specs.append(_dft_mm(m2_ref, a_ref[:, :w], 3))
        xf, xb = specs
        g = jnp.concatenate([xf[:N2] + xb[:N2], xf[N2:] - xb[N2:]], axis=0)
        o_ref[k] = g[:, :CT]
        if kp is not None:
            o_ref[kp] = g[:, CT:]


def _filter_spectrum(hf, hb, L):
    c = _dft_consts(L)
    C = hf.shape[1]
    S, R = c["S"], 2 * HY_N2
    return pl.pallas_call(
        functools.partial(_filter_spec_kernel, L=L),
        grid=(C // HY_CT,),
        in_specs=[
            pl.BlockSpec((L, HY_CT), lambda j: (0, j)),
            pl.BlockSpec((L, HY_CT), lambda j: (0, j)),
            pl.BlockSpec((S, 2, HY_N2, LANES), lambda j: (0, 0, 0, 0)),
            pl.BlockSpec((2, R, R), lambda j: (0, 0, 0)),
        ],
        out_specs=pl.BlockSpec((S, R, HY_CT), lambda j: (0, 0, j)),
        out_shape=jax.ShapeDtypeStruct((S, R, C), F32),
        scratch_shapes=[pltpu.VMEM((R, 2 * HY_CT), F32)],
        compiler_params=_cparams(("parallel",)),
        name="hy_filter_spectrum",
    )(hf, hb, jnp.asarray(c["tw"]), jnp.asarray(c["m2"]))


def _long_conv_kernel(u_ref, gate_ref, spec_ref, skip_ref, tw_ref, m2_ref, m2t_ref, o_ref, a_ref, b_ref, acc_ref,
                      *, L):
    c = _dft_consts(L)
    N2, CT = HY_N2, HY_CT
    nb = L // N2
    cr, ci = c["stage1"](nb)
    icr, ici = c["icr"] * c["N"], c["ici"] * c["N"]
    for gi_, (k, kp) in enumerate(_slot_groups(c["N1"])):
        w = CT if kp is None else 2 * CT
        _stage1_group(lambda n1, r0: u_ref[0, n1 * N2 + r0:n1 * N2 + r0 + HY_SUB, :], nb, cr, ci, k, kp, tw_ref, a_ref)
        x = _dft_mm(m2_ref, a_ref[:, :w], HY_PASSES)
        g = spec_ref[k] if kp is None else jnp.concatenate([spec_ref[k], spec_ref[kp]], axis=1)
        xr, xi, gr, gi = x[:N2], x[N2:], g[:N2], g[N2:]
        y = jnp.concatenate([xr * gr - xi * gi, xr * gi + xi * gr], axis=0)
        b_ref[:, :w] = _dft_mm(m2t_ref, y, HY_PASSES)
        _stage1_inv_group(b_ref, nb, icr, ici, k, kp, tw_ref, acc_ref, gi_ == 0)
    o_ref[0] = (gate_ref[0] * (acc_ref[...] * (1.0 / c["N"]) + skip_ref[...] * u_ref[0])).astype(o_ref.dtype)


def _long_conv(u_arr, u_col, gate_arr, gate_col, spec, spec_col, skip, L, out_dtype):
    Bx, T, _ = u_arr.shape
    nseq = T // L
    c = _dft_consts(L)
    C = skip.shape[0]
    S, R = c["S"], 2 * HY_N2
    ub, gb, sb = u_col // HY_CT, gate_col // HY_CT, spec_col // HY_CT
    return pl.pallas_call(
        functools.partial(_long_conv_kernel, L=L),
        grid=(C // HY_CT, Bx, nseq),
        in_specs=[
            pl.BlockSpec((1, L, HY_CT), lambda j, b, s: (b, s, ub + j)),
            pl.BlockSpec((1, L, HY_CT), lambda j, b, s: (b, s, gb + j)),
            pl.BlockSpec((S, R, HY_CT), lambda j, b, s: (0, 0, sb + j)),
            pl.BlockSpec((1, HY_CT), lambda j, b, s: (0, j)),
            pl.BlockSpec((S, 2, HY_N2, LANES), lambda j, b, s: (0, 0, 0, 0)),
            pl.BlockSpec((2, R, R), lambda j, b, s: (0, 0, 0)),
            pl.BlockSpec((2, R, R), lambda j, b, s: (0, 0, 0)),
        ],
        out_specs=pl.BlockSpec((1, L, HY_CT), lambda j, b, s: (b, s, j)),
        out_shape=jax.ShapeDtypeStruct((Bx, T, C), out_dtype),
        scratch_shapes=[pltpu.VMEM((R, 2 * HY_CT), F32), pltpu.VMEM((R, 2 * HY_CT), F32),
                        pltpu.VMEM((L, HY_CT), F32)],
        compiler_params=_cparams(("parallel", "parallel", "parallel")),
        name="hy_long_conv",
    )(u_arr, gate_arr, spec, skip.reshape(1, C), jnp.asarray(c["tw"]), jnp.asarray(c["m2"]), jnp.asarray(c["m2t"]))


def _hyena_filters(L, w1, b1, w2, b2, w3, freq):
    t = jnp.arange(L, dtype=F32)
    t_norm = t / max(L - 1, 1)
    w = 2.0 * math.pi * t / L
    bands = jnp.linspace(1e-4, HY_EMB_BANDS - 1, HY_EMB_BANDS, dtype=F32)
    z = jnp.concatenate([t_norm[:, None], jnp.cos(w[:, None] * bands), -jnp.sin(w[:, None] * bands)], axis=-1)
    hp = lax.Precision.HIGHEST
    h = jnp.sin(freq[0] * (jnp.dot(z, w1, precision=hp) + b1))
    h = jnp.sin(freq[1] * (jnp.dot(h, w2, precision=hp) + b2))
    h = jnp.dot(h, w3, precision=hp).astype(F32).reshape(L, HY_ORDER, 2, HY_WIDTH)
    deltas = jnp.abs(jnp.linspace(math.log(HY_DECAY_TARGET) / HY_DECAY_SLOW,
                                  math.log(HY_DECAY_TARGET) / HY_DECAY_FAST, HY_WIDTH, dtype=F32))
    h = h * jnp.exp(-t_norm[:, None] * deltas)[:, None, None, :]
    return h / jnp.sum(jnp.abs(h), axis=0, keepdims=True)


def _hyena_spectra(L, hy_fp):
    filt = _hyena_filters(L, *hy_fp)
    hf = filt[:, :, 0].reshape(L, HY_ORDER * HY_WIDTH)
    hb = filt[:, :, 1].reshape(L, HY_ORDER * HY_WIDTH)
    return _filter_spectrum(hf, hb, L)


def _hyena(p_all, sw, sb, spec, skip, L):
    u3 = _short_conv(p_all, sw, sb, L)
    W = HY_WIDTH
    z1 = _long_conv(u3, 2 * W, u3, 0, spec, 0, skip[0], L, F32)
    return _long_conv(z1, 0, u3, W, spec, W, skip[1], L, BF16)


def _merge_kernel(att_ref, hf_ref, hb_ref, op_ref, hy_ref, gp0_ref, gp1_ref, gp2_ref, mg_ref, wb_ref, wo_ref,
                  res_ref, g_ref, o_ref):
    d = ML_HEAD_DIM
    h = hf_ref[0] + hb_ref[0]
    parts = []
    for hh in range(ML_HEADS):
        hs = h[:, hh * d:(hh + 1) * d]
        parts.append(hs * lax.rsqrt(jnp.mean(hs * hs, axis=-1, keepdims=True) + EPS))
    mls = jnp.concatenate(parts, axis=1) * mg_ref[...] * _sigmoid(op_ref[0].astype(F32))
    y = _sigmoid(gp0_ref[0].astype(F32)) * jnp.dot(att_ref[0], wb_ref[0], preferred_element_type=F32)
    y = y + _sigmoid(gp1_ref[0].astype(F32)) * jnp.dot(mls.astype(BF16), wb_ref[1], preferred_element_type=F32)
    y = y + _sigmoid(gp2_ref[0].astype(F32)) * jnp.dot(hy_ref[0], wb_ref[2], preferred_element_type=F32)
    o_ref[0] = res_ref[0] + g_ref[0] * jnp.dot(y.astype(BF16), wo_ref[...], preferred_element_type=F32)


def _merge(att, hf, hb, p_all, hy, ml_g, wb, wo, res, g, tm):
    Bx, T, D = res.shape
    W = ML_W
    row = lambda b, i: (b, i, 0)
    return pl.pallas_call(
        _merge_kernel,
        grid=(Bx, T // tm),
        in_specs=[
            pl.BlockSpec((1, tm, W), row),
            pl.BlockSpec((1, tm, W), row),
            pl.BlockSpec((1, tm, W), row),
            pl.BlockSpec((1, tm, W), lambda b, i: (b, i, COL_ML // W + 3)),
            pl.BlockSpec((1, tm, W), row),
            pl.BlockSpec((1, tm, D), lambda b, i: (b, i, COL_BG // D)),
            pl.BlockSpec((1, tm, D), lambda b, i: (b, i, COL_BG // D + 1)),
            pl.BlockSpec((1, tm, D), lambda b, i: (b, i, COL_BG // D + 2)),
            pl.BlockSpec((1, W), lambda b, i: (0, 0)),
            pl.BlockSpec((N_BRANCH, W, D), lambda b, i: (0, 0, 0)),
            pl.BlockSpec((D, D), lambda b, i: (0, 0)),
            pl.BlockSpec((1, tm, D), row),
            pl.BlockSpec((1, 1, D), lambda b, i: (b, 0, 0)),
        ],
        out_specs=pl.BlockSpec((1, tm, D), row),
        out_shape=jax.ShapeDtypeStruct((Bx, T, D), F32),
        compiler_params=_cparams(("parallel", "parallel")),
        name="merge",
    )(att, hf, hb, p_all, hy, p_all, p_all, p_all, ml_g.reshape(1, W), wb, wo, res, g)


FFN_TC = 1408
FFN_HALO = 8


def _ffn_up_kernel(x_ref, xp_ref, xn_ref, gs_ref, sh_ref, wg_ref, wv_ref, cwg_ref, cwv_ref, cbg_ref, cbv_ref,
                   o_ref, h_ref, hh_ref, *, tiles_per_seq):
    i = pl.program_id(1)
    tm = x_ref.shape[1]

    @pl.when(pl.program_id(2) == 0)
    def _():
        h_ref[...] = _rms_mod(x_ref[0], gs_ref[0], sh_ref[0])
        halo = jnp.concatenate([xp_ref[0], xn_ref[0]], axis=0)
        hh_ref[...] = _rms_mod(halo, gs_ref[0], sh_ref[0])

    first = (i % tiles_per_seq) == 0
    last = (i % tiles_per_seq) == tiles_per_seq - 1
    row8 = lax.broadcasted_iota(jnp.int32, (8, FFN_TC), 0)

    def conv_half(w_ref, cw_ref, cb_ref):
        u = jnp.dot(h_ref[...], w_ref[...], preferred_element_type=F32)
        uh = jnp.dot(hh_ref[...], w_ref[...], preferred_element_type=F32)
        pr = jnp.where(first, 0.0, uh[FFN_HALO - 1:FFN_HALO])
        nx = jnp.where(last, 0.0, uh[FFN_HALO:FFN_HALO + 1])
        prev = pltpu.roll(u, 1, axis=0)
        nxt = pltpu.roll(u, tm - 1, axis=0)
        prev = jnp.concatenate([jnp.where(row8 == 0, pr, prev[:8]), prev[8:]], axis=0)
        nxt = jnp.concatenate([nxt[:tm - 8], jnp.where(row8 == 7, nx, nxt[tm - 8:])], axis=0)
        cw = cw_ref[...]
        return cb_ref[...] + prev * cw[0:1] + u * cw[1:2] + nxt * cw[2:3]

    gate = conv_half(wg_ref, cwg_ref, cbg_ref)
    val = conv_half(wv_ref, cwv_ref, cbv_ref)
    o_ref[0] = (gate * _sigmoid(gate) * val).astype(o_ref.dtype)


def _ffn_up(x, gs, sh, wu, cw, cb, seq_len):
    Bx, T, D = x.shape
    tm = min(seq_len, 512)
    nj = D_FF // FFN_TC
    hb = tm // FFN_HALO
    nh = T // FFN_HALO
    col = lambda b, i, j: (0, j)
    col2 = lambda b, i, j: (0, nj + j)
    return pl.pallas_call(
        functools.partial(_ffn_up_kernel, tiles_per_seq=seq_len // tm),
        grid=(Bx, T // tm, nj),
        in_specs=[
            pl.BlockSpec((1, tm, D), lambda b, i, j: (b, i, 0)),
            pl.BlockSpec((1, FFN_HALO, D), lambda b, i, j: (b, jnp.maximum(i * hb - 1, 0), 0)),
            pl.BlockSpec((1, FFN_HALO, D), lambda b, i, j: (b, jnp.minimum((i + 1) * hb, nh - 1), 0)),
            pl.BlockSpec((1, 1, D), lambda b, i, j: (b, 0, 0)),
            pl.BlockSpec((1, 1, D), lambda b, i, j: (b, 0, 0)),
            pl.BlockSpec((D, FFN_TC), col), pl.BlockSpec((D, FFN_TC), col2),
            pl.BlockSpec((3, FFN_TC), col), pl.BlockSpec((3, FFN_TC), col2),
            pl.BlockSpec((1, FFN_TC), col), pl.BlockSpec((1, FFN_TC), col2),
        ],
        out_specs=pl.BlockSpec((1, tm, FFN_TC), lambda b, i, j: (b, i, j)),
        out_shape=jax.ShapeDtypeStruct((Bx, T, D_FF), BF16),
        scratch_shapes=[pltpu.VMEM((tm, D), BF16), pltpu.VMEM((2 * FFN_HALO, D), BF16)],
        compiler_params=_cparams(("parallel", "parallel", "arbitrary")),
        name="ffn_up",
    )(x, x, x, gs, sh, wu, wu, cw, cw, cb.reshape(1, -1), cb.reshape(1, -1))


def _proj_res_kernel(a_ref, w_ref, res_ref, g_ref, o_ref):
    o_ref[0] = res_ref[0] + g_ref[0] * jnp.dot(a_ref[0], w_ref[...], preferred_element_type=F32)


def _proj_res(a, w, res, g):
    Bx, T, D = res.shape
    K = a.shape[2]
    tm = min(T, 512)
    return pl.pallas_call(
        _proj_res_kernel,
        grid=(Bx, T // tm),
        in_specs=[
            pl.BlockSpec((1, tm, K), lambda b, i: (b, i, 0)),
            pl.BlockSpec((K, D), lambda b, i: (0, 0)),
            pl.BlockSpec((1, tm, D), lambda b, i: (b, i, 0)),
            pl.BlockSpec((1, 1, D), lambda b, i: (b, 0, 0)),
        ],
        out_specs=pl.BlockSpec((1, tm, D), lambda b, i: (b, i, 0)),
        out_shape=jax.ShapeDtypeStruct((Bx, T, D), F32),
        compiler_params=_cparams(("parallel", "parallel")),
        name="ffn_down",
    )(a, w, res, g)


def _pair_perm():
    hd, half = ATT_HEAD_DIM, ATT_HEAD_DIM // 2
    n_pair = ATT_Q // LANES
    qperm = []
    for p in range(n_pair):
        for sub in range(4):
            head = p if sub % 2 == 0 else n_pair + p
            qperm += [head * hd + (sub // 2) * half + dd for dd in range(half)]
    kperm = []
    for sub in range(4):
        kperm += [(sub % 2) * hd + (sub // 2) * half + dd for dd in range(half)]
    return np.asarray(qperm), np.asarray(kperm)


def _rope_tables(L):
    rows = L // GRID_W
    row = jnp.repeat(jnp.arange(rows, dtype=F32), GRID_W)
    col = jnp.tile(jnp.arange(GRID_W, dtype=F32), rows)
    nf = ATT_HEAD_DIM // 4
    inv = ROPE_THETA ** (-jnp.arange(nf, dtype=F32) / nf)
    ang = jnp.concatenate([row[:, None] * inv, col[:, None] * inv], axis=-1)
    cos, sin = jnp.cos(ang), jnp.sin(ang)
    return jnp.concatenate([cos, cos, cos, cos, -sin, -sin, sin, sin], axis=1)


def _prep_w_in(w):
    qperm, kperm = _pair_perm()
    o = IN_OFFSETS
    cols = [w[:, :ATT_Q][:, qperm], w[:, o[2]:o[6]], w[:, o[7]:o[8]], w[:, o[8]:],
            w[:, o[0]:o[1]][:, kperm], w[:, o[1]:o[2]][:, kperm]]
    wc = jnp.concatenate(cols, axis=1)
    w_gate = jnp.pad(w[:, o[6]:o[7]], ((0, 0), (0, LANES - ML_GATES))).astype(BF16)
    return jnp.pad(wc, ((0, 0), (0, IN_PAD - wc.shape[1]))).astype(BF16), w_gate


def _sink_cols(sink, rows_per_pair):
    n_pair = ATT_Q // LANES
    lo = jnp.repeat(sink[:n_pair], rows_per_pair)
    hi = jnp.repeat(sink[n_pair:], rows_per_pair)
    return jnp.stack([lo, hi])[:, :, None]


def kernel(x, c, ctx, c_ctx, ada_w, ada_b, norm1_g, norm2_g, w_in, att_sink, ml_gate_b, ml_norm_g,
           hy_short_w, hy_short_b, hy_w1, hy_b1, hy_w2, hy_b2, hy_w3, hy_freq, hy_skip,
           w_branch, w_out, w_up, ffn_conv_w, ffn_conv_b, w_down, final_g):
    B, L, D = x.shape
    Lc = ctx.shape[1]
    qperm, _ = _pair_perm()
    tabs = _rope_tables(L)
    sc_rows = jnp.concatenate([jax.nn.silu(c), jax.nn.silu(c_ctx)[None], jnp.zeros((8 - B - 1, D), F32)], axis=0)
    xl = x
    xc = ctx.reshape(1, B * Lc, D)
    for l in range(DEPTH):
        need_ctx = l < DEPTH - 1
        mod = _ada_proj(sc_rows, ada_w[l], ada_b[l])
        sh1, sc1, g1, sh2, sc2, g2 = [m[:B, None, :] for m in jnp.split(mod, 6, axis=-1)]
        csh1, csc1, cg1, csh2, csc2, cg2 = [m[B:B + 1, None, :] for m in jnp.split(mod, 6, axis=-1)]
        n1 = norm1_g[l][None, None, :]
        n2 = norm2_g[l][None, None, :]
        w_in_b, w_gate = _prep_w_in(w_in[l])
        wb = jnp.concatenate([w_branch[l][:1][:, qperm], w_branch[l][1:]], axis=0).astype(BF16)
        wo = w_out[l].astype(BF16)
        wu = w_up[l].astype(BF16)
        wd = w_down[l].astype(BF16)
        gate_bias = jnp.pad(ml_gate_b[l].reshape(1, ML_GATES), ((0, 0), (0, LANES - ML_GATES)))
        hy_fp = (hy_w1[l], hy_b1[l], hy_w2[l], hy_b2[l], hy_w3[l], hy_freq[l])

        pl_all, gl = _norm_proj(xl, n1 * (1.0 + sc1), sh1, w_in_b, IN_TN, w_gate)
        pc_all, gc = _norm_proj(xc, n1 * (1.0 + csc1), csh1, w_in_b, IN_TN, w_gate)
        att_l = _win_attention(pl_all, pc_all, tabs, _sink_cols(att_sink[l], ATT_BLOCK))
        hf, hb = _mlstm(pl_all, pc_all, gl, gc, gate_bias)
        hy_l = _hyena(pl_all, hy_short_w[l], hy_short_b[l], _hyena_spectra(L, hy_fp), hy_skip[l], L)
        xl = _merge(att_l, hf, hb, pl_all, hy_l, ml_norm_g[l], wb, wo, xl, g1, 512)
        act = _ffn_up(xl, n2 * (1.0 + sc2), sh2, wu, ffn_conv_w[l], ffn_conv_b[l], L)
        xl = _proj_res(act, wd, xl, g2)
        if need_ctx:
            att_c = _ctx_attention(pc_all, _sink_cols(att_sink[l], Lc), B)
            hy_c = _hyena(pc_all, hy_short_w[l], hy_short_b[l], _hyena_spectra(Lc, hy_fp), hy_skip[l], Lc)
            hfc = hf[:, L:].reshape(1, B * Lc, ML_W)
            hbc = hb[:, L:].reshape(1, B * Lc, ML_W)
            xc = _merge(att_c, hfc, hbc, pc_all, hy_c, ml_norm_g[l], wb, wo, xc, cg1, Lc)
            act_c = _ffn_up(xc, n2 * (1.0 + csc2), csh2, wu, ffn_conv_w[l], ffn_conv_b[l], Lc)
            xc = _proj_res(act_c, wd, xc, cg2)
    return _final_norm(xl, final_g)
```

```python
import functools
import math

import numpy as np
import jax
import jax.numpy as jnp
from jax import lax
from jax.experimental import pallas as pl
from jax.experimental.pallas import tpu as pltpu

F32 = jnp.float32
BF16 = jnp.bfloat16

D_MODEL = 1024
DEPTH = 4
GRID_W = 64
EPS = 1e-6
ATT_HEADS = 8
ATT_KV_HEADS = 2
ATT_HEAD_DIM = 64
ATT_BLOCK = 128
ROPE_THETA = 10000.0
ML_HEADS = 4
ML_HEAD_DIM = 128
ML_CHUNK = 128
ML_NB = 2
HY_WIDTH = 512
HY_ORDER = 2
HY_EMB_BANDS = 16
HY_DECAY_TARGET = 1e-2
HY_DECAY_FAST = 0.3
HY_DECAY_SLOW = 1.5
D_FF = 2816
N_BRANCH = 3
LANES = 128

ATT_Q = ATT_HEADS * ATT_HEAD_DIM
ATT_KV = ATT_KV_HEADS * ATT_HEAD_DIM
ML_W = ML_HEADS * ML_HEAD_DIM
ML_GATES = 2 * 2 * ML_HEADS
IN_SIZES = (ATT_Q, ATT_KV, ATT_KV, ML_W, ML_W, ML_W, ML_W, ML_GATES, 3 * HY_WIDTH, N_BRANCH * D_MODEL)
IN_OFFSETS = tuple(int(o) for o in np.cumsum(IN_SIZES)[:-1])

COL_Q = 0
COL_ML = COL_Q + ATT_Q
COL_HY = COL_ML + 4 * ML_W
COL_BG = COL_HY + 3 * HY_WIDTH
COL_K = COL_BG + N_BRANCH * D_MODEL
COL_V = COL_K + ATT_KV
IN_PAD = 7680
IN_TN = 1536

HY_N2 = 256
HY_CT = 128
HY_SUB = 32
HY_PASSES = 1

VMEM_LIMIT = 56 * 1024 * 1024


def _cparams(sem):
    return pltpu.CompilerParams(dimension_semantics=sem, vmem_limit_bytes=VMEM_LIMIT)


def _sigmoid(x):
    return 0.5 * jnp.tanh(0.5 * x) + 0.5


def _log_sigmoid(x):
    return jnp.minimum(x, 0.0) - jnp.log(1.0 + jnp.exp(-jnp.abs(x)))


def _rms_mod(x, gs, sh):
    ms = jnp.mean(x * x, axis=-1, keepdims=True)
    return (x * lax.rsqrt(ms + EPS) * gs + sh).astype(BF16)


def _norm_proj_kernel(x_ref, gs_ref, sh_ref, w_ref, *rest, with_aux):
    if with_aux:
        wa_ref, o_ref, oa_ref, h_ref = rest
    else:
        o_ref, h_ref = rest

    @pl.when(pl.program_id(2) == 0)
    def _():
        h = _rms_mod(x_ref[0], gs_ref[0], sh_ref[0])
        h_ref[...] = h
        if with_aux:
            oa_ref[0] = jnp.dot(h, wa_ref[...], preferred_element_type=F32)

    o_ref[0] = jnp.dot(h_ref[...], w_ref[...], preferred_element_type=F32).astype(o_ref.dtype)


def _norm_proj(x, gs, sh, w, tn, w_aux=None):
    B, T, D = x.shape
    N = w.shape[1]
    tm = min(T, 1024)
    in_specs = [
        pl.BlockSpec((1, tm, D), lambda b, i, j: (b, i, 0)),
        pl.BlockSpec((1, 1, D), lambda b, i, j: (b, 0, 0)),
        pl.BlockSpec((1, 1, D), lambda b, i, j: (b, 0, 0)),
        pl.BlockSpec((D, tn), lambda b, i, j: (0, j)),
    ]
    out_specs = [pl.BlockSpec((1, tm, tn), lambda b, i, j: (b, i, j))]
    out_shape = [jax.ShapeDtypeStruct((B, T, N), BF16)]
    args = [x, gs, sh, w]
    if w_aux is not None:
        na = w_aux.shape[1]
        in_specs.append(pl.BlockSpec((D, na), lambda b, i, j: (0, 0)))
        out_specs.append(pl.BlockSpec((1, tm, na), lambda b, i, j: (b, i, 0)))
        out_shape.append(jax.ShapeDtypeStruct((B, T, na), F32))
        args.append(w_aux)
    outs = pl.pallas_call(
        functools.partial(_norm_proj_kernel, with_aux=w_aux is not None),
        grid=(B, T // tm, N // tn),
        in_specs=in_specs,
        out_specs=out_specs,
        out_shape=out_shape,
        scratch_shapes=[pltpu.VMEM((tm, D), BF16)],
        compiler_params=_cparams(("parallel", "parallel", "arbitrary")),
        name="norm_proj",
    )(*args)
    return outs if w_aux is not None else outs[0]


def _small_proj_kernel(a_ref, w_ref, b_ref, o_ref):
    a = a_ref[...]
    a_hi = a.astype(BF16)
    a_lo = (a - a_hi.astype(F32)).astype(BF16)
    w = w_ref[...]
    w_hi = w.astype(BF16)
    w_lo = (w - w_hi.astype(F32)).astype(BF16)
    acc = jnp.dot(a_hi, w_hi, preferred_element_type=F32)
    acc = acc + (jnp.dot(a_hi, w_lo, preferred_element_type=F32) + jnp.dot(a_lo, w_hi, preferred_element_type=F32))
    o_ref[...] = acc + b_ref[...]


def _ada_proj(a, w, b):
    M, D = a.shape
    N = w.shape[1]
    tn = 1024
    return pl.pallas_call(
        _small_proj_kernel,
        grid=(N // tn,),
        in_specs=[
            pl.BlockSpec((M, D), lambda j: (0, 0)),
            pl.BlockSpec((D, tn), lambda j: (0, j)),
            pl.BlockSpec((1, tn), lambda j: (0, j)),
        ],
        out_specs=pl.BlockSpec((M, tn), lambda j: (0, j)),
        out_shape=jax.ShapeDtypeStruct((M, N), F32),
        compiler_params=_cparams(("parallel",)),
        name="ada_proj",
    )(a, w, b.reshape(1, N))


def _final_norm_kernel(x_ref, g_ref, o_ref):
    x = x_ref[0]
    ms = jnp.mean(x * x, axis=-1, keepdims=True)
    o_ref[0] = x * lax.rsqrt(ms + EPS) * g_ref[...]


def _final_norm(x, g):
    B, T, D = x.shape
    tm = 512
    return pl.pallas_call(
        _final_norm_kernel,
        grid=(B, T // tm),
        in_specs=[
            pl.BlockSpec((1, tm, D), lambda b, i: (b, i, 0)),
            pl.BlockSpec((1, D), lambda b, i: (0, 0)),
        ],
        out_specs=pl.BlockSpec((1, tm, D), lambda b, i: (b, i, 0)),
        out_shape=jax.ShapeDtypeStruct((B, T, D), F32),
        compiler_params=_cparams(("parallel", "parallel")),
        name="final_norm",
    )(x, g.reshape(1, D))


def _lane_lo_mask(shape):
    lane = lax.broadcasted_iota(jnp.int32, shape, len(shape) - 1)
    return (lane % 64) < 32


def _rope(x, tab):
    c = tab[:, :LANES]
    s = tab[:, LANES:]
    outs = []
    for g in range(x.shape[1] // LANES):
        xg = x[:, g * LANES:(g + 1) * LANES]
        outs.append(xg * c + pltpu.roll(xg, 64, axis=1) * s)
    return outs[0] if len(outs) == 1 else jnp.concatenate(outs, axis=1)


def _attend(qs, kcat, vcat, bias, sink_lo, sink_hi):
    lo = _lane_lo_mask(kcat.shape)
    T = ATT_BLOCK
    outs = []
    for msk, sink in ((lo, sink_lo), (jnp.logical_not(lo), sink_hi)):
        kh = jnp.where(msk, kcat, 0.0).astype(BF16)
        s = lax.dot_general(qs, kh, (((1,), (1,)), ((), ())), preferred_element_type=F32)
        if bias is not None:
            c0, bias_prev, bias_next = bias
            s = jnp.concatenate([s[:, :c0], s[:, c0:c0 + T] + bias_prev, s[:, c0 + T:c0 + 2 * T],
                                 s[:, c0 + 2 * T:] + bias_next], axis=1)
        m = jnp.maximum(jnp.max(s, axis=-1, keepdims=True), sink)
        p = jnp.exp(s - m)
        den = jnp.sum(p, axis=-1, keepdims=True) + jnp.exp(sink - m)
        outs.append(jnp.dot(p.astype(BF16), vcat, preferred_element_type=F32) * (1.0 / den))
    return jnp.where(_lane_lo_mask(outs[0].shape), outs[0], outs[1])


def _win_attn_kernel(q_ref, kp_ref, kc_ref, kn_ref, vp_ref, vc_ref, vn_ref, kx_ref, vx_ref,
                     tp_ref, tc_ref, tn_ref, sink_ref, o_ref):
    i = pl.program_id(1)
    nb = pl.num_programs(1)
    T = ATT_BLOCK
    q = _rope(q_ref[0].astype(F32), tc_ref[...]) * (ATT_HEAD_DIM ** -0.5)
    n_pair = ATT_Q // LANES
    qs = jnp.concatenate([q[:, g * LANES:(g + 1) * LANES] for g in range(n_pair)], axis=0).astype(BF16)
    kcat = jnp.concatenate([kx_ref[0].astype(F32), _rope(kp_ref[0].astype(F32), tp_ref[...]),
                            _rope(kc_ref[0].astype(F32), tc_ref[...]),
                            _rope(kn_ref[0].astype(F32), tn_ref[...])], axis=0)
    vcat = jnp.concatenate([vx_ref[0], vp_ref[0], vc_ref[0], vn_ref[0]], axis=0)
    Lc = kx_ref.shape[1]
    t = lax.broadcasted_iota(jnp.int32, (T, T), 0)
    s = lax.broadcasted_iota(jnp.int32, (T, T), 1)
    bias_prev = jnp.where((s >= t) & (i > 0), 0.0, -1e30).astype(F32)
    bias_next = jnp.where((s <= t) & (i < nb - 1), 0.0, -1e30).astype(F32)
    bias = (Lc, jnp.concatenate([bias_prev] * n_pair, axis=0), jnp.concatenate([bias_next] * n_pair, axis=0))
    o = _attend(qs, kcat, vcat, bias, sink_ref[0], sink_ref[1])
    o_ref[0] = jnp.concatenate([o[g * T:(g + 1) * T] for g in range(n_pair)], axis=1).astype(o_ref.dtype)


def _win_attention(pl_all, pc_all, tabs, sinks):
    B, L, _ = pl_all.shape
    Lc = pc_all.shape[1] // B
    T = ATT_BLOCK
    nb = L // T
    kb, vb = COL_K // LANES, COL_V // LANES

    def blk(col, d):
        return pl.BlockSpec((1, T, LANES), lambda b, i: (b, jnp.clip(i + d, 0, nb - 1), col))

    def tab(d):
        return pl.BlockSpec((T, 2 * LANES), lambda b, i: (jnp.clip(i + d, 0, nb - 1), 0))

    return pl.pallas_call(
        _win_attn_kernel,
        grid=(B, nb),
        in_specs=[
            pl.BlockSpec((1, T, ATT_Q), lambda b, i: (b, i, 0)),
            blk(kb, -1), blk(kb, 0), blk(kb, 1),
            blk(vb, -1), blk(vb, 0), blk(vb, 1),
            pl.BlockSpec((1, Lc, LANES), lambda b, i: (0, b, kb)),
            pl.BlockSpec((1, Lc, LANES), lambda b, i: (0, b, vb)),
            tab(-1), tab(0), tab(1),
            pl.BlockSpec((2, ATT_Q, 1), lambda b, i: (0, 0, 0)),
        ],
        out_specs=pl.BlockSpec((1, T, ATT_Q), lambda b, i: (b, i, 0)),
        out_shape=jax.ShapeDtypeStruct((B, L, ATT_Q), BF16),
        compiler_params=_cparams(("parallel", "parallel")),
        name="win_attention",
    )(pl_all, pl_all, pl_all, pl_all, pl_all, pl_all, pl_all, pc_all, pc_all, tabs, tabs, tabs, sinks)


def _ctx_attn_kernel(q_ref, kx_ref, vx_ref, sink_ref, o_ref):
    Lc = q_ref.shape[1]
    n_pair = ATT_Q // LANES
    q = q_ref[0].astype(F32) * (ATT_HEAD_DIM ** -0.5)
    qs = jnp.concatenate([q[:, g * LANES:(g + 1) * LANES] for g in range(n_pair)], axis=0).astype(BF16)
    o = _attend(qs, kx_ref[0].astype(F32), vx_ref[0], None, sink_ref[0], sink_ref[1])
    o_ref[0] = jnp.concatenate([o[g * Lc:(g + 1) * Lc] for g in range(n_pair)], axis=1).astype(o_ref.dtype)


def _ctx_attention(pc_all, sinks, B):
    Lc = pc_all.shape[1] // B
    kb, vb = COL_K // LANES, COL_V // LANES
    return pl.pallas_call(
        _ctx_attn_kernel,
        grid=(B,),
        in_specs=[
            pl.BlockSpec((1, Lc, ATT_Q), lambda b: (0, b, 0)),
            pl.BlockSpec((1, Lc, LANES), lambda b: (0, b, kb)),
            pl.BlockSpec((1, Lc, LANES), lambda b: (0, b, vb)),
            pl.BlockSpec((2, ATT_Q // LANES * Lc, 1), lambda b: (0, 0, 0)),
        ],
        out_specs=pl.BlockSpec((1, Lc, ATT_Q), lambda b: (0, b, 0)),
        out_shape=jax.ShapeDtypeStruct((1, B * Lc, ATT_Q), BF16),
        compiler_params=_cparams(("parallel",)),
        name="ctx_attention",
    )(pc_all, pc_all, pc_all, sinks)


def _dot_hl(a_exact, x):
    x_hi = x.astype(BF16)
    x_lo = (x - x_hi.astype(F32)).astype(BF16)
    return jnp.dot(a_exact, x_hi, preferred_element_type=F32) + jnp.dot(a_exact, x_lo, preferred_element_type=F32)


def _mlstm_kernel(qf_l, kf_l, vf_l, gf_l, qb_l, kb_l, vb_l, gb_l,
                  qf_c, kf_c, vf_c, gf_c, qb_c, kb_c, vb_c, gb_c, bias_ref,
                  hf_ref, hb_ref, ct_ref, m_ref, *, n_ctx_chunks):
    j = pl.program_id(1)
    T = ML_CHUNK
    d = ML_HEAD_DIM
    is_ctx = j < n_ctx_chunks

    @pl.when(j == 0)
    def _():
        ct_ref[...] = jnp.zeros_like(ct_ref)
        m_ref[...] = jnp.zeros_like(m_ref)

    row = lax.broadcasted_iota(jnp.int32, (T, T), 0)
    col = lax.broadcasted_iota(jnp.int32, (T, T), 1)
    ones_td = jnp.ones((T, d), BF16)

    nt = (((1,), (1,)), ((), ()))
    sel_row = lax.broadcasted_iota(jnp.int32, (LANES, 2 * ML_HEADS * LANES), 0)
    sel_blk = lax.broadcasted_iota(jnp.int32, (LANES, 2 * ML_HEADS * LANES), 1) // LANES
    chains = []
    for bi in range(ML_NB):
        for di, (q_l, k_l, v_l, g_l, q_c, k_c, v_c, g_c) in enumerate(
                ((qf_l, kf_l, vf_l, gf_l, qf_c, kf_c, vf_c, gf_c),
                 (qb_l, kb_l, vb_l, gb_l, qb_c, kb_c, vb_c, gb_c))):
            keep = (col <= row) if di == 0 else (col >= row)
            keep_b = keep.astype(BF16)
            g = jnp.where(is_ctx, g_c[bi], g_l[bi]) + bias_ref[...]
            lane = lax.broadcasted_iota(jnp.int32, g.shape, 1)
            is_f = (lane % 8) >= 4
            gv = jnp.where(is_f, _log_sigmoid(g), g)
            gt = gv.T
            cum_c = _dot_hl(keep_b, gv)
            gt_hi = gt.astype(BF16)
            gt_lo = (gt - gt_hi.astype(F32)).astype(BF16)
            cum_r = lax.dot_general(gt_hi, keep_b, nt, preferred_element_type=F32) + \
                lax.dot_general(gt_lo, keep_b, nt, preferred_element_type=F32)
            src_col = jnp.where(sel_blk < ML_HEADS, di * 8 + ML_HEADS + sel_blk, di * 8 + sel_blk - ML_HEADS)
            sel = jnp.where(sel_row == src_col, 1.0, 0.0).astype(BF16)
            x = jnp.where(is_f, cum_c, gv)
            x_hi = x.astype(BF16)
            x_lo = (x - x_hi.astype(F32)).astype(BF16)
            full = jnp.dot(x_hi, sel, preferred_element_type=F32) + jnp.dot(x_lo, sel, preferred_element_type=F32)
            q_all = jnp.where(is_ctx, q_c[bi], q_l[bi])
            k_all = jnp.where(is_ctx, k_c[bi], k_l[bi]).astype(F32) * (d ** -0.5)
            v_all = jnp.where(is_ctx, v_c[bi], v_l[bi])
            for h in range(ML_HEADS):
                ci = di * 8 + h
                cf = di * 8 + 4 + h
                sl = slice(h * d, (h + 1) * d)
                b_full = full[:, h * LANES:(h + 1) * LANES]
                chains.append(dict(
                    idx=(bi * 2 + di) * ML_HEADS + h, keep=keep, q=q_all[:, sl], k=k_all[:, sl], v=v_all[:, sl],
                    b_full=b_full, i_full=full[:, (ML_HEADS + h) * LANES:(ML_HEADS + h + 1) * LANES],
                    b_r=cum_r[cf:cf + 1, :], i_r=gt[ci:ci + 1, :],
                    bl=b_full[T - 1:T, :] if di == 0 else b_full[0:1, :]))
    for c in chains:
        c["m_old"] = m_ref[c["idx"]]
        c["a"] = c["b_full"] + c["m_old"]
        c["dmat"] = jnp.where(c["keep"], c["b_full"] - c["b_r"] + c["i_r"], -1e30)
    for c in chains:
        c["mt"] = jnp.maximum(c["a"], jnp.broadcast_to(jnp.max(c["dmat"], axis=-1, keepdims=True), (T, T)))
    for c in chains:
        c["qk"] = lax.dot_general(c["q"], c["k"].astype(BF16), nt, preferred_element_type=F32)
    for c in chains:
        s = (c["qk"] * jnp.exp(c["dmat"] - c["mt"])).astype(BF16)
        wq = (c["q"].astype(F32) * jnp.exp(c["a"] - c["mt"])).astype(BF16)
        c["v_aug"] = jnp.concatenate([c["v"], ones_td], axis=1)
        c["ct"] = ct_ref[c["idx"]]
        lhs = jnp.concatenate([wq, s], axis=1)
        rhs = jnp.concatenate([c["ct"].astype(BF16), c["v_aug"]], axis=0)
        c["r"] = jnp.dot(lhs, rhs, preferred_element_type=F32)
    for c in chains:
        r = c["r"]
        c["h"] = r[:, :d] / jnp.maximum(jnp.abs(r[:, d:]), jnp.exp(-c["mt"]))
    for c in chains:
        src = c["bl"] - c["b_full"] + c["i_full"]
        m_new = jnp.maximum(c["bl"] + c["m_old"], jnp.max(src, axis=0, keepdims=True))
        gk = (jnp.exp(src - m_new) * c["k"]).T.astype(BF16)
        decay = jnp.exp(c["bl"] + c["m_old"] - m_new)
        ct_ref[c["idx"]] = jnp.concatenate([decay, decay], axis=1) * c["ct"] + \
            jnp.dot(gk, c["v_aug"], preferred_element_type=F32)
        m_ref[c["idx"]] = m_new
    for bi in range(ML_NB):
        for di, h_ref in enumerate((hf_ref, hb_ref)):
            base = (bi * 2 + di) * ML_HEADS
            h_ref[bi] = jnp.concatenate([chains[base + h]["h"] for h in range(ML_HEADS)], axis=1)


def _mlstm(pl_all, pc_all, gl, gc, gate_bias):
    B, L, _ = pl_all.shape
    Lc = pc_all.shape[1] // B
    T = ML_CHUNK
    nl, ncx = L // T, Lc // T
    nsteps = nl + ncx
    pc3 = pc_all.reshape(B, Lc, pc_all.shape[2])
    gc3 = gc.reshape(B, Lc, gc.shape[2])

    def lat_f(j):
        return jnp.clip(j - ncx, 0, nl - 1)

    def lat_b(j):
        return jnp.clip(nsteps - 1 - j, 0, nl - 1)

    def ctx_f(j):
        return jnp.clip(j, 0, ncx - 1)

    def ctx_b(j):
        return jnp.clip(ncx - 1 - j, 0, ncx - 1)

    def lat_specs(fn):
        base = COL_ML // ML_W
        return [pl.BlockSpec((ML_NB, T, ML_W), lambda b, j, o=o: (b, fn(j), base + o)) for o in range(3)] + \
               [pl.BlockSpec((ML_NB, T, LANES), lambda b, j: (b, fn(j), 0))]

    def ctx_specs(fn):
        base = COL_ML // ML_W
        return [pl.BlockSpec((ML_NB, T, ML_W), lambda b, j, o=o: (b, fn(j), base + o)) for o in range(3)] + \
               [pl.BlockSpec((ML_NB, T, LANES), lambda b, j: (b, fn(j), 0))]

    def out_f(b, j):
        return (b, jnp.where(j < ncx, nl + j, j - ncx), 0)

    def out_b(b, j):
        return (b, jnp.where(j < ncx, nl + ncx - 1 - j, nsteps - 1 - j), 0)

    return pl.pallas_call(
        functools.partial(_mlstm_kernel, n_ctx_chunks=ncx),
        grid=(B // ML_NB, nsteps),
        in_specs=lat_specs(lat_f) + lat_specs(lat_b) + ctx_specs(ctx_f) + ctx_specs(ctx_b) +
        [pl.BlockSpec((1, LANES), lambda b, j: (0, 0))],
        out_specs=[pl.BlockSpec((ML_NB, T, ML_W), out_f), pl.BlockSpec((ML_NB, T, ML_W), out_b)],
        out_shape=[jax.ShapeDtypeStruct((B, L + Lc, ML_W), F32)] * 2,
        scratch_shapes=[pltpu.VMEM((ML_NB * 2 * ML_HEADS, ML_HEAD_DIM, 2 * ML_HEAD_DIM), F32),
                        pltpu.VMEM((ML_NB * 2 * ML_HEADS, 1, LANES), F32)],
        compiler_params=_cparams(("parallel", "arbitrary")),
        name="mlstm",
    )(*(([pl_all] * 3 + [gl]) * 2 + ([pc3] * 3 + [gc3]) * 2 + [gate_bias]))


def _short_conv_kernel(x_ref, w_ref, b_ref, o_ref):
    x = x_ref[0].astype(F32)
    L = x.shape[0]
    row = lax.broadcasted_iota(jnp.int32, x.shape, 0)
    prev = jnp.where(row == 0, 0.0, pltpu.roll(x, 1, axis=0))
    nxt = jnp.where(row == L - 1, 0.0, pltpu.roll(x, L - 1, axis=0))
    w = w_ref[...]
    o_ref[0] = b_ref[...] + prev * w[0:1] + x * w[1:2] + nxt * w[2:3]


def _short_conv(p_all, w, b, seq_len):
    Bx, T, _ = p_all.shape
    nseq = T // seq_len
    C = w.shape[1]
    cb = COL_HY // LANES
    return pl.pallas_call(
        _short_conv_kernel,
        grid=(Bx, nseq, C // LANES),
        in_specs=[
            pl.BlockSpec((1, seq_len, LANES), lambda b, s, j: (b, s, cb + j)),
            pl.BlockSpec((3, LANES), lambda b, s, j: (0, j)),
            pl.BlockSpec((1, LANES), lambda b, s, j: (0, j)),
        ],
        out_specs=pl.BlockSpec((1, seq_len, LANES), lambda b, s, j: (b, s, j)),
        out_shape=jax.ShapeDtypeStruct((Bx, T, C), F32),
        compiler_params=_cparams(("parallel", "parallel", "parallel")),
        name="hy_short_conv",
    )(p_all, w, b.reshape(1, C))


@functools.lru_cache(maxsize=None)
def _dft_consts(L):
    N2 = HY_N2
    N = 2 * L
    N1 = N // N2
    S = N1 // 2 + 1
    k1 = np.arange(S)[:, None]
    n2 = np.arange(N2)
    tw = np.exp(-2j * np.pi * k1 * n2[None, :] / N)
    tw_tab = np.stack([np.repeat(tw.real[:, :, None], LANES, 2), np.repeat(tw.imag[:, :, None], LANES, 2)], 1)
    F = np.exp(-2j * np.pi * np.outer(n2, n2) / N2)
    M2 = np.block([[F.real, -F.imag], [F.imag, F.real]])

    def split(m):
        m32 = m.astype(np.float32)
        hi = m32.astype(BF16)
        lo = (m32 - hi.astype(np.float32)).astype(BF16)
        return np.stack([hi, lo])

    def stage1(nb):
        n1 = np.arange(nb)[None, :]
        ang = 2 * np.pi * k1 * n1 / N1
        return np.cos(ang), -np.sin(ang)

    n1o = np.arange(L // N2)[:, None]
    k1o = np.arange(S)[None, :]
    ang = 2 * np.pi * n1o * k1o / N1
    wgt = np.where((k1o == 0) | (k1o == N1 // 2), 1.0, 2.0) / N
    return dict(N=N, N1=N1, S=S, stage1=stage1, tw=tw_tab.astype(np.float32),
                m2=split(M2), m2t=split(M2.T), icr=np.cos(ang) * wgt, ici=-np.sin(ang) * wgt)


def _dft_mm(m_ref, x, passes):
    x_hi = x.astype(BF16)
    acc = jnp.dot(m_ref[0], x_hi, preferred_element_type=F32)
    if passes >= 2:
        x_lo = (x - x_hi.astype(F32)).astype(BF16)
        acc = acc + jnp.dot(m_ref[0], x_lo, preferred_element_type=F32)
    if passes >= 3:
        acc = acc + jnp.dot(m_ref[1], x_hi, preferred_element_type=F32)
    return acc


def _coef_acc(acc, c, x):
    if x is None or abs(c) < 1e-12:
        return acc
    if abs(c - 1.0) < 1e-12:
        return x if acc is None else acc + x
    if abs(c + 1.0) < 1e-12:
        return -x if acc is None else acc - x
    return c * x if acc is None else acc + c * x


def _vadd(a, b):
    return b if a is None else a if b is None else a + b


def _vsub(a, b):
    return (None if b is None else -b) if a is None else a if b is None else a - b


def _slot_groups(N1):
    half = N1 // 2
    return [(k, half - k if half - k != k else None) for k in range(half // 2 + 1)]


def _stage1_group(load, nb, cr, ci, k, kp, tw_ref, a_ref):
    N2, CT = HY_N2, HY_CT
    for r0 in range(0, N2, HY_SUB):
        if kp is None:
            ar = ai = None
            for n1 in range(nb):
                xb = load(n1, r0)
                ar = _coef_acc(ar, cr[k, n1], xb)
                ai = _coef_acc(ai, ci[k, n1], xb)
            slots = [(k, ar, ai)]
        else:
            even = [None, None]
            odd = [None, None]
            for n1 in range(nb):
                xb = load(n1, r0)
                tgt = even if n1 % 2 == 0 else odd
                tgt[0] = _coef_acc(tgt[0], cr[k, n1], xb)
                tgt[1] = _coef_acc(tgt[1], ci[k, n1], xb)
            slots = [(k, _vadd(even[0], odd[0]), _vadd(even[1], odd[1])),
                     (kp, _vsub(even[0], odd[0]), _vsub(odd[1], even[1]))]
        for idx, (kk, ar, ai) in enumerate(slots):
            zero = jnp.zeros((HY_SUB, CT), F32)
            ar = zero if ar is None else ar
            if kk > 0:
                twr = tw_ref[kk, 0, r0:r0 + HY_SUB, :]
                twi = tw_ref[kk, 1, r0:r0 + HY_SUB, :]
                if ai is None:
                    ar, ai = ar * twr, ar * twi
                else:
                    ar, ai = ar * twr - ai * twi, ar * twi + ai * twr
            ai = zero if ai is None else ai
            a_ref[r0:r0 + HY_SUB, idx * CT:(idx + 1) * CT] = ar
            a_ref[N2 + r0:N2 + r0 + HY_SUB, idx * CT:(idx + 1) * CT] = ai


def _stage1_inv_group(b_ref, nb, icr, ici, k, kp, tw_ref, acc_ref, first):
    N2, CT = HY_N2, HY_CT

    def load(idx, kk, r0):
        br = b_ref[r0:r0 + HY_SUB, idx * CT:(idx + 1) * CT]
        bi = b_ref[N2 + r0:N2 + r0 + HY_SUB, idx * CT:(idx + 1) * CT]
        if kk > 0:
            twr = tw_ref[kk, 0, r0:r0 + HY_SUB, :]
            twi = tw_ref[kk, 1, r0:r0 + HY_SUB, :]
            br, bi = br * twr + bi * twi, bi * twr - br * twi
        return br, bi

    for r0 in range(0, N2, HY_SUB):
        br, bi = load(0, k, r0)
        if kp is None:
            q_even = q_odd = (br, bi)
        else:
            br2, bi2 = load(1, kp, r0)
            q_even = (br + br2, bi - bi2)
            q_odd = (br - br2, bi + bi2)
        for n1 in range(nb):
            qr, qi = q_even if n1 % 2 == 0 else q_odd
            contrib = _coef_acc(_coef_acc(None, icr[n1, k], qr), ici[n1, k], qi)
            rows = slice(n1 * N2 + r0, n1 * N2 + r0 + HY_SUB)
            if first:
                acc_ref[rows, :] = contrib
            elif contrib is not None:
                acc_ref[rows, :] += contrib


def _filter_spec_kernel(hf_ref, hb_ref, tw_ref, m2_ref, o_ref, a_ref, *, L):
    c = _dft_consts(L)
    N2, CT = HY_N2, HY_CT
    nb = L // N2
    cr, ci = c["stage1"](nb)
    for k, kp in _slot_groups(c["N1"]):
        w = CT if kp is None else 2 * CT
        specs = []
        for h_ref in (hf_ref, hb_ref):
            _stage1_group(lambda n1, r0: h_ref[n1 * N2 + r0:n1 * N2 + r0 + HY_SUB, :], nb, cr, ci, k, kp, tw_ref, a_ref)
            specs.append(_dft_mm(m2_ref, a_ref[:, :w], 3))
        xf, xb = specs
        g = jnp.concatenate([xf[:N2] + xb[:N2], xf[N2:] - xb[N2:]], axis=0)
        o_ref[k] = g[:, :CT]
        if kp is not None:
            o_ref[kp] = g[:, CT:]


def _filter_spectrum(hf, hb, L):
    c = _dft_consts(L)
    C = hf.shape[1]
    S, R = c["S"], 2 * HY_N2
    return pl.pallas_call(
        functools.partial(_filter_spec_kernel, L=L),
        grid=(C // HY_CT,),
        in_specs=[
            pl.BlockSpec((L, HY_CT), lambda j: (0, j)),
            pl.BlockSpec((L, HY_CT), lambda j: (0, j)),
            pl.BlockSpec((S, 2, HY_N2, LANES), lambda j: (0, 0, 0, 0)),
            pl.BlockSpec((2, R, R), lambda j: (0, 0, 0)),
        ],
        out_specs=pl.BlockSpec((S, R, HY_CT), lambda j: (0, 0, j)),
        out_shape=jax.ShapeDtypeStruct((S, R, C), F32),
        scratch_shapes=[pltpu.VMEM((R, 2 * HY_CT), F32)],
        compiler_params=_cparams(("parallel",)),
        name="hy_filter_spectrum",
    )(hf, hb, jnp.asarray(c["tw"]), jnp.asarray(c["m2"]))


def _long_conv_kernel(u_ref, gate_ref, spec_ref, skip_ref, tw_ref, m2_ref, m2t_ref, o_ref, a_ref, b_ref, acc_ref,
                      *, L):
    c = _dft_consts(L)
    N2, CT = HY_N2, HY_CT
    nb = L // N2
    cr, ci = c["stage1"](nb)
    icr, ici = c["icr"] * c["N"], c["ici"] * c["N"]
    for gi_, (k, kp) in enumerate(_slot_groups(c["N1"])):
        w = CT if kp is None else 2 * CT
        _stage1_group(lambda n1, r0: u_ref[0, n1 * N2 + r0:n1 * N2 + r0 + HY_SUB, :], nb, cr, ci, k, kp, tw_ref, a_ref)
        x = _dft_mm(m2_ref, a_ref[:, :w], HY_PASSES)
        g = spec_ref[k] if kp is None else jnp.concatenate([spec_ref[k], spec_ref[kp]], axis=1)
        xr, xi, gr, gi = x[:N2], x[N2:], g[:N2], g[N2:]
        y = jnp.concatenate([xr * gr - xi * gi, xr * gi + xi * gr], axis=0)
        b_ref[:, :w] = _dft_mm(m2t_ref, y, HY_PASSES)
        _stage1_inv_group(b_ref, nb, icr, ici, k, kp, tw_ref, acc_ref, gi_ == 0)
    o_ref[0] = (gate_ref[0] * (acc_ref[...] * (1.0 / c["N"]) + skip_ref[...] * u_ref[0])).astype(o_ref.dtype)


def _long_conv(u_arr, u_col, gate_arr, gate_col, spec, spec_col, skip, L, out_dtype):
    Bx, T, _ = u_arr.shape
    nseq = T // L
    c = _dft_consts(L)
    C = skip.shape[0]
    S, R = c["S"], 2 * HY_N2
    ub, gb, sb = u_col // HY_CT, gate_col // HY_CT, spec_col // HY_CT
    return pl.pallas_call(
        functools.partial(_long_conv_kernel, L=L),
        grid=(C // HY_CT, Bx, nseq),
        in_specs=[
            pl.BlockSpec((1, L, HY_CT), lambda j, b, s: (b, s, ub + j)),
            pl.BlockSpec((1, L, HY_CT), lambda j, b, s: (b, s, gb + j)),
            pl.BlockSpec((S, R, HY_CT), lambda j, b, s: (0, 0, sb + j)),
            pl.BlockSpec((1, HY_CT), lambda j, b, s: (0, j)),
            pl.BlockSpec((S, 2, HY_N2, LANES), lambda j, b, s: (0, 0, 0, 0)),
            pl.BlockSpec((2, R, R), lambda j, b, s: (0, 0, 0)),
            pl.BlockSpec((2, R, R), lambda j, b, s: (0, 0, 0)),
        ],
        out_specs=pl.BlockSpec((1, L, HY_CT), lambda j, b, s: (b, s, j)),
        out_shape=jax.ShapeDtypeStruct((Bx, T, C), out_dtype),
        scratch_shapes=[pltpu.VMEM((R, 2 * HY_CT), F32), pltpu.VMEM((R, 2 * HY_CT), F32),
                        pltpu.VMEM((L, HY_CT), F32)],
        compiler_params=_cparams(("parallel", "parallel", "parallel")),
        name="hy_long_conv",
    )(u_arr, gate_arr, spec, skip.reshape(1, C), jnp.asarray(c["tw"]), jnp.asarray(c["m2"]), jnp.asarray(c["m2t"]))


def _hyena_filters(L, w1, b1, w2, b2, w3, freq):
    t = jnp.arange(L, dtype=F32)
    t_norm = t / max(L - 1, 1)
    w = 2.0 * math.pi * t / L
    bands = jnp.linspace(1e-4, HY_EMB_BANDS - 1, HY_EMB_BANDS, dtype=F32)
    z = jnp.concatenate([t_norm[:, None], jnp.cos(w[:, None] * bands), -jnp.sin(w[:, None] * bands)], axis=-1)
    hp = lax.Precision.HIGHEST
    h = jnp.sin(freq[0] * (jnp.dot(z, w1, precision=hp) + b1))
    h = jnp.sin(freq[1] * (jnp.dot(h, w2, precision=hp) + b2))
    h = jnp.dot(h, w3, precision=hp).astype(F32).reshape(L, HY_ORDER, 2, HY_WIDTH)
    deltas = jnp.abs(jnp.linspace(math.log(HY_DECAY_TARGET) / HY_DECAY_SLOW,
                                  math.log(HY_DECAY_TARGET) / HY_DECAY_FAST, HY_WIDTH, dtype=F32))
    h = h * jnp.exp(-t_norm[:, None] * deltas)[:, None, None, :]
    return h / jnp.sum(jnp.abs(h), axis=0, keepdims=True)


def _hyena_spectra(L, hy_fp):
    filt = _hyena_filters(L, *hy_fp)
    hf = filt[:, :, 0].reshape(L, HY_ORDER * HY_WIDTH)
    hb = filt[:, :, 1].reshape(L, HY_ORDER * HY_WIDTH)
    return _filter_spectrum(hf, hb, L)


def _hyena(p_all, sw, sb, spec, skip, L):
    u3 = _short_conv(p_all, sw, sb, L)
    W = HY_WIDTH
    z1 = _long_conv(u3, 2 * W, u3, 0, spec, 0, skip[0], L, F32)
    return _long_conv(z1, 0, u3, W, spec, W, skip[1], L, BF16)


def _merge_kernel(att_ref, hf_ref, hb_ref, op_ref, hy_ref, gp0_ref, gp1_ref, gp2_ref, mg_ref, wb_ref, wo_ref,
                  res_ref, g_ref, o_ref):
    d = ML_HEAD_DIM
    h = hf_ref[0] + hb_ref[0]
    parts = []
    for hh in range(ML_HEADS):
        hs = h[:, hh * d:(hh + 1) * d]
        parts.append(hs * lax.rsqrt(jnp.mean(hs * hs, axis=-1, keepdims=True) + EPS))
    mls = jnp.concatenate(parts, axis=1) * mg_ref[...] * _sigmoid(op_ref[0].astype(F32))
    y = _sigmoid(gp0_ref[0].astype(F32)) * jnp.dot(att_ref[0], wb_ref[0], preferred_element_type=F32)
    y = y + _sigmoid(gp1_ref[0].astype(F32)) * jnp.dot(mls.astype(BF16), wb_ref[1], preferred_element_type=F32)
    y = y + _sigmoid(gp2_ref[0].astype(F32)) * jnp.dot(hy_ref[0], wb_ref[2], preferred_element_type=F32)
    o_ref[0] = res_ref[0] + g_ref[0] * jnp.dot(y.astype(BF16), wo_ref[...], preferred_element_type=F32)


def _merge(att, hf, hb, p_all, hy, ml_g, wb, wo, res, g, tm):
    Bx, T, D = res.shape
    W = ML_W
    row = lambda b, i: (b, i, 0)
    return pl.pallas_call(
        _merge_kernel,
        grid=(Bx, T // tm),
        in_specs=[
            pl.BlockSpec((1, tm, W), row),
            pl.BlockSpec((1, tm, W), row),
            pl.BlockSpec((1, tm, W), row),
            pl.BlockSpec((1, tm, W), lambda b, i: (b, i, COL_ML // W + 3)),
            pl.BlockSpec((1, tm, W), row),
            pl.BlockSpec((1, tm, D), lambda b, i: (b, i, COL_BG // D)),
            pl.BlockSpec((1, tm, D), lambda b, i: (b, i, COL_BG // D + 1)),
            pl.BlockSpec((1, tm, D), lambda b, i: (b, i, COL_BG // D + 2)),
            pl.BlockSpec((1, W), lambda b, i: (0, 0)),
            pl.BlockSpec((N_BRANCH, W, D), lambda b, i: (0, 0, 0)),
            pl.BlockSpec((D, D), lambda b, i: (0, 0)),
            pl.BlockSpec((1, tm, D), row),
            pl.BlockSpec((1, 1, D), lambda b, i: (b, 0, 0)),
        ],
        out_specs=pl.BlockSpec((1, tm, D), row),
        out_shape=jax.ShapeDtypeStruct((Bx, T, D), F32),
        compiler_params=_cparams(("parallel", "parallel")),
        name="merge",
    )(att, hf, hb, p_all, hy, p_all, p_all, p_all, ml_g.reshape(1, W), wb, wo, res, g)


FFN_TC = 1408
FFN_HALO = 8


def _ffn_up_kernel(x_ref, xp_ref, xn_ref, gs_ref, sh_ref, wg_ref, wv_ref, cwg_ref, cwv_ref, cbg_ref, cbv_ref,
                   o_ref, h_ref, hh_ref, *, tiles_per_seq):
    i = pl.program_id(1)
    tm = x_ref.shape[1]

    @pl.when(pl.program_id(2) == 0)
    def _():
        h_ref[...] = _rms_mod(x_ref[0], gs_ref[0], sh_ref[0])
        halo = jnp.concatenate([xp_ref[0], xn_ref[0]], axis=0)
        hh_ref[...] = _rms_mod(halo, gs_ref[0], sh_ref[0])

    first = (i % tiles_per_seq) == 0
    last = (i % tiles_per_seq) == tiles_per_seq - 1
    row8 = lax.broadcasted_iota(jnp.int32, (8, FFN_TC), 0)

    def conv_half(w_ref, cw_ref, cb_ref):
        u = jnp.dot(h_ref[...], w_ref[...], preferred_element_type=F32)
        uh = jnp.dot(hh_ref[...], w_ref[...], preferred_element_type=F32)
        pr = jnp.where(first, 0.0, uh[FFN_HALO - 1:FFN_HALO])
        nx = jnp.where(last, 0.0, uh[FFN_HALO:FFN_HALO + 1])
        prev = pltpu.roll(u, 1, axis=0)
        nxt = pltpu.roll(u, tm - 1, axis=0)
        prev = jnp.concatenate([jnp.where(row8 == 0, pr, prev[:8]), prev[8:]], axis=0)
        nxt = jnp.concatenate([nxt[:tm - 8], jnp.where(row8 == 7, nx, nxt[tm - 8:])], axis=0)
        cw = cw_ref[...]
        return cb_ref[...] + prev * cw[0:1] + u * cw[1:2] + nxt * cw[2:3]

    gate = conv_half(wg_ref, cwg_ref, cbg_ref)
    val = conv_half(wv_ref, cwv_ref, cbv_ref)
    o_ref[0] = (gate * _sigmoid(gate) * val).astype(o_ref.dtype)


def _ffn_up(x, gs, sh, wu, cw, cb, seq_len):
    Bx, T, D = x.shape
    tm = min(seq_len, 512)
    nj = D_FF // FFN_TC
    hb = tm // FFN_HALO
    nh = T // FFN_HALO
    col = lambda b, i, j: (0, j)
    col2 = lambda b, i, j: (0, nj + j)
    return pl.pallas_call(
        functools.partial(_ffn_up_kernel, tiles_per_seq=seq_len // tm),
        grid=(Bx, T // tm, nj),
        in_specs=[
            pl.BlockSpec((1, tm, D), lambda b, i, j: (b, i, 0)),
            pl.BlockSpec((1, FFN_HALO, D), lambda b, i, j: (b, jnp.maximum(i * hb - 1, 0), 0)),
            pl.BlockSpec((1, FFN_HALO, D), lambda b, i, j: (b, jnp.minimum((i + 1) * hb, nh - 1), 0)),
            pl.BlockSpec((1, 1, D), lambda b, i, j: (b, 0, 0)),
            pl.BlockSpec((1, 1, D), lambda b, i, j: (b, 0, 0)),
            pl.BlockSpec((D, FFN_TC), col), pl.BlockSpec((D, FFN_TC), col2),
            pl.BlockSpec((3, FFN_TC), col), pl.BlockSpec((3, FFN_TC), col2),
            pl.BlockSpec((1, FFN_TC), col), pl.BlockSpec((1, FFN_TC), col2),
        ],
        out_specs=pl.BlockSpec((1, tm, FFN_TC), lambda b, i, j: (b, i, j)),
        out_shape=jax.ShapeDtypeStruct((Bx, T, D_FF), BF16),
        scratch_shapes=[pltpu.VMEM((tm, D), BF16), pltpu.VMEM((2 * FFN_HALO, D), BF16)],
        compiler_params=_cparams(("parallel", "parallel", "arbitrary")),
        name="ffn_up",
    )(x, x, x, gs, sh, wu, wu, cw, cw, cb.reshape(1, -1), cb.reshape(1, -1))


def _proj_res_kernel(a_ref, w_ref, res_ref, g_ref, o_ref):
    o_ref[0] = res_ref[0] + g_ref[0] * jnp.dot(a_ref[0], w_ref[...], preferred_element_type=F32)


def _proj_res(a, w, res, g):
    Bx, T, D = res.shape
    K = a.shape[2]
    tm = min(T, 512)
    return pl.pallas_call(
        _proj_res_kernel,
        grid=(Bx, T // tm),
        in_specs=[
            pl.BlockSpec((1, tm, K), lambda b, i: (b, i, 0)),
            pl.BlockSpec((K, D), lambda b, i: (0, 0)),
            pl.BlockSpec((1, tm, D), lambda b, i: (b, i, 0)),
            pl.BlockSpec((1, 1, D), lambda b, i: (b, 0, 0)),
        ],
        out_specs=pl.BlockSpec((1, tm, D), lambda b, i: (b, i, 0)),
        out_shape=jax.ShapeDtypeStruct((Bx, T, D), F32),
        compiler_params=_cparams(("parallel", "parallel")),
        name="ffn_down",
    )(a, w, res, g)


def _pair_perm():
    hd, half = ATT_HEAD_DIM, ATT_HEAD_DIM // 2
    n_pair = ATT_Q // LANES
    qperm = []
    for p in range(n_pair):
        for sub in range(4):
            head = p if sub % 2 == 0 else n_pair + p
            qperm += [head * hd + (sub // 2) * half + dd for dd in range(half)]
    kperm = []
    for sub in range(4):
        kperm += [(sub % 2) * hd + (sub // 2) * half + dd for dd in range(half)]
    return np.asarray(qperm), np.asarray(kperm)


def _rope_tables(L):
    rows = L // GRID_W
    row = jnp.repeat(jnp.arange(rows, dtype=F32), GRID_W)
    col = jnp.tile(jnp.arange(GRID_W, dtype=F32), rows)
    nf = ATT_HEAD_DIM // 4
    inv = ROPE_THETA ** (-jnp.arange(nf, dtype=F32) / nf)
    ang = jnp.concatenate([row[:, None] * inv, col[:, None] * inv], axis=-1)
    cos, sin = jnp.cos(ang), jnp.sin(ang)
    return jnp.concatenate([cos, cos, cos, cos, -sin, -sin, sin, sin], axis=1)


def _prep_w_in(w):
    qperm, kperm = _pair_perm()
    o = IN_OFFSETS
    cols = [w[:, :ATT_Q][:, qperm], w[:, o[2]:o[6]], w[:, o[7]:o[8]], w[:, o[8]:],
            w[:, o[0]:o[1]][:, kperm], w[:, o[1]:o[2]][:, kperm]]
    wc = jnp.concatenate(cols, axis=1)
    w_gate = jnp.pad(w[:, o[6]:o[7]], ((0, 0), (0, LANES - ML_GATES))).astype(BF16)
    return jnp.pad(wc, ((0, 0), (0, IN_PAD - wc.shape[1]))).astype(BF16), w_gate


def _sink_cols(sink, rows_per_pair):
    n_pair = ATT_Q // LANES
    lo = jnp.repeat(sink[:n_pair], rows_per_pair)
    hi = jnp.repeat(sink[n_pair:], rows_per_pair)
    return jnp.stack([lo, hi])[:, :, None]


def kernel(x, c, ctx, c_ctx, ada_w, ada_b, norm1_g, norm2_g, w_in, att_sink, ml_gate_b, ml_norm_g,
           hy_short_w, hy_short_b, hy_w1, hy_b1, hy_w2, hy_b2, hy_w3, hy_freq, hy_skip,
           w_branch, w_out, w_up, ffn_conv_w, ffn_conv_b, w_down, final_g):
    B, L, D = x.shape
    Lc = ctx.shape[1]
    qperm, _ = _pair_perm()
    tabs = _rope_tables(L)
    sc_rows = jnp.concatenate([jax.nn.silu(c), jax.nn.silu(c_ctx)[None], jnp.zeros((8 - B - 1, D), F32)], axis=0)
    xl = x
    xc = ctx.reshape(1, B * Lc, D)
    for l in range(DEPTH):
        need_ctx = l < DEPTH - 1
        mod = _ada_proj(sc_rows, ada_w[l], ada_b[l])
        sh1, sc1, g1, sh2, sc2, g2 = [m[:B, None, :] for m in jnp.split(mod, 6, axis=-1)]
        csh1, csc1, cg1, csh2, csc2, cg2 = [m[B:B + 1, None, :] for m in jnp.split(mod, 6, axis=-1)]
        n1 = norm1_g[l][None, None, :]
        n2 = norm2_g[l][None, None, :]
        w_in_b, w_gate = _prep_w_in(w_in[l])
        wb = jnp.concatenate([w_branch[l][:1][:, qperm], w_branch[l][1:]], axis=0).astype(BF16)
        wo = w_out[l].astype(BF16)
        wu = w_up[l].astype(BF16)
        wd = w_down[l].astype(BF16)
        gate_bias = jnp.pad(ml_gate_b[l].reshape(1, ML_GATES), ((0, 0), (0, LANES - ML_GATES)))
        hy_fp = (hy_w1[l], hy_b1[l], hy_w2[l], hy_b2[l], hy_w3[l], hy_freq[l])

        pl_all, gl = _norm_proj(xl, n1 * (1.0 + sc1), sh1, w_in_b, IN_TN, w_gate)
        pc_all, gc = _norm_proj(xc, n1 * (1.0 + csc1), csh1, w_in_b, IN_TN, w_gate)
        att_l = _win_attention(pl_all, pc_all, tabs, _sink_cols(att_sink[l], ATT_BLOCK))
        hf, hb = _mlstm(pl_all, pc_all, gl, gc, gate_bias)
        hy_l = _hyena(pl_all, hy_short_w[l], hy_short_b[l], _hyena_spectra(L, hy_fp), hy_skip[l], L)
        xl = _merge(att_l, hf, hb, pl_all, hy_l, ml_norm_g[l], wb, wo, xl, g1, 512)
        act = _ffn_up(xl, n2 * (1.0 + sc2), sh2, wu, ffn_conv_w[l], ffn_conv_b[l], L)
        xl = _proj_res(act, wd, xl, g2)
        if need_ctx:
            att_c = _ctx_attention(pc_all, _sink_cols(att_sink[l], Lc), B)
            hy_c = _hyena(pc_all, hy_short_w[l], hy_short_b[l], _hyena_spectra(Lc, hy_fp), hy_skip[l], Lc)
            hfc = hf[:, L:].reshape(1, B * Lc, ML_W)
            hbc = hb[:, L:].reshape(1, B * Lc, ML_W)
            xc = _merge(att_c, hfc, hbc, pc_all, hy_c, ml_norm_g[l], wb, wo, xc, cg1, Lc)
            act_c = _ffn_up(xc, n2 * (1.0 + csc2), csh2, wu, ffn_conv_w[l], ffn_conv_b[l], Lc)
            xc = _proj_res(act_c, wd, xc, cg2)
    return _final_norm(xl, final_g)
```

```python
import functools
import math

import numpy as np
import jax
import jax.numpy as jnp
from jax import lax
from jax.experimental import pallas as pl
from jax.experimental.pallas import tpu as pltpu

F32 = jnp.float32
BF16 = jnp.bfloat16

D_MODEL = 1024
DEPTH = 4
GRID_W = 64
EPS = 1e-6
ATT_HEADS = 8
ATT_KV_HEADS = 2
ATT_HEAD_DIM = 64
ATT_BLOCK = 128
ROPE_THETA = 10000.0
ML_HEADS = 4
ML_HEAD_DIM = 128
ML_CHUNK = 128
ML_NB = 2
HY_WIDTH = 512
HY_ORDER = 2
HY_EMB_BANDS = 16
HY_DECAY_TARGET = 1e-2
HY_DECAY_FAST = 0.3
HY_DECAY_SLOW = 1.5
D_FF = 2816
N_BRANCH = 3
LANES = 128

ATT_Q = ATT_HEADS * ATT_HEAD_DIM
ATT_KV = ATT_KV_HEADS * ATT_HEAD_DIM
ML_W = ML_HEADS * ML_HEAD_DIM
ML_GATES = 2 * 2 * ML_HEADS
IN_SIZES = (ATT_Q, ATT_KV, ATT_KV, ML_W, ML_W, ML_W, ML_W, ML_GATES, 3 * HY_WIDTH, N_BRANCH * D_MODEL)
IN_OFFSETS = tuple(int(o) for o in np.cumsum(IN_SIZES)[:-1])

COL_Q = 0
COL_ML = COL_Q + ATT_Q
COL_HY = COL_ML + 4 * ML_W
COL_BG = COL_HY + 3 * HY_WIDTH
COL_K = COL_BG + N_BRANCH * D_MODEL
COL_V = COL_K + ATT_KV
IN_PAD = 7680
IN_TN = 1536
IN_TM = 2048

HY_N2 = 256
HY_CT = 128
HY_SUB = 32
HY_PASSES = 1

VMEM_LIMIT = 56 * 1024 * 1024


def _cparams(sem):
    return pltpu.CompilerParams(dimension_semantics=sem, vmem_limit_bytes=VMEM_LIMIT)


def _sigmoid(x):
    return 0.5 * jnp.tanh(0.5 * x) + 0.5


def _log_sigmoid(x):
    return jnp.minimum(x, 0.0) - jnp.log(1.0 + jnp.exp(-jnp.abs(x)))


def _rms_mod(x, gs, sh):
    ms = jnp.mean(x * x, axis=-1, keepdims=True)
    return (x * lax.rsqrt(ms + EPS) * gs + sh).astype(BF16)


def _norm_proj_kernel(x_ref, gs_ref, sh_ref, w_ref, *rest, with_aux):
    if with_aux:
        wa_ref, o_ref, oa_ref, h_ref = rest
    else:
        o_ref, h_ref = rest

    @pl.when(pl.program_id(2) == 0)
    def _():
        h = _rms_mod(x_ref[0], gs_ref[0], sh_ref[0])
        h_ref[...] = h
        if with_aux:
            oa_ref[0] = jnp.dot(h, wa_ref[...], preferred_element_type=F32)

    o_ref[0] = jnp.dot(h_ref[...], w_ref[...], preferred_element_type=F32).astype(o_ref.dtype)


def _norm_proj(x, gs, sh, w, tn, w_aux=None):
    B, T, D = x.shape
    N = w.shape[1]
    tm = min(T, IN_TM)
    in_specs = [
        pl.BlockSpec((1, tm, D), lambda b, i, j: (b, i, 0)),
        pl.BlockSpec((1, 1, D), lambda b, i, j: (b, 0, 0)),
        pl.BlockSpec((1, 1, D), lambda b, i, j: (b, 0, 0)),
        pl.BlockSpec((D, tn), lambda b, i, j: (0, j)),
    ]
    out_specs = [pl.BlockSpec((1, tm, tn), lambda b, i, j: (b, i, j))]
    out_shape = [jax.ShapeDtypeStruct((B, T, N), BF16)]
    args = [x, gs, sh, w]
    if w_aux is not None:
        na = w_aux.shape[1]
        in_specs.append(pl.BlockSpec((D, na), lambda b, i, j: (0, 0)))
        out_specs.append(pl.BlockSpec((1, tm, na), lambda b, i, j: (b, i, 0)))
        out_shape.append(jax.ShapeDtypeStruct((B, T, na), F32))
        args.append(w_aux)
    outs = pl.pallas_call(
        functools.partial(_norm_proj_kernel, with_aux=w_aux is not None),
        grid=(B, T // tm, N // tn),
        in_specs=in_specs,
        out_specs=out_specs,
        out_shape=out_shape,
        scratch_shapes=[pltpu.VMEM((tm, D), BF16)],
        compiler_params=_cparams(("parallel", "parallel", "arbitrary")),
        name="norm_proj",
    )(*args)
    return outs if w_aux is not None else outs[0]


def _small_proj_kernel(a_ref, w_ref, b_ref, o_ref):
    a = a_ref[...]
    a_hi = a.astype(BF16)
    a_lo = (a - a_hi.astype(F32)).astype(BF16)
    w = w_ref[...]
    w_hi = w.astype(BF16)
    w_lo = (w - w_hi.astype(F32)).astype(BF16)
    acc = jnp.dot(a_hi, w_hi, preferred_element_type=F32)
    acc = acc + (jnp.dot(a_hi, w_lo, preferred_element_type=F32) + jnp.dot(a_lo, w_hi, preferred_element_type=F32))
    o_ref[...] = acc + b_ref[...]


def _ada_proj(a, w, b):
    M, D = a.shape
    N = w.shape[1]
    tn = 1024
    return pl.pallas_call(
        _small_proj_kernel,
        grid=(N // tn,),
        in_specs=[
            pl.BlockSpec((M, D), lambda j: (0, 0)),
            pl.BlockSpec((D, tn), lambda j: (0, j)),
            pl.BlockSpec((1, tn), lambda j: (0, j)),
        ],
        out_specs=pl.BlockSpec((M, tn), lambda j: (0, j)),
        out_shape=jax.ShapeDtypeStruct((M, N), F32),
        compiler_params=_cparams(("parallel",)),
        name="ada_proj",
    )(a, w, b.reshape(1, N))


def _final_norm_kernel(x_ref, g_ref, o_ref):
    x = x_ref[0]
    ms = jnp.mean(x * x, axis=-1, keepdims=True)
    o_ref[0] = x * lax.rsqrt(ms + EPS) * g_ref[...]


def _final_norm(x, g):
    B, T, D = x.shape
    tm = 512
    return pl.pallas_call(
        _final_norm_kernel,
        grid=(B, T // tm),
        in_specs=[
            pl.BlockSpec((1, tm, D), lambda b, i: (b, i, 0)),
            pl.BlockSpec((1, D), lambda b, i: (0, 0)),
        ],
        out_specs=pl.BlockSpec((1, tm, D), lambda b, i: (b, i, 0)),
        out_shape=jax.ShapeDtypeStruct((B, T, D), F32),
        compiler_params=_cparams(("parallel", "parallel")),
        name="final_norm",
    )(x, g.reshape(1, D))


def _lane_lo_mask(shape):
    lane = lax.broadcasted_iota(jnp.int32, shape, len(shape) - 1)
    return (lane % 64) < 32


def _rope(x, tab):
    c = tab[:, :LANES]
    s = tab[:, LANES:]
    outs = []
    for g in range(x.shape[1] // LANES):
        xg = x[:, g * LANES:(g + 1) * LANES]
        outs.append(xg * c + pltpu.roll(xg, 64, axis=1) * s)
    return outs[0] if len(outs) == 1 else jnp.concatenate(outs, axis=1)


def _attend(qs, kcat, vcat, bias, sink_lo, sink_hi):
    lo = _lane_lo_mask(kcat.shape)
    T = ATT_BLOCK
    outs = []
    for msk, sink in ((lo, sink_lo), (jnp.logical_not(lo), sink_hi)):
        kh = jnp.where(msk, kcat, 0.0).astype(BF16)
        s = lax.dot_general(qs, kh, (((1,), (1,)), ((), ())), preferred_element_type=F32)
        if bias is not None:
            c0, bias_prev, bias_next = bias
            s = jnp.concatenate([s[:, :c0], s[:, c0:c0 + T] + bias_prev, s[:, c0 + T:c0 + 2 * T],
                                 s[:, c0 + 2 * T:] + bias_next], axis=1)
        m = jnp.maximum(jnp.max(s, axis=-1, keepdims=True), sink)
        p = jnp.exp(s - m)
        den = jnp.sum(p, axis=-1, keepdims=True) + jnp.exp(sink - m)
        outs.append(jnp.dot(p.astype(BF16), vcat, preferred_element_type=F32) * (1.0 / den))
    return jnp.where(_lane_lo_mask(outs[0].shape), outs[0], outs[1])


ATT_QB = 2


def _win_attn_kernel(*refs):
    nkb = ATT_QB + 2
    q_ref = refs[0]
    k_refs = refs[1:1 + nkb]
    v_refs = refs[1 + nkb:1 + 2 * nkb]
    kx_ref, vx_ref = refs[1 + 2 * nkb:3 + 2 * nkb]
    t_refs = refs[3 + 2 * nkb:3 + 3 * nkb]
    sink_ref, o_ref = refs[3 + 3 * nkb:]
    i = pl.program_id(1)
    nb = pl.num_programs(1) * ATT_QB
    T = ATT_BLOCK
    n_pair = ATT_Q // LANES
    Lc = kx_ref.shape[1]
    kx = kx_ref[0].astype(F32)
    k_rot = [_rope(k_refs[j][0].astype(F32), t_refs[j][...]) for j in range(nkb)]
    t = lax.broadcasted_iota(jnp.int32, (T, T), 0)
    s = lax.broadcasted_iota(jnp.int32, (T, T), 1)
    for sb in range(ATT_QB):
        blk = i * ATT_QB + sb
        q = _rope(q_ref[0, sb * T:(sb + 1) * T, :].astype(F32), t_refs[sb + 1][...]) * (ATT_HEAD_DIM ** -0.5)
        qs = jnp.concatenate([q[:, g * LANES:(g + 1) * LANES] for g in range(n_pair)], axis=0).astype(BF16)
        kcat = jnp.concatenate([kx, k_rot[sb], k_rot[sb + 1], k_rot[sb + 2]], axis=0)
        vcat = jnp.concatenate([vx_ref[0], v_refs[sb][0], v_refs[sb + 1][0], v_refs[sb + 2][0]], axis=0)
        bias_prev = jnp.where((s >= t) & (blk > 0), 0.0, -1e30).astype(F32)
        bias_next = jnp.where((s <= t) & (blk < nb - 1), 0.0, -1e30).astype(F32)
        bias = (Lc, jnp.concatenate([bias_prev] * n_pair, axis=0), jnp.concatenate([bias_next] * n_pair, axis=0))
        o = _attend(qs, kcat, vcat, bias, sink_ref[0], sink_ref[1])
        o_ref[0, sb * T:(sb + 1) * T, :] = jnp.concatenate(
            [o[g * T:(g + 1) * T] for g in range(n_pair)], axis=1).astype(o_ref.dtype)


def _win_attention(pl_all, pc_all, tabs, sinks):
    B, L, _ = pl_all.shape
    Lc = pc_all.shape[1] // B
    T = ATT_BLOCK
    nb = L // T
    kb, vb = COL_K // LANES, COL_V // LANES
    offs = range(-1, ATT_QB + 1)

    def blk(col, d):
        return pl.BlockSpec((1, T, LANES), lambda b, i: (b, jnp.clip(i * ATT_QB + d, 0, nb - 1), col))

    def tab(d):
        return pl.BlockSpec((T, 2 * LANES), lambda b, i: (jnp.clip(i * ATT_QB + d, 0, nb - 1), 0))

    nkb = ATT_QB + 2
    return pl.pallas_call(
        _win_attn_kernel,
        grid=(B, nb // ATT_QB),
        in_specs=[pl.BlockSpec((1, ATT_QB * T, ATT_Q), lambda b, i: (b, i, 0))] +
        [blk(kb, d) for d in offs] + [blk(vb, d) for d in offs] +
        [pl.BlockSpec((1, Lc, LANES), lambda b, i: (0, b, kb)),
         pl.BlockSpec((1, Lc, LANES), lambda b, i: (0, b, vb))] +
        [tab(d) for d in offs] +
        [pl.BlockSpec((2, ATT_Q, 1), lambda b, i: (0, 0, 0))],
        out_specs=pl.BlockSpec((1, ATT_QB * T, ATT_Q), lambda b, i: (b, i, 0)),
        out_shape=jax.ShapeDtypeStruct((B, L, ATT_Q), BF16),
        compiler_params=_cparams(("parallel", "parallel")),
        name="win_attention",
    )(*([pl_all] * (1 + 2 * nkb) + [pc_all] * 2 + [tabs] * nkb + [sinks]))


def _ctx_attn_kernel(q_ref, kx_ref, vx_ref, sink_ref, o_ref):
    Lc = q_ref.shape[1]
    n_pair = ATT_Q // LANES
    q = q_ref[0].astype(F32) * (ATT_HEAD_DIM ** -0.5)
    qs = jnp.concatenate([q[:, g * LANES:(g + 1) * LANES] for g in range(n_pair)], axis=0).astype(BF16)
    o = _attend(qs, kx_ref[0].astype(F32), vx_ref[0], None, sink_ref[0], sink_ref[1])
    o_ref[0] = jnp.concatenate([o[g * Lc:(g + 1) * Lc] for g in range(n_pair)], axis=1).astype(o_ref.dtype)


def _ctx_attention(pc_all, sinks, B):
    Lc = pc_all.shape[1] // B
    kb, vb = COL_K // LANES, COL_V // LANES
    return pl.pallas_call(
        _ctx_attn_kernel,
        grid=(B,),
        in_specs=[
            pl.BlockSpec((1, Lc, ATT_Q), lambda b: (0, b, 0)),
            pl.BlockSpec((1, Lc, LANES), lambda b: (0, b, kb)),
            pl.BlockSpec((1, Lc, LANES), lambda b: (0, b, vb)),
            pl.BlockSpec((2, ATT_Q // LANES * Lc, 1), lambda b: (0, 0, 0)),
        ],
        out_specs=pl.BlockSpec((1, Lc, ATT_Q), lambda b: (0, b, 0)),
        out_shape=jax.ShapeDtypeStruct((1, B * Lc, ATT_Q), BF16),
        compiler_params=_cparams(("parallel",)),
        name="ctx_attention",
    )(pc_all, pc_all, pc_all, sinks)


def _dot_hl(a_exact, x):
    x_hi = x.astype(BF16)
    x_lo = (x - x_hi.astype(F32)).astype(BF16)
    return jnp.dot(a_exact, x_hi, preferred_element_type=F32) + jnp.dot(a_exact, x_lo, preferred_element_type=F32)


def _mlstm_kernel(qf_l, kf_l, vf_l, gf_l, qb_l, kb_l, vb_l, gb_l,
                  qf_c, kf_c, vf_c, gf_c, qb_c, kb_c, vb_c, gb_c, bias_ref,
                  hf_ref, hb_ref, ct_ref, m_ref, *, n_ctx_chunks):
    j = pl.program_id(1)
    T = ML_CHUNK
    d = ML_HEAD_DIM
    is_ctx = j < n_ctx_chunks

    @pl.when(j == 0)
    def _():
        ct_ref[...] = jnp.zeros_like(ct_ref)
        m_ref[...] = jnp.zeros_like(m_ref)

    row = lax.broadcasted_iota(jnp.int32, (T, T), 0)
    col = lax.broadcasted_iota(jnp.int32, (T, T), 1)
    ones_td = jnp.ones((T, d), BF16)

    nt = (((1,), (1,)), ((), ()))
    sel_row = lax.broadcasted_iota(jnp.int32, (LANES, 2 * ML_HEADS * LANES), 0)
    sel_blk = lax.broadcasted_iota(jnp.int32, (LANES, 2 * ML_HEADS * LANES), 1) // LANES
    chains = []
    for bi in range(ML_NB):
        for di, (q_l, k_l, v_l, g_l, q_c, k_c, v_c, g_c) in enumerate(
                ((qf_l, kf_l, vf_l, gf_l, qf_c, kf_c, vf_c, gf_c),
                 (qb_l, kb_l, vb_l, gb_l, qb_c, kb_c, vb_c, gb_c))):
            keep = (col <= row) if di == 0 else (col >= row)
            keep_b = keep.astype(BF16)
            g = jnp.where(is_ctx, g_c[bi], g_l[bi]) + bias_ref[...]
            lane = lax.broadcasted_iota(jnp.int32, g.shape, 1)
            is_f = (lane % 8) >= 4
            gv = jnp.where(is_f, _log_sigmoid(g), g)
            gt = gv.T
            cum_c = _dot_hl(keep_b, gv)
            gt_hi = gt.astype(BF16)
            gt_lo = (gt - gt_hi.astype(F32)).astype(BF16)
            cum_r = lax.dot_general(gt_hi, keep_b, nt, preferred_element_type=F32) + \
                lax.dot_general(gt_lo, keep_b, nt, preferred_element_type=F32)
            src_col = jnp.where(sel_blk < ML_HEADS, di * 8 + ML_HEADS + sel_blk, di * 8 + sel_blk - ML_HEADS)
            sel = jnp.where(sel_row == src_col, 1.0, 0.0).astype(BF16)
            x = jnp.where(is_f, cum_c, gv)
            x_hi = x.astype(BF16)
            x_lo = (x - x_hi.astype(F32)).astype(BF16)
            full = jnp.dot(x_hi, sel, preferred_element_type=F32) + jnp.dot(x_lo, sel, preferred_element_type=F32)
            q_all = jnp.where(is_ctx, q_c[bi], q_l[bi])
            k_all = jnp.where(is_ctx, k_c[bi], k_l[bi]).astype(F32) * (d ** -0.5)
            v_all = jnp.where(is_ctx, v_c[bi], v_l[bi])
            for h in range(ML_HEADS):
                ci = di * 8 + h
                cf = di * 8 + 4 + h
                sl = slice(h * d, (h + 1) * d)
                b_full = full[:, h * LANES:(h + 1) * LANES]
                chains.append(dict(
                    idx=(bi * 2 + di) * ML_HEADS + h, keep=keep, q=q_all[:, sl], k=k_all[:, sl], v=v_all[:, sl],
                    b_full=b_full, i_full=full[:, (ML_HEADS + h) * LANES:(ML_HEADS + h + 1) * LANES],
                    b_r=cum_r[cf:cf + 1, :], i_r=gt[ci:ci + 1, :],
                    bl=b_full[T - 1:T, :] if di == 0 else b_full[0:1, :]))
    for c in chains:
        c["m_old"] = m_ref[c["idx"]]
        c["a"] = c["b_full"] + c["m_old"]
        c["dmat"] = jnp.where(c["keep"], c["b_full"] - c["b_r"] + c["i_r"], -1e30)
    for c in chains:
        c["mt"] = jnp.maximum(c["a"], jnp.broadcast_to(jnp.max(c["dmat"], axis=-1, keepdims=True), (T, T)))
    for c in chains:
        c["qk"] = lax.dot_general(c["q"], c["k"].astype(BF16), nt, preferred_element_type=F32)
    for c in chains:
        s = (c["qk"] * jnp.exp(c["dmat"] - c["mt"])).astype(BF16)
        wq = (c["q"].astype(F32) * jnp.exp(c["a"] - c["mt"])).astype(BF16)
        c["v_aug"] = jnp.concatenate([c["v"], ones_td], axis=1)
        c["ct"] = ct_ref[c["idx"]]
        lhs = jnp.concatenate([wq, s], axis=1)
        rhs = jnp.concatenate([c["ct"].astype(BF16), c["v_aug"]], axis=0)
        c["r"] = jnp.dot(lhs, rhs, preferred_element_type=F32)
    for c in chains:
        r = c["r"]
        c["h"] = r[:, :d] / jnp.maximum(jnp.abs(r[:, d:]), jnp.exp(-c["mt"]))
    for c in chains:
        src = c["bl"] - c["b_full"] + c["i_full"]
        m_new = jnp.maximum(c["bl"] + c["m_old"], jnp.max(src, axis=0, keepdims=True))
        gk = (jnp.exp(src - m_new) * c["k"]).T.astype(BF16)
        decay = jnp.exp(c["bl"] + c["m_old"] - m_new)
        ct_ref[c["idx"]] = jnp.concatenate([decay, decay], axis=1) * c["ct"] + \
            jnp.dot(gk, c["v_aug"], preferred_element_type=F32)
        m_ref[c["idx"]] = m_new
    for bi in range(ML_NB):
        for di, h_ref in enumerate((hf_ref, hb_ref)):
            base = (bi * 2 + di) * ML_HEADS
            h_ref[bi] = jnp.concatenate([chains[base + h]["h"] for h in range(ML_HEADS)], axis=1)


def _mlstm(pl_all, pc_all, gl, gc, gate_bias):
    B, L, _ = pl_all.shape
    Lc = pc_all.shape[1] // B
    T = ML_CHUNK
    nl, ncx = L // T, Lc // T
    nsteps = nl + ncx
    pc3 = pc_all.reshape(B, Lc, pc_all.shape[2])
    gc3 = gc.reshape(B, Lc, gc.shape[2])

    def lat_f(j):
        return jnp.clip(j - ncx, 0, nl - 1)

    def lat_b(j):
        return jnp.clip(nsteps - 1 - j, 0, nl - 1)

    def ctx_f(j):
        return jnp.clip(j, 0, ncx - 1)

    def ctx_b(j):
        return jnp.clip(ncx - 1 - j, 0, ncx - 1)

    def lat_specs(fn):
        base = COL_ML // ML_W
        return [pl.BlockSpec((ML_NB, T, ML_W), lambda b, j, o=o: (b, fn(j), base + o)) for o in range(3)] + \
               [pl.BlockSpec((ML_NB, T, LANES), lambda b, j: (b, fn(j), 0))]

    def ctx_specs(fn):
        base = COL_ML // ML_W
        return [pl.BlockSpec((ML_NB, T, ML_W), lambda b, j, o=o: (b, fn(j), base + o)) for o in range(3)] + \
               [pl.BlockSpec((ML_NB, T, LANES), lambda b, j: (b, fn(j), 0))]

    def out_f(b, j):
        return (b, jnp.where(j < ncx, nl + j, j - ncx), 0)

    def out_b(b, j):
        return (b, jnp.where(j < ncx, nl + ncx - 1 - j, nsteps - 1 - j), 0)

    return pl.pallas_call(
        functools.partial(_mlstm_kernel, n_ctx_chunks=ncx),
        grid=(B // ML_NB, nsteps),
        in_specs=lat_specs(lat_f) + lat_specs(lat_b) + ctx_specs(ctx_f) + ctx_specs(ctx_b) +
        [pl.BlockSpec((1, LANES), lambda b, j: (0, 0))],
        out_specs=[pl.BlockSpec((ML_NB, T, ML_W), out_f), pl.BlockSpec((ML_NB, T, ML_W), out_b)],
        out_shape=[jax.ShapeDtypeStruct((B, L + Lc, ML_W), F32)] * 2,
        scratch_shapes=[pltpu.VMEM((ML_NB * 2 * ML_HEADS, ML_HEAD_DIM, 2 * ML_HEAD_DIM), F32),
                        pltpu.VMEM((ML_NB * 2 * ML_HEADS, 1, LANES), F32)],
        compiler_params=_cparams(("parallel", "arbitrary")),
        name="mlstm",
    )(*(([pl_all] * 3 + [gl]) * 2 + ([pc3] * 3 + [gc3]) * 2 + [gate_bias]))


def _short_conv_kernel(x_ref, w_ref, b_ref, o_ref):
    x = x_ref[0].astype(F32)
    L = x.shape[0]
    row = lax.broadcasted_iota(jnp.int32, x.shape, 0)
    prev = jnp.where(row == 0, 0.0, pltpu.roll(x, 1, axis=0))
    nxt = jnp.where(row == L - 1, 0.0, pltpu.roll(x, L - 1, axis=0))
    w = w_ref[...]
    o_ref[0] = b_ref[...] + prev * w[0:1] + x * w[1:2] + nxt * w[2:3]


def _short_conv(p_all, w, b, seq_len):
    Bx, T, _ = p_all.shape
    nseq = T // seq_len
    C = w.shape[1]
    cb = COL_HY // LANES
    return pl.pallas_call(
        _short_conv_kernel,
        grid=(Bx, nseq, C // LANES),
        in_specs=[
            pl.BlockSpec((1, seq_len, LANES), lambda b, s, j: (b, s, cb + j)),
            pl.BlockSpec((3, LANES), lambda b, s, j: (0, j)),
            pl.BlockSpec((1, LANES), lambda b, s, j: (0, j)),
        ],
        out_specs=pl.BlockSpec((1, seq_len, LANES), lambda b, s, j: (b, s, j)),
        out_shape=jax.ShapeDtypeStruct((Bx, T, C), F32),
        compiler_params=_cparams(("parallel", "parallel", "parallel")),
        name="hy_short_conv",
    )(p_all, w, b.reshape(1, C))


@functools.lru_cache(maxsize=None)
def _dft_consts(L):
    N2 = HY_N2
    N = 2 * L
    N1 = N // N2
    S = N1 // 2 + 1
    k1 = np.arange(S)[:, None]
    n2 = np.arange(N2)
    tw = np.exp(-2j * np.pi * k1 * n2[None, :] / N)
    tw_tab = np.stack([np.repeat(tw.real[:, :, None], LANES, 2), np.repeat(tw.imag[:, :, None], LANES, 2)], 1)
    F = np.exp(-2j * np.pi * np.outer(n2, n2) / N2)
    M2 = np.block([[F.real, -F.imag], [F.imag, F.real]])

    def split(m):
        m32 = m.astype(np.float32)
        hi = m32.astype(BF16)
        lo = (m32 - hi.astype(np.float32)).astype(BF16)
        return np.stack([hi, lo])

    def stage1(nb):
        n1 = np.arange(nb)[None, :]
        ang = 2 * np.pi * k1 * n1 / N1
        return np.cos(ang), -np.sin(ang)

    n1o = np.arange(L // N2)[:, None]
    k1o = np.arange(S)[None, :]
    ang = 2 * np.pi * n1o * k1o / N1
    wgt = np.where((k1o == 0) | (k1o == N1 // 2), 1.0, 2.0) / N
    return dict(N=N, N1=N1, S=S, stage1=stage1, tw=tw_tab.astype(np.float32),
                m2=split(M2), m2t=split(M2.T), icr=np.cos(ang) * wgt, ici=-np.sin(ang) * wgt)


def _dft_mm(m_ref, x, passes):
    x_hi = x.astype(BF16)
    acc = jnp.dot(m_ref[0], x_hi, preferred_element_type=F32)
    if passes >= 2:
        x_lo = (x - x_hi.astype(F32)).astype(BF16)
        acc = acc + jnp.dot(m_ref[0], x_lo, preferred_element_type=F32)
    if passes >= 3:
        acc = acc + jnp.dot(m_ref[1], x_hi, preferred_element_type=F32)
    return acc


def _coef_acc(acc, c, x):
    if x is None or abs(c) < 1e-12:
        return acc
    if abs(c - 1.0) < 1e-12:
        return x if acc is None else acc + x
    if abs(c + 1.0) < 1e-12:
        return -x if acc is None else acc - x
    return c * x if acc is None else acc + c * x


def _vadd(a, b):
    return b if a is None else a if b is None else a + b


def _vsub(a, b):
    return (None if b is None else -b) if a is None else a if b is None else a - b


def _slot_groups(N1):
    half = N1 // 2
    return [(k, half - k if half - k != k else None) for k in range(half // 2 + 1)]


def _stage1_group(load, nb, cr, ci, k, kp, tw_ref, a_ref):
    N2, CT = HY_N2, HY_CT
    for r0 in range(0, N2, HY_SUB):
        if kp is None:
            ar = ai = None
            for n1 in range(nb):
                xb = load(n1, r0)
                ar = _coef_acc(ar, cr[k, n1], xb)
                ai = _coef_acc(ai, ci[k, n1], xb)
            slots = [(k, ar, ai)]
        else:
            even = [None, None]
            odd = [None, None]
            for n1 in range(nb):
                xb = load(n1, r0)
                tgt = even if n1 % 2 == 0 else odd
                tgt[0] = _coef_acc(tgt[0], cr[k, n1], xb)
                tgt[1] = _coef_acc(tgt[1], ci[k, n1], xb)
            slots = [(k, _vadd(even[0], odd[0]), _vadd(even[1], odd[1])),
                     (kp, _vsub(even[0], odd[0]), _vsub(odd[1], even[1]))]
        for idx, (kk, ar, ai) in enumerate(slots):
            zero = jnp.zeros((HY_SUB, CT), F32)
            ar = zero if ar is None else ar
            if kk > 0:
                twr = tw_ref[kk, 0, r0:r0 + HY_SUB, :]
                twi = tw_ref[kk, 1, r0:r0 + HY_SUB, :]
                if ai is None:
                    ar, ai = ar * twr, ar * twi
                else:
                    ar, ai = ar * twr - ai * twi, ar * twi + ai * twr
            ai = zero if ai is None else ai
            a_ref[r0:r0 + HY_SUB, idx * CT:(idx + 1) * CT] = ar
            a_ref[N2 + r0:N2 + r0 + HY_SUB, idx * CT:(idx + 1) * CT] = ai


def _stage1_inv_group(b_ref, nb, icr, ici, k, kp, tw_ref, acc_ref, first):
    N2, CT = HY_N2, HY_CT

    def load(idx, kk, r0):
        br = b_ref[r0:r0 + HY_SUB, idx * CT:(idx + 1) * CT]
        bi = b_ref[N2 + r0:N2 + r0 + HY_SUB, idx * CT:(idx + 1) * CT]
        if kk > 0:
            twr = tw_ref[kk, 0, r0:r0 + HY_SUB, :]
            twi = tw_ref[kk, 1, r0:r0 + HY_SUB, :]
            br, bi = br * twr + bi * twi, bi * twr - br * twi
        return br, bi

    for r0 in range(0, N2, HY_SUB):
        br, bi = load(0, k, r0)
        if kp is None:
            q_even = q_odd = (br, bi)
        else:
            br2, bi2 = load(1, kp, r0)
            q_even = (br + br2, bi - bi2)
            q_odd = (br - br2, bi + bi2)
        for n1 in range(nb):
            qr, qi = q_even if n1 % 2 == 0 else q_odd
            contrib = _coef_acc(_coef_acc(None, icr[n1, k], qr), ici[n1, k], qi)
            rows = slice(n1 * N2 + r0, n1 * N2 + r0 + HY_SUB)
            if first:
                acc_ref[rows, :] = contrib
            elif contrib is not None:
                acc_ref[rows, :] += contrib


def _filter_spec_kernel(hf_ref, hb_ref, tw_ref, m2_ref, o_ref, a_ref, *, L):
    c = _dft_consts(L)
    N2, CT = HY_N2, HY_CT
    nb = L // N2
    cr, ci = c["stage1"](nb)
    for k, kp in _slot_groups(c["N1"]):
        w = CT if kp is None else 2 * CT
        specs = []
        for h_ref in (hf_ref, hb_ref):
            _stage1_group(lambda n1, r0: h_ref[n1 * N2 + r0:n1 * N2 + r0 + HY_SUB, :], nb, cr, ci, k, kp, tw_ref, a_ref)
            specs.append(_dft_mm(m2_ref, a_ref[:, :w], 3))
        xf, xb = specs
        g = jnp.concatenate([xf[:N2] + xb[:N2], xf[N2:] - xb[N2:]], axis=0)
        o_ref[k] = g[:, :CT]
        if kp is not None:
            o_ref[kp] = g[:, CT:]


def _filter_spectrum(hfb, L):
    c = _dft_consts(L)
    C = hfb.shape[2]
    S, R = c["S"], 2 * HY_N2
    return pl.pallas_call(
        functools.partial(_filter_spec_kernel, L=L),
        grid=(C // HY_CT,),
        in_specs=[
            pl.BlockSpec((None, L, HY_CT), lambda j: (0, 0, j)),
            pl.BlockSpec((None, L, HY_CT), lambda j: (1, 0, j)),
            pl.BlockSpec((S, 2, HY_N2, LANES), lambda j: (0, 0, 0, 0)),
            pl.BlockSpec((2, R, R), lambda j: (0, 0, 0)),
        ],
        out_specs=pl.BlockSpec((S, R, HY_CT), lambda j: (0, 0, j)),
        out_shape=jax.ShapeDtypeStruct((S, R, C), F32),
        scratch_shapes=[pltpu.VMEM((R, 2 * HY_CT), F32)],
        compiler_params=_cparams(("parallel",)),
        name="hy_filter_spectrum",
    )(hfb, hfb, jnp.asarray(c["tw"]), jnp.asarray(c["m2"]))


def _long_conv_kernel(u_ref, gate_ref, spec_ref, skip_ref, tw_ref, m2_ref, m2t_ref, o_ref, a_ref, b_ref, acc_ref,
                      *, L):
    c = _dft_consts(L)
    N2, CT = HY_N2, HY_CT
    nb = L // N2
    cr, ci = c["stage1"](nb)
    icr, ici = c["icr"] * c["N"], c["ici"] * c["N"]
    for gi_, (k, kp) in enumerate(_slot_groups(c["N1"])):
        w = CT if kp is None else 2 * CT
        _stage1_group(lambda n1, r0: u_ref[0, n1 * N2 + r0:n1 * N2 + r0 + HY_SUB, :], nb, cr, ci, k, kp, tw_ref, a_ref)
        x = _dft_mm(m2_ref, a_ref[:, :w], HY_PASSES)
        g = spec_ref[k] if kp is None else jnp.concatenate([spec_ref[k], spec_ref[kp]], axis=1)
        xr, xi, gr, gi = x[:N2], x[N2:], g[:N2], g[N2:]
        y = jnp.concatenate([xr * gr - xi * gi, xr * gi + xi * gr], axis=0)
        b_ref[:, :w] = _dft_mm(m2t_ref, y, HY_PASSES)
        _stage1_inv_group(b_ref, nb, icr, ici, k, kp, tw_ref, acc_ref, gi_ == 0)
    o_ref[0] = (gate_ref[0] * (acc_ref[...] * (1.0 / c["N"]) + skip_ref[...] * u_ref[0])).astype(o_ref.dtype)


def _long_conv(u_arr, u_col, gate_arr, gate_col, spec, spec_col, skip, L, out_dtype):
    Bx, T, _ = u_arr.shape
    nseq = T // L
    c = _dft_consts(L)
    C = skip.shape[0]
    S, R = c["S"], 2 * HY_N2
    ub, gb, sb = u_col // HY_CT, gate_col // HY_CT, spec_col // HY_CT
    return pl.pallas_call(
        functools.partial(_long_conv_kernel, L=L),
        grid=(C // HY_CT, Bx, nseq),
        in_specs=[
            pl.BlockSpec((1, L, HY_CT), lambda j, b, s: (b, s, ub + j)),
            pl.BlockSpec((1, L, HY_CT), lambda j, b, s: (b, s, gb + j)),
            pl.BlockSpec((S, R, HY_CT), lambda j, b, s: (0, 0, sb + j)),
            pl.BlockSpec((1, HY_CT), lambda j, b, s: (0, j)),
            pl.BlockSpec((S, 2, HY_N2, LANES), lambda j, b, s: (0, 0, 0, 0)),
            pl.BlockSpec((2, R, R), lambda j, b, s: (0, 0, 0)),
            pl.BlockSpec((2, R, R), lambda j, b, s: (0, 0, 0)),
        ],
        out_specs=pl.BlockSpec((1, L, HY_CT), lambda j, b, s: (b, s, j)),
        out_shape=jax.ShapeDtypeStruct((Bx, T, C), out_dtype),
        scratch_shapes=[pltpu.VMEM((R, 2 * HY_CT), F32), pltpu.VMEM((R, 2 * HY_CT), F32),
                        pltpu.VMEM((L, HY_CT), F32)],
        compiler_params=_cparams(("parallel", "parallel", "parallel")),
        name="hy_long_conv",
    )(u_arr, gate_arr, spec, skip.reshape(1, C), jnp.asarray(c["tw"]), jnp.asarray(c["m2"]), jnp.asarray(c["m2t"]))


def _dot3(a, b):
    a_hi = a.astype(BF16)
    a_lo = (a - a_hi.astype(F32)).astype(BF16)
    b_hi = b.astype(BF16)
    b_lo = (b - b_hi.astype(F32)).astype(BF16)
    return jnp.dot(a_hi, b_hi, preferred_element_type=F32) + \
        (jnp.dot(a_hi, b_lo, preferred_element_type=F32) + jnp.dot(a_lo, b_hi, preferred_element_type=F32))


def _filter_gen_kernel(z_ref, w1_ref, b1_ref, w2_ref, b2_ref, freq_ref, w3_ref, delta_ref, o_ref, h_ref):
    L = z_ref.shape[0]

    @pl.when(pl.program_id(0) == 0)
    def _():
        h = jnp.sin(freq_ref[0:1, :] * (_dot3(z_ref[...], w1_ref[...]) + b1_ref[...]))
        h_ref[...] = jnp.sin(freq_ref[1:2, :] * (_dot3(h, w2_ref[...]) + b2_ref[...]))

    t_norm = lax.broadcasted_iota(jnp.int32, (L, LANES), 0).astype(F32) / max(L - 1, 1)
    h = _dot3(h_ref[...], w3_ref[...]) * jnp.exp(-t_norm * delta_ref[...])
    o_ref[...] = h / jnp.sum(jnp.abs(h), axis=0, keepdims=True)


def _hyena_filters(L, w1, b1, w2, b2, w3, freq):
    t = jnp.arange(L, dtype=F32)
    t_norm = t / max(L - 1, 1)
    w = 2.0 * math.pi * t / L
    bands = jnp.linspace(1e-4, HY_EMB_BANDS - 1, HY_EMB_BANDS, dtype=F32)
    z = jnp.concatenate([t_norm[:, None], jnp.cos(w[:, None] * bands), -jnp.sin(w[:, None] * bands)], axis=-1)
    deltas = jnp.abs(jnp.linspace(math.log(HY_DECAY_TARGET) / HY_DECAY_SLOW,
                                  math.log(HY_DECAY_TARGET) / HY_DECAY_FAST, HY_WIDTH, dtype=F32))
    E, FW = z.shape[1], w1.shape[1]
    nct = HY_WIDTH // LANES
    return pl.pallas_call(
        _filter_gen_kernel,
        grid=(HY_ORDER * 2 * nct,),
        in_specs=[
            pl.BlockSpec((L, E), lambda j: (0, 0)),
            pl.BlockSpec((E, FW), lambda j: (0, 0)),
            pl.BlockSpec((1, FW), lambda j: (0, 0)),
            pl.BlockSpec((FW, FW), lambda j: (0, 0)),
            pl.BlockSpec((1, FW), lambda j: (0, 0)),
            pl.BlockSpec((2, FW), lambda j: (0, 0)),
            pl.BlockSpec((FW, LANES), lambda j: (0, j)),
            pl.BlockSpec((1, LANES), lambda j: (0, j % nct)),
        ],
        out_specs=pl.BlockSpec((None, L, LANES), lambda j: ((j // nct) % 2, 0, (j // (2 * nct)) * nct + j % nct)),
        out_shape=jax.ShapeDtypeStruct((2, L, HY_ORDER * HY_WIDTH), F32),
        scratch_shapes=[pltpu.VMEM((L, FW), F32)],
        compiler_params=_cparams(("arbitrary",)),
        name="hy_filter_gen",
    )(z, w1, b1.reshape(1, FW), w2, b2.reshape(1, FW), freq, w3, deltas.reshape(1, HY_WIDTH))


def _hyena_spectra(L, hy_fp):
    return _filter_spectrum(_hyena_filters(L, *hy_fp), L)


def _hyena(p_all, sw, sb, spec, skip, L):
    u3 = _short_conv(p_all, sw, sb, L)
    W = HY_WIDTH
    z1 = _long_conv(u3, 2 * W, u3, 0, spec, 0, skip[0], L, F32)
    return _long_conv(z1, 0, u3, W, spec, W, skip[1], L, BF16)


def _merge_kernel(att_ref, hf_ref, hb_ref, op_ref, hy_ref, gp0_ref, gp1_ref, gp2_ref, mg_ref, wb_ref, wo_ref,
                  res_ref, g_ref, o_ref):
    d = ML_HEAD_DIM
    h = hf_ref[0] + hb_ref[0]
    parts = []
    for hh in range(ML_HEADS):
        hs = h[:, hh * d:(hh + 1) * d]
        parts.append(hs * lax.rsqrt(jnp.mean(hs * hs, axis=-1, keepdims=True) + EPS))
    mls = jnp.concatenate(parts, axis=1) * mg_ref[...] * _sigmoid(op_ref[0].astype(F32))
    y = _sigmoid(gp0_ref[0].astype(F32)) * jnp.dot(att_ref[0], wb_ref[0], preferred_element_type=F32)
    y = y + _sigmoid(gp1_ref[0].astype(F32)) * jnp.dot(mls.astype(BF16), wb_ref[1], preferred_element_type=F32)
    y = y + _sigmoid(gp2_ref[0].astype(F32)) * jnp.dot(hy_ref[0], wb_ref[2], preferred_element_type=F32)
    o_ref[0] = res_ref[0] + g_ref[0] * jnp.dot(y.astype(BF16), wo_ref[...], preferred_element_type=F32)


def _merge(att, hf, hb, p_all, hy, ml_g, wb, wo, res, g, tm):
    Bx, T, D = res.shape
    W = ML_W
    row = lambda b, i: (b, i, 0)
    return pl.pallas_call(
        _merge_kernel,
        grid=(Bx, T // tm),
        in_specs=[
            pl.BlockSpec((1, tm, W), row),
            pl.BlockSpec((1, tm, W), row),
            pl.BlockSpec((1, tm, W), row),
            pl.BlockSpec((1, tm, W), lambda b, i: (b, i, COL_ML // W + 3)),
            pl.BlockSpec((1, tm, W), row),
            pl.BlockSpec((1, tm, D), lambda b, i: (b, i, COL_BG // D)),
            pl.BlockSpec((1, tm, D), lambda b, i: (b, i, COL_BG // D + 1)),
            pl.BlockSpec((1, tm, D), lambda b, i: (b, i, COL_BG // D + 2)),
            pl.BlockSpec((1, W), lambda b, i: (0, 0)),
            pl.BlockSpec((N_BRANCH, W, D), lambda b, i: (0, 0, 0)),
            pl.BlockSpec((D, D), lambda b, i: (0, 0)),
            pl.BlockSpec((1, tm, D), row),
            pl.BlockSpec((1, 1, D), lambda b, i: (b, 0, 0)),
        ],
        out_specs=pl.BlockSpec((1, tm, D), row),
        out_shape=jax.ShapeDtypeStruct((Bx, T, D), F32),
        compiler_params=_cparams(("parallel", "parallel")),
        name="merge",
    )(att, hf, hb, p_all, hy, p_all, p_all, p_all, ml_g.reshape(1, W), wb, wo, res, g)


FFN_TC = 1408
FFN_HALO = 8


def _ffn_up_kernel(x_ref, xp_ref, xn_ref, gs_ref, sh_ref, wg_ref, wv_ref, cwg_ref, cwv_ref, cbg_ref, cbv_ref,
                   o_ref, h_ref, hh_ref, *, tiles_per_seq):
    i = pl.program_id(1)
    tm = x_ref.shape[1]

    @pl.when(pl.program_id(2) == 0)
    def _():
        h_ref[...] = _rms_mod(x_ref[0], gs_ref[0], sh_ref[0])
        halo = jnp.concatenate([xp_ref[0], xn_ref[0]], axis=0)
        hh_ref[...] = _rms_mod(halo, gs_ref[0], sh_ref[0])

    first = (i % tiles_per_seq) == 0
    last = (i % tiles_per_seq) == tiles_per_seq - 1
    row8 = lax.broadcasted_iota(jnp.int32, (8, FFN_TC), 0)

    def conv_half(w_ref, cw_ref, cb_ref):
        u = jnp.dot(h_ref[...], w_ref[...], preferred_element_type=F32)
        uh = jnp.dot(hh_ref[...], w_ref[...], preferred_element_type=F32)
        pr = jnp.where(first, 0.0, uh[FFN_HALO - 1:FFN_HALO])
        nx = jnp.where(last, 0.0, uh[FFN_HALO:FFN_HALO + 1])
        prev = pltpu.roll(u, 1, axis=0)
        nxt = pltpu.roll(u, tm - 1, axis=0)
        prev = jnp.concatenate([jnp.where(row8 == 0, pr, prev[:8]), prev[8:]], axis=0)
        nxt = jnp.concatenate([nxt[:tm - 8], jnp.where(row8 == 7, nx, nxt[tm - 8:])], axis=0)
        cw = cw_ref[...]
        return cb_ref[...] + prev * cw[0:1] + u * cw[1:2] + nxt * cw[2:3]

    gate = conv_half(wg_ref, cwg_ref, cbg_ref)
    val = conv_half(wv_ref, cwv_ref, cbv_ref)
    o_ref[0] = (gate * _sigmoid(gate) * val).astype(o_ref.dtype)


def _ffn_up(x, gs, sh, wu, cw, cb, seq_len):
    Bx, T, D = x.shape
    tm = min(seq_len, 512)
    nj = D_FF // FFN_TC
    hb = tm // FFN_HALO
    nh = T // FFN_HALO
    col = lambda b, i, j: (0, j)
    col2 = lambda b, i, j: (0, nj + j)
    return pl.pallas_call(
        functools.partial(_ffn_up_kernel, tiles_per_seq=seq_len // tm),
        grid=(Bx, T // tm, nj),
        in_specs=[
            pl.BlockSpec((1, tm, D), lambda b, i, j: (b, i, 0)),
            pl.BlockSpec((1, FFN_HALO, D), lambda b, i, j: (b, jnp.maximum(i * hb - 1, 0), 0)),
            pl.BlockSpec((1, FFN_HALO, D), lambda b, i, j: (b, jnp.minimum((i + 1) * hb, nh - 1), 0)),
            pl.BlockSpec((1, 1, D), lambda b, i, j: (b, 0, 0)),
            pl.BlockSpec((1, 1, D), lambda b, i, j: (b, 0, 0)),
            pl.BlockSpec((D, FFN_TC), col), pl.BlockSpec((D, FFN_TC), col2),
            pl.BlockSpec((3, FFN_TC), col), pl.BlockSpec((3, FFN_TC), col2),
            pl.BlockSpec((1, FFN_TC), col), pl.BlockSpec((1, FFN_TC), col2),
        ],
        out_specs=pl.BlockSpec((1, tm, FFN_TC), lambda b, i, j: (b, i, j)),
        out_shape=jax.ShapeDtypeStruct((Bx, T, D_FF), BF16),
        scratch_shapes=[pltpu.VMEM((tm, D), BF16), pltpu.VMEM((2 * FFN_HALO, D), BF16)],
        compiler_params=_cparams(("parallel", "parallel", "arbitrary")),
        name="ffn_up",
    )(x, x, x, gs, sh, wu, wu, cw, cw, cb.reshape(1, -1), cb.reshape(1, -1))


def _proj_res_kernel(a_ref, w_ref, res_ref, g_ref, o_ref):
    o_ref[0] = res_ref[0] + g_ref[0] * jnp.dot(a_ref[0], w_ref[...], preferred_element_type=F32)


def _proj_res(a, w, res, g):
    Bx, T, D = res.shape
    K = a.shape[2]
    tm = min(T, 512)
    return pl.pallas_call(
        _proj_res_kernel,
        grid=(Bx, T // tm),
        in_specs=[
            pl.BlockSpec((1, tm, K), lambda b, i: (b, i, 0)),
            pl.BlockSpec((K, D), lambda b, i: (0, 0)),
            pl.BlockSpec((1, tm, D), lambda b, i: (b, i, 0)),
            pl.BlockSpec((1, 1, D), lambda b, i: (b, 0, 0)),
        ],
        out_specs=pl.BlockSpec((1, tm, D), lambda b, i: (b, i, 0)),
        out_shape=jax.ShapeDtypeStruct((Bx, T, D), F32),
        compiler_params=_cparams(("parallel", "parallel")),
        name="ffn_down",
    )(a, w, res, g)


def _pair_perm():
    hd, half = ATT_HEAD_DIM, ATT_HEAD_DIM // 2
    n_pair = ATT_Q // LANES
    qperm = []
    for p in range(n_pair):
        for sub in range(4):
            head = p if sub % 2 == 0 else n_pair + p
            qperm += [head * hd + (sub // 2) * half + dd for dd in range(half)]
    kperm = []
    for sub in range(4):
        kperm += [(sub % 2) * hd + (sub // 2) * half + dd for dd in range(half)]
    return np.asarray(qperm), np.asarray(kperm)


def _rope_tables(L):
    rows = L // GRID_W
    row = jnp.repeat(jnp.arange(rows, dtype=F32), GRID_W)
    col = jnp.tile(jnp.arange(GRID_W, dtype=F32), rows)
    nf = ATT_HEAD_DIM // 4
    inv = ROPE_THETA ** (-jnp.arange(nf, dtype=F32) / nf)
    ang = jnp.concatenate([row[:, None] * inv, col[:, None] * inv], axis=-1)
    cos, sin = jnp.cos(ang), jnp.sin(ang)
    return jnp.concatenate([cos, cos, cos, cos, -sin, -sin, sin, sin], axis=1)


def _prep_w_in(w):
    qperm, kperm = _pair_perm()
    o = IN_OFFSETS
    cols = [w[:, :ATT_Q][:, qperm], w[:, o[2]:o[6]], w[:, o[7]:o[8]], w[:, o[8]:],
            w[:, o[0]:o[1]][:, kperm], w[:, o[1]:o[2]][:, kperm]]
    wc = jnp.concatenate(cols, axis=1)
    w_gate = jnp.pad(w[:, o[6]:o[7]], ((0, 0), (0, LANES - ML_GATES))).astype(BF16)
    return jnp.pad(wc, ((0, 0), (0, IN_PAD - wc.shape[1]))).astype(BF16), w_gate


def _sink_cols(sink, rows_per_pair):
    n_pair = ATT_Q // LANES
    lo = jnp.repeat(sink[:n_pair], rows_per_pair)
    hi = jnp.repeat(sink[n_pair:], rows_per_pair)
    return jnp.stack([lo, hi])[:, :, None]


def kernel(x, c, ctx, c_ctx, ada_w, ada_b, norm1_g, norm2_g, w_in, att_sink, ml_gate_b, ml_norm_g,
           hy_short_w, hy_short_b, hy_w1, hy_b1, hy_w2, hy_b2, hy_w3, hy_freq, hy_skip,
           w_branch, w_out, w_up, ffn_conv_w, ffn_conv_b, w_down, final_g):
    B, L, D = x.shape
    Lc = ctx.shape[1]
    qperm, _ = _pair_perm()
    tabs = _rope_tables(L)
    sc_rows = jnp.concatenate([jax.nn.silu(c), jax.nn.silu(c_ctx)[None], jnp.zeros((8 - B - 1, D), F32)], axis=0)
    xl = x
    xc = ctx.reshape(1, B * Lc, D)
    for l in range(DEPTH):
        need_ctx = l < DEPTH - 1
        mod = _ada_proj(sc_rows, ada_w[l], ada_b[l])
        sh1, sc1, g1, sh2, sc2, g2 = [m[:B, None, :] for m in jnp.split(mod, 6, axis=-1)]
        csh1, csc1, cg1, csh2, csc2, cg2 = [m[B:B + 1, None, :] for m in jnp.split(mod, 6, axis=-1)]
        n1 = norm1_g[l][None, None, :]
        n2 = norm2_g[l][None, None, :]
        w_in_b, w_gate = _prep_w_in(w_in[l])
        wb = jnp.concatenate([w_branch[l][:1][:, qperm], w_branch[l][1:]], axis=0).astype(BF16)
        wo = w_out[l].astype(BF16)
        wu = w_up[l].astype(BF16)
        wd = w_down[l].astype(BF16)
        gate_bias = jnp.pad(ml_gate_b[l].reshape(1, ML_GATES), ((0, 0), (0, LANES - ML_GATES)))
        hy_fp = (hy_w1[l], hy_b1[l], hy_w2[l], hy_b2[l], hy_w3[l], hy_freq[l])

        pl_all, gl = _norm_proj(xl, n1 * (1.0 + sc1), sh1, w_in_b, IN_TN, w_gate)
        pc_all, gc = _norm_proj(xc, n1 * (1.0 + csc1), csh1, w_in_b, IN_TN, w_gate)
        att_l = _win_attention(pl_all, pc_all, tabs, _sink_cols(att_sink[l], ATT_BLOCK))
        hf, hb = _mlstm(pl_all, pc_all, gl, gc, gate_bias)
        hy_l = _hyena(pl_all, hy_short_w[l], hy_short_b[l], _hyena_spectra(L, hy_fp), hy_skip[l], L)
        xl = _merge(att_l, hf, hb, pl_all, hy_l, ml_norm_g[l], wb, wo, xl, g1, 512)
        act = _ffn_up(xl, n2 * (1.0 + sc2), sh2, wu, ffn_conv_w[l], ffn_conv_b[l], L)
        xl = _proj_res(act, wd, xl, g2)
        if need_ctx:
            att_c = _ctx_attention(pc_all, _sink_cols(att_sink[l], Lc), B)
            hy_c = _hyena(pc_all, hy_short_w[l], hy_short_b[l], _hyena_spectra(Lc, hy_fp), hy_skip[l], Lc)
            hfc = hf[:, L:].reshape(1, B * Lc, ML_W)
            hbc = hb[:, L:].reshape(1, B * Lc, ML_W)
            xc = _merge(att_c, hfc, hbc, pc_all, hy_c, ml_norm_g[l], wb, wo, xc, cg1, Lc)
            act_c = _ffn_up(xc, n2 * (1.0 + csc2), csh2, wu, ffn_conv_w[l], ffn_conv_b[l], Lc)
            xc = _proj_res(act_c, wd, xc, cg2)
    return _final_norm(xl, final_g)
```

```python
import functools
import math

import numpy as np
import jax
import jax.numpy as jnp
from jax import lax
from jax.experimental import pallas as pl
from jax.experimental.pallas import tpu as pltpu

F32 = jnp.float32
BF16 = jnp.bfloat16

D_MODEL = 1024
DEPTH = 4
GRID_W = 64
EPS = 1e-6
ATT_HEADS = 8
ATT_KV_HEADS = 2
ATT_HEAD_DIM = 64
ATT_BLOCK = 128
ROPE_THETA = 10000.0
ML_HEADS = 4
ML_HEAD_DIM = 128
ML_CHUNK = 128
ML_NB = 2
HY_WIDTH = 512
HY_ORDER = 2
HY_EMB_BANDS = 16
HY_DECAY_TARGET = 1e-2
HY_DECAY_FAST = 0.3
HY_DECAY_SLOW = 1.5
D_FF = 2816
N_BRANCH = 3
LANES = 128

ATT_Q = ATT_HEADS * ATT_HEAD_DIM
ATT_KV = ATT_KV_HEADS * ATT_HEAD_DIM
ML_W = ML_HEADS * ML_HEAD_DIM
ML_GATES = 2 * 2 * ML_HEADS
IN_SIZES = (ATT_Q, ATT_KV, ATT_KV, ML_W, ML_W, ML_W, ML_W, ML_GATES, 3 * HY_WIDTH, N_BRANCH * D_MODEL)
IN_OFFSETS = tuple(int(o) for o in np.cumsum(IN_SIZES)[:-1])

COL_Q = 0
COL_ML = COL_Q + ATT_Q
COL_HY = COL_ML + 4 * ML_W
COL_K = COL_HY + 3 * HY_WIDTH
COL_V = COL_K + ATT_KV
IN_PAD = 4608
IN_TN = 1536
IN_TM = 2048

HY_N2 = 256
HY_CT = 128
HY_SUB = 32
HY_PASSES = 1

VMEM_LIMIT = 56 * 1024 * 1024


def _cparams(sem):
    return pltpu.CompilerParams(dimension_semantics=sem, vmem_limit_bytes=VMEM_LIMIT)


def _sigmoid(x):
    return 0.5 * jnp.tanh(0.5 * x) + 0.5


def _log_sigmoid(x):
    return jnp.minimum(x, 0.0) - jnp.log(1.0 + jnp.exp(-jnp.abs(x)))


def _rms_mod(x, gs, sh):
    ms = jnp.mean(x * x, axis=-1, keepdims=True)
    return (x * lax.rsqrt(ms + EPS) * gs + sh).astype(BF16)


def _norm_proj_kernel(x_ref, gs_ref, sh_ref, w_ref, *rest, with_aux):
    if with_aux:
        wa_ref, o_ref, oa_ref, h_ref = rest
    else:
        o_ref, h_ref = rest

    @pl.when(pl.program_id(2) == 0)
    def _():
        h = _rms_mod(x_ref[0], gs_ref[0], sh_ref[0])
        h_ref[...] = h
        if with_aux:
            oa_ref[0] = jnp.dot(h, wa_ref[...], preferred_element_type=F32)

    o_ref[0] = jnp.dot(h_ref[...], w_ref[...], preferred_element_type=F32).astype(o_ref.dtype)


def _norm_proj(x, gs, sh, w, tn, w_aux=None):
    B, T, D = x.shape
    N = w.shape[1]
    tm = min(T, IN_TM)
    in_specs = [
        pl.BlockSpec((1, tm, D), lambda b, i, j: (b, i, 0)),
        pl.BlockSpec((1, 1, D), lambda b, i, j: (b, 0, 0)),
        pl.BlockSpec((1, 1, D), lambda b, i, j: (b, 0, 0)),
        pl.BlockSpec((D, tn), lambda b, i, j: (0, j)),
    ]
    out_specs = [pl.BlockSpec((1, tm, tn), lambda b, i, j: (b, i, j))]
    out_shape = [jax.ShapeDtypeStruct((B, T, N), BF16)]
    args = [x, gs, sh, w]
    if w_aux is not None:
        na = w_aux.shape[1]
        in_specs.append(pl.BlockSpec((D, na), lambda b, i, j: (0, 0)))
        out_specs.append(pl.BlockSpec((1, tm, na), lambda b, i, j: (b, i, 0)))
        out_shape.append(jax.ShapeDtypeStruct((B, T, na), F32))
        args.append(w_aux)
    outs = pl.pallas_call(
        functools.partial(_norm_proj_kernel, with_aux=w_aux is not None),
        grid=(B, T // tm, N // tn),
        in_specs=in_specs,
        out_specs=out_specs,
        out_shape=out_shape,
        scratch_shapes=[pltpu.VMEM((tm, D), BF16)],
        compiler_params=_cparams(("parallel", "parallel", "arbitrary")),
        name="norm_proj",
    )(*args)
    return outs if w_aux is not None else outs[0]


def _small_proj_kernel(a_ref, w_ref, b_ref, o_ref):
    a = a_ref[...]
    a_hi = a.astype(BF16)
    a_lo = (a - a_hi.astype(F32)).astype(BF16)
    w = w_ref[...]
    w_hi = w.astype(BF16)
    w_lo = (w - w_hi.astype(F32)).astype(BF16)
    acc = jnp.dot(a_hi, w_hi, preferred_element_type=F32)
    acc = acc + (jnp.dot(a_hi, w_lo, preferred_element_type=F32) + jnp.dot(a_lo, w_hi, preferred_element_type=F32))
    o_ref[...] = acc + b_ref[...]


def _ada_proj(a, w, b):
    M, D = a.shape
    N = w.shape[1]
    tn = 1024
    return pl.pallas_call(
        _small_proj_kernel,
        grid=(N // tn,),
        in_specs=[
            pl.BlockSpec((M, D), lambda j: (0, 0)),
            pl.BlockSpec((D, tn), lambda j: (0, j)),
            pl.BlockSpec((1, tn), lambda j: (0, j)),
        ],
        out_specs=pl.BlockSpec((M, tn), lambda j: (0, j)),
        out_shape=jax.ShapeDtypeStruct((M, N), F32),
        compiler_params=_cparams(("parallel",)),
        name="ada_proj",
    )(a, w, b.reshape(1, N))


def _final_norm_kernel(x_ref, g_ref, o_ref):
    x = x_ref[0]
    ms = jnp.mean(x * x, axis=-1, keepdims=True)
    o_ref[0] = x * lax.rsqrt(ms + EPS) * g_ref[...]


def _final_norm(x, g):
    B, T, D = x.shape
    tm = 512
    return pl.pallas_call(
        _final_norm_kernel,
        grid=(B, T // tm),
        in_specs=[
            pl.BlockSpec((1, tm, D), lambda b, i: (b, i, 0)),
            pl.BlockSpec((1, D), lambda b, i: (0, 0)),
        ],
        out_specs=pl.BlockSpec((1, tm, D), lambda b, i: (b, i, 0)),
        out_shape=jax.ShapeDtypeStruct((B, T, D), F32),
        compiler_params=_cparams(("parallel", "parallel")),
        name="final_norm",
    )(x, g.reshape(1, D))


def _lane_lo_mask(shape):
    lane = lax.broadcasted_iota(jnp.int32, shape, len(shape) - 1)
    return (lane % 64) < 32


def _rope(x, tab):
    c = tab[:, :LANES]
    s = tab[:, LANES:]
    outs = []
    for g in range(x.shape[1] // LANES):
        xg = x[:, g * LANES:(g + 1) * LANES]
        outs.append(xg * c + pltpu.roll(xg, 64, axis=1) * s)
    return outs[0] if len(outs) == 1 else jnp.concatenate(outs, axis=1)


def _attend(qs, kcat, vcat, bias, sink_lo, sink_hi):
    lo = _lane_lo_mask(kcat.shape)
    T = ATT_BLOCK
    outs = []
    for msk, sink in ((lo, sink_lo), (jnp.logical_not(lo), sink_hi)):
        kh = jnp.where(msk, kcat, 0.0).astype(BF16)
        s = lax.dot_general(qs, kh, (((1,), (1,)), ((), ())), preferred_element_type=F32)
        if bias is not None:
            c0, bias_prev, bias_next = bias
            s = jnp.concatenate([s[:, :c0], s[:, c0:c0 + T] + bias_prev, s[:, c0 + T:c0 + 2 * T],
                                 s[:, c0 + 2 * T:] + bias_next], axis=1)
        m = jnp.maximum(jnp.max(s, axis=-1, keepdims=True), sink)
        p = jnp.exp(s - m)
        den = jnp.sum(p, axis=-1, keepdims=True) + jnp.exp(sink - m)
        outs.append(jnp.dot(p.astype(BF16), vcat, preferred_element_type=F32) * (1.0 / den))
    return jnp.where(_lane_lo_mask(outs[0].shape), outs[0], outs[1])


ATT_QB = 2


def _win_attn_kernel(*refs):
    nkb = ATT_QB + 2
    q_ref = refs[0]
    k_refs = refs[1:1 + nkb]
    v_refs = refs[1 + nkb:1 + 2 * nkb]
    kx_ref, vx_ref = refs[1 + 2 * nkb:3 + 2 * nkb]
    t_refs = refs[3 + 2 * nkb:3 + 3 * nkb]
    sink_ref, o_ref = refs[3 + 3 * nkb:]
    i = pl.program_id(1)
    nb = pl.num_programs(1) * ATT_QB
    T = ATT_BLOCK
    n_pair = ATT_Q // LANES
    Lc = kx_ref.shape[1]
    kx = kx_ref[0].astype(F32)
    k_rot = [_rope(k_refs[j][0].astype(F32), t_refs[j][...]) for j in range(nkb)]
    t = lax.broadcasted_iota(jnp.int32, (T, T), 0)
    s = lax.broadcasted_iota(jnp.int32, (T, T), 1)
    for sb in range(ATT_QB):
        blk = i * ATT_QB + sb
        q = _rope(q_ref[0, sb * T:(sb + 1) * T, :].astype(F32), t_refs[sb + 1][...]) * (ATT_HEAD_DIM ** -0.5)
        qs = jnp.concatenate([q[:, g * LANES:(g + 1) * LANES] for g in range(n_pair)], axis=0).astype(BF16)
        kcat = jnp.concatenate([kx, k_rot[sb], k_rot[sb + 1], k_rot[sb + 2]], axis=0)
        vcat = jnp.concatenate([vx_ref[0], v_refs[sb][0], v_refs[sb + 1][0], v_refs[sb + 2][0]], axis=0)
        bias_prev = jnp.where((s >= t) & (blk > 0), 0.0, -1e30).astype(F32)
        bias_next = jnp.where((s <= t) & (blk < nb - 1), 0.0, -1e30).astype(F32)
        bias = (Lc, jnp.concatenate([bias_prev] * n_pair, axis=0), jnp.concatenate([bias_next] * n_pair, axis=0))
        o = _attend(qs, kcat, vcat, bias, sink_ref[0], sink_ref[1])
        o_ref[0, sb * T:(sb + 1) * T, :] = jnp.concatenate(
            [o[g * T:(g + 1) * T] for g in range(n_pair)], axis=1).astype(o_ref.dtype)


def _win_attention(pl_all, pc_all, tabs, sinks):
    B, L, _ = pl_all.shape
    Lc = pc_all.shape[1] // B
    T = ATT_BLOCK
    nb = L // T
    kb, vb = COL_K // LANES, COL_V // LANES
    offs = range(-1, ATT_QB + 1)

    def blk(col, d):
        return pl.BlockSpec((1, T, LANES), lambda b, i: (b, jnp.clip(i * ATT_QB + d, 0, nb - 1), col))

    def tab(d):
        return pl.BlockSpec((T, 2 * LANES), lambda b, i: (jnp.clip(i * ATT_QB + d, 0, nb - 1), 0))

    nkb = ATT_QB + 2
    return pl.pallas_call(
        _win_attn_kernel,
        grid=(B, nb // ATT_QB),
        in_specs=[pl.BlockSpec((1, ATT_QB * T, ATT_Q), lambda b, i: (b, i, 0))] +
        [blk(kb, d) for d in offs] + [blk(vb, d) for d in offs] +
        [pl.BlockSpec((1, Lc, LANES), lambda b, i: (0, b, kb)),
         pl.BlockSpec((1, Lc, LANES), lambda b, i: (0, b, vb))] +
        [tab(d) for d in offs] +
        [pl.BlockSpec((2, ATT_Q, 1), lambda b, i: (0, 0, 0))],
        out_specs=pl.BlockSpec((1, ATT_QB * T, ATT_Q), lambda b, i: (b, i, 0)),
        out_shape=jax.ShapeDtypeStruct((B, L, ATT_Q), BF16),
        compiler_params=_cparams(("parallel", "parallel")),
        name="win_attention",
    )(*([pl_all] * (1 + 2 * nkb) + [pc_all] * 2 + [tabs] * nkb + [sinks]))


def _ctx_attn_kernel(q_ref, kx_ref, vx_ref, sink_ref, o_ref):
    Lc = q_ref.shape[1]
    n_pair = ATT_Q // LANES
    q = q_ref[0].astype(F32) * (ATT_HEAD_DIM ** -0.5)
    qs = jnp.concatenate([q[:, g * LANES:(g + 1) * LANES] for g in range(n_pair)], axis=0).astype(BF16)
    o = _attend(qs, kx_ref[0].astype(F32), vx_ref[0], None, sink_ref[0], sink_ref[1])
    o_ref[0] = jnp.concatenate([o[g * Lc:(g + 1) * Lc] for g in range(n_pair)], axis=1).astype(o_ref.dtype)


def _ctx_attention(pc_all, sinks, B):
    Lc = pc_all.shape[1] // B
    kb, vb = COL_K // LANES, COL_V // LANES
    return pl.pallas_call(
        _ctx_attn_kernel,
        grid=(B,),
        in_specs=[
            pl.BlockSpec((1, Lc, ATT_Q), lambda b: (0, b, 0)),
            pl.BlockSpec((1, Lc, LANES), lambda b: (0, b, kb)),
            pl.BlockSpec((1, Lc, LANES), lambda b: (0, b, vb)),
            pl.BlockSpec((2, ATT_Q // LANES * Lc, 1), lambda b: (0, 0, 0)),
        ],
        out_specs=pl.BlockSpec((1, Lc, ATT_Q), lambda b: (0, b, 0)),
        out_shape=jax.ShapeDtypeStruct((1, B * Lc, ATT_Q), BF16),
        compiler_params=_cparams(("parallel",)),
        name="ctx_attention",
    )(pc_all, pc_all, pc_all, sinks)


def _dot_hl(a_exact, x):
    x_hi = x.astype(BF16)
    x_lo = (x - x_hi.astype(F32)).astype(BF16)
    return jnp.dot(a_exact, x_hi, preferred_element_type=F32) + jnp.dot(a_exact, x_lo, preferred_element_type=F32)


def _mlstm_kernel(qf_l, kf_l, vf_l, gf_l, qb_l, kb_l, vb_l, gb_l,
                  qf_c, kf_c, vf_c, gf_c, qb_c, kb_c, vb_c, gb_c, bias_ref,
                  hf_ref, hb_ref, ct_ref, m_ref, *, n_ctx_chunks):
    j = pl.program_id(1)
    T = ML_CHUNK
    d = ML_HEAD_DIM
    is_ctx = j < n_ctx_chunks

    @pl.when(j == 0)
    def _():
        ct_ref[...] = jnp.zeros_like(ct_ref)
        m_ref[...] = jnp.zeros_like(m_ref)

    row = lax.broadcasted_iota(jnp.int32, (T, T), 0)
    col = lax.broadcasted_iota(jnp.int32, (T, T), 1)
    ones_td = jnp.ones((T, d), BF16)

    nt = (((1,), (1,)), ((), ()))
    sel_row = lax.broadcasted_iota(jnp.int32, (LANES, 2 * ML_HEADS * LANES), 0)
    sel_blk = lax.broadcasted_iota(jnp.int32, (LANES, 2 * ML_HEADS * LANES), 1) // LANES
    chains = []
    for bi in range(ML_NB):
        for di, (q_l, k_l, v_l, g_l, q_c, k_c, v_c, g_c) in enumerate(
                ((qf_l, kf_l, vf_l, gf_l, qf_c, kf_c, vf_c, gf_c),
                 (qb_l, kb_l, vb_l, gb_l, qb_c, kb_c, vb_c, gb_c))):
            keep = (col <= row) if di == 0 else (col >= row)
            keep_b = keep.astype(BF16)
            g = jnp.where(is_ctx, g_c[bi], g_l[bi]) + bias_ref[...]
            lane = lax.broadcasted_iota(jnp.int32, g.shape, 1)
            is_f = (lane % 8) >= 4
            gv = jnp.where(is_f, _log_sigmoid(g), g)
            gt = gv.T
            cum_c = _dot_hl(keep_b, gv)
            gt_hi = gt.astype(BF16)
            gt_lo = (gt - gt_hi.astype(F32)).astype(BF16)
            cum_r = lax.dot_general(gt_hi, keep_b, nt, preferred_element_type=F32) + \
                lax.dot_general(gt_lo, keep_b, nt, preferred_element_type=F32)
            src_col = jnp.where(sel_blk < ML_HEADS, di * 8 + ML_HEADS + sel_blk, di * 8 + sel_blk - ML_HEADS)
            sel = jnp.where(sel_row == src_col, 1.0, 0.0).astype(BF16)
            x = jnp.where(is_f, cum_c, gv)
            x_hi = x.astype(BF16)
            x_lo = (x - x_hi.astype(F32)).astype(BF16)
            full = jnp.dot(x_hi, sel, preferred_element_type=F32) + jnp.dot(x_lo, sel, preferred_element_type=F32)
            q_all = jnp.where(is_ctx, q_c[bi], q_l[bi])
            k_all = jnp.where(is_ctx, k_c[bi], k_l[bi]).astype(F32) * (d ** -0.5)
            v_all = jnp.where(is_ctx, v_c[bi], v_l[bi])
            for h in range(ML_HEADS):
                ci = di * 8 + h
                cf = di * 8 + 4 + h
                sl = slice(h * d, (h + 1) * d)
                b_full = full[:, h * LANES:(h + 1) * LANES]
                chains.append(dict(
                    idx=(bi * 2 + di) * ML_HEADS + h, keep=keep, q=q_all[:, sl], k=k_all[:, sl], v=v_all[:, sl],
                    b_full=b_full, i_full=full[:, (ML_HEADS + h) * LANES:(ML_HEADS + h + 1) * LANES],
                    b_r=cum_r[cf:cf + 1, :], i_r=gt[ci:ci + 1, :],
                    bl=b_full[T - 1:T, :] if di == 0 else b_full[0:1, :]))
    for c in chains:
        c["m_old"] = m_ref[c["idx"]]
        c["a"] = c["b_full"] + c["m_old"]
        c["dmat"] = jnp.where(c["keep"], c["b_full"] - c["b_r"] + c["i_r"], -1e30)
    for c in chains:
        c["mt"] = jnp.maximum(c["a"], jnp.broadcast_to(jnp.max(c["dmat"], axis=-1, keepdims=True), (T, T)))
    for c in chains:
        c["qk"] = lax.dot_general(c["q"], c["k"].astype(BF16), nt, preferred_element_type=F32)
    for c in chains:
        s = (c["qk"] * jnp.exp(c["dmat"] - c["mt"])).astype(BF16)
        wq = (c["q"].astype(F32) * jnp.exp(c["a"] - c["mt"])).astype(BF16)
        c["v_aug"] = jnp.concatenate([c["v"], ones_td], axis=1)
        c["ct"] = ct_ref[c["idx"]]
        lhs = jnp.concatenate([wq, s], axis=1)
        rhs = jnp.concatenate([c["ct"].astype(BF16), c["v_aug"]], axis=0)
        c["r"] = jnp.dot(lhs, rhs, preferred_element_type=F32)
    for c in chains:
        r = c["r"]
        c["h"] = r[:, :d] / jnp.maximum(jnp.abs(r[:, d:]), jnp.exp(-c["mt"]))
    for c in chains:
        src = c["bl"] - c["b_full"] + c["i_full"]
        m_new = jnp.maximum(c["bl"] + c["m_old"], jnp.max(src, axis=0, keepdims=True))
        gk = (jnp.exp(src - m_new) * c["k"]).T.astype(BF16)
        decay = jnp.exp(c["bl"] + c["m_old"] - m_new)
        ct_ref[c["idx"]] = jnp.concatenate([decay, decay], axis=1) * c["ct"] + \
            jnp.dot(gk, c["v_aug"], preferred_element_type=F32)
        m_ref[c["idx"]] = m_new
    for bi in range(ML_NB):
        for di, h_ref in enumerate((hf_ref, hb_ref)):
            base = (bi * 2 + di) * ML_HEADS
            h_ref[bi] = jnp.concatenate([chains[base + h]["h"] for h in range(ML_HEADS)], axis=1)


def _mlstm(pl_all, pc_all, gl, gc, gate_bias):
    B, L, _ = pl_all.shape
    Lc = pc_all.shape[1] // B
    T = ML_CHUNK
    nl, ncx = L // T, Lc // T
    nsteps = nl + ncx
    pc3 = pc_all.reshape(B, Lc, pc_all.shape[2])
    gc3 = gc.reshape(B, Lc, gc.shape[2])

    def lat_f(j):
        return jnp.clip(j - ncx, 0, nl - 1)

    def lat_b(j):
        return jnp.clip(nsteps - 1 - j, 0, nl - 1)

    def ctx_f(j):
        return jnp.clip(j, 0, ncx - 1)

    def ctx_b(j):
        return jnp.clip(ncx - 1 - j, 0, ncx - 1)

    def lat_specs(fn):
        base = COL_ML // ML_W
        return [pl.BlockSpec((ML_NB, T, ML_W), lambda b, j, o=o: (b, fn(j), base + o)) for o in range(3)] + \
               [pl.BlockSpec((ML_NB, T, LANES), lambda b, j: (b, fn(j), 0))]

    def ctx_specs(fn):
        base = COL_ML // ML_W
        return [pl.BlockSpec((ML_NB, T, ML_W), lambda b, j, o=o: (b, fn(j), base + o)) for o in range(3)] + \
               [pl.BlockSpec((ML_NB, T, LANES), lambda b, j: (b, fn(j), 0))]

    def out_f(b, j):
        return (b, jnp.where(j < ncx, nl + j, j - ncx), 0)

    def out_b(b, j):
        return (b, jnp.where(j < ncx, nl + ncx - 1 - j, nsteps - 1 - j), 0)

    return pl.pallas_call(
        functools.partial(_mlstm_kernel, n_ctx_chunks=ncx),
        grid=(B // ML_NB, nsteps),
        in_specs=lat_specs(lat_f) + lat_specs(lat_b) + ctx_specs(ctx_f) + ctx_specs(ctx_b) +
        [pl.BlockSpec((1, LANES), lambda b, j: (0, 0))],
        out_specs=[pl.BlockSpec((ML_NB, T, ML_W), out_f), pl.BlockSpec((ML_NB, T, ML_W), out_b)],
        out_shape=[jax.ShapeDtypeStruct((B, L + Lc, ML_W), F32)] * 2,
        scratch_shapes=[pltpu.VMEM((ML_NB * 2 * ML_HEADS, ML_HEAD_DIM, 2 * ML_HEAD_DIM), F32),
                        pltpu.VMEM((ML_NB * 2 * ML_HEADS, 1, LANES), F32)],
        compiler_params=_cparams(("parallel", "arbitrary")),
        name="mlstm",
    )(*(([pl_all] * 3 + [gl]) * 2 + ([pc3] * 3 + [gc3]) * 2 + [gate_bias]))


def _short_conv_kernel(x_ref, w_ref, b_ref, o_ref):
    x = x_ref[0].astype(F32)
    L = x.shape[0]
    row = lax.broadcasted_iota(jnp.int32, x.shape, 0)
    prev = jnp.where(row == 0, 0.0, pltpu.roll(x, 1, axis=0))
    nxt = jnp.where(row == L - 1, 0.0, pltpu.roll(x, L - 1, axis=0))
    w = w_ref[...]
    o_ref[0] = b_ref[...] + prev * w[0:1] + x * w[1:2] + nxt * w[2:3]


def _short_conv(p_all, w, b, seq_len):
    Bx, T, _ = p_all.shape
    nseq = T // seq_len
    C = w.shape[1]
    cb = COL_HY // LANES
    return pl.pallas_call(
        _short_conv_kernel,
        grid=(Bx, nseq, C // LANES),
        in_specs=[
            pl.BlockSpec((1, seq_len, LANES), lambda b, s, j: (b, s, cb + j)),
            pl.BlockSpec((3, LANES), lambda b, s, j: (0, j)),
            pl.BlockSpec((1, LANES), lambda b, s, j: (0, j)),
        ],
        out_specs=pl.BlockSpec((1, seq_len, LANES), lambda b, s, j: (b, s, j)),
        out_shape=jax.ShapeDtypeStruct((Bx, T, C), F32),
        compiler_params=_cparams(("parallel", "parallel", "parallel")),
        name="hy_short_conv",
    )(p_all, w, b.reshape(1, C))


@functools.lru_cache(maxsize=None)
def _dft_consts(L):
    N2 = HY_N2
    N = 2 * L
    N1 = N // N2
    S = N1 // 2 + 1
    k1 = np.arange(S)[:, None]
    n2 = np.arange(N2)
    tw = np.exp(-2j * np.pi * k1 * n2[None, :] / N)
    tw_tab = np.stack([np.repeat(tw.real[:, :, None], LANES, 2), np.repeat(tw.imag[:, :, None], LANES, 2)], 1)
    F = np.exp(-2j * np.pi * np.outer(n2, n2) / N2)
    M2 = np.block([[F.real, -F.imag], [F.imag, F.real]])

    def split(m):
        m32 = m.astype(np.float32)
        hi = m32.astype(BF16)
        lo = (m32 - hi.astype(np.float32)).astype(BF16)
        return np.stack([hi, lo])

    def stage1(nb):
        n1 = np.arange(nb)[None, :]
        ang = 2 * np.pi * k1 * n1 / N1
        return np.cos(ang), -np.sin(ang)

    n1o = np.arange(L // N2)[:, None]
    k1o = np.arange(S)[None, :]
    ang = 2 * np.pi * n1o * k1o / N1
    wgt = np.where((k1o == 0) | (k1o == N1 // 2), 1.0, 2.0) / N
    return dict(N=N, N1=N1, S=S, stage1=stage1, tw=tw_tab.astype(np.float32),
                m2=split(M2), m2t=split(M2.T), icr=np.cos(ang) * wgt, ici=-np.sin(ang) * wgt)


def _dft_mm(m_ref, x, passes):
    x_hi = x.astype(BF16)
    acc = jnp.dot(m_ref[0], x_hi, preferred_element_type=F32)
    if passes >= 2:
        x_lo = (x - x_hi.astype(F32)).astype(BF16)
        acc = acc + jnp.dot(m_ref[0], x_lo, preferred_element_type=F32)
    if passes >= 3:
        acc = acc + jnp.dot(m_ref[1], x_hi, preferred_element_type=F32)
    return acc


def _coef_acc(acc, c, x):
    if x is None or abs(c) < 1e-12:
        return acc
    if abs(c - 1.0) < 1e-12:
        return x if acc is None else acc + x
    if abs(c + 1.0) < 1e-12:
        return -x if acc is None else acc - x
    return c * x if acc is None else acc + c * x


def _vadd(a, b):
    return b if a is None else a if b is None else a + b


def _vsub(a, b):
    return (None if b is None else -b) if a is None else a if b is None else a - b


def _slot_groups(N1):
    half = N1 // 2
    return [(k, half - k if half - k != k else None) for k in range(half // 2 + 1)]


def _stage1_group(load, nb, cr, ci, k, kp, tw_ref, a_ref):
    N2, CT = HY_N2, HY_CT
    for r0 in range(0, N2, HY_SUB):
        if kp is None:
            ar = ai = None
            for n1 in range(nb):
                xb = load(n1, r0)
                ar = _coef_acc(ar, cr[k, n1], xb)
                ai = _coef_acc(ai, ci[k, n1], xb)
            slots = [(k, ar, ai)]
        else:
            even = [None, None]
            odd = [None, None]
            for n1 in range(nb):
                xb = load(n1, r0)
                tgt = even if n1 % 2 == 0 else odd
                tgt[0] = _coef_acc(tgt[0], cr[k, n1], xb)
                tgt[1] = _coef_acc(tgt[1], ci[k, n1], xb)
            slots = [(k, _vadd(even[0], odd[0]), _vadd(even[1], odd[1])),
                     (kp, _vsub(even[0], odd[0]), _vsub(odd[1], even[1]))]
        for idx, (kk, ar, ai) in enumerate(slots):
            zero = jnp.zeros((HY_SUB, CT), F32)
            ar = zero if ar is None else ar
            if kk > 0:
                twr = tw_ref[kk, 0, r0:r0 + HY_SUB, :]
                twi = tw_ref[kk, 1, r0:r0 + HY_SUB, :]
                if ai is None:
                    ar, ai = ar * twr, ar * twi
                else:
                    ar, ai = ar * twr - ai * twi, ar * twi + ai * twr
            ai = zero if ai is None else ai
            a_ref[r0:r0 + HY_SUB, idx * CT:(idx + 1) * CT] = ar
            a_ref[N2 + r0:N2 + r0 + HY_SUB, idx * CT:(idx + 1) * CT] = ai


def _stage1_inv_group(b_ref, nb, icr, ici, k, kp, tw_ref, acc_ref, first):
    N2, CT = HY_N2, HY_CT

    def load(idx, kk, r0):
        br = b_ref[r0:r0 + HY_SUB, idx * CT:(idx + 1) * CT]
        bi = b_ref[N2 + r0:N2 + r0 + HY_SUB, idx * CT:(idx + 1) * CT]
        if kk > 0:
            twr = tw_ref[kk, 0, r0:r0 + HY_SUB, :]
            twi = tw_ref[kk, 1, r0:r0 + HY_SUB, :]
            br, bi = br * twr + bi * twi, bi * twr - br * twi
        return br, bi

    for r0 in range(0, N2, HY_SUB):
        br, bi = load(0, k, r0)
        if kp is None:
            q_even = q_odd = (br, bi)
        else:
            br2, bi2 = load(1, kp, r0)
            q_even = (br + br2, bi - bi2)
            q_odd = (br - br2, bi + bi2)
        for n1 in range(nb):
            qr, qi = q_even if n1 % 2 == 0 else q_odd
            contrib = _coef_acc(_coef_acc(None, icr[n1, k], qr), ici[n1, k], qi)
            rows = slice(n1 * N2 + r0, n1 * N2 + r0 + HY_SUB)
            if first:
                acc_ref[rows, :] = contrib
            elif contrib is not None:
                acc_ref[rows, :] += contrib


def _filter_spec_kernel(hf_ref, hb_ref, tw_ref, m2_ref, o_ref, a_ref, *, L):
    c = _dft_consts(L)
    N2, CT = HY_N2, HY_CT
    nb = L // N2
    cr, ci = c["stage1"](nb)
    for k, kp in _slot_groups(c["N1"]):
        w = CT if kp is None else 2 * CT
        specs = []
        for h_ref in (hf_ref, hb_ref):
            _stage1_group(lambda n1, r0: h_ref[n1 * N2 + r0:n1 * N2 + r0 + HY_SUB, :], nb, cr, ci, k, kp, tw_ref, a_ref)
            specs.append(_dft_mm(m2_ref, a_ref[:, :w], 3))
        xf, xb = specs
        g = jnp.concatenate([xf[:N2] + xb[:N2], xf[N2:] - xb[N2:]], axis=0)
        o_ref[k] = g[:, :CT]
        if kp is not None:
            o_ref[kp] = g[:, CT:]


def _filter_spectrum(hfb, L):
    c = _dft_consts(L)
    C = hfb.shape[2]
    S, R = c["S"], 2 * HY_N2
    return pl.pallas_call(
        functools.partial(_filter_spec_kernel, L=L),
        grid=(C // HY_CT,),
        in_specs=[
            pl.BlockSpec((None, L, HY_CT), lambda j: (0, 0, j)),
            pl.BlockSpec((None, L, HY_CT), lambda j: (1, 0, j)),
            pl.BlockSpec((S, 2, HY_N2, LANES), lambda j: (0, 0, 0, 0)),
            pl.BlockSpec((2, R, R), lambda j: (0, 0, 0)),
        ],
        out_specs=pl.BlockSpec((S, R, HY_CT), lambda j: (0, 0, j)),
        out_shape=jax.ShapeDtypeStruct((S, R, C), F32),
        scratch_shapes=[pltpu.VMEM((R, 2 * HY_CT), F32)],
        compiler_params=_cparams(("parallel",)),
        name="hy_filter_spectrum",
    )(hfb, hfb, jnp.asarray(c["tw"]), jnp.asarray(c["m2"]))


def _long_conv_kernel(u_ref, gate_ref, spec_ref, skip_ref, tw_ref, m2_ref, m2t_ref, o_ref, a_ref, b_ref, acc_ref,
                      *, L):
    c = _dft_consts(L)
    N2, CT = HY_N2, HY_CT
    nb = L // N2
    cr, ci = c["stage1"](nb)
    icr, ici = c["icr"] * c["N"], c["ici"] * c["N"]
    for gi_, (k, kp) in enumerate(_slot_groups(c["N1"])):
        w = CT if kp is None else 2 * CT
        _stage1_group(lambda n1, r0: u_ref[0, n1 * N2 + r0:n1 * N2 + r0 + HY_SUB, :], nb, cr, ci, k, kp, tw_ref, a_ref)
        x = _dft_mm(m2_ref, a_ref[:, :w], HY_PASSES)
        g = spec_ref[k] if kp is None else jnp.concatenate([spec_ref[k], spec_ref[kp]], axis=1)
        xr, xi, gr, gi = x[:N2], x[N2:], g[:N2], g[N2:]
        y = jnp.concatenate([xr * gr - xi * gi, xr * gi + xi * gr], axis=0)
        b_ref[:, :w] = _dft_mm(m2t_ref, y, HY_PASSES)
        _stage1_inv_group(b_ref, nb, icr, ici, k, kp, tw_ref, acc_ref, gi_ == 0)
    o_ref[0] = (gate_ref[0] * (acc_ref[...] * (1.0 / c["N"]) + skip_ref[...] * u_ref[0])).astype(o_ref.dtype)


def _long_conv(u_arr, u_col, gate_arr, gate_col, spec, spec_col, skip, L, out_dtype):
    Bx, T, _ = u_arr.shape
    nseq = T // L
    c = _dft_consts(L)
    C = skip.shape[0]
    S, R = c["S"], 2 * HY_N2
    ub, gb, sb = u_col // HY_CT, gate_col // HY_CT, spec_col // HY_CT
    return pl.pallas_call(
        functools.partial(_long_conv_kernel, L=L),
        grid=(C // HY_CT, Bx, nseq),
        in_specs=[
            pl.BlockSpec((1, L, HY_CT), lambda j, b, s: (b, s, ub + j)),
            pl.BlockSpec((1, L, HY_CT), lambda j, b, s: (b, s, gb + j)),
            pl.BlockSpec((S, R, HY_CT), lambda j, b, s: (0, 0, sb + j)),
            pl.BlockSpec((1, HY_CT), lambda j, b, s: (0, j)),
            pl.BlockSpec((S, 2, HY_N2, LANES), lambda j, b, s: (0, 0, 0, 0)),
            pl.BlockSpec((2, R, R), lambda j, b, s: (0, 0, 0)),
            pl.BlockSpec((2, R, R), lambda j, b, s: (0, 0, 0)),
        ],
        out_specs=pl.BlockSpec((1, L, HY_CT), lambda j, b, s: (b, s, j)),
        out_shape=jax.ShapeDtypeStruct((Bx, T, C), out_dtype),
        scratch_shapes=[pltpu.VMEM((R, 2 * HY_CT), F32), pltpu.VMEM((R, 2 * HY_CT), F32),
                        pltpu.VMEM((L, HY_CT), F32)],
        compiler_params=_cparams(("parallel", "parallel", "parallel")),
        name="hy_long_conv",
    )(u_arr, gate_arr, spec, skip.reshape(1, C), jnp.asarray(c["tw"]), jnp.asarray(c["m2"]), jnp.asarray(c["m2t"]))


def _dot3(a, b):
    a_hi = a.astype(BF16)
    a_lo = (a - a_hi.astype(F32)).astype(BF16)
    b_hi = b.astype(BF16)
    b_lo = (b - b_hi.astype(F32)).astype(BF16)
    return jnp.dot(a_hi, b_hi, preferred_element_type=F32) + \
        (jnp.dot(a_hi, b_lo, preferred_element_type=F32) + jnp.dot(a_lo, b_hi, preferred_element_type=F32))


def _filter_gen_kernel(z_ref, w1_ref, b1_ref, w2_ref, b2_ref, freq_ref, w3_ref, delta_ref, o_ref, h_ref):
    L = z_ref.shape[0]

    @pl.when(pl.program_id(0) == 0)
    def _():
        h = jnp.sin(freq_ref[0:1, :] * (_dot3(z_ref[...], w1_ref[...]) + b1_ref[...]))
        h_ref[...] = jnp.sin(freq_ref[1:2, :] * (_dot3(h, w2_ref[...]) + b2_ref[...]))

    t_norm = lax.broadcasted_iota(jnp.int32, (L, LANES), 0).astype(F32) / max(L - 1, 1)
    h = _dot3(h_ref[...], w3_ref[...]) * jnp.exp(-t_norm * delta_ref[...])
    o_ref[...] = h / jnp.sum(jnp.abs(h), axis=0, keepdims=True)


def _hyena_filters(L, w1, b1, w2, b2, w3, freq):
    t = jnp.arange(L, dtype=F32)
    t_norm = t / max(L - 1, 1)
    w = 2.0 * math.pi * t / L
    bands = jnp.linspace(1e-4, HY_EMB_BANDS - 1, HY_EMB_BANDS, dtype=F32)
    z = jnp.concatenate([t_norm[:, None], jnp.cos(w[:, None] * bands), -jnp.sin(w[:, None] * bands)], axis=-1)
    deltas = jnp.abs(jnp.linspace(math.log(HY_DECAY_TARGET) / HY_DECAY_SLOW,
                                  math.log(HY_DECAY_TARGET) / HY_DECAY_FAST, HY_WIDTH, dtype=F32))
    E, FW = z.shape[1], w1.shape[1]
    nct = HY_WIDTH // LANES
    return pl.pallas_call(
        _filter_gen_kernel,
        grid=(HY_ORDER * 2 * nct,),
        in_specs=[
            pl.BlockSpec((L, E), lambda j: (0, 0)),
            pl.BlockSpec((E, FW), lambda j: (0, 0)),
            pl.BlockSpec((1, FW), lambda j: (0, 0)),
            pl.BlockSpec((FW, FW), lambda j: (0, 0)),
            pl.BlockSpec((1, FW), lambda j: (0, 0)),
            pl.BlockSpec((2, FW), lambda j: (0, 0)),
            pl.BlockSpec((FW, LANES), lambda j: (0, j)),
            pl.BlockSpec((1, LANES), lambda j: (0, j % nct)),
        ],
        out_specs=pl.BlockSpec((None, L, LANES), lambda j: ((j // nct) % 2, 0, (j // (2 * nct)) * nct + j % nct)),
        out_shape=jax.ShapeDtypeStruct((2, L, HY_ORDER * HY_WIDTH), F32),
        scratch_shapes=[pltpu.VMEM((L, FW), F32)],
        compiler_params=_cparams(("arbitrary",)),
        name="hy_filter_gen",
    )(z, w1, b1.reshape(1, FW), w2, b2.reshape(1, FW), freq, w3, deltas.reshape(1, HY_WIDTH))


def _hyena_spectra(L, hy_fp):
    return _filter_spectrum(_hyena_filters(L, *hy_fp), L)


def _hyena(p_all, sw, sb, spec, skip, L):
    u3 = _short_conv(p_all, sw, sb, L)
    W = HY_WIDTH
    z1 = _long_conv(u3, 2 * W, u3, 0, spec, 0, skip[0], L, F32)
    return _long_conv(z1, 0, u3, W, spec, W, skip[1], L, BF16)


def _merge_kernel(x_ref, gs_ref, sh_ref, att_ref, hf_ref, hb_ref, op_ref, hy_ref, mg_ref, wb_ref, wg_ref, wo_ref,
                  g_ref, o_ref):
    d = ML_HEAD_DIM
    D = D_MODEL
    x = x_ref[0]
    h1 = _rms_mod(x, gs_ref[0], sh_ref[0])
    hsum = hf_ref[0] + hb_ref[0]
    parts = []
    for hh in range(ML_HEADS):
        hs = hsum[:, hh * d:(hh + 1) * d]
        parts.append(hs * lax.rsqrt(jnp.mean(hs * hs, axis=-1, keepdims=True) + EPS))
    mls = jnp.concatenate(parts, axis=1) * mg_ref[...] * _sigmoid(op_ref[0].astype(F32))
    y = None
    for r, br in enumerate((att_ref[0], mls.astype(BF16), hy_ref[0])):
        gate = _sigmoid(jnp.dot(h1, wg_ref[:, r * D:(r + 1) * D], preferred_element_type=F32))
        term = gate * jnp.dot(br, wb_ref[r], preferred_element_type=F32)
        y = term if y is None else y + term
    o_ref[0] = x + g_ref[0] * jnp.dot(y.astype(BF16), wo_ref[...], preferred_element_type=F32)


def _merge(x, gs, sh, att, hf, hb, p_all, hy, ml_g, wb, wg, wo, g, tm):
    Bx, T, D = x.shape
    W = ML_W
    row = lambda b, i: (b, i, 0)
    vec = lambda b, i: (b, 0, 0)
    return pl.pallas_call(
        _merge_kernel,
        grid=(Bx, T // tm),
        in_specs=[
            pl.BlockSpec((1, tm, D), row),
            pl.BlockSpec((1, 1, D), vec),
            pl.BlockSpec((1, 1, D), vec),
            pl.BlockSpec((1, tm, W), row),
            pl.BlockSpec((1, tm, W), row),
            pl.BlockSpec((1, tm, W), row),
            pl.BlockSpec((1, tm, W), lambda b, i: (b, i, COL_ML // W + 3)),
            pl.BlockSpec((1, tm, W), row),
            pl.BlockSpec((1, W), lambda b, i: (0, 0)),
            pl.BlockSpec((N_BRANCH, W, D), lambda b, i: (0, 0, 0)),
            pl.BlockSpec((D, N_BRANCH * D), lambda b, i: (0, 0)),
            pl.BlockSpec((D, D), lambda b, i: (0, 0)),
            pl.BlockSpec((1, 1, D), vec),
        ],
        out_specs=pl.BlockSpec((1, tm, D), row),
        out_shape=jax.ShapeDtypeStruct((Bx, T, D), F32),
        compiler_params=_cparams(("parallel", "parallel")),
        name="merge",
    )(x, gs, sh, att, hf, hb, p_all, hy, ml_g.reshape(1, W), wb, wg, wo, g)


FFN_TC = 1408
FFN_HALO = 8


def _ffn_kernel(x_ref, xp_ref, xn_ref, gs_ref, sh_ref, wg_ref, wv_ref, cwg_ref, cwv_ref, cbg_ref, cbv_ref,
                wd_ref, g_ref, o_ref, h_ref, hh_ref, acc_ref, *, tiles_per_seq):
    i = pl.program_id(1)
    j = pl.program_id(2)
    tm = x_ref.shape[1]

    @pl.when(j == 0)
    def _():
        h_ref[...] = _rms_mod(x_ref[0], gs_ref[0], sh_ref[0])
        halo = jnp.concatenate([xp_ref[0], xn_ref[0]], axis=0)
        hh_ref[...] = _rms_mod(halo, gs_ref[0], sh_ref[0])

    first = (i % tiles_per_seq) == 0
    last = (i % tiles_per_seq) == tiles_per_seq - 1
    row8 = lax.broadcasted_iota(jnp.int32, (8, FFN_TC), 0)

    def conv_half(w_ref, cw_ref, cb_ref):
        u = jnp.dot(h_ref[...], w_ref[...], preferred_element_type=F32)
        uh = jnp.dot(hh_ref[...], w_ref[...], preferred_element_type=F32)
        pr = jnp.where(first, 0.0, uh[FFN_HALO - 1:FFN_HALO])
        nx = jnp.where(last, 0.0, uh[FFN_HALO:FFN_HALO + 1])
        prev = pltpu.roll(u, 1, axis=0)
        nxt = pltpu.roll(u, tm - 1, axis=0)
        prev = jnp.concatenate([jnp.where(row8 == 0, pr, prev[:8]), prev[8:]], axis=0)
        nxt = jnp.concatenate([nxt[:tm - 8], jnp.where(row8 == 7, nx, nxt[tm - 8:])], axis=0)
        cw = cw_ref[...]
        return cb_ref[...] + prev * cw[0:1] + u * cw[1:2] + nxt * cw[2:3]

    gate = conv_half(wg_ref, cwg_ref, cbg_ref)
    val = conv_half(wv_ref, cwv_ref, cbv_ref)
    act = (gate * _sigmoid(gate) * val).astype(BF16)
    part = jnp.dot(act, wd_ref[...], preferred_element_type=F32)

    @pl.when(j == 0)
    def _():
        acc_ref[...] = part

    @pl.when(j > 0)
    def _():
        acc_ref[...] += part

    @pl.when(j == pl.num_programs(2) - 1)
    def _():
        o_ref[0] = x_ref[0] + g_ref[0] * acc_ref[...]


def _conv_ffn(x, gs, sh, wu, cw, cb, wd, g, seq_len):
    Bx, T, D = x.shape
    tm = min(seq_len, 512)
    nj = D_FF // FFN_TC
    hb = tm // FFN_HALO
    nh = T // FFN_HALO
    col = lambda b, i, j: (0, j)
    col2 = lambda b, i, j: (0, nj + j)
    vec = lambda b, i, j: (b, 0, 0)
    return pl.pallas_call(
        functools.partial(_ffn_kernel, tiles_per_seq=seq_len // tm),
        grid=(Bx, T // tm, nj),
        in_specs=[
            pl.BlockSpec((1, tm, D), lambda b, i, j: (b, i, 0)),
            pl.BlockSpec((1, FFN_HALO, D), lambda b, i, j: (b, jnp.maximum(i * hb - 1, 0), 0)),
            pl.BlockSpec((1, FFN_HALO, D), lambda b, i, j: (b, jnp.minimum((i + 1) * hb, nh - 1), 0)),
            pl.BlockSpec((1, 1, D), vec),
            pl.BlockSpec((1, 1, D), vec),
            pl.BlockSpec((D, FFN_TC), col), pl.BlockSpec((D, FFN_TC), col2),
            pl.BlockSpec((3, FFN_TC), col), pl.BlockSpec((3, FFN_TC), col2),
            pl.BlockSpec((1, FFN_TC), col), pl.BlockSpec((1, FFN_TC), col2),
            pl.BlockSpec((FFN_TC, D), lambda b, i, j: (j, 0)),
            pl.BlockSpec((1, 1, D), vec),
        ],
        out_specs=pl.BlockSpec((1, tm, D), lambda b, i, j: (b, i, 0)),
        out_shape=jax.ShapeDtypeStruct((Bx, T, D), F32),
        scratch_shapes=[pltpu.VMEM((tm, D), BF16), pltpu.VMEM((2 * FFN_HALO, D), BF16), pltpu.VMEM((tm, D), F32)],
        compiler_params=_cparams(("parallel", "parallel", "arbitrary")),
        name="conv_ffn",
    )(x, x, x, gs, sh, wu, wu, cw, cw, cb.reshape(1, -1), cb.reshape(1, -1), wd, g)


def _pair_perm():
    hd, half = ATT_HEAD_DIM, ATT_HEAD_DIM // 2
    n_pair = ATT_Q // LANES
    qperm = []
    for p in range(n_pair):
        for sub in range(4):
            head = p if sub % 2 == 0 else n_pair + p
            qperm += [head * hd + (sub // 2) * half + dd for dd in range(half)]
    kperm = []
    for sub in range(4):
        kperm += [(sub % 2) * hd + (sub // 2) * half + dd for dd in range(half)]
    return np.asarray(qperm), np.asarray(kperm)


def _rope_tables(L):
    rows = L // GRID_W
    row = jnp.repeat(jnp.arange(rows, dtype=F32), GRID_W)
    col = jnp.tile(jnp.arange(GRID_W, dtype=F32), rows)
    nf = ATT_HEAD_DIM // 4
    inv = ROPE_THETA ** (-jnp.arange(nf, dtype=F32) / nf)
    ang = jnp.concatenate([row[:, None] * inv, col[:, None] * inv], axis=-1)
    cos, sin = jnp.cos(ang), jnp.sin(ang)
    return jnp.concatenate([cos, cos, cos, cos, -sin, -sin, sin, sin], axis=1)


def _prep_w_in(w):
    qperm, kperm = _pair_perm()
    o = IN_OFFSETS
    w = w.astype(BF16)
    cols = [w[:, :ATT_Q][:, qperm], w[:, o[2]:o[6]], w[:, o[7]:o[8]],
            w[:, o[0]:o[1]][:, kperm], w[:, o[1]:o[2]][:, kperm]]
    wc = jnp.concatenate(cols, axis=1)
    w_gate = jnp.pad(w[:, o[6]:o[7]], ((0, 0), (0, LANES - ML_GATES)))
    return jnp.pad(wc, ((0, 0), (0, IN_PAD - wc.shape[1]))), w_gate, w[:, o[8]:]


def _sink_cols(sink, rows_per_pair):
    n_pair = ATT_Q // LANES
    lo = jnp.repeat(sink[:n_pair], rows_per_pair)
    hi = jnp.repeat(sink[n_pair:], rows_per_pair)
    return jnp.stack([lo, hi])[:, :, None]


def kernel(x, c, ctx, c_ctx, ada_w, ada_b, norm1_g, norm2_g, w_in, att_sink, ml_gate_b, ml_norm_g,
           hy_short_w, hy_short_b, hy_w1, hy_b1, hy_w2, hy_b2, hy_w3, hy_freq, hy_skip,
           w_branch, w_out, w_up, ffn_conv_w, ffn_conv_b, w_down, final_g):
    B, L, D = x.shape
    Lc = ctx.shape[1]
    qperm, _ = _pair_perm()
    tabs = _rope_tables(L)
    sc_rows = jnp.concatenate([jax.nn.silu(c), jax.nn.silu(c_ctx)[None], jnp.zeros((8 - B - 1, D), F32)], axis=0)
    xl = x
    xc = ctx.reshape(1, B * Lc, D)
    for l in range(DEPTH):
        need_ctx = l < DEPTH - 1
        mod = _ada_proj(sc_rows, ada_w[l], ada_b[l])
        sh1, sc1, g1, sh2, sc2, g2 = [m[:B, None, :] for m in jnp.split(mod, 6, axis=-1)]
        csh1, csc1, cg1, csh2, csc2, cg2 = [m[B:B + 1, None, :] for m in jnp.split(mod, 6, axis=-1)]
        n1 = norm1_g[l][None, None, :]
        n2 = norm2_g[l][None, None, :]
        w_in_b, w_gate, w_bg = _prep_w_in(w_in[l])
        wb_l = w_branch[l].astype(BF16)
        wb = jnp.concatenate([wb_l[:1][:, qperm], wb_l[1:]], axis=0)
        wo = w_out[l].astype(BF16)
        wu = w_up[l].astype(BF16)
        wd = w_down[l].astype(BF16)
        gate_bias = jnp.pad(ml_gate_b[l].reshape(1, ML_GATES), ((0, 0), (0, LANES - ML_GATES)))
        hy_fp = (hy_w1[l], hy_b1[l], hy_w2[l], hy_b2[l], hy_w3[l], hy_freq[l])

        pl_all, gl = _norm_proj(xl, n1 * (1.0 + sc1), sh1, w_in_b, IN_TN, w_gate)
        pc_all, gc = _norm_proj(xc, n1 * (1.0 + csc1), csh1, w_in_b, IN_TN, w_gate)
        att_l = _win_attention(pl_all, pc_all, tabs, _sink_cols(att_sink[l], ATT_BLOCK))
        hf, hb = _mlstm(pl_all, pc_all, gl, gc, gate_bias)
        hy_l = _hyena(pl_all, hy_short_w[l], hy_short_b[l], _hyena_spectra(L, hy_fp), hy_skip[l], L)
        xl = _merge(xl, n1 * (1.0 + sc1), sh1, att_l, hf, hb, pl_all, hy_l, ml_norm_g[l], wb, w_bg, wo, g1, 512)
        xl = _conv_ffn(xl, n2 * (1.0 + sc2), sh2, wu, ffn_conv_w[l], ffn_conv_b[l], wd, g2, L)
        if need_ctx:
            att_c = _ctx_attention(pc_all, _sink_cols(att_sink[l], Lc), B)
            hy_c = _hyena(pc_all, hy_short_w[l], hy_short_b[l], _hyena_spectra(Lc, hy_fp), hy_skip[l], Lc)
            hfc = hf[:, L:].reshape(1, B * Lc, ML_W)
            hbc = hb[:, L:].reshape(1, B * Lc, ML_W)
            xc = _merge(xc, n1 * (1.0 + csc1), csh1, att_c, hfc, hbc, pc_all, hy_c, ml_norm_g[l], wb, w_bg, wo,
                        cg1, Lc)
            xc = _conv_ffn(xc, n2 * (1.0 + csc2), csh2, wu, ffn_conv_w[l], ffn_conv_b[l], wd, cg2, Lc)
    return _final_norm(xl, final_g)
```

```python
import functools
import math

import numpy as np
import jax
import jax.numpy as jnp
from jax import lax
from jax.experimental import pallas as pl
from jax.experimental.pallas import tpu as pltpu

F32 = jnp.float32
BF16 = jnp.bfloat16

D_MODEL = 1024
DEPTH = 4
GRID_W = 64
EPS = 1e-6
ATT_HEADS = 8
ATT_KV_HEADS = 2
ATT_HEAD_DIM = 64
ATT_BLOCK = 128
ROPE_THETA = 10000.0
ML_HEADS = 4
ML_HEAD_DIM = 128
ML_CHUNK = 128
ML_NB = 2
HY_WIDTH = 512
HY_ORDER = 2
HY_EMB_BANDS = 16
HY_DECAY_TARGET = 1e-2
HY_DECAY_FAST = 0.3
HY_DECAY_SLOW = 1.5
D_FF = 2816
N_BRANCH = 3
LANES = 128

ATT_Q = ATT_HEADS * ATT_HEAD_DIM
ATT_KV = ATT_KV_HEADS * ATT_HEAD_DIM
ML_W = ML_HEADS * ML_HEAD_DIM
ML_GATES = 2 * 2 * ML_HEADS
IN_SIZES = (ATT_Q, ATT_KV, ATT_KV, ML_W, ML_W, ML_W, ML_W, ML_GATES, 3 * HY_WIDTH, N_BRANCH * D_MODEL)
IN_OFFSETS = tuple(int(o) for o in np.cumsum(IN_SIZES)[:-1])

COL_Q = 0
COL_ML = COL_Q + ATT_Q
COL_HY = COL_ML + 4 * ML_W
COL_K = COL_HY + 3 * HY_WIDTH
COL_V = COL_K + ATT_KV
IN_PAD = 4608
IN_TN = 1536
IN_TM = 2048

HY_N2 = 256
HY_CT = 128
HY_SUB = 32

VMEM_LIMIT = 56 * 1024 * 1024


def _cparams(sem):
    return pltpu.CompilerParams(dimension_semantics=sem, vmem_limit_bytes=VMEM_LIMIT)


def _sigmoid(x):
    return 0.5 * jnp.tanh(0.5 * x) + 0.5


def _log_sigmoid(x):
    return jnp.minimum(x, 0.0) - jnp.log(1.0 + jnp.exp(-jnp.abs(x)))


def _rms_mod(x, gs, sh):
    ms = jnp.mean(x * x, axis=-1, keepdims=True)
    return (x * lax.rsqrt(ms + EPS) * gs + sh).astype(BF16)


def _norm_proj_kernel(x_ref, gs_ref, sh_ref, w_ref, *rest, with_aux):
    if with_aux:
        wa_ref, o_ref, oa_ref, h_ref = rest
    else:
        o_ref, h_ref = rest

    @pl.when(pl.program_id(2) == 0)
    def _():
        h = _rms_mod(x_ref[0], gs_ref[0], sh_ref[0])
        h_ref[...] = h
        if with_aux:
            oa_ref[0] = jnp.dot(h, wa_ref[...], preferred_element_type=F32)

    o_ref[0] = jnp.dot(h_ref[...], w_ref[...], preferred_element_type=F32).astype(o_ref.dtype)


def _norm_proj(x, gs, sh, w, tn, w_aux=None):
    B, T, D = x.shape
    N = w.shape[1]
    tm = min(T, IN_TM)
    in_specs = [
        pl.BlockSpec((1, tm, D), lambda b, i, j: (b, i, 0)),
        pl.BlockSpec((1, 1, D), lambda b, i, j: (b, 0, 0)),
        pl.BlockSpec((1, 1, D), lambda b, i, j: (b, 0, 0)),
        pl.BlockSpec((D, tn), lambda b, i, j: (0, j)),
    ]
    out_specs = [pl.BlockSpec((1, tm, tn), lambda b, i, j: (b, i, j))]
    out_shape = [jax.ShapeDtypeStruct((B, T, N), BF16)]
    args = [x, gs, sh, w]
    if w_aux is not None:
        na = w_aux.shape[1]
        in_specs.append(pl.BlockSpec((D, na), lambda b, i, j: (0, 0)))
        out_specs.append(pl.BlockSpec((1, tm, na), lambda b, i, j: (b, i, 0)))
        out_shape.append(jax.ShapeDtypeStruct((B, T, na), F32))
        args.append(w_aux)
    outs = pl.pallas_call(
        functools.partial(_norm_proj_kernel, with_aux=w_aux is not None),
        grid=(B, T // tm, N // tn),
        in_specs=in_specs,
        out_specs=out_specs,
        out_shape=out_shape,
        scratch_shapes=[pltpu.VMEM((tm, D), BF16)],
        compiler_params=_cparams(("parallel", "parallel", "arbitrary")),
        name="norm_proj",
    )(*args)
    return outs if w_aux is not None else outs[0]


def _small_proj_kernel(a_ref, w_ref, b_ref, o_ref):
    a = a_ref[...]
    a_hi = a.astype(BF16)
    a_lo = (a - a_hi.astype(F32)).astype(BF16)
    w = w_ref[...]
    w_hi = w.astype(BF16)
    w_lo = (w - w_hi.astype(F32)).astype(BF16)
    acc = jnp.dot(a_hi, w_hi, preferred_element_type=F32)
    acc = acc + (jnp.dot(a_hi, w_lo, preferred_element_type=F32) + jnp.dot(a_lo, w_hi, preferred_element_type=F32))
    o_ref[...] = acc + b_ref[...]


def _ada_proj(a, w, b):
    M, D = a.shape
    N = w.shape[1]
    tn = 1024
    return pl.pallas_call(
        _small_proj_kernel,
        grid=(N // tn,),
        in_specs=[
            pl.BlockSpec((M, D), lambda j: (0, 0)),
            pl.BlockSpec((D, tn), lambda j: (0, j)),
            pl.BlockSpec((1, tn), lambda j: (0, j)),
        ],
        out_specs=pl.BlockSpec((M, tn), lambda j: (0, j)),
        out_shape=jax.ShapeDtypeStruct((M, N), F32),
        compiler_params=_cparams(("parallel",)),
        name="ada_proj",
    )(a, w, b.reshape(1, N))


def _final_norm_kernel(x_ref, g_ref, o_ref):
    x = x_ref[0]
    ms = jnp.mean(x * x, axis=-1, keepdims=True)
    o_ref[0] = x * lax.rsqrt(ms + EPS) * g_ref[...]


def _final_norm(x, g):
    B, T, D = x.shape
    tm = 512
    return pl.pallas_call(
        _final_norm_kernel,
        grid=(B, T // tm),
        in_specs=[
            pl.BlockSpec((1, tm, D), lambda b, i: (b, i, 0)),
            pl.BlockSpec((1, D), lambda b, i: (0, 0)),
        ],
        out_specs=pl.BlockSpec((1, tm, D), lambda b, i: (b, i, 0)),
        out_shape=jax.ShapeDtypeStruct((B, T, D), F32),
        compiler_params=_cparams(("parallel", "parallel")),
        name="final_norm",
    )(x, g.reshape(1, D))


def _lane_lo_mask(shape):
    lane = lax.broadcasted_iota(jnp.int32, shape, len(shape) - 1)
    return (lane % 64) < 32


def _rope(x, tab):
    c = tab[:, :LANES]
    s = tab[:, LANES:]
    outs = []
    for g in range(x.shape[1] // LANES):
        xg = x[:, g * LANES:(g + 1) * LANES]
        outs.append(xg * c + pltpu.roll(xg, 64, axis=1) * s)
    return outs[0] if len(outs) == 1 else jnp.concatenate(outs, axis=1)


def _attend(qs, kcat, vcat, bias, sink_lo, sink_hi):
    lo = _lane_lo_mask(kcat.shape)
    T = ATT_BLOCK
    outs = []
    for msk, sink in ((lo, sink_lo), (jnp.logical_not(lo), sink_hi)):
        kh = jnp.where(msk, kcat, 0.0).astype(BF16)
        s = lax.dot_general(qs, kh, (((1,), (1,)), ((), ())), preferred_element_type=F32)
        if bias is not None:
            c0, bias_prev, bias_next = bias
            s = jnp.concatenate([s[:, :c0], s[:, c0:c0 + T] + bias_prev, s[:, c0 + T:c0 + 2 * T],
                                 s[:, c0 + 2 * T:] + bias_next], axis=1)
        m = jnp.maximum(jnp.max(s, axis=-1, keepdims=True), sink)
        p = jnp.exp(s - m)
        den = jnp.sum(p, axis=-1, keepdims=True) + jnp.exp(sink - m)
        outs.append(jnp.dot(p.astype(BF16), vcat, preferred_element_type=F32) * (1.0 / den))
    return jnp.where(_lane_lo_mask(outs[0].shape), outs[0], outs[1])


ATT_QB = 2


def _win_attn_kernel(*refs):
    nkb = ATT_QB + 2
    q_ref = refs[0]
    k_refs = refs[1:1 + nkb]
    v_refs = refs[1 + nkb:1 + 2 * nkb]
    kx_ref, vx_ref = refs[1 + 2 * nkb:3 + 2 * nkb]
    t_refs = refs[3 + 2 * nkb:3 + 3 * nkb]
    sink_ref, o_ref = refs[3 + 3 * nkb:]
    i = pl.program_id(1)
    nb = pl.num_programs(1) * ATT_QB
    T = ATT_BLOCK
    n_pair = ATT_Q // LANES
    Lc = kx_ref.shape[1]
    kx = kx_ref[0].astype(F32)
    k_rot = [_rope(k_refs[j][0].astype(F32), t_refs[j][...]) for j in range(nkb)]
    t = lax.broadcasted_iota(jnp.int32, (T, T), 0)
    s = lax.broadcasted_iota(jnp.int32, (T, T), 1)
    for sb in range(ATT_QB):
        blk = i * ATT_QB + sb
        q = _rope(q_ref[0, sb * T:(sb + 1) * T, :].astype(F32), t_refs[sb + 1][...]) * (ATT_HEAD_DIM ** -0.5)
        qs = jnp.concatenate([q[:, g * LANES:(g + 1) * LANES] for g in range(n_pair)], axis=0).astype(BF16)
        kcat = jnp.concatenate([kx, k_rot[sb], k_rot[sb + 1], k_rot[sb + 2]], axis=0)
        vcat = jnp.concatenate([vx_ref[0], v_refs[sb][0], v_refs[sb + 1][0], v_refs[sb + 2][0]], axis=0)
        bias_prev = jnp.where((s >= t) & (blk > 0), 0.0, -1e30).astype(F32)
        bias_next = jnp.where((s <= t) & (blk < nb - 1), 0.0, -1e30).astype(F32)
        bias = (Lc, jnp.concatenate([bias_prev] * n_pair, axis=0), jnp.concatenate([bias_next] * n_pair, axis=0))
        o = _attend(qs, kcat, vcat, bias, sink_ref[0], sink_ref[1])
        o_ref[0, sb * T:(sb + 1) * T, :] = jnp.concatenate(
            [o[g * T:(g + 1) * T] for g in range(n_pair)], axis=1).astype(o_ref.dtype)


def _win_attention(pl_all, pc_all, tabs, sinks):
    B, L, _ = pl_all.shape
    Lc = pc_all.shape[1] // B
    T = ATT_BLOCK
    nb = L // T
    kb, vb = COL_K // LANES, COL_V // LANES
    offs = range(-1, ATT_QB + 1)

    def blk(col, d):
        return pl.BlockSpec((1, T, LANES), lambda b, i: (b, jnp.clip(i * ATT_QB + d, 0, nb - 1), col))

    def tab(d):
        return pl.BlockSpec((T, 2 * LANES), lambda b, i: (jnp.clip(i * ATT_QB + d, 0, nb - 1), 0))

    nkb = ATT_QB + 2
    return pl.pallas_call(
        _win_attn_kernel,
        grid=(B, nb // ATT_QB),
        in_specs=[pl.BlockSpec((1, ATT_QB * T, ATT_Q), lambda b, i: (b, i, 0))] +
        [blk(kb, d) for d in offs] + [blk(vb, d) for d in offs] +
        [pl.BlockSpec((1, Lc, LANES), lambda b, i: (0, b, kb)),
         pl.BlockSpec((1, Lc, LANES), lambda b, i: (0, b, vb))] +
        [tab(d) for d in offs] +
        [pl.BlockSpec((2, ATT_Q, 1), lambda b, i: (0, 0, 0))],
        out_specs=pl.BlockSpec((1, ATT_QB * T, ATT_Q), lambda b, i: (b, i, 0)),
        out_shape=jax.ShapeDtypeStruct((B, L, ATT_Q), BF16),
        compiler_params=_cparams(("parallel", "parallel")),
        name="win_attention",
    )(*([pl_all] * (1 + 2 * nkb) + [pc_all] * 2 + [tabs] * nkb + [sinks]))


def _ctx_attn_kernel(q_ref, kx_ref, vx_ref, sink_ref, o_ref):
    Lc = q_ref.shape[1]
    n_pair = ATT_Q // LANES
    q = q_ref[0].astype(F32) * (ATT_HEAD_DIM ** -0.5)
    qs = jnp.concatenate([q[:, g * LANES:(g + 1) * LANES] for g in range(n_pair)], axis=0).astype(BF16)
    o = _attend(qs, kx_ref[0].astype(F32), vx_ref[0], None, sink_ref[0], sink_ref[1])
    o_ref[0] = jnp.concatenate([o[g * Lc:(g + 1) * Lc] for g in range(n_pair)], axis=1).astype(o_ref.dtype)


def _ctx_attention(pc_all, sinks, B):
    Lc = pc_all.shape[1] // B
    kb, vb = COL_K // LANES, COL_V // LANES
    return pl.pallas_call(
        _ctx_attn_kernel,
        grid=(B,),
        in_specs=[
            pl.BlockSpec((1, Lc, ATT_Q), lambda b: (0, b, 0)),
            pl.BlockSpec((1, Lc, LANES), lambda b: (0, b, kb)),
            pl.BlockSpec((1, Lc, LANES), lambda b: (0, b, vb)),
            pl.BlockSpec((2, ATT_Q // LANES * Lc, 1), lambda b: (0, 0, 0)),
        ],
        out_specs=pl.BlockSpec((1, Lc, ATT_Q), lambda b: (0, b, 0)),
        out_shape=jax.ShapeDtypeStruct((1, B * Lc, ATT_Q), BF16),
        compiler_params=_cparams(("parallel",)),
        name="ctx_attention",
    )(pc_all, pc_all, pc_all, sinks)


def _dot_hl(a_exact, x):
    x_hi = x.astype(BF16)
    x_lo = (x - x_hi.astype(F32)).astype(BF16)
    return jnp.dot(a_exact, x_hi, preferred_element_type=F32) + jnp.dot(a_exact, x_lo, preferred_element_type=F32)


def _mlstm_kernel(qf_l, kf_l, vf_l, gf_l, qb_l, kb_l, vb_l, gb_l,
                  qf_c, kf_c, vf_c, gf_c, qb_c, kb_c, vb_c, gb_c, bias_ref,
                  hf_ref, hb_ref, ct_ref, m_ref, *, n_ctx_chunks):
    j = pl.program_id(1)
    T = ML_CHUNK
    d = ML_HEAD_DIM
    is_ctx = j < n_ctx_chunks

    @pl.when(j == 0)
    def _():
        ct_ref[...] = jnp.zeros_like(ct_ref)
        m_ref[...] = jnp.zeros_like(m_ref)

    row = lax.broadcasted_iota(jnp.int32, (T, T), 0)
    col = lax.broadcasted_iota(jnp.int32, (T, T), 1)
    ones_td = jnp.ones((T, d), BF16)

    nt = (((1,), (1,)), ((), ()))
    sel_row = lax.broadcasted_iota(jnp.int32, (LANES, 2 * ML_HEADS * LANES), 0)
    sel_blk = lax.broadcasted_iota(jnp.int32, (LANES, 2 * ML_HEADS * LANES), 1) // LANES
    chains = []
    for bi in range(ML_NB):
        for di, (q_l, k_l, v_l, g_l, q_c, k_c, v_c, g_c) in enumerate(
                ((qf_l, kf_l, vf_l, gf_l, qf_c, kf_c, vf_c, gf_c),
                 (qb_l, kb_l, vb_l, gb_l, qb_c, kb_c, vb_c, gb_c))):
            keep = (col <= row) if di == 0 else (col >= row)
            keep_b = keep.astype(BF16)
            g = jnp.where(is_ctx, g_c[bi], g_l[bi]) + bias_ref[...]
            lane = lax.broadcasted_iota(jnp.int32, g.shape, 1)
            is_f = (lane % 8) >= 4
            gv = jnp.where(is_f, _log_sigmoid(g), g)
            gt = gv.T
            cum_c = _dot_hl(keep_b, gv)
            gt_hi = gt.astype(BF16)
            gt_lo = (gt - gt_hi.astype(F32)).astype(BF16)
            cum_r = lax.dot_general(gt_hi, keep_b, nt, preferred_element_type=F32) + \
                lax.dot_general(gt_lo, keep_b, nt, preferred_element_type=F32)
            src_col = jnp.where(sel_blk < ML_HEADS, di * 8 + ML_HEADS + sel_blk, di * 8 + sel_blk - ML_HEADS)
            sel = jnp.where(sel_row == src_col, 1.0, 0.0).astype(BF16)
            x = jnp.where(is_f, cum_c, gv)
            x_hi = x.astype(BF16)
            x_lo = (x - x_hi.astype(F32)).astype(BF16)
            full = jnp.dot(x_hi, sel, preferred_element_type=F32) + jnp.dot(x_lo, sel, preferred_element_type=F32)
            q_all = jnp.where(is_ctx, q_c[bi], q_l[bi])
            k_all = jnp.where(is_ctx, k_c[bi], k_l[bi]).astype(F32) * (d ** -0.5)
            v_all = jnp.where(is_ctx, v_c[bi], v_l[bi])
            for h in range(ML_HEADS):
                ci = di * 8 + h
                cf = di * 8 + 4 + h
                sl = slice(h * d, (h + 1) * d)
                b_full = full[:, h * LANES:(h + 1) * LANES]
                chains.append(dict(
                    idx=(bi * 2 + di) * ML_HEADS + h, keep=keep, q=q_all[:, sl], k=k_all[:, sl], v=v_all[:, sl],
                    b_full=b_full, i_full=full[:, (ML_HEADS + h) * LANES:(ML_HEADS + h + 1) * LANES],
                    b_r=cum_r[cf:cf + 1, :], i_r=gt[ci:ci + 1, :],
                    bl=b_full[T - 1:T, :] if di == 0 else b_full[0:1, :]))
    for c in chains:
        c["m_old"] = m_ref[c["idx"]]
        c["a"] = c["b_full"] + c["m_old"]
        c["dmat"] = jnp.where(c["keep"], c["b_full"] - c["b_r"] + c["i_r"], -1e30)
    for c in chains:
        c["mt"] = jnp.maximum(c["a"], jnp.broadcast_to(jnp.max(c["dmat"], axis=-1, keepdims=True), (T, T)))
    for c in chains:
        c["qk"] = lax.dot_general(c["q"], c["k"].astype(BF16), nt, preferred_element_type=F32)
    for c in chains:
        s = (c["qk"] * jnp.exp(c["dmat"] - c["mt"])).astype(BF16)
        wq = (c["q"].astype(F32) * jnp.exp(c["a"] - c["mt"])).astype(BF16)
        c["v_aug"] = jnp.concatenate([c["v"], ones_td], axis=1)
        c["ct"] = ct_ref[c["idx"]]
        lhs = jnp.concatenate([wq, s], axis=1)
        rhs = jnp.concatenate([c["ct"].astype(BF16), c["v_aug"]], axis=0)
        c["r"] = jnp.dot(lhs, rhs, preferred_element_type=F32)
    for c in chains:
        r = c["r"]
        c["h"] = r[:, :d] / jnp.maximum(jnp.abs(r[:, d:]), jnp.exp(-c["mt"]))
    for c in chains:
        src = c["bl"] - c["b_full"] + c["i_full"]
        m_new = jnp.maximum(c["bl"] + c["m_old"], jnp.max(src, axis=0, keepdims=True))
        gk = (jnp.exp(src - m_new) * c["k"]).T.astype(BF16)
        decay = jnp.exp(c["bl"] + c["m_old"] - m_new)
        ct_ref[c["idx"]] = jnp.concatenate([decay, decay], axis=1) * c["ct"] + \
            jnp.dot(gk, c["v_aug"], preferred_element_type=F32)
        m_ref[c["idx"]] = m_new
    for bi in range(ML_NB):
        for di, h_ref in enumerate((hf_ref, hb_ref)):
            base = (bi * 2 + di) * ML_HEADS
            h_ref[bi] = jnp.concatenate([chains[base + h]["h"] for h in range(ML_HEADS)], axis=1)


def _mlstm(pl_all, pc_all, gl, gc, gate_bias):
    B, L, _ = pl_all.shape
    Lc = pc_all.shape[1] // B
    T = ML_CHUNK
    nl, ncx = L // T, Lc // T
    nsteps = nl + ncx
    pc3 = pc_all.reshape(B, Lc, pc_all.shape[2])
    gc3 = gc.reshape(B, Lc, gc.shape[2])

    def lat_f(j):
        return jnp.clip(j - ncx, 0, nl - 1)

    def lat_b(j):
        return jnp.clip(nsteps - 1 - j, 0, nl - 1)

    def ctx_f(j):
        return jnp.clip(j, 0, ncx - 1)

    def ctx_b(j):
        return jnp.clip(ncx - 1 - j, 0, ncx - 1)

    def lat_specs(fn):
        base = COL_ML // ML_W
        return [pl.BlockSpec((ML_NB, T, ML_W), lambda b, j, o=o: (b, fn(j), base + o)) for o in range(3)] + \
               [pl.BlockSpec((ML_NB, T, LANES), lambda b, j: (b, fn(j), 0))]

    def ctx_specs(fn):
        base = COL_ML // ML_W
        return [pl.BlockSpec((ML_NB, T, ML_W), lambda b, j, o=o: (b, fn(j), base + o)) for o in range(3)] + \
               [pl.BlockSpec((ML_NB, T, LANES), lambda b, j: (b, fn(j), 0))]

    def out_f(b, j):
        return (b, jnp.where(j < ncx, nl + j, j - ncx), 0)

    def out_b(b, j):
        return (b, jnp.where(j < ncx, nl + ncx - 1 - j, nsteps - 1 - j), 0)

    return pl.pallas_call(
        functools.partial(_mlstm_kernel, n_ctx_chunks=ncx),
        grid=(B // ML_NB, nsteps),
        in_specs=lat_specs(lat_f) + lat_specs(lat_b) + ctx_specs(ctx_f) + ctx_specs(ctx_b) +
        [pl.BlockSpec((1, LANES), lambda b, j: (0, 0))],
        out_specs=[pl.BlockSpec((ML_NB, T, ML_W), out_f), pl.BlockSpec((ML_NB, T, ML_W), out_b)],
        out_shape=[jax.ShapeDtypeStruct((B, L + Lc, ML_W), F32)] * 2,
        scratch_shapes=[pltpu.VMEM((ML_NB * 2 * ML_HEADS, ML_HEAD_DIM, 2 * ML_HEAD_DIM), F32),
                        pltpu.VMEM((ML_NB * 2 * ML_HEADS, 1, LANES), F32)],
        compiler_params=_cparams(("parallel", "arbitrary")),
        name="mlstm",
    )(*(([pl_all] * 3 + [gl]) * 2 + ([pc3] * 3 + [gc3]) * 2 + [gate_bias]))


def _short_conv_kernel(x_ref, w_ref, b_ref, o_ref):
    x = x_ref[0].astype(F32)
    L = x.shape[0]
    row = lax.broadcasted_iota(jnp.int32, x.shape, 0)
    prev = jnp.where(row == 0, 0.0, pltpu.roll(x, 1, axis=0))
    nxt = jnp.where(row == L - 1, 0.0, pltpu.roll(x, L - 1, axis=0))
    w = w_ref[...]
    o_ref[0] = b_ref[...] + prev * w[0:1] + x * w[1:2] + nxt * w[2:3]


def _short_conv(p_all, w, b, seq_len):
    Bx, T, _ = p_all.shape
    nseq = T // seq_len
    C = w.shape[1]
    cb = COL_HY // LANES
    return pl.pallas_call(
        _short_conv_kernel,
        grid=(Bx, nseq, C // LANES),
        in_specs=[
            pl.BlockSpec((1, seq_len, LANES), lambda b, s, j: (b, s, cb + j)),
            pl.BlockSpec((3, LANES), lambda b, s, j: (0, j)),
            pl.BlockSpec((1, LANES), lambda b, s, j: (0, j)),
        ],
        out_specs=pl.BlockSpec((1, seq_len, LANES), lambda b, s, j: (b, s, j)),
        out_shape=jax.ShapeDtypeStruct((Bx, T, C), F32),
        compiler_params=_cparams(("parallel", "parallel", "parallel")),
        name="hy_short_conv",
    )(p_all, w, b.reshape(1, C))


@functools.lru_cache(maxsize=None)
def _dft_consts(L):
    N2 = HY_N2
    N = 2 * L
    N1 = N // N2
    S = N1 // 2 + 1
    k1 = np.arange(S)[:, None]
    n2 = np.arange(N2)
    tw = np.exp(-2j * np.pi * k1 * n2[None, :] / N)
    tw_tab = np.stack([np.repeat(tw.real[:, :, None], LANES, 2), np.repeat(tw.imag[:, :, None], LANES, 2)], 1)
    F = np.exp(-2j * np.pi * np.outer(n2, n2) / N2)
    M2 = np.block([[F.real, -F.imag], [F.imag, F.real]])

    def stage1(nb):
        n1 = np.arange(nb)[None, :]
        ang = 2 * np.pi * k1 * n1 / N1
        return np.cos(ang), -np.sin(ang)

    n1o = np.arange(L // N2)[:, None]
    k1o = np.arange(S)[None, :]
    ang = 2 * np.pi * n1o * k1o / N1
    wgt = np.where((k1o == 0) | (k1o == N1 // 2), 1.0, 2.0) / N
    return dict(N=N, N1=N1, S=S, stage1=stage1, tw=tw_tab.astype(np.float32),
                m2=M2.astype(np.float32).astype(BF16), m2t=M2.T.astype(np.float32).astype(BF16),
                icr=np.cos(ang) * wgt, ici=-np.sin(ang) * wgt)


def _dft_mm(m_ref, x):
    return jnp.dot(m_ref[...], x.astype(BF16), preferred_element_type=F32)


def _coef_acc(acc, c, x):
    if x is None or abs(c) < 1e-12:
        return acc
    if abs(c - 1.0) < 1e-12:
        return x if acc is None else acc + x
    if abs(c + 1.0) < 1e-12:
        return -x if acc is None else acc - x
    return c * x if acc is None else acc + c * x


def _vadd(a, b):
    return b if a is None else a if b is None else a + b


def _vsub(a, b):
    return (None if b is None else -b) if a is None else a if b is None else a - b


def _slot_groups(N1):
    half = N1 // 2
    return [(k, half - k if half - k != k else None) for k in range(half // 2 + 1)]


def _stage1_group(load, nb, cr, ci, k, kp, tw_ref, a_ref):
    N2, CT = HY_N2, HY_CT
    for r0 in range(0, N2, HY_SUB):
        if kp is None:
            ar = ai = None
            for n1 in range(nb):
                xb = load(n1, r0)
                ar = _coef_acc(ar, cr[k, n1], xb)
                ai = _coef_acc(ai, ci[k, n1], xb)
            slots = [(k, ar, ai)]
        else:
            even = [None, None]
            odd = [None, None]
            for n1 in range(nb):
                xb = load(n1, r0)
                tgt = even if n1 % 2 == 0 else odd
                tgt[0] = _coef_acc(tgt[0], cr[k, n1], xb)
                tgt[1] = _coef_acc(tgt[1], ci[k, n1], xb)
            slots = [(k, _vadd(even[0], odd[0]), _vadd(even[1], odd[1])),
                     (kp, _vsub(even[0], odd[0]), _vsub(odd[1], even[1]))]
        for idx, (kk, ar, ai) in enumerate(slots):
            zero = jnp.zeros((HY_SUB, CT), F32)
            ar = zero if ar is None else ar
            if kk > 0:
                twr = tw_ref[kk, 0, r0:r0 + HY_SUB, :]
                twi = tw_ref[kk, 1, r0:r0 + HY_SUB, :]
                if ai is None:
                    ar, ai = ar * twr, ar * twi
                else:
                    ar, ai = ar * twr - ai * twi, ar * twi + ai * twr
            ai = zero if ai is None else ai
            a_ref[r0:r0 + HY_SUB, idx * CT:(idx + 1) * CT] = ar
            a_ref[N2 + r0:N2 + r0 + HY_SUB, idx * CT:(idx + 1) * CT] = ai


def _stage1_inv_group(b_ref, nb, icr, ici, k, kp, tw_ref, acc_ref, first):
    N2, CT = HY_N2, HY_CT

    def load(idx, kk, r0):
        br = b_ref[r0:r0 + HY_SUB, idx * CT:(idx + 1) * CT]
        bi = b_ref[N2 + r0:N2 + r0 + HY_SUB, idx * CT:(idx + 1) * CT]
        if kk > 0:
            twr = tw_ref[kk, 0, r0:r0 + HY_SUB, :]
            twi = tw_ref[kk, 1, r0:r0 + HY_SUB, :]
            br, bi = br * twr + bi * twi, bi * twr - br * twi
        return br, bi

    for r0 in range(0, N2, HY_SUB):
        br, bi = load(0, k, r0)
        if kp is None:
            q_even = q_odd = (br, bi)
        else:
            br2, bi2 = load(1, kp, r0)
            q_even = (br + br2, bi - bi2)
            q_odd = (br - br2, bi + bi2)
        for n1 in range(nb):
            qr, qi = q_even if n1 % 2 == 0 else q_odd
            contrib = _coef_acc(_coef_acc(None, icr[n1, k], qr), ici[n1, k], qi)
            rows = slice(n1 * N2 + r0, n1 * N2 + r0 + HY_SUB)
            if first:
                acc_ref[rows, :] = contrib
            elif contrib is not None:
                acc_ref[rows, :] += contrib


def _filter_spec_kernel(hf_ref, hb_ref, tw_ref, m2_ref, o_ref, a_ref, *, L):
    c = _dft_consts(L)
    N2, CT = HY_N2, HY_CT
    nb = L // N2
    cr, ci = c["stage1"](nb)
    for k, kp in _slot_groups(c["N1"]):
        w = CT if kp is None else 2 * CT
        specs = []
        for h_ref in (hf_ref, hb_ref):
            _stage1_group(lambda n1, r0: h_ref[n1 * N2 + r0:n1 * N2 + r0 + HY_SUB, :], nb, cr, ci, k, kp, tw_ref, a_ref)
            specs.append(_dft_mm(m2_ref, a_ref[:, :w]))
        xf, xb = specs
        g = jnp.concatenate([xf[:N2] + xb[:N2], xf[N2:] - xb[N2:]], axis=0)
        o_ref[k] = g[:, :CT]
        if kp is not None:
            o_ref[kp] = g[:, CT:]


def _filter_spectrum(hfb, L):
    c = _dft_consts(L)
    C = hfb.shape[2]
    S, R = c["S"], 2 * HY_N2
    return pl.pallas_call(
        functools.partial(_filter_spec_kernel, L=L),
        grid=(C // HY_CT,),
        in_specs=[
            pl.BlockSpec((None, L, HY_CT), lambda j: (0, 0, j)),
            pl.BlockSpec((None, L, HY_CT), lambda j: (1, 0, j)),
            pl.BlockSpec((S, 2, HY_N2, LANES), lambda j: (0, 0, 0, 0)),
            pl.BlockSpec((R, R), lambda j: (0, 0)),
        ],
        out_specs=pl.BlockSpec((S, R, HY_CT), lambda j: (0, 0, j)),
        out_shape=jax.ShapeDtypeStruct((S, R, C), F32),
        scratch_shapes=[pltpu.VMEM((R, 2 * HY_CT), F32)],
        compiler_params=_cparams(("parallel",)),
        name="hy_filter_spectrum",
    )(hfb, hfb, jnp.asarray(c["tw"]), jnp.asarray(c["m2"]))


def _long_conv_kernel(u_ref, gate_ref, spec_ref, skip_ref, tw_ref, m2_ref, m2t_ref, o_ref, a_ref, b_ref, acc_ref,
                      *, L):
    c = _dft_consts(L)
    N2, CT = HY_N2, HY_CT
    nb = L // N2
    cr, ci = c["stage1"](nb)
    icr, ici = c["icr"] * c["N"], c["ici"] * c["N"]
    for gi_, (k, kp) in enumerate(_slot_groups(c["N1"])):
        w = CT if kp is None else 2 * CT
        _stage1_group(lambda n1, r0: u_ref[0, n1 * N2 + r0:n1 * N2 + r0 + HY_SUB, :], nb, cr, ci, k, kp, tw_ref, a_ref)
        x = _dft_mm(m2_ref, a_ref[:, :w])
        g = spec_ref[k] if kp is None else jnp.concatenate([spec_ref[k], spec_ref[kp]], axis=1)
        xr, xi, gr, gi = x[:N2], x[N2:], g[:N2], g[N2:]
        y = jnp.concatenate([xr * gr - xi * gi, xr * gi + xi * gr], axis=0)
        b_ref[:, :w] = _dft_mm(m2t_ref, y)
        _stage1_inv_group(b_ref, nb, icr, ici, k, kp, tw_ref, acc_ref, gi_ == 0)
    o_ref[0] = (gate_ref[0] * (acc_ref[...] * (1.0 / c["N"]) + skip_ref[...] * u_ref[0])).astype(o_ref.dtype)


def _long_conv(u_arr, u_col, gate_arr, gate_col, spec, spec_col, skip, L, out_dtype):
    Bx, T, _ = u_arr.shape
    nseq = T // L
    c = _dft_consts(L)
    C = skip.shape[0]
    S, R = c["S"], 2 * HY_N2
    ub, gb, sb = u_col // HY_CT, gate_col // HY_CT, spec_col // HY_CT
    return pl.pallas_call(
        functools.partial(_long_conv_kernel, L=L),
        grid=(C // HY_CT, Bx, nseq),
        in_specs=[
            pl.BlockSpec((1, L, HY_CT), lambda j, b, s: (b, s, ub + j)),
            pl.BlockSpec((1, L, HY_CT), lambda j, b, s: (b, s, gb + j)),
            pl.BlockSpec((S, R, HY_CT), lambda j, b, s: (0, 0, sb + j)),
            pl.BlockSpec((1, HY_CT), lambda j, b, s: (0, j)),
            pl.BlockSpec((S, 2, HY_N2, LANES), lambda j, b, s: (0, 0, 0, 0)),
            pl.BlockSpec((R, R), lambda j, b, s: (0, 0)),
            pl.BlockSpec((R, R), lambda j, b, s: (0, 0)),
        ],
        out_specs=pl.BlockSpec((1, L, HY_CT), lambda j, b, s: (b, s, j)),
        out_shape=jax.ShapeDtypeStruct((Bx, T, C), out_dtype),
        scratch_shapes=[pltpu.VMEM((R, 2 * HY_CT), F32), pltpu.VMEM((R, 2 * HY_CT), F32),
                        pltpu.VMEM((L, HY_CT), F32)],
        compiler_params=_cparams(("parallel", "parallel", "parallel")),
        name="hy_long_conv",
    )(u_arr, gate_arr, spec, skip.reshape(1, C), jnp.asarray(c["tw"]), jnp.asarray(c["m2"]), jnp.asarray(c["m2t"]))


def _dot3(a, b):
    a_hi = a.astype(BF16)
    a_lo = (a - a_hi.astype(F32)).astype(BF16)
    b_hi = b.astype(BF16)
    b_lo = (b - b_hi.astype(F32)).astype(BF16)
    return jnp.dot(a_hi, b_hi, preferred_element_type=F32) + \
        (jnp.dot(a_hi, b_lo, preferred_element_type=F32) + jnp.dot(a_lo, b_hi, preferred_element_type=F32))


def _filter_gen_kernel(z_ref, w1_ref, b1_ref, w2_ref, b2_ref, freq_ref, w3_ref, delta_ref, o_ref, h_ref):
    L = z_ref.shape[0]

    @pl.when(pl.program_id(0) == 0)
    def _():
        h = jnp.sin(freq_ref[0:1, :] * (_dot3(z_ref[...], w1_ref[...]) + b1_ref[...]))
        h_ref[...] = jnp.sin(freq_ref[1:2, :] * (_dot3(h, w2_ref[...]) + b2_ref[...]))

    t_norm = lax.broadcasted_iota(jnp.int32, (L, LANES), 0).astype(F32) / max(L - 1, 1)
    h = _dot3(h_ref[...], w3_ref[...]) * jnp.exp(-t_norm * delta_ref[...])
    o_ref[...] = h / jnp.sum(jnp.abs(h), axis=0, keepdims=True)


def _hyena_filters(L, w1, b1, w2, b2, w3, freq):
    t = jnp.arange(L, dtype=F32)
    t_norm = t / max(L - 1, 1)
    w = 2.0 * math.pi * t / L
    bands = jnp.linspace(1e-4, HY_EMB_BANDS - 1, HY_EMB_BANDS, dtype=F32)
    z = jnp.concatenate([t_norm[:, None], jnp.cos(w[:, None] * bands), -jnp.sin(w[:, None] * bands)], axis=-1)
    deltas = jnp.abs(jnp.linspace(math.log(HY_DECAY_TARGET) / HY_DECAY_SLOW,
                                  math.log(HY_DECAY_TARGET) / HY_DECAY_FAST, HY_WIDTH, dtype=F32))
    E, FW = z.shape[1], w1.shape[1]
    nct = HY_WIDTH // LANES
    return pl.pallas_call(
        _filter_gen_kernel,
        grid=(HY_ORDER * 2 * nct,),
        in_specs=[
            pl.BlockSpec((L, E), lambda j: (0, 0)),
            pl.BlockSpec((E, FW), lambda j: (0, 0)),
            pl.BlockSpec((1, FW), lambda j: (0, 0)),
            pl.BlockSpec((FW, FW), lambda j: (0, 0)),
            pl.BlockSpec((1, FW), lambda j: (0, 0)),
            pl.BlockSpec((2, FW), lambda j: (0, 0)),
            pl.BlockSpec((FW, LANES), lambda j: (0, j)),
            pl.BlockSpec((1, LANES), lambda j: (0, j % nct)),
        ],
        out_specs=pl.BlockSpec((None, L, LANES), lambda j: ((j // nct) % 2, 0, (j // (2 * nct)) * nct + j % nct)),
        out_shape=jax.ShapeDtypeStruct((2, L, HY_ORDER * HY_WIDTH), F32),
        scratch_shapes=[pltpu.VMEM((L, FW), F32)],
        compiler_params=_cparams(("arbitrary",)),
        name="hy_filter_gen",
    )(z, w1, b1.reshape(1, FW), w2, b2.reshape(1, FW), freq, w3, deltas.reshape(1, HY_WIDTH))


def _hyena_spectra(L, hy_fp):
    return _filter_spectrum(_hyena_filters(L, *hy_fp), L)


def _hyena(p_all, sw, sb, spec, skip, L):
    u3 = _short_conv(p_all, sw, sb, L)
    W = HY_WIDTH
    z1 = _long_conv(u3, 2 * W, u3, 0, spec, 0, skip[0], L, F32)
    return _long_conv(z1, 0, u3, W, spec, W, skip[1], L, BF16)


def _merge_kernel(x_ref, gs_ref, sh_ref, att_ref, hf_ref, hb_ref, op_ref, hy_ref, mg_ref, wb_ref, wg_ref, wo_ref,
                  g_ref, o_ref):
    d = ML_HEAD_DIM
    D = D_MODEL
    x = x_ref[0]
    h1 = _rms_mod(x, gs_ref[0], sh_ref[0])
    hsum = hf_ref[0] + hb_ref[0]
    parts = []
    for hh in range(ML_HEADS):
        hs = hsum[:, hh * d:(hh + 1) * d]
        parts.append(hs * lax.rsqrt(jnp.mean(hs * hs, axis=-1, keepdims=True) + EPS))
    mls = jnp.concatenate(parts, axis=1) * mg_ref[...] * _sigmoid(op_ref[0].astype(F32))
    y = None
    for r, br in enumerate((att_ref[0], mls.astype(BF16), hy_ref[0])):
        gate = _sigmoid(jnp.dot(h1, wg_ref[:, r * D:(r + 1) * D], preferred_element_type=F32))
        term = gate * jnp.dot(br, wb_ref[r], preferred_element_type=F32)
        y = term if y is None else y + term
    o_ref[0] = x + g_ref[0] * jnp.dot(y.astype(BF16), wo_ref[...], preferred_element_type=F32)


def _merge(x, gs, sh, att, hf, hb, p_all, hy, ml_g, wb, wg, wo, g, tm):
    Bx, T, D = x.shape
    W = ML_W
    row = lambda b, i: (b, i, 0)
    vec = lambda b, i: (b, 0, 0)
    return pl.pallas_call(
        _merge_kernel,
        grid=(Bx, T // tm),
        in_specs=[
            pl.BlockSpec((1, tm, D), row),
            pl.BlockSpec((1, 1, D), vec),
            pl.BlockSpec((1, 1, D), vec),
            pl.BlockSpec((1, tm, W), row),
            pl.BlockSpec((1, tm, W), row),
            pl.BlockSpec((1, tm, W), row),
            pl.BlockSpec((1, tm, W), lambda b, i: (b, i, COL_ML // W + 3)),
            pl.BlockSpec((1, tm, W), row),
            pl.BlockSpec((1, W), lambda b, i: (0, 0)),
            pl.BlockSpec((N_BRANCH, W, D), lambda b, i: (0, 0, 0)),
            pl.BlockSpec((D, N_BRANCH * D), lambda b, i: (0, 0)),
            pl.BlockSpec((D, D), lambda b, i: (0, 0)),
            pl.BlockSpec((1, 1, D), vec),
        ],
        out_specs=pl.BlockSpec((1, tm, D), row),
        out_shape=jax.ShapeDtypeStruct((Bx, T, D), F32),
        compiler_params=_cparams(("parallel", "parallel")),
        name="merge",
    )(x, gs, sh, att, hf, hb, p_all, hy, ml_g.reshape(1, W), wb, wg, wo, g)


FFN_TC = 1408
FFN_HALO = 8


def _ffn_kernel(x_ref, xp_ref, xn_ref, gs_ref, sh_ref, wg_ref, wv_ref, cwg_ref, cwv_ref, cbg_ref, cbv_ref,
                wd_ref, g_ref, o_ref, h_ref, hh_ref, acc_ref, *, tiles_per_seq):
    i = pl.program_id(1)
    j = pl.program_id(2)
    tm = x_ref.shape[1]

    @pl.when(j == 0)
    def _():
        h_ref[...] = _rms_mod(x_ref[0], gs_ref[0], sh_ref[0])
        halo = jnp.concatenate([xp_ref[0], xn_ref[0]], axis=0)
        hh_ref[...] = _rms_mod(halo, gs_ref[0], sh_ref[0])

    first = (i % tiles_per_seq) == 0
    last = (i % tiles_per_seq) == tiles_per_seq - 1
    row8 = lax.broadcasted_iota(jnp.int32, (8, FFN_TC), 0)

    def conv_half(w_ref, cw_ref, cb_ref):
        u = jnp.dot(h_ref[...], w_ref[...], preferred_element_type=F32)
        uh = jnp.dot(hh_ref[...], w_ref[...], preferred_element_type=F32)
        pr = jnp.where(first, 0.0, uh[FFN_HALO - 1:FFN_HALO])
        nx = jnp.where(last, 0.0, uh[FFN_HALO:FFN_HALO + 1])
        prev = pltpu.roll(u, 1, axis=0)
        nxt = pltpu.roll(u, tm - 1, axis=0)
        prev = jnp.concatenate([jnp.where(row8 == 0, pr, prev[:8]), prev[8:]], axis=0)
        nxt = jnp.concatenate([nxt[:tm - 8], jnp.where(row8 == 7, nx, nxt[tm - 8:])], axis=0)
        cw = cw_ref[...]
        return cb_ref[...] + prev * cw[0:1] + u * cw[1:2] + nxt * cw[2:3]

    gate = conv_half(wg_ref, cwg_ref, cbg_ref)
    val = conv_half(wv_ref, cwv_ref, cbv_ref)
    act = (gate * _sigmoid(gate) * val).astype(BF16)
    part = jnp.dot(act, wd_ref[...], preferred_element_type=F32)

    @pl.when(j == 0)
    def _():
        acc_ref[...] = part

    @pl.when(j > 0)
    def _():
        acc_ref[...] += part

    @pl.when(j == pl.num_programs(2) - 1)
    def _():
        o_ref[0] = x_ref[0] + g_ref[0] * acc_ref[...]


def _conv_ffn(x, gs, sh, wu, cw, cb, wd, g, seq_len):
    Bx, T, D = x.shape
    tm = min(seq_len, 512)
    nj = D_FF // FFN_TC
    hb = tm // FFN_HALO
    nh = T // FFN_HALO
    col = lambda b, i, j: (0, j)
    col2 = lambda b, i, j: (0, nj + j)
    vec = lambda b, i, j: (b, 0, 0)
    return pl.pallas_call(
        functools.partial(_ffn_kernel, tiles_per_seq=seq_len // tm),
        grid=(Bx, T // tm, nj),
        in_specs=[
            pl.BlockSpec((1, tm, D), lambda b, i, j: (b, i, 0)),
            pl.BlockSpec((1, FFN_HALO, D), lambda b, i, j: (b, jnp.maximum(i * hb - 1, 0), 0)),
            pl.BlockSpec((1, FFN_HALO, D), lambda b, i, j: (b, jnp.minimum((i + 1) * hb, nh - 1), 0)),
            pl.BlockSpec((1, 1, D), vec),
            pl.BlockSpec((1, 1, D), vec),
            pl.BlockSpec((D, FFN_TC), col), pl.BlockSpec((D, FFN_TC), col2),
            pl.BlockSpec((3, FFN_TC), col), pl.BlockSpec((3, FFN_TC), col2),
            pl.BlockSpec((1, FFN_TC), col), pl.BlockSpec((1, FFN_TC), col2),
            pl.BlockSpec((FFN_TC, D), lambda b, i, j: (j, 0)),
            pl.BlockSpec((1, 1, D), vec),
        ],
        out_specs=pl.BlockSpec((1, tm, D), lambda b, i, j: (b, i, 0)),
        out_shape=jax.ShapeDtypeStruct((Bx, T, D), F32),
        scratch_shapes=[pltpu.VMEM((tm, D), BF16), pltpu.VMEM((2 * FFN_HALO, D), BF16), pltpu.VMEM((tm, D), F32)],
        compiler_params=_cparams(("parallel", "parallel", "arbitrary")),
        name="conv_ffn",
    )(x, x, x, gs, sh, wu, wu, cw, cw, cb.reshape(1, -1), cb.reshape(1, -1), wd, g)


def _pair_perm():
    hd, half = ATT_HEAD_DIM, ATT_HEAD_DIM // 2
    n_pair = ATT_Q // LANES
    qperm = []
    for p in range(n_pair):
        for sub in range(4):
            head = p if sub % 2 == 0 else n_pair + p
            qperm += [head * hd + (sub // 2) * half + dd for dd in range(half)]
    kperm = []
    for sub in range(4):
        kperm += [(sub % 2) * hd + (sub // 2) * half + dd for dd in range(half)]
    return np.asarray(qperm), np.asarray(kperm)


def _rope_tables(L):
    rows = L // GRID_W
    row = jnp.repeat(jnp.arange(rows, dtype=F32), GRID_W)
    col = jnp.tile(jnp.arange(GRID_W, dtype=F32), rows)
    nf = ATT_HEAD_DIM // 4
    inv = ROPE_THETA ** (-jnp.arange(nf, dtype=F32) / nf)
    ang = jnp.concatenate([row[:, None] * inv, col[:, None] * inv], axis=-1)
    cos, sin = jnp.cos(ang), jnp.sin(ang)
    return jnp.concatenate([cos, cos, cos, cos, -sin, -sin, sin, sin], axis=1)


def _prep_w_in(w):
    qperm, kperm = _pair_perm()
    o = IN_OFFSETS
    w = w.astype(BF16)
    cols = [w[:, :ATT_Q][:, qperm], w[:, o[2]:o[6]], w[:, o[7]:o[8]],
            w[:, o[0]:o[1]][:, kperm], w[:, o[1]:o[2]][:, kperm]]
    wc = jnp.concatenate(cols, axis=1)
    w_gate = jnp.pad(w[:, o[6]:o[7]], ((0, 0), (0, LANES - ML_GATES)))
    return jnp.pad(wc, ((0, 0), (0, IN_PAD - wc.shape[1]))), w_gate, w[:, o[8]:]


def _sink_cols(sink, rows_per_pair):
    n_pair = ATT_Q // LANES
    lo = jnp.repeat(sink[:n_pair], rows_per_pair)
    hi = jnp.repeat(sink[n_pair:], rows_per_pair)
    return jnp.stack([lo, hi])[:, :, None]


def kernel(x, c, ctx, c_ctx, ada_w, ada_b, norm1_g, norm2_g, w_in, att_sink, ml_gate_b, ml_norm_g,
           hy_short_w, hy_short_b, hy_w1, hy_b1, hy_w2, hy_b2, hy_w3, hy_freq, hy_skip,
           w_branch, w_out, w_up, ffn_conv_w, ffn_conv_b, w_down, final_g):
    B, L, D = x.shape
    Lc = ctx.shape[1]
    qperm, _ = _pair_perm()
    tabs = _rope_tables(L)
    sc_rows = jnp.concatenate([jax.nn.silu(c), jax.nn.silu(c_ctx)[None], jnp.zeros((8 - B - 1, D), F32)], axis=0)
    xl = x
    xc = ctx.reshape(1, B * Lc, D)
    for l in range(DEPTH):
        need_ctx = l < DEPTH - 1
        mod = _ada_proj(sc_rows, ada_w[l], ada_b[l])
        sh1, sc1, g1, sh2, sc2, g2 = [m[:B, None, :] for m in jnp.split(mod, 6, axis=-1)]
        csh1, csc1, cg1, csh2, csc2, cg2 = [m[B:B + 1, None, :] for m in jnp.split(mod, 6, axis=-1)]
        n1 = norm1_g[l][None, None, :]
        n2 = norm2_g[l][None, None, :]
        w_in_b, w_gate, w_bg = _prep_w_in(w_in[l])
        wb_l = w_branch[l].astype(BF16)
        wb = jnp.concatenate([wb_l[:1][:, qperm], wb_l[1:]], axis=0)
        wo = w_out[l].astype(BF16)
        wu = w_up[l].astype(BF16)
        wd = w_down[l].astype(BF16)
        gate_bias = jnp.pad(ml_gate_b[l].reshape(1, ML_GATES), ((0, 0), (0, LANES - ML_GATES)))
        hy_fp = (hy_w1[l], hy_b1[l], hy_w2[l], hy_b2[l], hy_w3[l], hy_freq[l])

        pl_all, gl = _norm_proj(xl, n1 * (1.0 + sc1), sh1, w_in_b, IN_TN, w_gate)
        pc_all, gc = _norm_proj(xc, n1 * (1.0 + csc1), csh1, w_in_b, IN_TN, w_gate)
        att_l = _win_attention(pl_all, pc_all, tabs, _sink_cols(att_sink[l], ATT_BLOCK))
        hf, hb = _mlstm(pl_all, pc_all, gl, gc, gate_bias)
        hy_l = _hyena(pl_all, hy_short_w[l], hy_short_b[l], _hyena_spectra(L, hy_fp), hy_skip[l], L)
        xl = _merge(xl, n1 * (1.0 + sc1), sh1, att_l, hf, hb, pl_all, hy_l, ml_norm_g[l], wb, w_bg, wo, g1, 512)
        xl = _conv_ffn(xl, n2 * (1.0 + sc2), sh2, wu, ffn_conv_w[l], ffn_conv_b[l], wd, g2, L)
        if need_ctx:
            att_c = _ctx_attention(pc_all, _sink_cols(att_sink[l], Lc), B)
            hy_c = _hyena(pc_all, hy_short_w[l], hy_short_b[l], _hyena_spectra(Lc, hy_fp), hy_skip[l], Lc)
            hfc = hf[:, L:].reshape(1, B * Lc, ML_W)
            hbc = hb[:, L:].reshape(1, B * Lc, ML_W)
            xc = _merge(xc, n1 * (1.0 + csc1), csh1, att_c, hfc, hbc, pc_all, hy_c, ml_norm_g[l], wb, w_bg, wo,
                        cg1, Lc)
            xc = _conv_ffn(xc, n2 * (1.0 + csc2), csh2, wu, ffn_conv_w[l], ffn_conv_b[l], wd, cg2, Lc)
    return _final_norm(xl, final_g)
```

```python
import functools
import math

import numpy as np
import jax
import jax.numpy as jnp
from jax import lax
from jax.experimental import pallas as pl
from jax.experimental.pallas import tpu as pltpu

F32 = jnp.float32
BF16 = jnp.bfloat16

D_MODEL = 1024
DEPTH = 4
GRID_W = 64
EPS = 1e-6
ATT_HEADS = 8
ATT_KV_HEADS = 2
ATT_HEAD_DIM = 64
ATT_BLOCK = 128
ROPE_THETA = 10000.0
ML_HEADS = 4
ML_HEAD_DIM = 128
ML_CHUNK = 128
ML_NB = 2
HY_WIDTH = 512
HY_ORDER = 2
HY_EMB_BANDS = 16
HY_DECAY_TARGET = 1e-2
HY_DECAY_FAST = 0.3
HY_DECAY_SLOW = 1.5
D_FF = 2816
N_BRANCH = 3
LANES = 128

ATT_Q = ATT_HEADS * ATT_HEAD_DIM
ATT_KV = ATT_KV_HEADS * ATT_HEAD_DIM
ML_W = ML_HEADS * ML_HEAD_DIM
ML_GATES = 2 * 2 * ML_HEADS
IN_SIZES = (ATT_Q, ATT_KV, ATT_KV, ML_W, ML_W, ML_W, ML_W, ML_GATES, 3 * HY_WIDTH, N_BRANCH * D_MODEL)
IN_OFFSETS = tuple(int(o) for o in np.cumsum(IN_SIZES)[:-1])

COL_Q = 0
COL_ML = COL_Q + ATT_Q
COL_HY = COL_ML + 4 * ML_W
COL_K = COL_HY + 3 * HY_WIDTH
COL_V = COL_K + ATT_KV
IN_PAD = 4608
IN_TN = 1536
IN_TM = 2048

HY_N2 = 256
HY_CT = 128
HY_SUB = 32

VMEM_LIMIT = 56 * 1024 * 1024


def _cparams(sem):
    return pltpu.CompilerParams(dimension_semantics=sem, vmem_limit_bytes=VMEM_LIMIT)


def _sigmoid(x):
    return 0.5 * jnp.tanh(0.5 * x) + 0.5


def _log_sigmoid(x):
    return jnp.minimum(x, 0.0) - jnp.log(1.0 + jnp.exp(-jnp.abs(x)))


def _rms_mod(x, gs, sh):
    ms = jnp.mean(x * x, axis=-1, keepdims=True)
    return (x * lax.rsqrt(ms + EPS) * gs + sh).astype(BF16)


def _norm_proj_kernel(x_ref, gs_ref, sh_ref, w_ref, *rest, with_aux):
    if with_aux:
        wa_ref, o_ref, oa_ref, h_ref = rest
    else:
        o_ref, h_ref = rest

    @pl.when(pl.program_id(2) == 0)
    def _():
        h = _rms_mod(x_ref[0], gs_ref[0], sh_ref[0])
        h_ref[...] = h
        if with_aux:
            oa_ref[0] = jnp.dot(h, wa_ref[...], preferred_element_type=F32)

    o_ref[0] = jnp.dot(h_ref[...], w_ref[...], preferred_element_type=F32).astype(o_ref.dtype)


def _norm_proj(x, gs, sh, w, tn, w_aux=None):
    B, T, D = x.shape
    N = w.shape[1]
    tm = min(T, IN_TM)
    in_specs = [
        pl.BlockSpec((1, tm, D), lambda b, i, j: (b, i, 0)),
        pl.BlockSpec((1, 1, D), lambda b, i, j: (b, 0, 0)),
        pl.BlockSpec((1, 1, D), lambda b, i, j: (b, 0, 0)),
        pl.BlockSpec((D, tn), lambda b, i, j: (0, j)),
    ]
    out_specs = [pl.BlockSpec((1, tm, tn), lambda b, i, j: (b, i, j))]
    out_shape = [jax.ShapeDtypeStruct((B, T, N), BF16)]
    args = [x, gs, sh, w]
    if w_aux is not None:
        na = w_aux.shape[1]
        in_specs.append(pl.BlockSpec((D, na), lambda b, i, j: (0, 0)))
        out_specs.append(pl.BlockSpec((1, tm, na), lambda b, i, j: (b, i, 0)))
        out_shape.append(jax.ShapeDtypeStruct((B, T, na), F32))
        args.append(w_aux)
    outs = pl.pallas_call(
        functools.partial(_norm_proj_kernel, with_aux=w_aux is not None),
        grid=(B, T // tm, N // tn),
        in_specs=in_specs,
        out_specs=out_specs,
        out_shape=out_shape,
        scratch_shapes=[pltpu.VMEM((tm, D), BF16)],
        compiler_params=_cparams(("parallel", "parallel", "arbitrary")),
        name="norm_proj",
    )(*args)
    return outs if w_aux is not None else outs[0]


def _small_proj_kernel(a_ref, w_ref, b_ref, o_ref):
    a = a_ref[...]
    a_hi = a.astype(BF16)
    a_lo = (a - a_hi.astype(F32)).astype(BF16)
    w = w_ref[...]
    w_hi = w.astype(BF16)
    w_lo = (w - w_hi.astype(F32)).astype(BF16)
    acc = jnp.dot(a_hi, w_hi, preferred_element_type=F32)
    acc = acc + (jnp.dot(a_hi, w_lo, preferred_element_type=F32) + jnp.dot(a_lo, w_hi, preferred_element_type=F32))
    o_ref[...] = acc + b_ref[...]


def _ada_proj(a, w, b):
    M, D = a.shape
    N = w.shape[1]
    tn = 1024
    return pl.pallas_call(
        _small_proj_kernel,
        grid=(N // tn,),
        in_specs=[
            pl.BlockSpec((M, D), lambda j: (0, 0)),
            pl.BlockSpec((D, tn), lambda j: (0, j)),
            pl.BlockSpec((1, tn), lambda j: (0, j)),
        ],
        out_specs=pl.BlockSpec((M, tn), lambda j: (0, j)),
        out_shape=jax.ShapeDtypeStruct((M, N), F32),
        compiler_params=_cparams(("parallel",)),
        name="ada_proj",
    )(a, w, b.reshape(1, N))


def _final_norm_kernel(x_ref, g_ref, o_ref):
    x = x_ref[0]
    ms = jnp.mean(x * x, axis=-1, keepdims=True)
    o_ref[0] = x * lax.rsqrt(ms + EPS) * g_ref[...]


def _final_norm(x, g):
    B, T, D = x.shape
    tm = 512
    return pl.pallas_call(
        _final_norm_kernel,
        grid=(B, T // tm),
        in_specs=[
            pl.BlockSpec((1, tm, D), lambda b, i: (b, i, 0)),
            pl.BlockSpec((1, D), lambda b, i: (0, 0)),
        ],
        out_specs=pl.BlockSpec((1, tm, D), lambda b, i: (b, i, 0)),
        out_shape=jax.ShapeDtypeStruct((B, T, D), F32),
        compiler_params=_cparams(("parallel", "parallel")),
        name="final_norm",
    )(x, g.reshape(1, D))


def _lane_lo_mask(shape):
    lane = lax.broadcasted_iota(jnp.int32, shape, len(shape) - 1)
    return (lane % 64) < 32


def _rope(x, tab):
    c = tab[:, :LANES]
    s = tab[:, LANES:]
    outs = []
    for g in range(x.shape[1] // LANES):
        xg = x[:, g * LANES:(g + 1) * LANES]
        outs.append(xg * c + pltpu.roll(xg, 64, axis=1) * s)
    return outs[0] if len(outs) == 1 else jnp.concatenate(outs, axis=1)


def _attend(qs, kcat, vcat, bias, sink_lo, sink_hi):
    lo = _lane_lo_mask(kcat.shape)
    T = ATT_BLOCK
    outs = []
    for msk, sink in ((lo, sink_lo), (jnp.logical_not(lo), sink_hi)):
        kh = jnp.where(msk, kcat, 0.0).astype(BF16)
        s = lax.dot_general(qs, kh, (((1,), (1,)), ((), ())), preferred_element_type=F32)
        if bias is not None:
            c0, bias_prev, bias_next = bias
            s = jnp.concatenate([s[:, :c0], s[:, c0:c0 + T] + bias_prev, s[:, c0 + T:c0 + 2 * T],
                                 s[:, c0 + 2 * T:] + bias_next], axis=1)
        m = jnp.maximum(jnp.max(s, axis=-1, keepdims=True), sink)
        p = jnp.exp(s - m)
        den = jnp.sum(p, axis=-1, keepdims=True) + jnp.exp(sink - m)
        outs.append(jnp.dot(p.astype(BF16), vcat, preferred_element_type=F32) * (1.0 / den))
    return jnp.where(_lane_lo_mask(outs[0].shape), outs[0], outs[1])


ATT_QB = 2


def _win_attn_kernel(*refs):
    nkb = ATT_QB + 2
    q_ref = refs[0]
    k_refs = refs[1:1 + nkb]
    v_refs = refs[1 + nkb:1 + 2 * nkb]
    kx_ref, vx_ref = refs[1 + 2 * nkb:3 + 2 * nkb]
    t_refs = refs[3 + 2 * nkb:3 + 3 * nkb]
    sink_ref, o_ref = refs[3 + 3 * nkb:]
    i = pl.program_id(1)
    nb = pl.num_programs(1) * ATT_QB
    T = ATT_BLOCK
    n_pair = ATT_Q // LANES
    Lc = kx_ref.shape[1]
    kx = kx_ref[0].astype(F32)
    k_rot = [_rope(k_refs[j][0].astype(F32), t_refs[j][...]) for j in range(nkb)]
    t = lax.broadcasted_iota(jnp.int32, (T, T), 0)
    s = lax.broadcasted_iota(jnp.int32, (T, T), 1)
    for sb in range(ATT_QB):
        blk = i * ATT_QB + sb
        q = _rope(q_ref[0, sb * T:(sb + 1) * T, :].astype(F32), t_refs[sb + 1][...]) * (ATT_HEAD_DIM ** -0.5)
        qs = jnp.concatenate([q[:, g * LANES:(g + 1) * LANES] for g in range(n_pair)], axis=0).astype(BF16)
        kcat = jnp.concatenate([kx, k_rot[sb], k_rot[sb + 1], k_rot[sb + 2]], axis=0)
        vcat = jnp.concatenate([vx_ref[0], v_refs[sb][0], v_refs[sb + 1][0], v_refs[sb + 2][0]], axis=0)
        bias_prev = jnp.where((s >= t) & (blk > 0), 0.0, -1e30).astype(F32)
        bias_next = jnp.where((s <= t) & (blk < nb - 1), 0.0, -1e30).astype(F32)
        bias = (Lc, jnp.concatenate([bias_prev] * n_pair, axis=0), jnp.concatenate([bias_next] * n_pair, axis=0))
        o = _attend(qs, kcat, vcat, bias, sink_ref[0], sink_ref[1])
        o_ref[0, sb * T:(sb + 1) * T, :] = jnp.concatenate(
            [o[g * T:(g + 1) * T] for g in range(n_pair)], axis=1).astype(o_ref.dtype)


def _win_attention(pl_all, pc_all, tabs, sinks):
    B, L, _ = pl_all.shape
    Lc = pc_all.shape[1] // B
    T = ATT_BLOCK
    nb = L // T
    kb, vb = COL_K // LANES, COL_V // LANES
    offs = range(-1, ATT_QB + 1)

    def blk(col, d):
        return pl.BlockSpec((1, T, LANES), lambda b, i: (b, jnp.clip(i * ATT_QB + d, 0, nb - 1), col))

    def tab(d):
        return pl.BlockSpec((T, 2 * LANES), lambda b, i: (jnp.clip(i * ATT_QB + d, 0, nb - 1), 0))

    nkb = ATT_QB + 2
    return pl.pallas_call(
        _win_attn_kernel,
        grid=(B, nb // ATT_QB),
        in_specs=[pl.BlockSpec((1, ATT_QB * T, ATT_Q), lambda b, i: (b, i, 0))] +
        [blk(kb, d) for d in offs] + [blk(vb, d) for d in offs] +
        [pl.BlockSpec((1, Lc, LANES), lambda b, i: (0, b, kb)),
         pl.BlockSpec((1, Lc, LANES), lambda b, i: (0, b, vb))] +
        [tab(d) for d in offs] +
        [pl.BlockSpec((2, ATT_Q, 1), lambda b, i: (0, 0, 0))],
        out_specs=pl.BlockSpec((1, ATT_QB * T, ATT_Q), lambda b, i: (b, i, 0)),
        out_shape=jax.ShapeDtypeStruct((B, L, ATT_Q), BF16),
        compiler_params=_cparams(("parallel", "parallel")),
        name="win_attention",
    )(*([pl_all] * (1 + 2 * nkb) + [pc_all] * 2 + [tabs] * nkb + [sinks]))


def _ctx_attn_kernel(q_ref, kx_ref, vx_ref, sink_ref, o_ref):
    Lc = q_ref.shape[1]
    n_pair = ATT_Q // LANES
    q = q_ref[0].astype(F32) * (ATT_HEAD_DIM ** -0.5)
    qs = jnp.concatenate([q[:, g * LANES:(g + 1) * LANES] for g in range(n_pair)], axis=0).astype(BF16)
    o = _attend(qs, kx_ref[0].astype(F32), vx_ref[0], None, sink_ref[0], sink_ref[1])
    o_ref[0] = jnp.concatenate([o[g * Lc:(g + 1) * Lc] for g in range(n_pair)], axis=1).astype(o_ref.dtype)


def _ctx_attention(pc_all, sinks, B):
    Lc = pc_all.shape[1] // B
    kb, vb = COL_K // LANES, COL_V // LANES
    return pl.pallas_call(
        _ctx_attn_kernel,
        grid=(B,),
        in_specs=[
            pl.BlockSpec((1, Lc, ATT_Q), lambda b: (0, b, 0)),
            pl.BlockSpec((1, Lc, LANES), lambda b: (0, b, kb)),
            pl.BlockSpec((1, Lc, LANES), lambda b: (0, b, vb)),
            pl.BlockSpec((2, ATT_Q // LANES * Lc, 1), lambda b: (0, 0, 0)),
        ],
        out_specs=pl.BlockSpec((1, Lc, ATT_Q), lambda b: (0, b, 0)),
        out_shape=jax.ShapeDtypeStruct((1, B * Lc, ATT_Q), BF16),
        compiler_params=_cparams(("parallel",)),
        name="ctx_attention",
    )(pc_all, pc_all, pc_all, sinks)


def _dot_hl(a_exact, x):
    x_hi = x.astype(BF16)
    x_lo = (x - x_hi.astype(F32)).astype(BF16)
    return jnp.dot(a_exact, x_hi, preferred_element_type=F32) + jnp.dot(a_exact, x_lo, preferred_element_type=F32)


def _mlstm_kernel(qf_l, kf_l, vf_l, gf_l, qb_l, kb_l, vb_l, gb_l,
                  qf_c, kf_c, vf_c, gf_c, qb_c, kb_c, vb_c, gb_c, bias_ref,
                  hf_ref, hb_ref, ct_ref, m_ref, *, n_ctx_chunks):
    j = pl.program_id(1)
    T = ML_CHUNK
    d = ML_HEAD_DIM
    is_ctx = j < n_ctx_chunks

    @pl.when(j == 0)
    def _():
        ct_ref[...] = jnp.zeros_like(ct_ref)
        m_ref[...] = jnp.zeros_like(m_ref)

    row = lax.broadcasted_iota(jnp.int32, (T, T), 0)
    col = lax.broadcasted_iota(jnp.int32, (T, T), 1)
    ones_td = jnp.ones((T, d), BF16)

    nt = (((1,), (1,)), ((), ()))
    sel_row = lax.broadcasted_iota(jnp.int32, (LANES, 2 * ML_HEADS * LANES), 0)
    sel_blk = lax.broadcasted_iota(jnp.int32, (LANES, 2 * ML_HEADS * LANES), 1) // LANES
    chains = []
    for bi in range(ML_NB):
        for di, (q_l, k_l, v_l, g_l, q_c, k_c, v_c, g_c) in enumerate(
                ((qf_l, kf_l, vf_l, gf_l, qf_c, kf_c, vf_c, gf_c),
                 (qb_l, kb_l, vb_l, gb_l, qb_c, kb_c, vb_c, gb_c))):
            keep = (col <= row) if di == 0 else (col >= row)
            keep_b = keep.astype(BF16)
            g = jnp.where(is_ctx, g_c[bi], g_l[bi]) + bias_ref[...]
            lane = lax.broadcasted_iota(jnp.int32, g.shape, 1)
            is_f = (lane % 8) >= 4
            gv = jnp.where(is_f, _log_sigmoid(g), g)
            gt = gv.T
            cum_c = _dot_hl(keep_b, gv)
            gt_hi = gt.astype(BF16)
            gt_lo = (gt - gt_hi.astype(F32)).astype(BF16)
            cum_r = lax.dot_general(gt_hi, keep_b, nt, preferred_element_type=F32) + \
                lax.dot_general(gt_lo, keep_b, nt, preferred_element_type=F32)
            src_col = jnp.where(sel_blk < ML_HEADS, di * 8 + ML_HEADS + sel_blk, di * 8 + sel_blk - ML_HEADS)
            sel = jnp.where(sel_row == src_col, 1.0, 0.0).astype(BF16)
            x = jnp.where(is_f, cum_c, gv)
            x_hi = x.astype(BF16)
            x_lo = (x - x_hi.astype(F32)).astype(BF16)
            full = jnp.dot(x_hi, sel, preferred_element_type=F32) + jnp.dot(x_lo, sel, preferred_element_type=F32)
            q_all = jnp.where(is_ctx, q_c[bi], q_l[bi])
            k_all = jnp.where(is_ctx, k_c[bi], k_l[bi]).astype(F32) * (d ** -0.5)
            v_all = jnp.where(is_ctx, v_c[bi], v_l[bi])
            for h in range(ML_HEADS):
                ci = di * 8 + h
                cf = di * 8 + 4 + h
                sl = slice(h * d, (h + 1) * d)
                b_full = full[:, h * LANES:(h + 1) * LANES]
                chains.append(dict(
                    idx=(bi * 2 + di) * ML_HEADS + h, keep=keep, q=q_all[:, sl], k=k_all[:, sl], v=v_all[:, sl],
                    b_full=b_full, i_full=full[:, (ML_HEADS + h) * LANES:(ML_HEADS + h + 1) * LANES],
                    b_r=cum_r[cf:cf + 1, :], i_r=gt[ci:ci + 1, :],
                    bl=b_full[T - 1:T, :] if di == 0 else b_full[0:1, :]))
    for c in chains:
        c["m_old"] = m_ref[c["idx"]]
        c["a"] = c["b_full"] + c["m_old"]
        c["dmat"] = jnp.where(c["keep"], c["b_full"] - c["b_r"] + c["i_r"], -1e30)
    for c in chains:
        c["mt"] = jnp.maximum(c["a"], jnp.broadcast_to(jnp.max(c["dmat"], axis=-1, keepdims=True), (T, T)))
    for c in chains:
        c["qk"] = lax.dot_general(c["q"], c["k"].astype(BF16), nt, preferred_element_type=F32)
    for c in chains:
        s = (c["qk"] * jnp.exp(c["dmat"] - c["mt"])).astype(BF16)
        wq = (c["q"].astype(F32) * jnp.exp(c["a"] - c["mt"])).astype(BF16)
        c["v_aug"] = jnp.concatenate([c["v"], ones_td], axis=1)
        c["ct"] = ct_ref[c["idx"]]
        lhs = jnp.concatenate([wq, s], axis=1)
        rhs = jnp.concatenate([c["ct"].astype(BF16), c["v_aug"]], axis=0)
        c["r"] = jnp.dot(lhs, rhs, preferred_element_type=F32)
    for c in chains:
        r = c["r"]
        c["h"] = r[:, :d] / jnp.maximum(jnp.abs(r[:, d:]), jnp.exp(-c["mt"]))
    for c in chains:
        src = c["bl"] - c["b_full"] + c["i_full"]
        m_new = jnp.maximum(c["bl"] + c["m_old"], jnp.max(src, axis=0, keepdims=True))
        gk = (jnp.exp(src - m_new) * c["k"]).T.astype(BF16)
        decay = jnp.exp(c["bl"] + c["m_old"] - m_new)
        ct_ref[c["idx"]] = jnp.concatenate([decay, decay], axis=1) * c["ct"] + \
            jnp.dot(gk, c["v_aug"], preferred_element_type=F32)
        m_ref[c["idx"]] = m_new
    for bi in range(ML_NB):
        for di, h_ref in enumerate((hf_ref, hb_ref)):
            base = (bi * 2 + di) * ML_HEADS
            h_ref[bi] = jnp.concatenate([chains[base + h]["h"] for h in range(ML_HEADS)], axis=1)


def _mlstm(pl_all, pc_all, gl, gc, gate_bias):
    B, L, _ = pl_all.shape
    Lc = pc_all.shape[1] // B
    T = ML_CHUNK
    nl, ncx = L // T, Lc // T
    nsteps = nl + ncx
    pc3 = pc_all.reshape(B, Lc, pc_all.shape[2])
    gc3 = gc.reshape(B, Lc, gc.shape[2])

    def lat_f(j):
        return jnp.clip(j - ncx, 0, nl - 1)

    def lat_b(j):
        return jnp.clip(nsteps - 1 - j, 0, nl - 1)

    def ctx_f(j):
        return jnp.clip(j, 0, ncx - 1)

    def ctx_b(j):
        return jnp.clip(ncx - 1 - j, 0, ncx - 1)

    def lat_specs(fn):
        base = COL_ML // ML_W
        return [pl.BlockSpec((ML_NB, T, ML_W), lambda b, j, o=o: (b, fn(j), base + o)) for o in range(3)] + \
               [pl.BlockSpec((ML_NB, T, LANES), lambda b, j: (b, fn(j), 0))]

    def ctx_specs(fn):
        base = COL_ML // ML_W
        return [pl.BlockSpec((ML_NB, T, ML_W), lambda b, j, o=o: (b, fn(j), base + o)) for o in range(3)] + \
               [pl.BlockSpec((ML_NB, T, LANES), lambda b, j: (b, fn(j), 0))]

    def out_f(b, j):
        return (b, jnp.where(j < ncx, nl + j, j - ncx), 0)

    def out_b(b, j):
        return (b, jnp.where(j < ncx, nl + ncx - 1 - j, nsteps - 1 - j), 0)

    return pl.pallas_call(
        functools.partial(_mlstm_kernel, n_ctx_chunks=ncx),
        grid=(B // ML_NB, nsteps),
        in_specs=lat_specs(lat_f) + lat_specs(lat_b) + ctx_specs(ctx_f) + ctx_specs(ctx_b) +
        [pl.BlockSpec((1, LANES), lambda b, j: (0, 0))],
        out_specs=[pl.BlockSpec((ML_NB, T, ML_W), out_f), pl.BlockSpec((ML_NB, T, ML_W), out_b)],
        out_shape=[jax.ShapeDtypeStruct((B, L + Lc, ML_W), F32)] * 2,
        scratch_shapes=[pltpu.VMEM((ML_NB * 2 * ML_HEADS, ML_HEAD_DIM, 2 * ML_HEAD_DIM), F32),
                        pltpu.VMEM((ML_NB * 2 * ML_HEADS, 1, LANES), F32)],
        compiler_params=_cparams(("parallel", "arbitrary")),
        name="mlstm",
    )(*(([pl_all] * 3 + [gl]) * 2 + ([pc3] * 3 + [gc3]) * 2 + [gate_bias]))


def _short_conv_kernel(x_ref, w_ref, b_ref, o_ref):
    x = x_ref[0].astype(F32)
    L = x.shape[0]
    row = lax.broadcasted_iota(jnp.int32, x.shape, 0)
    prev = jnp.where(row == 0, 0.0, pltpu.roll(x, 1, axis=0))
    nxt = jnp.where(row == L - 1, 0.0, pltpu.roll(x, L - 1, axis=0))
    w = w_ref[...]
    o_ref[0] = b_ref[...] + prev * w[0:1] + x * w[1:2] + nxt * w[2:3]


def _short_conv(p_all, w, b, seq_len):
    Bx, T, _ = p_all.shape
    nseq = T // seq_len
    C = w.shape[1]
    cb = COL_HY // LANES
    return pl.pallas_call(
        _short_conv_kernel,
        grid=(Bx, nseq, C // LANES),
        in_specs=[
            pl.BlockSpec((1, seq_len, LANES), lambda b, s, j: (b, s, cb + j)),
            pl.BlockSpec((3, LANES), lambda b, s, j: (0, j)),
            pl.BlockSpec((1, LANES), lambda b, s, j: (0, j)),
        ],
        out_specs=pl.BlockSpec((1, seq_len, LANES), lambda b, s, j: (b, s, j)),
        out_shape=jax.ShapeDtypeStruct((Bx, T, C), F32),
        compiler_params=_cparams(("parallel", "parallel", "parallel")),
        name="hy_short_conv",
    )(p_all, w, b.reshape(1, C))


@functools.lru_cache(maxsize=None)
def _dft_consts(L):
    N2 = HY_N2
    N = 2 * L
    N1 = N // N2
    S = N1 // 2 + 1
    k1 = np.arange(S)[:, None]
    n2 = np.arange(N2)
    tw = np.exp(-2j * np.pi * k1 * n2[None, :] / N)
    tw_tab = np.stack([np.repeat(tw.real[:, :, None], LANES, 2), np.repeat(tw.imag[:, :, None], LANES, 2)], 1)
    F = np.exp(-2j * np.pi * np.outer(n2, n2) / N2)
    M2 = np.block([[F.real, -F.imag], [F.imag, F.real]])

    def stage1(nb):
        n1 = np.arange(nb)[None, :]
        ang = 2 * np.pi * k1 * n1 / N1
        return np.cos(ang), -np.sin(ang)

    n1o = np.arange(L // N2)[:, None]
    k1o = np.arange(S)[None, :]
    ang = 2 * np.pi * n1o * k1o / N1
    wgt = np.where((k1o == 0) | (k1o == N1 // 2), 1.0, 2.0) / N
    return dict(N=N, N1=N1, S=S, stage1=stage1, tw=tw_tab.astype(np.float32),
                m2=M2.astype(np.float32).astype(BF16), m2t=M2.T.astype(np.float32).astype(BF16),
                icr=np.cos(ang) * wgt, ici=-np.sin(ang) * wgt)


def _dft_mm(m_ref, x):
    return jnp.dot(m_ref[...], x.astype(BF16), preferred_element_type=F32)


def _coef_acc(acc, c, x):
    if x is None or abs(c) < 1e-12:
        return acc
    if abs(c - 1.0) < 1e-12:
        return x if acc is None else acc + x
    if abs(c + 1.0) < 1e-12:
        return -x if acc is None else acc - x
    return c * x if acc is None else acc + c * x


def _vadd(a, b):
    return b if a is None else a if b is None else a + b


def _vsub(a, b):
    return (None if b is None else -b) if a is None else a if b is None else a - b


def _slot_groups(N1):
    half = N1 // 2
    return [(k, half - k if half - k != k else None) for k in range(half // 2 + 1)]


def _stage1_group(load, nb, cr, ci, k, kp, tw_ref, a_ref):
    N2, CT = HY_N2, HY_CT
    for r0 in range(0, N2, HY_SUB):
        if kp is None:
            ar = ai = None
            for n1 in range(nb):
                xb = load(n1, r0)
                ar = _coef_acc(ar, cr[k, n1], xb)
                ai = _coef_acc(ai, ci[k, n1], xb)
            slots = [(k, ar, ai)]
        else:
            even = [None, None]
            odd = [None, None]
            for n1 in range(nb):
                xb = load(n1, r0)
                tgt = even if n1 % 2 == 0 else odd
                tgt[0] = _coef_acc(tgt[0], cr[k, n1], xb)
                tgt[1] = _coef_acc(tgt[1], ci[k, n1], xb)
            slots = [(k, _vadd(even[0], odd[0]), _vadd(even[1], odd[1])),
                     (kp, _vsub(even[0], odd[0]), _vsub(odd[1], even[1]))]
        for idx, (kk, ar, ai) in enumerate(slots):
            zero = jnp.zeros((HY_SUB, CT), F32)
            ar = zero if ar is None else ar
            if kk > 0:
                twr = tw_ref[kk, 0, r0:r0 + HY_SUB, :]
                twi = tw_ref[kk, 1, r0:r0 + HY_SUB, :]
                if ai is None:
                    ar, ai = ar * twr, ar * twi
                else:
                    ar, ai = ar * twr - ai * twi, ar * twi + ai * twr
            ai = zero if ai is None else ai
            a_ref[r0:r0 + HY_SUB, idx * CT:(idx + 1) * CT] = ar
            a_ref[N2 + r0:N2 + r0 + HY_SUB, idx * CT:(idx + 1) * CT] = ai


def _stage1_inv_group(b_ref, nb, icr, ici, k, kp, tw_ref, acc_ref, first):
    N2, CT = HY_N2, HY_CT

    def load(idx, kk, r0):
        br = b_ref[r0:r0 + HY_SUB, idx * CT:(idx + 1) * CT]
        bi = b_ref[N2 + r0:N2 + r0 + HY_SUB, idx * CT:(idx + 1) * CT]
        if kk > 0:
            twr = tw_ref[kk, 0, r0:r0 + HY_SUB, :]
            twi = tw_ref[kk, 1, r0:r0 + HY_SUB, :]
            br, bi = br * twr + bi * twi, bi * twr - br * twi
        return br, bi

    for r0 in range(0, N2, HY_SUB):
        br, bi = load(0, k, r0)
        if kp is None:
            q_even = q_odd = (br, bi)
        else:
            br2, bi2 = load(1, kp, r0)
            q_even = (br + br2, bi - bi2)
            q_odd = (br - br2, bi + bi2)
        for n1 in range(nb):
            qr, qi = q_even if n1 % 2 == 0 else q_odd
            contrib = _coef_acc(_coef_acc(None, icr[n1, k], qr), ici[n1, k], qi)
            rows = slice(n1 * N2 + r0, n1 * N2 + r0 + HY_SUB)
            if first:
                acc_ref[rows, :] = contrib
            elif contrib is not None:
                acc_ref[rows, :] += contrib


def _filter_spec_kernel(hf_ref, hb_ref, tw_ref, m2_ref, o_ref, a_ref, *, L):
    c = _dft_consts(L)
    N2, CT = HY_N2, HY_CT
    nb = L // N2
    cr, ci = c["stage1"](nb)
    for k, kp in _slot_groups(c["N1"]):
        w = CT if kp is None else 2 * CT
        specs = []
        for h_ref in (hf_ref, hb_ref):
            _stage1_group(lambda n1, r0: h_ref[n1 * N2 + r0:n1 * N2 + r0 + HY_SUB, :], nb, cr, ci, k, kp, tw_ref, a_ref)
            specs.append(_dft_mm(m2_ref, a_ref[:, :w]))
        xf, xb = specs
        g = jnp.concatenate([xf[:N2] + xb[:N2], xf[N2:] - xb[N2:]], axis=0)
        o_ref[k] = g[:, :CT]
        if kp is not None:
            o_ref[kp] = g[:, CT:]


def _filter_spectrum(hfb, L):
    c = _dft_consts(L)
    C = hfb.shape[2]
    S, R = c["S"], 2 * HY_N2
    return pl.pallas_call(
        functools.partial(_filter_spec_kernel, L=L),
        grid=(C // HY_CT,),
        in_specs=[
            pl.BlockSpec((None, L, HY_CT), lambda j: (0, 0, j)),
            pl.BlockSpec((None, L, HY_CT), lambda j: (1, 0, j)),
            pl.BlockSpec((S, 2, HY_N2, LANES), lambda j: (0, 0, 0, 0)),
            pl.BlockSpec((R, R), lambda j: (0, 0)),
        ],
        out_specs=pl.BlockSpec((S, R, HY_CT), lambda j: (0, 0, j)),
        out_shape=jax.ShapeDtypeStruct((S, R, C), F32),
        scratch_shapes=[pltpu.VMEM((R, 2 * HY_CT), F32)],
        compiler_params=_cparams(("parallel",)),
        name="hy_filter_spectrum",
    )(hfb, hfb, jnp.asarray(c["tw"]), jnp.asarray(c["m2"]))


def _long_conv_kernel(u_ref, gate_ref, spec_ref, skip_ref, tw_ref, m2_ref, m2t_ref, o_ref, a_ref, b_ref, acc_ref,
                      *, L):
    c = _dft_consts(L)
    N2, CT = HY_N2, HY_CT
    nb = L // N2
    cr, ci = c["stage1"](nb)
    icr, ici = c["icr"] * c["N"], c["ici"] * c["N"]
    for gi_, (k, kp) in enumerate(_slot_groups(c["N1"])):
        w = CT if kp is None else 2 * CT
        _stage1_group(lambda n1, r0: u_ref[0, n1 * N2 + r0:n1 * N2 + r0 + HY_SUB, :], nb, cr, ci, k, kp, tw_ref, a_ref)
        x = _dft_mm(m2_ref, a_ref[:, :w])
        g = spec_ref[k] if kp is None else jnp.concatenate([spec_ref[k], spec_ref[kp]], axis=1)
        xr, xi, gr, gi = x[:N2], x[N2:], g[:N2], g[N2:]
        y = jnp.concatenate([xr * gr - xi * gi, xr * gi + xi * gr], axis=0)
        b_ref[:, :w] = _dft_mm(m2t_ref, y)
        _stage1_inv_group(b_ref, nb, icr, ici, k, kp, tw_ref, acc_ref, gi_ == 0)
    o_ref[0] = (gate_ref[0] * (acc_ref[...] * (1.0 / c["N"]) + skip_ref[...] * u_ref[0])).astype(o_ref.dtype)


def _long_conv(u_arr, u_col, gate_arr, gate_col, spec, spec_col, skip, L, out_dtype):
    Bx, T, _ = u_arr.shape
    nseq = T // L
    c = _dft_consts(L)
    C = skip.shape[0]
    S, R = c["S"], 2 * HY_N2
    ub, gb, sb = u_col // HY_CT, gate_col // HY_CT, spec_col // HY_CT
    return pl.pallas_call(
        functools.partial(_long_conv_kernel, L=L),
        grid=(C // HY_CT, Bx, nseq),
        in_specs=[
            pl.BlockSpec((1, L, HY_CT), lambda j, b, s: (b, s, ub + j)),
            pl.BlockSpec((1, L, HY_CT), lambda j, b, s: (b, s, gb + j)),
            pl.BlockSpec((S, R, HY_CT), lambda j, b, s: (0, 0, sb + j)),
            pl.BlockSpec((1, HY_CT), lambda j, b, s: (0, j)),
            pl.BlockSpec((S, 2, HY_N2, LANES), lambda j, b, s: (0, 0, 0, 0)),
            pl.BlockSpec((R, R), lambda j, b, s: (0, 0)),
            pl.BlockSpec((R, R), lambda j, b, s: (0, 0)),
        ],
        out_specs=pl.BlockSpec((1, L, HY_CT), lambda j, b, s: (b, s, j)),
        out_shape=jax.ShapeDtypeStruct((Bx, T, C), out_dtype),
        scratch_shapes=[pltpu.VMEM((R, 2 * HY_CT), F32), pltpu.VMEM((R, 2 * HY_CT), F32),
                        pltpu.VMEM((L, HY_CT), F32)],
        compiler_params=_cparams(("parallel", "parallel", "parallel")),
        name="hy_long_conv",
    )(u_arr, gate_arr, spec, skip.reshape(1, C), jnp.asarray(c["tw"]), jnp.asarray(c["m2"]), jnp.asarray(c["m2t"]))


def _dot3(a, b):
    a_hi = a.astype(BF16)
    a_lo = (a - a_hi.astype(F32)).astype(BF16)
    b_hi = b.astype(BF16)
    b_lo = (b - b_hi.astype(F32)).astype(BF16)
    return jnp.dot(a_hi, b_hi, preferred_element_type=F32) + \
        (jnp.dot(a_hi, b_lo, preferred_element_type=F32) + jnp.dot(a_lo, b_hi, preferred_element_type=F32))


def _filter_gen_kernel(z_ref, w1_ref, b1_ref, w2_ref, b2_ref, freq_ref, w3_ref, delta_ref, o_ref, h_ref):
    L = z_ref.shape[0]

    @pl.when(pl.program_id(0) == 0)
    def _():
        h = jnp.sin(freq_ref[0:1, :] * (_dot3(z_ref[...], w1_ref[...]) + b1_ref[...]))
        h_ref[...] = jnp.sin(freq_ref[1:2, :] * (_dot3(h, w2_ref[...]) + b2_ref[...]))

    t_norm = lax.broadcasted_iota(jnp.int32, (L, LANES), 0).astype(F32) / max(L - 1, 1)
    h = _dot3(h_ref[...], w3_ref[...]) * jnp.exp(-t_norm * delta_ref[...])
    o_ref[...] = h / jnp.sum(jnp.abs(h), axis=0, keepdims=True)


def _hyena_filters(L, w1, b1, w2, b2, w3, freq):
    t = jnp.arange(L, dtype=F32)
    t_norm = t / max(L - 1, 1)
    w = 2.0 * math.pi * t / L
    bands = jnp.linspace(1e-4, HY_EMB_BANDS - 1, HY_EMB_BANDS, dtype=F32)
    z = jnp.concatenate([t_norm[:, None], jnp.cos(w[:, None] * bands), -jnp.sin(w[:, None] * bands)], axis=-1)
    deltas = jnp.abs(jnp.linspace(math.log(HY_DECAY_TARGET) / HY_DECAY_SLOW,
                                  math.log(HY_DECAY_TARGET) / HY_DECAY_FAST, HY_WIDTH, dtype=F32))
    E, FW = z.shape[1], w1.shape[1]
    nct = HY_WIDTH // LANES
    return pl.pallas_call(
        _filter_gen_kernel,
        grid=(HY_ORDER * 2 * nct,),
        in_specs=[
            pl.BlockSpec((L, E), lambda j: (0, 0)),
            pl.BlockSpec((E, FW), lambda j: (0, 0)),
            pl.BlockSpec((1, FW), lambda j: (0, 0)),
            pl.BlockSpec((FW, FW), lambda j: (0, 0)),
            pl.BlockSpec((1, FW), lambda j: (0, 0)),
            pl.BlockSpec((2, FW), lambda j: (0, 0)),
            pl.BlockSpec((FW, LANES), lambda j: (0, j)),
            pl.BlockSpec((1, LANES), lambda j: (0, j % nct)),
        ],
        out_specs=pl.BlockSpec((None, L, LANES), lambda j: ((j // nct) % 2, 0, (j // (2 * nct)) * nct + j % nct)),
        out_shape=jax.ShapeDtypeStruct((2, L, HY_ORDER * HY_WIDTH), F32),
        scratch_shapes=[pltpu.VMEM((L, FW), F32)],
        compiler_params=_cparams(("arbitrary",)),
        name="hy_filter_gen",
    )(z, w1, b1.reshape(1, FW), w2, b2.reshape(1, FW), freq, w3, deltas.reshape(1, HY_WIDTH))


def _hyena_spectra(L, hy_fp):
    return _filter_spectrum(_hyena_filters(L, *hy_fp), L)


def _hyena(p_all, sw, sb, spec, skip, L):
    u3 = _short_conv(p_all, sw, sb, L)
    W = HY_WIDTH
    z1 = _long_conv(u3, 2 * W, u3, 0, spec, 0, skip[0], L, F32)
    return _long_conv(z1, 0, u3, W, spec, W, skip[1], L, BF16)


def _merge_kernel(x_ref, gs_ref, sh_ref, att_ref, hf_ref, hb_ref, op_ref, hy_ref, mg_ref, wb_ref, wg_ref, wo_ref,
                  g_ref, o_ref):
    d = ML_HEAD_DIM
    D = D_MODEL
    x = x_ref[0]
    h1 = _rms_mod(x, gs_ref[0], sh_ref[0])
    hsum = hf_ref[0] + hb_ref[0]
    parts = []
    for hh in range(ML_HEADS):
        hs = hsum[:, hh * d:(hh + 1) * d]
        parts.append(hs * lax.rsqrt(jnp.mean(hs * hs, axis=-1, keepdims=True) + EPS))
    mls = jnp.concatenate(parts, axis=1) * mg_ref[...] * _sigmoid(op_ref[0].astype(F32))
    y = None
    for r, br in enumerate((att_ref[0], mls.astype(BF16), hy_ref[0])):
        gate = _sigmoid(jnp.dot(h1, wg_ref[:, r * D:(r + 1) * D], preferred_element_type=F32))
        term = gate * jnp.dot(br, wb_ref[r], preferred_element_type=F32)
        y = term if y is None else y + term
    o_ref[0] = x + g_ref[0] * jnp.dot(y.astype(BF16), wo_ref[...], preferred_element_type=F32)


def _merge(x, gs, sh, att, hf, hb, p_all, hy, ml_g, wb, wg, wo, g, tm):
    Bx, T, D = x.shape
    W = ML_W
    row = lambda b, i: (b, i, 0)
    vec = lambda b, i: (b, 0, 0)
    return pl.pallas_call(
        _merge_kernel,
        grid=(Bx, T // tm),
        in_specs=[
            pl.BlockSpec((1, tm, D), row),
            pl.BlockSpec((1, 1, D), vec),
            pl.BlockSpec((1, 1, D), vec),
            pl.BlockSpec((1, tm, W), row),
            pl.BlockSpec((1, tm, W), row),
            pl.BlockSpec((1, tm, W), row),
            pl.BlockSpec((1, tm, W), lambda b, i: (b, i, COL_ML // W + 3)),
            pl.BlockSpec((1, tm, W), row),
            pl.BlockSpec((1, W), lambda b, i: (0, 0)),
            pl.BlockSpec((N_BRANCH, W, D), lambda b, i: (0, 0, 0)),
            pl.BlockSpec((D, N_BRANCH * D), lambda b, i: (0, 0)),
            pl.BlockSpec((D, D), lambda b, i: (0, 0)),
            pl.BlockSpec((1, 1, D), vec),
        ],
        out_specs=pl.BlockSpec((1, tm, D), row),
        out_shape=jax.ShapeDtypeStruct((Bx, T, D), F32),
        compiler_params=_cparams(("parallel", "parallel")),
        name="merge",
    )(x, gs, sh, att, hf, hb, p_all, hy, ml_g.reshape(1, W), wb, wg, wo, g)


FFN_TC = 1408
FFN_HALO = 8


def _ffn_kernel(x_ref, xp_ref, xn_ref, gs_ref, sh_ref, wu_ref, cw_ref, cb_ref, wd_ref, g_ref, o_ref, *, tiles_per_seq):
    i = pl.program_id(1)
    tm = x_ref.shape[1]
    x = x_ref[0]
    h = _rms_mod(x, gs_ref[0], sh_ref[0])
    hh = _rms_mod(jnp.concatenate([xp_ref[0], xn_ref[0]], axis=0), gs_ref[0], sh_ref[0])
    first = (i % tiles_per_seq) == 0
    last = (i % tiles_per_seq) == tiles_per_seq - 1
    row8 = lax.broadcasted_iota(jnp.int32, (8, FFN_TC), 0)

    def conv_cols(c0):
        cs = slice(c0, c0 + FFN_TC)
        u = jnp.dot(h, wu_ref[:, cs], preferred_element_type=F32)
        uh = jnp.dot(hh, wu_ref[:, cs], preferred_element_type=F32)
        pr = jnp.where(first, 0.0, uh[FFN_HALO - 1:FFN_HALO])
        nx = jnp.where(last, 0.0, uh[FFN_HALO:FFN_HALO + 1])
        prev = pltpu.roll(u, 1, axis=0)
        nxt = pltpu.roll(u, tm - 1, axis=0)
        prev = jnp.concatenate([jnp.where(row8 == 0, pr, prev[:8]), prev[8:]], axis=0)
        nxt = jnp.concatenate([nxt[:tm - 8], jnp.where(row8 == 7, nx, nxt[tm - 8:])], axis=0)
        cw = cw_ref[:, cs]
        return cb_ref[:, cs] + prev * cw[0:1] + u * cw[1:2] + nxt * cw[2:3]

    acc = None
    for c0 in range(0, D_FF, FFN_TC):
        gate = conv_cols(c0)
        val = conv_cols(D_FF + c0)
        act = (gate * _sigmoid(gate) * val).astype(BF16)
        part = jnp.dot(act, wd_ref[c0:c0 + FFN_TC, :], preferred_element_type=F32)
        acc = part if acc is None else acc + part
    o_ref[0] = x + g_ref[0] * acc


def _conv_ffn(x, gs, sh, wu, cw, cb, wd, g, seq_len):
    Bx, T, D = x.shape
    tm = min(seq_len, 512)
    hb = tm // FFN_HALO
    nh = T // FFN_HALO
    vec = lambda b, i: (b, 0, 0)
    const = lambda b, i: (0, 0)
    once = pl.Buffered(1)
    return pl.pallas_call(
        functools.partial(_ffn_kernel, tiles_per_seq=seq_len // tm),
        grid=(Bx, T // tm),
        in_specs=[
            pl.BlockSpec((1, tm, D), lambda b, i: (b, i, 0)),
            pl.BlockSpec((1, FFN_HALO, D), lambda b, i: (b, jnp.maximum(i * hb - 1, 0), 0)),
            pl.BlockSpec((1, FFN_HALO, D), lambda b, i: (b, jnp.minimum((i + 1) * hb, nh - 1), 0)),
            pl.BlockSpec((1, 1, D), vec),
            pl.BlockSpec((1, 1, D), vec),
            pl.BlockSpec((D, 2 * D_FF), const, pipeline_mode=once),
            pl.BlockSpec((3, 2 * D_FF), const, pipeline_mode=once),
            pl.BlockSpec((1, 2 * D_FF), const, pipeline_mode=once),
            pl.BlockSpec((D_FF, D), const, pipeline_mode=once),
            pl.BlockSpec((1, 1, D), vec),
        ],
        out_specs=pl.BlockSpec((1, tm, D), lambda b, i: (b, i, 0)),
        out_shape=jax.ShapeDtypeStruct((Bx, T, D), F32),
        compiler_params=_cparams(("parallel", "parallel")),
        name="conv_ffn",
    )(x, x, x, gs, sh, wu, cw, cb.reshape(1, -1), wd, g)


def _pair_perm():
    hd, half = ATT_HEAD_DIM, ATT_HEAD_DIM // 2
    n_pair = ATT_Q // LANES
    qperm = []
    for p in range(n_pair):
        for sub in range(4):
            head = p if sub % 2 == 0 else n_pair + p
            qperm += [head * hd + (sub // 2) * half + dd for dd in range(half)]
    kperm = []
    for sub in range(4):
        kperm += [(sub % 2) * hd + (sub // 2) * half + dd for dd in range(half)]
    return np.asarray(qperm), np.asarray(kperm)


def _rope_tables(L):
    rows = L // GRID_W
    row = jnp.repeat(jnp.arange(rows, dtype=F32), GRID_W)
    col = jnp.tile(jnp.arange(GRID_W, dtype=F32), rows)
    nf = ATT_HEAD_DIM // 4
    inv = ROPE_THETA ** (-jnp.arange(nf, dtype=F32) / nf)
    ang = jnp.concatenate([row[:, None] * inv, col[:, None] * inv], axis=-1)
    cos, sin = jnp.cos(ang), jnp.sin(ang)
    return jnp.concatenate([cos, cos, cos, cos, -sin, -sin, sin, sin], axis=1)


def _prep_w_in(w):
    qperm, kperm = _pair_perm()
    o = IN_OFFSETS
    w = w.astype(BF16)
    cols = [w[:, :ATT_Q][:, qperm], w[:, o[2]:o[6]], w[:, o[7]:o[8]],
            w[:, o[0]:o[1]][:, kperm], w[:, o[1]:o[2]][:, kperm]]
    wc = jnp.concatenate(cols, axis=1)
    w_gate = jnp.pad(w[:, o[6]:o[7]], ((0, 0), (0, LANES - ML_GATES)))
    return jnp.pad(wc, ((0, 0), (0, IN_PAD - wc.shape[1]))), w_gate, w[:, o[8]:]


def _sink_cols(sink, rows_per_pair):
    n_pair = ATT_Q // LANES
    lo = jnp.repeat(sink[:n_pair], rows_per_pair)
    hi = jnp.repeat(sink[n_pair:], rows_per_pair)
    return jnp.stack([lo, hi])[:, :, None]


def kernel(x, c, ctx, c_ctx, ada_w, ada_b, norm1_g, norm2_g, w_in, att_sink, ml_gate_b, ml_norm_g,
           hy_short_w, hy_short_b, hy_w1, hy_b1, hy_w2, hy_b2, hy_w3, hy_freq, hy_skip,
           w_branch, w_out, w_up, ffn_conv_w, ffn_conv_b, w_down, final_g):
    B, L, D = x.shape
    Lc = ctx.shape[1]
    qperm, _ = _pair_perm()
    tabs = _rope_tables(L)
    sc_rows = jnp.concatenate([jax.nn.silu(c), jax.nn.silu(c_ctx)[None], jnp.zeros((8 - B - 1, D), F32)], axis=0)
    xl = x
    xc = ctx.reshape(1, B * Lc, D)
    for l in range(DEPTH):
        need_ctx = l < DEPTH - 1
        mod = _ada_proj(sc_rows, ada_w[l], ada_b[l])
        sh1, sc1, g1, sh2, sc2, g2 = [m[:B, None, :] for m in jnp.split(mod, 6, axis=-1)]
        csh1, csc1, cg1, csh2, csc2, cg2 = [m[B:B + 1, None, :] for m in jnp.split(mod, 6, axis=-1)]
        n1 = norm1_g[l][None, None, :]
        n2 = norm2_g[l][None, None, :]
        w_in_b, w_gate, w_bg = _prep_w_in(w_in[l])
        wb_l = w_branch[l].astype(BF16)
        wb = jnp.concatenate([wb_l[:1][:, qperm], wb_l[1:]], axis=0)
        wo = w_out[l].astype(BF16)
        wu = w_up[l].astype(BF16)
        wd = w_down[l].astype(BF16)
        gate_bias = jnp.pad(ml_gate_b[l].reshape(1, ML_GATES), ((0, 0), (0, LANES - ML_GATES)))
        hy_fp = (hy_w1[l], hy_b1[l], hy_w2[l], hy_b2[l], hy_w3[l], hy_freq[l])

        pl_all, gl = _norm_proj(xl, n1 * (1.0 + sc1), sh1, w_in_b, IN_TN, w_gate)
        pc_all, gc = _norm_proj(xc, n1 * (1.0 + csc1), csh1, w_in_b, IN_TN, w_gate)
        att_l = _win_attention(pl_all, pc_all, tabs, _sink_cols(att_sink[l], ATT_BLOCK))
        hf, hb = _mlstm(pl_all, pc_all, gl, gc, gate_bias)
        hy_l = _hyena(pl_all, hy_short_w[l], hy_short_b[l], _hyena_spectra(L, hy_fp), hy_skip[l], L)
        xl = _merge(xl, n1 * (1.0 + sc1), sh1, att_l, hf, hb, pl_all, hy_l, ml_norm_g[l], wb, w_bg, wo, g1, 512)
        xl = _conv_ffn(xl, n2 * (1.0 + sc2), sh2, wu, ffn_conv_w[l], ffn_conv_b[l], wd, g2, L)
        if need_ctx:
            att_c = _ctx_attention(pc_all, _sink_cols(att_sink[l], Lc), B)
            hy_c = _hyena(pc_all, hy_short_w[l], hy_short_b[l], _hyena_spectra(Lc, hy_fp), hy_skip[l], Lc)
            hfc = hf[:, L:].reshape(1, B * Lc, ML_W)
            hbc = hb[:, L:].reshape(1, B * Lc, ML_W)
            xc = _merge(xc, n1 * (1.0 + csc1), csh1, att_c, hfc, hbc, pc_all, hy_c, ml_norm_g[l], wb, w_bg, wo,
                        cg1, Lc)
            xc = _conv_ffn(xc, n2 * (1.0 + csc2), csh2, wu, ffn_conv_w[l], ffn_conv_b[l], wd, cg2, Lc)
    return _final_norm(xl, final_g)
```

```python
import functools
import math

import numpy as np
import jax
import jax.numpy as jnp
from jax import lax
from jax.experimental import pallas as pl
from jax.experimental.pallas import tpu as pltpu

F32 = jnp.float32
BF16 = jnp.bfloat16

D_MODEL = 1024
DEPTH = 4
GRID_W = 64
EPS = 1e-6
ATT_HEADS = 8
ATT_KV_HEADS = 2
ATT_HEAD_DIM = 64
ATT_BLOCK = 128
ROPE_THETA = 10000.0
ML_HEADS = 4
ML_HEAD_DIM = 128
ML_CHUNK = 128
ML_NB = 2
HY_WIDTH = 512
HY_ORDER = 2
HY_EMB_BANDS = 16
HY_DECAY_TARGET = 1e-2
HY_DECAY_FAST = 0.3
HY_DECAY_SLOW = 1.5
D_FF = 2816
N_BRANCH = 3
LANES = 128

ATT_Q = ATT_HEADS * ATT_HEAD_DIM
ATT_KV = ATT_KV_HEADS * ATT_HEAD_DIM
ML_W = ML_HEADS * ML_HEAD_DIM
ML_GATES = 2 * 2 * ML_HEADS
IN_SIZES = (ATT_Q, ATT_KV, ATT_KV, ML_W, ML_W, ML_W, ML_W, ML_GATES, 3 * HY_WIDTH, N_BRANCH * D_MODEL)
IN_OFFSETS = tuple(int(o) for o in np.cumsum(IN_SIZES)[:-1])

COL_Q = 0
COL_ML = COL_Q + ATT_Q
COL_HY = COL_ML + 4 * ML_W
COL_K = COL_HY + 3 * HY_WIDTH
COL_V = COL_K + ATT_KV
IN_PAD = 4608
IN_TN = 1536
IN_TM = 2048

HY_N2 = 256
HY_CT = 128
HY_SUB = 32

VMEM_LIMIT = 56 * 1024 * 1024


def _cparams(sem):
    return pltpu.CompilerParams(dimension_semantics=sem, vmem_limit_bytes=VMEM_LIMIT)


def _sigmoid(x):
    return 0.5 * jnp.tanh(0.5 * x) + 0.5


def _log_sigmoid(x):
    return jnp.minimum(x, 0.0) - jnp.log(1.0 + jnp.exp(-jnp.abs(x)))


def _rms_mod(x, gs, sh):
    ms = jnp.mean(x * x, axis=-1, keepdims=True)
    return (x * lax.rsqrt(ms + EPS) * gs + sh).astype(BF16)


def _norm_proj_kernel(x_ref, gs_ref, sh_ref, w_ref, *rest, with_aux):
    if with_aux:
        wa_ref, o_ref, oa_ref, h_ref = rest
    else:
        o_ref, h_ref = rest

    @pl.when(pl.program_id(2) == 0)
    def _():
        h = _rms_mod(x_ref[0], gs_ref[0], sh_ref[0])
        h_ref[...] = h
        if with_aux:
            oa_ref[0] = jnp.dot(h, wa_ref[...], preferred_element_type=F32)

    o_ref[0] = jnp.dot(h_ref[...], w_ref[...], preferred_element_type=F32).astype(o_ref.dtype)


def _norm_proj(x, gs, sh, w, tn, w_aux=None):
    B, T, D = x.shape
    N = w.shape[1]
    tm = min(T, IN_TM)
    in_specs = [
        pl.BlockSpec((1, tm, D), lambda b, i, j: (b, i, 0)),
        pl.BlockSpec((1, 1, D), lambda b, i, j: (b, 0, 0)),
        pl.BlockSpec((1, 1, D), lambda b, i, j: (b, 0, 0)),
        pl.BlockSpec((D, tn), lambda b, i, j: (0, j)),
    ]
    out_specs = [pl.BlockSpec((1, tm, tn), lambda b, i, j: (b, i, j))]
    out_shape = [jax.ShapeDtypeStruct((B, T, N), BF16)]
    args = [x, gs, sh, w]
    if w_aux is not None:
        na = w_aux.shape[1]
        in_specs.append(pl.BlockSpec((D, na), lambda b, i, j: (0, 0)))
        out_specs.append(pl.BlockSpec((1, tm, na), lambda b, i, j: (b, i, 0)))
        out_shape.append(jax.ShapeDtypeStruct((B, T, na), F32))
        args.append(w_aux)
    outs = pl.pallas_call(
        functools.partial(_norm_proj_kernel, with_aux=w_aux is not None),
        grid=(B, T // tm, N // tn),
        in_specs=in_specs,
        out_specs=out_specs,
        out_shape=out_shape,
        scratch_shapes=[pltpu.VMEM((tm, D), BF16)],
        compiler_params=_cparams(("parallel", "parallel", "arbitrary")),
        name="norm_proj",
    )(*args)
    return outs if w_aux is not None else outs[0]


def _small_proj_kernel(a_ref, w_ref, b_ref, o_ref):
    a = a_ref[...]
    a_hi = a.astype(BF16)
    a_lo = (a - a_hi.astype(F32)).astype(BF16)
    w = w_ref[...]
    w_hi = w.astype(BF16)
    w_lo = (w - w_hi.astype(F32)).astype(BF16)
    acc = jnp.dot(a_hi, w_hi, preferred_element_type=F32)
    acc = acc + (jnp.dot(a_hi, w_lo, preferred_element_type=F32) + jnp.dot(a_lo, w_hi, preferred_element_type=F32))
    o_ref[...] = acc + b_ref[...]


def _ada_proj(a, w, b, layer):
    M, D = a.shape
    depth, _, N = w.shape
    tn = 1024
    return pl.pallas_call(
        _small_proj_kernel,
        grid=(N // tn,),
        in_specs=[
            pl.BlockSpec((M, D), lambda j: (0, 0)),
            pl.BlockSpec((None, D, tn), lambda j: (layer, 0, j)),
            pl.BlockSpec((None, 1, tn), lambda j: (layer, 0, j)),
        ],
        out_specs=pl.BlockSpec((M, tn), lambda j: (0, j)),
        out_shape=jax.ShapeDtypeStruct((M, N), F32),
        compiler_params=_cparams(("parallel",)),
        name="ada_proj",
    )(a, w, b.reshape(depth, 1, N))


def _final_norm_kernel(x_ref, g_ref, o_ref):
    x = x_ref[0]
    ms = jnp.mean(x * x, axis=-1, keepdims=True)
    o_ref[0] = x * lax.rsqrt(ms + EPS) * g_ref[...]


def _final_norm(x, g):
    B, T, D = x.shape
    tm = 512
    return pl.pallas_call(
        _final_norm_kernel,
        grid=(B, T // tm),
        in_specs=[
            pl.BlockSpec((1, tm, D), lambda b, i: (b, i, 0)),
            pl.BlockSpec((1, D), lambda b, i: (0, 0)),
        ],
        out_specs=pl.BlockSpec((1, tm, D), lambda b, i: (b, i, 0)),
        out_shape=jax.ShapeDtypeStruct((B, T, D), F32),
        compiler_params=_cparams(("parallel", "parallel")),
        name="final_norm",
    )(x, g.reshape(1, D))


def _lane_lo_mask(shape):
    lane = lax.broadcasted_iota(jnp.int32, shape, len(shape) - 1)
    return (lane % 64) < 32


def _rope(x, tab):
    c = tab[:, :LANES]
    s = tab[:, LANES:]
    outs = []
    for g in range(x.shape[1] // LANES):
        xg = x[:, g * LANES:(g + 1) * LANES]
        outs.append(xg * c + pltpu.roll(xg, 64, axis=1) * s)
    return outs[0] if len(outs) == 1 else jnp.concatenate(outs, axis=1)


def _attend(qs, kcat, vcat, bias, sink_lo, sink_hi):
    lo = _lane_lo_mask(kcat.shape)
    T = ATT_BLOCK
    outs = []
    for msk, sink in ((lo, sink_lo), (jnp.logical_not(lo), sink_hi)):
        kh = jnp.where(msk, kcat, 0.0).astype(BF16)
        s = lax.dot_general(qs, kh, (((1,), (1,)), ((), ())), preferred_element_type=F32)
        if bias is not None:
            c0, bias_prev, bias_next = bias
            s = jnp.concatenate([s[:, :c0], s[:, c0:c0 + T] + bias_prev, s[:, c0 + T:c0 + 2 * T],
                                 s[:, c0 + 2 * T:] + bias_next], axis=1)
        m = jnp.maximum(jnp.max(s, axis=-1, keepdims=True), sink)
        p = jnp.exp(s - m)
        den = jnp.sum(p, axis=-1, keepdims=True) + jnp.exp(sink - m)
        outs.append(jnp.dot(p.astype(BF16), vcat, preferred_element_type=F32) * (1.0 / den))
    return jnp.where(_lane_lo_mask(outs[0].shape), outs[0], outs[1])


ATT_QB = 2


def _win_attn_kernel(*refs):
    nkb = ATT_QB + 2
    q_ref = refs[0]
    k_refs = refs[1:1 + nkb]
    v_refs = refs[1 + nkb:1 + 2 * nkb]
    kx_ref, vx_ref = refs[1 + 2 * nkb:3 + 2 * nkb]
    t_refs = refs[3 + 2 * nkb:3 + 3 * nkb]
    sink_ref, o_ref = refs[3 + 3 * nkb:]
    i = pl.program_id(1)
    nb = pl.num_programs(1) * ATT_QB
    T = ATT_BLOCK
    n_pair = ATT_Q // LANES
    Lc = kx_ref.shape[1]
    kx = kx_ref[0].astype(F32)
    k_rot = [_rope(k_refs[j][0].astype(F32), t_refs[j][...]) for j in range(nkb)]
    t = lax.broadcasted_iota(jnp.int32, (T, T), 0)
    s = lax.broadcasted_iota(jnp.int32, (T, T), 1)
    for sb in range(ATT_QB):
        blk = i * ATT_QB + sb
        q = _rope(q_ref[0, sb * T:(sb + 1) * T, :].astype(F32), t_refs[sb + 1][...]) * (ATT_HEAD_DIM ** -0.5)
        qs = jnp.concatenate([q[:, g * LANES:(g + 1) * LANES] for g in range(n_pair)], axis=0).astype(BF16)
        kcat = jnp.concatenate([kx, k_rot[sb], k_rot[sb + 1], k_rot[sb + 2]], axis=0)
        vcat = jnp.concatenate([vx_ref[0], v_refs[sb][0], v_refs[sb + 1][0], v_refs[sb + 2][0]], axis=0)
        bias_prev = jnp.where((s >= t) & (blk > 0), 0.0, -1e30).astype(F32)
        bias_next = jnp.where((s <= t) & (blk < nb - 1), 0.0, -1e30).astype(F32)
        bias = (Lc, jnp.concatenate([bias_prev] * n_pair, axis=0), jnp.concatenate([bias_next] * n_pair, axis=0))
        o = _attend(qs, kcat, vcat, bias, sink_ref[0], sink_ref[1])
        o_ref[0, sb * T:(sb + 1) * T, :] = jnp.concatenate(
            [o[g * T:(g + 1) * T] for g in range(n_pair)], axis=1).astype(o_ref.dtype)


def _win_attention(pl_all, pc_all, tabs, sinks):
    B, L, _ = pl_all.shape
    Lc = pc_all.shape[1] // B
    T = ATT_BLOCK
    nb = L // T
    kb, vb = COL_K // LANES, COL_V // LANES
    offs = range(-1, ATT_QB + 1)

    def blk(col, d):
        return pl.BlockSpec((1, T, LANES), lambda b, i: (b, jnp.clip(i * ATT_QB + d, 0, nb - 1), col))

    def tab(d):
        return pl.BlockSpec((T, 2 * LANES), lambda b, i: (jnp.clip(i * ATT_QB + d, 0, nb - 1), 0))

    nkb = ATT_QB + 2
    return pl.pallas_call(
        _win_attn_kernel,
        grid=(B, nb // ATT_QB),
        in_specs=[pl.BlockSpec((1, ATT_QB * T, ATT_Q), lambda b, i: (b, i, 0))] +
        [blk(kb, d) for d in offs] + [blk(vb, d) for d in offs] +
        [pl.BlockSpec((1, Lc, LANES), lambda b, i: (0, b, kb)),
         pl.BlockSpec((1, Lc, LANES), lambda b, i: (0, b, vb))] +
        [tab(d) for d in offs] +
        [pl.BlockSpec((2, ATT_Q, 1), lambda b, i: (0, 0, 0))],
        out_specs=pl.BlockSpec((1, ATT_QB * T, ATT_Q), lambda b, i: (b, i, 0)),
        out_shape=jax.ShapeDtypeStruct((B, L, ATT_Q), BF16),
        compiler_params=_cparams(("parallel", "parallel")),
        name="win_attention",
    )(*([pl_all] * (1 + 2 * nkb) + [pc_all] * 2 + [tabs] * nkb + [sinks]))


def _ctx_attn_kernel(q_ref, kx_ref, vx_ref, sink_ref, o_ref):
    Lc = q_ref.shape[1]
    n_pair = ATT_Q // LANES
    q = q_ref[0].astype(F32) * (ATT_HEAD_DIM ** -0.5)
    qs = jnp.concatenate([q[:, g * LANES:(g + 1) * LANES] for g in range(n_pair)], axis=0).astype(BF16)
    o = _attend(qs, kx_ref[0].astype(F32), vx_ref[0], None, sink_ref[0], sink_ref[1])
    o_ref[0] = jnp.concatenate([o[g * Lc:(g + 1) * Lc] for g in range(n_pair)], axis=1).astype(o_ref.dtype)


def _ctx_attention(pc_all, sinks, B):
    Lc = pc_all.shape[1] // B
    kb, vb = COL_K // LANES, COL_V // LANES
    return pl.pallas_call(
        _ctx_attn_kernel,
        grid=(B,),
        in_specs=[
            pl.BlockSpec((1, Lc, ATT_Q), lambda b: (0, b, 0)),
            pl.BlockSpec((1, Lc, LANES), lambda b: (0, b, kb)),
            pl.BlockSpec((1, Lc, LANES), lambda b: (0, b, vb)),
            pl.BlockSpec((2, ATT_Q // LANES * Lc, 1), lambda b: (0, 0, 0)),
        ],
        out_specs=pl.BlockSpec((1, Lc, ATT_Q), lambda b: (0, b, 0)),
        out_shape=jax.ShapeDtypeStruct((1, B * Lc, ATT_Q), BF16),
        compiler_params=_cparams(("parallel",)),
        name="ctx_attention",
    )(pc_all, pc_all, pc_all, sinks)


def _dot_hl(a_exact, x):
    x_hi = x.astype(BF16)
    x_lo = (x - x_hi.astype(F32)).astype(BF16)
    return jnp.dot(a_exact, x_hi, preferred_element_type=F32) + jnp.dot(a_exact, x_lo, preferred_element_type=F32)


def _mlstm_kernel(qf_l, kf_l, vf_l, gf_l, qb_l, kb_l, vb_l, gb_l,
                  qf_c, kf_c, vf_c, gf_c, qb_c, kb_c, vb_c, gb_c, bias_ref,
                  hf_ref, hb_ref, ct_ref, m_ref, *, n_ctx_chunks):
    j = pl.program_id(1)
    T = ML_CHUNK
    d = ML_HEAD_DIM
    is_ctx = j < n_ctx_chunks

    @pl.when(j == 0)
    def _():
        ct_ref[...] = jnp.zeros_like(ct_ref)
        m_ref[...] = jnp.zeros_like(m_ref)

    row = lax.broadcasted_iota(jnp.int32, (T, T), 0)
    col = lax.broadcasted_iota(jnp.int32, (T, T), 1)
    ones_td = jnp.ones((T, d), BF16)

    nt = (((1,), (1,)), ((), ()))
    sel_row = lax.broadcasted_iota(jnp.int32, (LANES, 2 * ML_HEADS * LANES), 0)
    sel_blk = lax.broadcasted_iota(jnp.int32, (LANES, 2 * ML_HEADS * LANES), 1) // LANES
    chains = []
    for bi in range(ML_NB):
        for di, (q_l, k_l, v_l, g_l, q_c, k_c, v_c, g_c) in enumerate(
                ((qf_l, kf_l, vf_l, gf_l, qf_c, kf_c, vf_c, gf_c),
                 (qb_l, kb_l, vb_l, gb_l, qb_c, kb_c, vb_c, gb_c))):
            keep = (col <= row) if di == 0 else (col >= row)
            keep_b = keep.astype(BF16)
            g = jnp.where(is_ctx, g_c[bi], g_l[bi]) + bias_ref[...]
            lane = lax.broadcasted_iota(jnp.int32, g.shape, 1)
            is_f = (lane % 8) >= 4
            gv = jnp.where(is_f, _log_sigmoid(g), g)
            gt = gv.T
            cum_c = _dot_hl(keep_b, gv)
            gt_hi = gt.astype(BF16)
            gt_lo = (gt - gt_hi.astype(F32)).astype(BF16)
            cum_r = lax.dot_general(gt_hi, keep_b, nt, preferred_element_type=F32) + \
                lax.dot_general(gt_lo, keep_b, nt, preferred_element_type=F32)
            src_col = jnp.where(sel_blk < ML_HEADS, di * 8 + ML_HEADS + sel_blk, di * 8 + sel_blk - ML_HEADS)
            sel = jnp.where(sel_row == src_col, 1.0, 0.0).astype(BF16)
            x = jnp.where(is_f, cum_c, gv)
            x_hi = x.astype(BF16)
            x_lo = (x - x_hi.astype(F32)).astype(BF16)
            full = jnp.dot(x_hi, sel, preferred_element_type=F32) + jnp.dot(x_lo, sel, preferred_element_type=F32)
            q_all = jnp.where(is_ctx, q_c[bi], q_l[bi])
            k_all = jnp.where(is_ctx, k_c[bi], k_l[bi]).astype(F32) * (d ** -0.5)
            v_all = jnp.where(is_ctx, v_c[bi], v_l[bi])
            for h in range(ML_HEADS):
                ci = di * 8 + h
                cf = di * 8 + 4 + h
                sl = slice(h * d, (h + 1) * d)
                b_full = full[:, h * LANES:(h + 1) * LANES]
                chains.append(dict(
                    idx=(bi * 2 + di) * ML_HEADS + h, keep=keep, q=q_all[:, sl], k=k_all[:, sl], v=v_all[:, sl],
                    b_full=b_full, i_full=full[:, (ML_HEADS + h) * LANES:(ML_HEADS + h + 1) * LANES],
                    b_r=cum_r[cf:cf + 1, :], i_r=gt[ci:ci + 1, :],
                    bl=b_full[T - 1:T, :] if di == 0 else b_full[0:1, :]))
    for c in chains:
        c["m_old"] = m_ref[c["idx"]]
        c["a"] = c["b_full"] + c["m_old"]
        c["dmat"] = jnp.where(c["keep"], c["b_full"] - c["b_r"] + c["i_r"], -1e30)
    for c in chains:
        c["mt"] = jnp.maximum(c["a"], jnp.broadcast_to(jnp.max(c["dmat"], axis=-1, keepdims=True), (T, T)))
    for c in chains:
        c["qk"] = lax.dot_general(c["q"], c["k"].astype(BF16), nt, preferred_element_type=F32)
    for c in chains:
        s = (c["qk"] * jnp.exp(c["dmat"] - c["mt"])).astype(BF16)
        wq = (c["q"].astype(F32) * jnp.exp(c["a"] - c["mt"])).astype(BF16)
        c["v_aug"] = jnp.concatenate([c["v"], ones_td], axis=1)
        c["ct"] = ct_ref[c["idx"]]
        lhs = jnp.concatenate([wq, s], axis=1)
        rhs = jnp.concatenate([c["ct"].astype(BF16), c["v_aug"]], axis=0)
        c["r"] = jnp.dot(lhs, rhs, preferred_element_type=F32)
    for c in chains:
        r = c["r"]
        c["h"] = r[:, :d] / jnp.maximum(jnp.abs(r[:, d:]), jnp.exp(-c["mt"]))
    for c in chains:
        src = c["bl"] - c["b_full"] + c["i_full"]
        m_new = jnp.maximum(c["bl"] + c["m_old"], jnp.max(src, axis=0, keepdims=True))
        gk = (jnp.exp(src - m_new) * c["k"]).T.astype(BF16)
        decay = jnp.exp(c["bl"] + c["m_old"] - m_new)
        ct_ref[c["idx"]] = jnp.concatenate([decay, decay], axis=1) * c["ct"] + \
            jnp.dot(gk, c["v_aug"], preferred_element_type=F32)
        m_ref[c["idx"]] = m_new
    for bi in range(ML_NB):
        for di, h_ref in enumerate((hf_ref, hb_ref)):
            base = (bi * 2 + di) * ML_HEADS
            h_ref[bi] = jnp.concatenate([chains[base + h]["h"] for h in range(ML_HEADS)], axis=1)


def _mlstm(pl_all, pc_all, gl, gc, gate_bias):
    B, L, _ = pl_all.shape
    Lc = pc_all.shape[1] // B
    T = ML_CHUNK
    nl, ncx = L // T, Lc // T
    nsteps = nl + ncx
    pc3 = pc_all.reshape(B, Lc, pc_all.shape[2])
    gc3 = gc.reshape(B, Lc, gc.shape[2])

    def lat_f(j):
        return jnp.clip(j - ncx, 0, nl - 1)

    def lat_b(j):
        return jnp.clip(nsteps - 1 - j, 0, nl - 1)

    def ctx_f(j):
        return jnp.clip(j, 0, ncx - 1)

    def ctx_b(j):
        return jnp.clip(ncx - 1 - j, 0, ncx - 1)

    def lat_specs(fn):
        base = COL_ML // ML_W
        return [pl.BlockSpec((ML_NB, T, ML_W), lambda b, j, o=o: (b, fn(j), base + o)) for o in range(3)] + \
               [pl.BlockSpec((ML_NB, T, LANES), lambda b, j: (b, fn(j), 0))]

    def ctx_specs(fn):
        base = COL_ML // ML_W
        return [pl.BlockSpec((ML_NB, T, ML_W), lambda b, j, o=o: (b, fn(j), base + o)) for o in range(3)] + \
               [pl.BlockSpec((ML_NB, T, LANES), lambda b, j: (b, fn(j), 0))]

    def out_f(b, j):
        return (b, jnp.where(j < ncx, nl + j, j - ncx), 0)

    def out_b(b, j):
        return (b, jnp.where(j < ncx, nl + ncx - 1 - j, nsteps - 1 - j), 0)

    return pl.pallas_call(
        functools.partial(_mlstm_kernel, n_ctx_chunks=ncx),
        grid=(B // ML_NB, nsteps),
        in_specs=lat_specs(lat_f) + lat_specs(lat_b) + ctx_specs(ctx_f) + ctx_specs(ctx_b) +
        [pl.BlockSpec((1, LANES), lambda b, j: (0, 0))],
        out_specs=[pl.BlockSpec((ML_NB, T, ML_W), out_f), pl.BlockSpec((ML_NB, T, ML_W), out_b)],
        out_shape=[jax.ShapeDtypeStruct((B, L + Lc, ML_W), F32)] * 2,
        scratch_shapes=[pltpu.VMEM((ML_NB * 2 * ML_HEADS, ML_HEAD_DIM, 2 * ML_HEAD_DIM), F32),
                        pltpu.VMEM((ML_NB * 2 * ML_HEADS, 1, LANES), F32)],
        compiler_params=_cparams(("parallel", "arbitrary")),
        name="mlstm",
    )(*(([pl_all] * 3 + [gl]) * 2 + ([pc3] * 3 + [gc3]) * 2 + [gate_bias]))


def _short_conv_kernel(x_ref, w_ref, b_ref, o_ref):
    x = x_ref[0].astype(F32)
    L = x.shape[0]
    row = lax.broadcasted_iota(jnp.int32, x.shape, 0)
    prev = jnp.where(row == 0, 0.0, pltpu.roll(x, 1, axis=0))
    nxt = jnp.where(row == L - 1, 0.0, pltpu.roll(x, L - 1, axis=0))
    w = w_ref[...]
    o_ref[0] = b_ref[...] + prev * w[0:1] + x * w[1:2] + nxt * w[2:3]


def _short_conv(p_all, w, b, seq_len):
    Bx, T, _ = p_all.shape
    nseq = T // seq_len
    C = w.shape[1]
    cb = COL_HY // LANES
    return pl.pallas_call(
        _short_conv_kernel,
        grid=(Bx, nseq, C // LANES),
        in_specs=[
            pl.BlockSpec((1, seq_len, LANES), lambda b, s, j: (b, s, cb + j)),
            pl.BlockSpec((3, LANES), lambda b, s, j: (0, j)),
            pl.BlockSpec((1, LANES), lambda b, s, j: (0, j)),
        ],
        out_specs=pl.BlockSpec((1, seq_len, LANES), lambda b, s, j: (b, s, j)),
        out_shape=jax.ShapeDtypeStruct((Bx, T, C), F32),
        compiler_params=_cparams(("parallel", "parallel", "parallel")),
        name="hy_short_conv",
    )(p_all, w, b.reshape(1, C))


@functools.lru_cache(maxsize=None)
def _dft_consts(L):
    N2 = HY_N2
    N = 2 * L
    N1 = N // N2
    S = N1 // 2 + 1
    k1 = np.arange(S)[:, None]
    n2 = np.arange(N2)
    tw = np.exp(-2j * np.pi * k1 * n2[None, :] / N)
    tw_tab = np.stack([np.repeat(tw.real[:, :, None], LANES, 2), np.repeat(tw.imag[:, :, None], LANES, 2)], 1)
    F = np.exp(-2j * np.pi * np.outer(n2, n2) / N2)
    M2 = np.block([[F.real, -F.imag], [F.imag, F.real]])

    def stage1(nb):
        n1 = np.arange(nb)[None, :]
        ang = 2 * np.pi * k1 * n1 / N1
        return np.cos(ang), -np.sin(ang)

    n1o = np.arange(L // N2)[:, None]
    k1o = np.arange(S)[None, :]
    ang = 2 * np.pi * n1o * k1o / N1
    wgt = np.where((k1o == 0) | (k1o == N1 // 2), 1.0, 2.0) / N
    return dict(N=N, N1=N1, S=S, stage1=stage1, tw=tw_tab.astype(np.float32),
                m2=M2.astype(np.float32).astype(BF16), m2t=M2.T.astype(np.float32).astype(BF16),
                icr=np.cos(ang) * wgt, ici=-np.sin(ang) * wgt)


def _dft_mm(m_ref, x):
    return jnp.dot(m_ref[...], x.astype(BF16), preferred_element_type=F32)


def _coef_acc(acc, c, x):
    if x is None or abs(c) < 1e-12:
        return acc
    if abs(c - 1.0) < 1e-12:
        return x if acc is None else acc + x
    if abs(c + 1.0) < 1e-12:
        return -x if acc is None else acc - x
    return c * x if acc is None else acc + c * x


def _vadd(a, b):
    return b if a is None else a if b is None else a + b


def _vsub(a, b):
    return (None if b is None else -b) if a is None else a if b is None else a - b


def _slot_groups(N1):
    half = N1 // 2
    return [(k, half - k if half - k != k else None) for k in range(half // 2 + 1)]


def _stage1_group(load, nb, cr, ci, k, kp, tw_ref, a_ref):
    N2, CT = HY_N2, HY_CT
    for r0 in range(0, N2, HY_SUB):
        if kp is None:
            ar = ai = None
            for n1 in range(nb):
                xb = load(n1, r0)
                ar = _coef_acc(ar, cr[k, n1], xb)
                ai = _coef_acc(ai, ci[k, n1], xb)
            slots = [(k, ar, ai)]
        else:
            even = [None, None]
            odd = [None, None]
            for n1 in range(nb):
                xb = load(n1, r0)
                tgt = even if n1 % 2 == 0 else odd
                tgt[0] = _coef_acc(tgt[0], cr[k, n1], xb)
                tgt[1] = _coef_acc(tgt[1], ci[k, n1], xb)
            slots = [(k, _vadd(even[0], odd[0]), _vadd(even[1], odd[1])),
                     (kp, _vsub(even[0], odd[0]), _vsub(odd[1], even[1]))]
        for idx, (kk, ar, ai) in enumerate(slots):
            zero = jnp.zeros((HY_SUB, CT), F32)
            ar = zero if ar is None else ar
            if kk > 0:
                twr = tw_ref[kk, 0, r0:r0 + HY_SUB, :]
                twi = tw_ref[kk, 1, r0:r0 + HY_SUB, :]
                if ai is None:
                    ar, ai = ar * twr, ar * twi
                else:
                    ar, ai = ar * twr - ai * twi, ar * twi + ai * twr
            ai = zero if ai is None else ai
            a_ref[r0:r0 + HY_SUB, idx * CT:(idx + 1) * CT] = ar
            a_ref[N2 + r0:N2 + r0 + HY_SUB, idx * CT:(idx + 1) * CT] = ai


def _stage1_inv_group(b_ref, nb, icr, ici, k, kp, tw_ref, acc_ref, first):
    N2, CT = HY_N2, HY_CT

    def load(idx, kk, r0):
        br = b_ref[r0:r0 + HY_SUB, idx * CT:(idx + 1) * CT]
        bi = b_ref[N2 + r0:N2 + r0 + HY_SUB, idx * CT:(idx + 1) * CT]
        if kk > 0:
            twr = tw_ref[kk, 0, r0:r0 + HY_SUB, :]
            twi = tw_ref[kk, 1, r0:r0 + HY_SUB, :]
            br, bi = br * twr + bi * twi, bi * twr - br * twi
        return br, bi

    for r0 in range(0, N2, HY_SUB):
        br, bi = load(0, k, r0)
        if kp is None:
            q_even = q_odd = (br, bi)
        else:
            br2, bi2 = load(1, kp, r0)
            q_even = (br + br2, bi - bi2)
            q_odd = (br - br2, bi + bi2)
        for n1 in range(nb):
            qr, qi = q_even if n1 % 2 == 0 else q_odd
            contrib = _coef_acc(_coef_acc(None, icr[n1, k], qr), ici[n1, k], qi)
            rows = slice(n1 * N2 + r0, n1 * N2 + r0 + HY_SUB)
            if first:
                acc_ref[rows, :] = contrib
            elif contrib is not None:
                acc_ref[rows, :] += contrib


def _filter_spec_kernel(hf_ref, hb_ref, tw_ref, m2_ref, o_ref, a_ref, *, L):
    c = _dft_consts(L)
    N2, CT = HY_N2, HY_CT
    nb = L // N2
    cr, ci = c["stage1"](nb)
    for k, kp in _slot_groups(c["N1"]):
        w = CT if kp is None else 2 * CT
        specs = []
        for h_ref in (hf_ref, hb_ref):
            _stage1_group(lambda n1, r0: h_ref[n1 * N2 + r0:n1 * N2 + r0 + HY_SUB, :], nb, cr, ci, k, kp, tw_ref, a_ref)
            specs.append(_dft_mm(m2_ref, a_ref[:, :w]))
        xf, xb = specs
        g = jnp.concatenate([xf[:N2] + xb[:N2], xf[N2:] - xb[N2:]], axis=0)
        o_ref[k] = g[:, :CT]
        if kp is not None:
            o_ref[kp] = g[:, CT:]


def _filter_spectrum(hfb, L):
    c = _dft_consts(L)
    C = hfb.shape[2]
    S, R = c["S"], 2 * HY_N2
    return pl.pallas_call(
        functools.partial(_filter_spec_kernel, L=L),
        grid=(C // HY_CT,),
        in_specs=[
            pl.BlockSpec((None, L, HY_CT), lambda j: (0, 0, j)),
            pl.BlockSpec((None, L, HY_CT), lambda j: (1, 0, j)),
            pl.BlockSpec((S, 2, HY_N2, LANES), lambda j: (0, 0, 0, 0)),
            pl.BlockSpec((R, R), lambda j: (0, 0)),
        ],
        out_specs=pl.BlockSpec((S, R, HY_CT), lambda j: (0, 0, j)),
        out_shape=jax.ShapeDtypeStruct((S, R, C), F32),
        scratch_shapes=[pltpu.VMEM((R, 2 * HY_CT), F32)],
        compiler_params=_cparams(("parallel",)),
        name="hy_filter_spectrum",
    )(hfb, hfb, jnp.asarray(c["tw"]), jnp.asarray(c["m2"]))


def _long_conv_kernel(u_ref, gate_ref, spec_ref, skip_ref, tw_ref, m2_ref, m2t_ref, o_ref, a_ref, b_ref, acc_ref,
                      *, L):
    c = _dft_consts(L)
    N2, CT = HY_N2, HY_CT
    nb = L // N2
    cr, ci = c["stage1"](nb)
    icr, ici = c["icr"] * c["N"], c["ici"] * c["N"]
    for gi_, (k, kp) in enumerate(_slot_groups(c["N1"])):
        w = CT if kp is None else 2 * CT
        _stage1_group(lambda n1, r0: u_ref[0, n1 * N2 + r0:n1 * N2 + r0 + HY_SUB, :], nb, cr, ci, k, kp, tw_ref, a_ref)
        x = _dft_mm(m2_ref, a_ref[:, :w])
        g = spec_ref[k] if kp is None else jnp.concatenate([spec_ref[k], spec_ref[kp]], axis=1)
        xr, xi, gr, gi = x[:N2], x[N2:], g[:N2], g[N2:]
        y = jnp.concatenate([xr * gr - xi * gi, xr * gi + xi * gr], axis=0)
        b_ref[:, :w] = _dft_mm(m2t_ref, y)
        _stage1_inv_group(b_ref, nb, icr, ici, k, kp, tw_ref, acc_ref, gi_ == 0)
    o_ref[0] = (gate_ref[0] * (acc_ref[...] * (1.0 / c["N"]) + skip_ref[...] * u_ref[0])).astype(o_ref.dtype)


def _long_conv(u_arr, u_col, gate_arr, gate_col, spec, spec_col, skip, L, out_dtype):
    Bx, T, _ = u_arr.shape
    nseq = T // L
    c = _dft_consts(L)
    C = skip.shape[0]
    S, R = c["S"], 2 * HY_N2
    ub, gb, sb = u_col // HY_CT, gate_col // HY_CT, spec_col // HY_CT
    return pl.pallas_call(
        functools.partial(_long_conv_kernel, L=L),
        grid=(C // HY_CT, Bx, nseq),
        in_specs=[
            pl.BlockSpec((1, L, HY_CT), lambda j, b, s: (b, s, ub + j)),
            pl.BlockSpec((1, L, HY_CT), lambda j, b, s: (b, s, gb + j)),
            pl.BlockSpec((S, R, HY_CT), lambda j, b, s: (0, 0, sb + j)),
            pl.BlockSpec((1, HY_CT), lambda j, b, s: (0, j)),
            pl.BlockSpec((S, 2, HY_N2, LANES), lambda j, b, s: (0, 0, 0, 0)),
            pl.BlockSpec((R, R), lambda j, b, s: (0, 0)),
            pl.BlockSpec((R, R), lambda j, b, s: (0, 0)),
        ],
        out_specs=pl.BlockSpec((1, L, HY_CT), lambda j, b, s: (b, s, j)),
        out_shape=jax.ShapeDtypeStruct((Bx, T, C), out_dtype),
        scratch_shapes=[pltpu.VMEM((R, 2 * HY_CT), F32), pltpu.VMEM((R, 2 * HY_CT), F32),
                        pltpu.VMEM((L, HY_CT), F32)],
        compiler_params=_cparams(("parallel", "parallel", "parallel")),
        name="hy_long_conv",
    )(u_arr, gate_arr, spec, skip.reshape(1, C), jnp.asarray(c["tw"]), jnp.asarray(c["m2"]), jnp.asarray(c["m2t"]))


def _dot3(a, b):
    a_hi = a.astype(BF16)
    a_lo = (a - a_hi.astype(F32)).astype(BF16)
    b_hi = b.astype(BF16)
    b_lo = (b - b_hi.astype(F32)).astype(BF16)
    return jnp.dot(a_hi, b_hi, preferred_element_type=F32) + \
        (jnp.dot(a_hi, b_lo, preferred_element_type=F32) + jnp.dot(a_lo, b_hi, preferred_element_type=F32))


def _filter_gen_kernel(z_ref, w1_ref, b1_ref, w2_ref, b2_ref, freq_ref, w3_ref, delta_ref, o_ref, h_ref):
    L = z_ref.shape[0]

    @pl.when(pl.program_id(0) == 0)
    def _():
        h = jnp.sin(freq_ref[0:1, :] * (_dot3(z_ref[...], w1_ref[...]) + b1_ref[...]))
        h_ref[...] = jnp.sin(freq_ref[1:2, :] * (_dot3(h, w2_ref[...]) + b2_ref[...]))

    t_norm = lax.broadcasted_iota(jnp.int32, (L, LANES), 0).astype(F32) / max(L - 1, 1)
    h = _dot3(h_ref[...], w3_ref[...]) * jnp.exp(-t_norm * delta_ref[...])
    o_ref[...] = h / jnp.sum(jnp.abs(h), axis=0, keepdims=True)


def _hyena_filters(L, w1, b1, w2, b2, w3, freq):
    t = jnp.arange(L, dtype=F32)
    t_norm = t / max(L - 1, 1)
    w = 2.0 * math.pi * t / L
    bands = jnp.linspace(1e-4, HY_EMB_BANDS - 1, HY_EMB_BANDS, dtype=F32)
    z = jnp.concatenate([t_norm[:, None], jnp.cos(w[:, None] * bands), -jnp.sin(w[:, None] * bands)], axis=-1)
    deltas = jnp.abs(jnp.linspace(math.log(HY_DECAY_TARGET) / HY_DECAY_SLOW,
                                  math.log(HY_DECAY_TARGET) / HY_DECAY_FAST, HY_WIDTH, dtype=F32))
    E, FW = z.shape[1], w1.shape[1]
    nct = HY_WIDTH // LANES
    return pl.pallas_call(
        _filter_gen_kernel,
        grid=(HY_ORDER * 2 * nct,),
        in_specs=[
            pl.BlockSpec((L, E), lambda j: (0, 0)),
            pl.BlockSpec((E, FW), lambda j: (0, 0)),
            pl.BlockSpec((1, FW), lambda j: (0, 0)),
            pl.BlockSpec((FW, FW), lambda j: (0, 0)),
            pl.BlockSpec((1, FW), lambda j: (0, 0)),
            pl.BlockSpec((2, FW), lambda j: (0, 0)),
            pl.BlockSpec((FW, LANES), lambda j: (0, j)),
            pl.BlockSpec((1, LANES), lambda j: (0, j % nct)),
        ],
        out_specs=pl.BlockSpec((None, L, LANES), lambda j: ((j // nct) % 2, 0, (j // (2 * nct)) * nct + j % nct)),
        out_shape=jax.ShapeDtypeStruct((2, L, HY_ORDER * HY_WIDTH), F32),
        scratch_shapes=[pltpu.VMEM((L, FW), F32)],
        compiler_params=_cparams(("arbitrary",)),
        name="hy_filter_gen",
    )(z, w1, b1.reshape(1, FW), w2, b2.reshape(1, FW), freq, w3, deltas.reshape(1, HY_WIDTH))


def _hyena_spectra(L, hy_fp):
    return _filter_spectrum(_hyena_filters(L, *hy_fp), L)


def _hyena(p_all, sw, sb, spec, skip, L):
    u3 = _short_conv(p_all, sw, sb, L)
    W = HY_WIDTH
    z1 = _long_conv(u3, 2 * W, u3, 0, spec, 0, skip[0], L, F32)
    return _long_conv(z1, 0, u3, W, spec, W, skip[1], L, BF16)


def _merge_kernel(x_ref, gs_ref, sh_ref, att_ref, hf_ref, hb_ref, op_ref, hy_ref, mg_ref, wb_ref, wg_ref, wo_ref,
                  g_ref, o_ref):
    d = ML_HEAD_DIM
    D = D_MODEL
    x = x_ref[0]
    h1 = _rms_mod(x, gs_ref[0], sh_ref[0])
    hsum = hf_ref[0] + hb_ref[0]
    parts = []
    for hh in range(ML_HEADS):
        hs = hsum[:, hh * d:(hh + 1) * d]
        parts.append(hs * lax.rsqrt(jnp.mean(hs * hs, axis=-1, keepdims=True) + EPS))
    mls = jnp.concatenate(parts, axis=1) * mg_ref[...] * _sigmoid(op_ref[0].astype(F32))
    y = None
    for r, br in enumerate((att_ref[0], mls.astype(BF16), hy_ref[0])):
        gate = _sigmoid(jnp.dot(h1, wg_ref[:, r * D:(r + 1) * D], preferred_element_type=F32))
        term = gate * jnp.dot(br, wb_ref[r], preferred_element_type=F32)
        y = term if y is None else y + term
    o_ref[0] = x + g_ref[0] * jnp.dot(y.astype(BF16), wo_ref[...], preferred_element_type=F32)


def _merge(x, gs, sh, att, hf, hb, p_all, hy, ml_g, wb, wg, wo, g, tm, layer):
    Bx, T, D = x.shape
    W = ML_W
    row = lambda b, i: (b, i, 0)
    vec = lambda b, i: (b, 0, 0)
    return pl.pallas_call(
        _merge_kernel,
        grid=(Bx, T // tm),
        in_specs=[
            pl.BlockSpec((1, tm, D), row),
            pl.BlockSpec((1, 1, D), vec),
            pl.BlockSpec((1, 1, D), vec),
            pl.BlockSpec((1, tm, W), row),
            pl.BlockSpec((1, tm, W), row),
            pl.BlockSpec((1, tm, W), row),
            pl.BlockSpec((1, tm, W), lambda b, i: (b, i, COL_ML // W + 3)),
            pl.BlockSpec((1, tm, W), row),
            pl.BlockSpec((1, W), lambda b, i: (0, 0)),
            pl.BlockSpec((N_BRANCH, W, D), lambda b, i: (0, 0, 0)),
            pl.BlockSpec((D, N_BRANCH * D), lambda b, i: (0, 0)),
            pl.BlockSpec((None, D, D), lambda b, i: (layer, 0, 0)),
            pl.BlockSpec((1, 1, D), vec),
        ],
        out_specs=pl.BlockSpec((1, tm, D), row),
        out_shape=jax.ShapeDtypeStruct((Bx, T, D), F32),
        compiler_params=_cparams(("parallel", "parallel")),
        name="merge",
    )(x, gs, sh, att, hf, hb, p_all, hy, ml_g.reshape(1, W), wb, wg, wo, g)


FFN_TC = 1408
FFN_HALO = 8


def _ffn_kernel(x_ref, xp_ref, xn_ref, gs_ref, sh_ref, wu_ref, cw_ref, cb_ref, wd_ref, g_ref, o_ref, *, tiles_per_seq):
    i = pl.program_id(1)
    tm = x_ref.shape[1]
    x = x_ref[0]
    h = _rms_mod(x, gs_ref[0], sh_ref[0])
    hh = _rms_mod(jnp.concatenate([xp_ref[0], xn_ref[0]], axis=0), gs_ref[0], sh_ref[0])
    first = (i % tiles_per_seq) == 0
    last = (i % tiles_per_seq) == tiles_per_seq - 1
    row8 = lax.broadcasted_iota(jnp.int32, (8, FFN_TC), 0)

    def conv_cols(c0):
        cs = slice(c0, c0 + FFN_TC)
        u = jnp.dot(h, wu_ref[:, cs], preferred_element_type=F32)
        uh = jnp.dot(hh, wu_ref[:, cs], preferred_element_type=F32)
        pr = jnp.where(first, 0.0, uh[FFN_HALO - 1:FFN_HALO])
        nx = jnp.where(last, 0.0, uh[FFN_HALO:FFN_HALO + 1])
        prev = pltpu.roll(u, 1, axis=0)
        nxt = pltpu.roll(u, tm - 1, axis=0)
        prev = jnp.concatenate([jnp.where(row8 == 0, pr, prev[:8]), prev[8:]], axis=0)
        nxt = jnp.concatenate([nxt[:tm - 8], jnp.where(row8 == 7, nx, nxt[tm - 8:])], axis=0)
        cw = cw_ref[:, cs]
        return cb_ref[:, cs] + prev * cw[0:1] + u * cw[1:2] + nxt * cw[2:3]

    acc = None
    for c0 in range(0, D_FF, FFN_TC):
        gate = conv_cols(c0)
        val = conv_cols(D_FF + c0)
        act = (gate * _sigmoid(gate) * val).astype(BF16)
        part = jnp.dot(act, wd_ref[c0:c0 + FFN_TC, :], preferred_element_type=F32)
        acc = part if acc is None else acc + part
    o_ref[0] = x + g_ref[0] * acc


def _conv_ffn(x, gs, sh, wu, cw, cb, wd, g, seq_len, layer):
    Bx, T, D = x.shape
    tm = min(seq_len, 512)
    hb = tm // FFN_HALO
    nh = T // FFN_HALO
    vec = lambda b, i: (b, 0, 0)
    const = lambda b, i: (0, 0)
    lconst = lambda b, i: (layer, 0, 0)
    once = pl.Buffered(1)
    return pl.pallas_call(
        functools.partial(_ffn_kernel, tiles_per_seq=seq_len // tm),
        grid=(Bx, T // tm),
        in_specs=[
            pl.BlockSpec((1, tm, D), lambda b, i: (b, i, 0)),
            pl.BlockSpec((1, FFN_HALO, D), lambda b, i: (b, jnp.maximum(i * hb - 1, 0), 0)),
            pl.BlockSpec((1, FFN_HALO, D), lambda b, i: (b, jnp.minimum((i + 1) * hb, nh - 1), 0)),
            pl.BlockSpec((1, 1, D), vec),
            pl.BlockSpec((1, 1, D), vec),
            pl.BlockSpec((None, D, 2 * D_FF), lconst, pipeline_mode=once),
            pl.BlockSpec((3, 2 * D_FF), const, pipeline_mode=once),
            pl.BlockSpec((1, 2 * D_FF), const, pipeline_mode=once),
            pl.BlockSpec((None, D_FF, D), lconst, pipeline_mode=once),
            pl.BlockSpec((1, 1, D), vec),
        ],
        out_specs=pl.BlockSpec((1, tm, D), lambda b, i: (b, i, 0)),
        out_shape=jax.ShapeDtypeStruct((Bx, T, D), F32),
        compiler_params=_cparams(("parallel", "parallel")),
        name="conv_ffn",
    )(x, x, x, gs, sh, wu, cw, cb.reshape(1, -1), wd, g)


def _pair_perm():
    hd, half = ATT_HEAD_DIM, ATT_HEAD_DIM // 2
    n_pair = ATT_Q // LANES
    qperm = []
    for p in range(n_pair):
        for sub in range(4):
            head = p if sub % 2 == 0 else n_pair + p
            qperm += [head * hd + (sub // 2) * half + dd for dd in range(half)]
    kperm = []
    for sub in range(4):
        kperm += [(sub % 2) * hd + (sub // 2) * half + dd for dd in range(half)]
    return np.asarray(qperm), np.asarray(kperm)


def _rope_tables(L):
    rows = L // GRID_W
    row = jnp.repeat(jnp.arange(rows, dtype=F32), GRID_W)
    col = jnp.tile(jnp.arange(GRID_W, dtype=F32), rows)
    nf = ATT_HEAD_DIM // 4
    inv = ROPE_THETA ** (-jnp.arange(nf, dtype=F32) / nf)
    ang = jnp.concatenate([row[:, None] * inv, col[:, None] * inv], axis=-1)
    cos, sin = jnp.cos(ang), jnp.sin(ang)
    return jnp.concatenate([cos, cos, cos, cos, -sin, -sin, sin, sin], axis=1)


def _take_runs(t, idx, axis):
    runs, start = [], 0
    for p in range(1, len(idx) + 1):
        if p == len(idx) or idx[p] != idx[p - 1] + 1:
            runs.append(lax.slice_in_dim(t, int(idx[start]), int(idx[p - 1]) + 1, axis=axis))
            start = p
    return jnp.concatenate(runs, axis=axis)


def _prep_w_in(w):
    qperm, kperm = _pair_perm()
    o = IN_OFFSETS
    w = w.astype(BF16)
    cols = [_take_runs(w, qperm, 1), w[:, o[2]:o[6]], w[:, o[7]:o[8]],
            _take_runs(w, o[0] + kperm, 1), _take_runs(w, o[1] + kperm, 1)]
    wc = jnp.concatenate(cols, axis=1)
    w_gate = jnp.pad(w[:, o[6]:o[7]], ((0, 0), (0, LANES - ML_GATES)))
    return jnp.pad(wc, ((0, 0), (0, IN_PAD - wc.shape[1]))), w_gate, w[:, o[8]:]


def _sink_cols(sink, rows_per_pair):
    n_pair = ATT_Q // LANES
    lo = jnp.repeat(sink[:n_pair], rows_per_pair)
    hi = jnp.repeat(sink[n_pair:], rows_per_pair)
    return jnp.stack([lo, hi])[:, :, None]


def kernel(x, c, ctx, c_ctx, ada_w, ada_b, norm1_g, norm2_g, w_in, att_sink, ml_gate_b, ml_norm_g,
           hy_short_w, hy_short_b, hy_w1, hy_b1, hy_w2, hy_b2, hy_w3, hy_freq, hy_skip,
           w_branch, w_out, w_up, ffn_conv_w, ffn_conv_b, w_down, final_g):
    B, L, D = x.shape
    Lc = ctx.shape[1]
    qperm, _ = _pair_perm()
    tabs = _rope_tables(L)
    sc_rows = jnp.concatenate([jax.nn.silu(c), jax.nn.silu(c_ctx)[None], jnp.zeros((8 - B - 1, D), F32)], axis=0)
    xl = x
    xc = ctx.reshape(1, B * Lc, D)
    wo, wu, wd = w_out.astype(BF16), w_up.astype(BF16), w_down.astype(BF16)
    for l in range(DEPTH):
        need_ctx = l < DEPTH - 1
        mod = _ada_proj(sc_rows, ada_w, ada_b, l)
        sh1, sc1, g1, sh2, sc2, g2 = [m[:B, None, :] for m in jnp.split(mod, 6, axis=-1)]
        csh1, csc1, cg1, csh2, csc2, cg2 = [m[B:B + 1, None, :] for m in jnp.split(mod, 6, axis=-1)]
        n1 = norm1_g[l][None, None, :]
        n2 = norm2_g[l][None, None, :]
        w_in_b, w_gate, w_bg = _prep_w_in(w_in[l])
        wb_l = w_branch[l].astype(BF16)
        wb = jnp.concatenate([_take_runs(wb_l[:1], qperm, 1), wb_l[1:]], axis=0)
        gate_bias = jnp.pad(ml_gate_b[l].reshape(1, ML_GATES), ((0, 0), (0, LANES - ML_GATES)))
        hy_fp = (hy_w1[l], hy_b1[l], hy_w2[l], hy_b2[l], hy_w3[l], hy_freq[l])

        pl_all, gl = _norm_proj(xl, n1 * (1.0 + sc1), sh1, w_in_b, IN_TN, w_gate)
        pc_all, gc = _norm_proj(xc, n1 * (1.0 + csc1), csh1, w_in_b, IN_TN, w_gate)
        att_l = _win_attention(pl_all, pc_all, tabs, _sink_cols(att_sink[l], ATT_BLOCK))
        hf, hb = _mlstm(pl_all, pc_all, gl, gc, gate_bias)
        hy_l = _hyena(pl_all, hy_short_w[l], hy_short_b[l], _hyena_spectra(L, hy_fp), hy_skip[l], L)
        xl = _merge(xl, n1 * (1.0 + sc1), sh1, att_l, hf, hb, pl_all, hy_l, ml_norm_g[l], wb, w_bg, wo, g1, 512, l)
        xl = _conv_ffn(xl, n2 * (1.0 + sc2), sh2, wu, ffn_conv_w[l], ffn_conv_b[l], wd, g2, L, l)
        if need_ctx:
            att_c = _ctx_attention(pc_all, _sink_cols(att_sink[l], Lc), B)
            hy_c = _hyena(pc_all, hy_short_w[l], hy_short_b[l], _hyena_spectra(Lc, hy_fp), hy_skip[l], Lc)
            hfc = hf[:, L:].reshape(1, B * Lc, ML_W)
            hbc = hb[:, L:].reshape(1, B * Lc, ML_W)
            xc = _merge(xc, n1 * (1.0 + csc1), csh1, att_c, hfc, hbc, pc_all, hy_c, ml_norm_g[l], wb, w_bg, wo,
                        cg1, Lc, l)
            xc = _conv_ffn(xc, n2 * (1.0 + csc2), csh2, wu, ffn_conv_w[l], ffn_conv_b[l], wd, cg2, Lc, l)
    return _final_norm(xl, final_g)
```

```python
import functools
import math

import numpy as np
import jax
import jax.numpy as jnp
from jax import lax
from jax.experimental import pallas as pl
from jax.experimental.pallas import tpu as pltpu

F32 = jnp.float32
BF16 = jnp.bfloat16

D_MODEL = 1024
DEPTH = 4
GRID_W = 64
EPS = 1e-6
ATT_HEADS = 8
ATT_KV_HEADS = 2
ATT_HEAD_DIM = 64
ATT_BLOCK = 128
ROPE_THETA = 10000.0
ML_HEADS = 4
ML_HEAD_DIM = 128
ML_CHUNK = 128
ML_NB = 4
HY_WIDTH = 512
HY_ORDER = 2
HY_EMB_BANDS = 16
HY_DECAY_TARGET = 1e-2
HY_DECAY_FAST = 0.3
HY_DECAY_SLOW = 1.5
D_FF = 2816
N_BRANCH = 3
LANES = 128

ATT_Q = ATT_HEADS * ATT_HEAD_DIM
ATT_KV = ATT_KV_HEADS * ATT_HEAD_DIM
ML_W = ML_HEADS * ML_HEAD_DIM
ML_GATES = 2 * 2 * ML_HEADS
IN_SIZES = (ATT_Q, ATT_KV, ATT_KV, ML_W, ML_W, ML_W, ML_W, ML_GATES, 3 * HY_WIDTH, N_BRANCH * D_MODEL)
IN_OFFSETS = tuple(int(o) for o in np.cumsum(IN_SIZES)[:-1])

COL_Q = 0
COL_ML = COL_Q + ATT_Q
COL_HY = COL_ML + 4 * ML_W
COL_K = COL_HY + 3 * HY_WIDTH
COL_V = COL_K + ATT_KV
IN_PAD = 4608
IN_TN = 1536
IN_TM = 2048

HY_N2 = 256
HY_CT = 128
HY_SUB = 32

VMEM_LIMIT = 56 * 1024 * 1024


def _cparams(sem):
    return pltpu.CompilerParams(dimension_semantics=sem, vmem_limit_bytes=VMEM_LIMIT)


def _sigmoid(x):
    return 0.5 * jnp.tanh(0.5 * x) + 0.5


def _log_sigmoid(x):
    return jnp.minimum(x, 0.0) - jnp.log(1.0 + jnp.exp(-jnp.abs(x)))


def _rms_mod(x, gs, sh):
    ms = jnp.mean(x * x, axis=-1, keepdims=True)
    return (x * lax.rsqrt(ms + EPS) * gs + sh).astype(BF16)


def _norm_proj_kernel(x_ref, gs_ref, sh_ref, w_ref, *rest, with_aux):
    if with_aux:
        wa_ref, o_ref, oa_ref, h_ref = rest
    else:
        o_ref, h_ref = rest

    @pl.when(pl.program_id(2) == 0)
    def _():
        h = _rms_mod(x_ref[0], gs_ref[0], sh_ref[0])
        h_ref[...] = h
        if with_aux:
            oa_ref[0] = jnp.dot(h, wa_ref[...], preferred_element_type=F32)

    o_ref[0] = jnp.dot(h_ref[...], w_ref[...], preferred_element_type=F32).astype(o_ref.dtype)


def _norm_proj(x, gs, sh, w, tn, w_aux=None):
    B, T, D = x.shape
    N = w.shape[1]
    tm = min(T, IN_TM)
    in_specs = [
        pl.BlockSpec((1, tm, D), lambda b, i, j: (b, i, 0)),
        pl.BlockSpec((1, 1, D), lambda b, i, j: (b, 0, 0)),
        pl.BlockSpec((1, 1, D), lambda b, i, j: (b, 0, 0)),
        pl.BlockSpec((D, tn), lambda b, i, j: (0, j)),
    ]
    out_specs = [pl.BlockSpec((1, tm, tn), lambda b, i, j: (b, i, j))]
    out_shape = [jax.ShapeDtypeStruct((B, T, N), BF16)]
    args = [x, gs, sh, w]
    if w_aux is not None:
        na = w_aux.shape[1]
        in_specs.append(pl.BlockSpec((D, na), lambda b, i, j: (0, 0)))
        out_specs.append(pl.BlockSpec((1, tm, na), lambda b, i, j: (b, i, 0)))
        out_shape.append(jax.ShapeDtypeStruct((B, T, na), F32))
        args.append(w_aux)
    outs = pl.pallas_call(
        functools.partial(_norm_proj_kernel, with_aux=w_aux is not None),
        grid=(B, T // tm, N // tn),
        in_specs=in_specs,
        out_specs=out_specs,
        out_shape=out_shape,
        scratch_shapes=[pltpu.VMEM((tm, D), BF16)],
        compiler_params=_cparams(("parallel", "parallel", "arbitrary")),
        name="norm_proj",
    )(*args)
    return outs if w_aux is not None else outs[0]


def _small_proj_kernel(a_ref, w_ref, b_ref, o_ref):
    a = a_ref[...]
    a_hi = a.astype(BF16)
    a_lo = (a - a_hi.astype(F32)).astype(BF16)
    w = w_ref[...]
    w_hi = w.astype(BF16)
    w_lo = (w - w_hi.astype(F32)).astype(BF16)
    acc = jnp.dot(a_hi, w_hi, preferred_element_type=F32)
    acc = acc + (jnp.dot(a_hi, w_lo, preferred_element_type=F32) + jnp.dot(a_lo, w_hi, preferred_element_type=F32))
    o_ref[...] = acc + b_ref[...]


def _ada_proj(a, w, b, layer):
    M, D = a.shape
    depth, _, N = w.shape
    tn = 1024
    return pl.pallas_call(
        _small_proj_kernel,
        grid=(N // tn,),
        in_specs=[
            pl.BlockSpec((M, D), lambda j: (0, 0)),
            pl.BlockSpec((None, D, tn), lambda j: (layer, 0, j)),
            pl.BlockSpec((None, 1, tn), lambda j: (layer, 0, j)),
        ],
        out_specs=pl.BlockSpec((M, tn), lambda j: (0, j)),
        out_shape=jax.ShapeDtypeStruct((M, N), F32),
        compiler_params=_cparams(("parallel",)),
        name="ada_proj",
    )(a, w, b.reshape(depth, 1, N))


def _final_norm_kernel(x_ref, g_ref, o_ref):
    x = x_ref[0]
    ms = jnp.mean(x * x, axis=-1, keepdims=True)
    o_ref[0] = x * lax.rsqrt(ms + EPS) * g_ref[...]


def _final_norm(x, g):
    B, T, D = x.shape
    tm = 512
    return pl.pallas_call(
        _final_norm_kernel,
        grid=(B, T // tm),
        in_specs=[
            pl.BlockSpec((1, tm, D), lambda b, i: (b, i, 0)),
            pl.BlockSpec((1, D), lambda b, i: (0, 0)),
        ],
        out_specs=pl.BlockSpec((1, tm, D), lambda b, i: (b, i, 0)),
        out_shape=jax.ShapeDtypeStruct((B, T, D), F32),
        compiler_params=_cparams(("parallel", "parallel")),
        name="final_norm",
    )(x, g.reshape(1, D))


def _lane_lo_mask(shape):
    lane = lax.broadcasted_iota(jnp.int32, shape, len(shape) - 1)
    return (lane % 64) < 32


def _rope(x, tab):
    c = tab[:, :LANES]
    s = tab[:, LANES:]
    outs = []
    for g in range(x.shape[1] // LANES):
        xg = x[:, g * LANES:(g + 1) * LANES]
        outs.append(xg * c + pltpu.roll(xg, 64, axis=1) * s)
    return outs[0] if len(outs) == 1 else jnp.concatenate(outs, axis=1)


def _attend(qs, kcat, vcat, bias, sink_lo, sink_hi):
    lo = _lane_lo_mask(kcat.shape)
    T = ATT_BLOCK
    outs = []
    for msk, sink in ((lo, sink_lo), (jnp.logical_not(lo), sink_hi)):
        kh = jnp.where(msk, kcat, 0.0).astype(BF16)
        s = lax.dot_general(qs, kh, (((1,), (1,)), ((), ())), preferred_element_type=F32)
        if bias is not None:
            c0, bias_prev, bias_next = bias
            s = jnp.concatenate([s[:, :c0], s[:, c0:c0 + T] + bias_prev, s[:, c0 + T:c0 + 2 * T],
                                 s[:, c0 + 2 * T:] + bias_next], axis=1)
        m = jnp.maximum(jnp.max(s, axis=-1, keepdims=True), sink)
        p = jnp.exp(s - m)
        den = jnp.sum(p, axis=-1, keepdims=True) + jnp.exp(sink - m)
        outs.append(jnp.dot(p.astype(BF16), vcat, preferred_element_type=F32) * (1.0 / den))
    return jnp.where(_lane_lo_mask(outs[0].shape), outs[0], outs[1])


ATT_QB = 4


def _win_attn_kernel(*refs):
    nkb = ATT_QB + 2
    q_ref = refs[0]
    k_refs = refs[1:1 + nkb]
    v_refs = refs[1 + nkb:1 + 2 * nkb]
    kx_ref, vx_ref = refs[1 + 2 * nkb:3 + 2 * nkb]
    t_refs = refs[3 + 2 * nkb:3 + 3 * nkb]
    sink_ref, o_ref = refs[3 + 3 * nkb:]
    i = pl.program_id(1)
    nb = pl.num_programs(1) * ATT_QB
    T = ATT_BLOCK
    n_pair = ATT_Q // LANES
    Lc = kx_ref.shape[1]
    kx = kx_ref[0].astype(F32)
    k_rot = [_rope(k_refs[j][0].astype(F32), t_refs[j][...]) for j in range(nkb)]
    t = lax.broadcasted_iota(jnp.int32, (T, T), 0)
    s = lax.broadcasted_iota(jnp.int32, (T, T), 1)
    for sb in range(ATT_QB):
        blk = i * ATT_QB + sb
        q = _rope(q_ref[0, sb * T:(sb + 1) * T, :].astype(F32), t_refs[sb + 1][...]) * (ATT_HEAD_DIM ** -0.5)
        qs = jnp.concatenate([q[:, g * LANES:(g + 1) * LANES] for g in range(n_pair)], axis=0).astype(BF16)
        kcat = jnp.concatenate([kx, k_rot[sb], k_rot[sb + 1], k_rot[sb + 2]], axis=0)
        vcat = jnp.concatenate([vx_ref[0], v_refs[sb][0], v_refs[sb + 1][0], v_refs[sb + 2][0]], axis=0)
        bias_prev = jnp.where((s >= t) & (blk > 0), 0.0, -1e30).astype(F32)
        bias_next = jnp.where((s <= t) & (blk < nb - 1), 0.0, -1e30).astype(F32)
        bias = (Lc, jnp.concatenate([bias_prev] * n_pair, axis=0), jnp.concatenate([bias_next] * n_pair, axis=0))
        o = _attend(qs, kcat, vcat, bias, sink_ref[0], sink_ref[1])
        o_ref[0, sb * T:(sb + 1) * T, :] = jnp.concatenate(
            [o[g * T:(g + 1) * T] for g in range(n_pair)], axis=1).astype(o_ref.dtype)


def _win_attention(pl_all, pc_all, tabs, sinks):
    B, L, _ = pl_all.shape
    Lc = pc_all.shape[1] // B
    T = ATT_BLOCK
    nb = L // T
    kb, vb = COL_K // LANES, COL_V // LANES
    offs = range(-1, ATT_QB + 1)

    def blk(col, d):
        return pl.BlockSpec((1, T, LANES), lambda b, i: (b, jnp.clip(i * ATT_QB + d, 0, nb - 1), col))

    def tab(d):
        return pl.BlockSpec((T, 2 * LANES), lambda b, i: (jnp.clip(i * ATT_QB + d, 0, nb - 1), 0))

    nkb = ATT_QB + 2
    return pl.pallas_call(
        _win_attn_kernel,
        grid=(B, nb // ATT_QB),
        in_specs=[pl.BlockSpec((1, ATT_QB * T, ATT_Q), lambda b, i: (b, i, 0))] +
        [blk(kb, d) for d in offs] + [blk(vb, d) for d in offs] +
        [pl.BlockSpec((1, Lc, LANES), lambda b, i: (0, b, kb)),
         pl.BlockSpec((1, Lc, LANES), lambda b, i: (0, b, vb))] +
        [tab(d) for d in offs] +
        [pl.BlockSpec((2, ATT_Q, 1), lambda b, i: (0, 0, 0))],
        out_specs=pl.BlockSpec((1, ATT_QB * T, ATT_Q), lambda b, i: (b, i, 0)),
        out_shape=jax.ShapeDtypeStruct((B, L, ATT_Q), BF16),
        compiler_params=_cparams(("parallel", "parallel")),
        name="win_attention",
    )(*([pl_all] * (1 + 2 * nkb) + [pc_all] * 2 + [tabs] * nkb + [sinks]))


def _ctx_attn_kernel(q_ref, kx_ref, vx_ref, sink_ref, o_ref):
    Lc = q_ref.shape[1]
    n_pair = ATT_Q // LANES
    q = q_ref[0].astype(F32) * (ATT_HEAD_DIM ** -0.5)
    qs = jnp.concatenate([q[:, g * LANES:(g + 1) * LANES] for g in range(n_pair)], axis=0).astype(BF16)
    o = _attend(qs, kx_ref[0].astype(F32), vx_ref[0], None, sink_ref[0], sink_ref[1])
    o_ref[0] = jnp.concatenate([o[g * Lc:(g + 1) * Lc] for g in range(n_pair)], axis=1).astype(o_ref.dtype)


def _ctx_attention(pc_all, sinks, B):
    Lc = pc_all.shape[1] // B
    kb, vb = COL_K // LANES, COL_V // LANES
    return pl.pallas_call(
        _ctx_attn_kernel,
        grid=(B,),
        in_specs=[
            pl.BlockSpec((1, Lc, ATT_Q), lambda b: (0, b, 0)),
            pl.BlockSpec((1, Lc, LANES), lambda b: (0, b, kb)),
            pl.BlockSpec((1, Lc, LANES), lambda b: (0, b, vb)),
            pl.BlockSpec((2, ATT_Q // LANES * Lc, 1), lambda b: (0, 0, 0)),
        ],
        out_specs=pl.BlockSpec((1, Lc, ATT_Q), lambda b: (0, b, 0)),
        out_shape=jax.ShapeDtypeStruct((1, B * Lc, ATT_Q), BF16),
        compiler_params=_cparams(("parallel",)),
        name="ctx_attention",
    )(pc_all, pc_all, pc_all, sinks)


def _dot_hl(a_exact, x):
    x_hi = x.astype(BF16)
    x_lo = (x - x_hi.astype(F32)).astype(BF16)
    return jnp.dot(a_exact, x_hi, preferred_element_type=F32) + jnp.dot(a_exact, x_lo, preferred_element_type=F32)


def _mlstm_kernel(qf_l, kf_l, vf_l, gf_l, qb_l, kb_l, vb_l, gb_l,
                  qf_c, kf_c, vf_c, gf_c, qb_c, kb_c, vb_c, gb_c, bias_ref,
                  hf_ref, hb_ref, ct_ref, m_ref, *, n_ctx_chunks):
    j = pl.program_id(1)
    T = ML_CHUNK
    d = ML_HEAD_DIM
    is_ctx = j < n_ctx_chunks

    @pl.when(j == 0)
    def _():
        ct_ref[...] = jnp.zeros_like(ct_ref)
        m_ref[...] = jnp.zeros_like(m_ref)

    row = lax.broadcasted_iota(jnp.int32, (T, T), 0)
    col = lax.broadcasted_iota(jnp.int32, (T, T), 1)
    ones_td = jnp.ones((T, d), BF16)

    nt = (((1,), (1,)), ((), ()))
    sel_row = lax.broadcasted_iota(jnp.int32, (LANES, 2 * ML_HEADS * LANES), 0)
    sel_blk = lax.broadcasted_iota(jnp.int32, (LANES, 2 * ML_HEADS * LANES), 1) // LANES
    chains = []
    for bi in range(ML_NB):
        for di, (q_l, k_l, v_l, g_l, q_c, k_c, v_c, g_c) in enumerate(
                ((qf_l, kf_l, vf_l, gf_l, qf_c, kf_c, vf_c, gf_c),
                 (qb_l, kb_l, vb_l, gb_l, qb_c, kb_c, vb_c, gb_c))):
            keep = (col <= row) if di == 0 else (col >= row)
            keep_b = keep.astype(BF16)
            g = jnp.where(is_ctx, g_c[bi], g_l[bi]) + bias_ref[...]
            lane = lax.broadcasted_iota(jnp.int32, g.shape, 1)
            is_f = (lane % 8) >= 4
            gv = jnp.where(is_f, _log_sigmoid(g), g)
            gt = gv.T
            cum_c = _dot_hl(keep_b, gv)
            gt_hi = gt.astype(BF16)
            gt_lo = (gt - gt_hi.astype(F32)).astype(BF16)
            cum_r = lax.dot_general(gt_hi, keep_b, nt, preferred_element_type=F32) + \
                lax.dot_general(gt_lo, keep_b, nt, preferred_element_type=F32)
            src_col = jnp.where(sel_blk < ML_HEADS, di * 8 + ML_HEADS + sel_blk, di * 8 + sel_blk - ML_HEADS)
            sel = jnp.where(sel_row == src_col, 1.0, 0.0).astype(BF16)
            x = jnp.where(is_f, cum_c, gv)
            x_hi = x.astype(BF16)
            x_lo = (x - x_hi.astype(F32)).astype(BF16)
            full = jnp.dot(x_hi, sel, preferred_element_type=F32) + jnp.dot(x_lo, sel, preferred_element_type=F32)
            q_all = jnp.where(is_ctx, q_c[bi], q_l[bi])
            k_all = jnp.where(is_ctx, k_c[bi], k_l[bi]).astype(F32) * (d ** -0.5)
            v_all = jnp.where(is_ctx, v_c[bi], v_l[bi])
            for h in range(ML_HEADS):
                ci = di * 8 + h
                cf = di * 8 + 4 + h
                sl = slice(h * d, (h + 1) * d)
                b_full = full[:, h * LANES:(h + 1) * LANES]
                chains.append(dict(
                    idx=(bi * 2 + di) * ML_HEADS + h, keep=keep, q=q_all[:, sl], k=k_all[:, sl], v=v_all[:, sl],
                    b_full=b_full, i_full=full[:, (ML_HEADS + h) * LANES:(ML_HEADS + h + 1) * LANES],
                    b_r=cum_r[cf:cf + 1, :], i_r=gt[ci:ci + 1, :],
                    bl=b_full[T - 1:T, :] if di == 0 else b_full[0:1, :]))
    for c in chains:
        c["m_old"] = m_ref[c["idx"]]
        c["a"] = c["b_full"] + c["m_old"]
        c["dmat"] = jnp.where(c["keep"], c["b_full"] - c["b_r"] + c["i_r"], -1e30)
    for c in chains:
        c["mt"] = jnp.maximum(c["a"], jnp.broadcast_to(jnp.max(c["dmat"], axis=-1, keepdims=True), (T, T)))
    for c in chains:
        c["qk"] = lax.dot_general(c["q"], c["k"].astype(BF16), nt, preferred_element_type=F32)
    for c in chains:
        s = (c["qk"] * jnp.exp(c["dmat"] - c["mt"])).astype(BF16)
        wq = (c["q"].astype(F32) * jnp.exp(c["a"] - c["mt"])).astype(BF16)
        c["v_aug"] = jnp.concatenate([c["v"], ones_td], axis=1)
        c["ct"] = ct_ref[c["idx"]]
        lhs = jnp.concatenate([wq, s], axis=1)
        rhs = jnp.concatenate([c["ct"].astype(BF16), c["v_aug"]], axis=0)
        c["r"] = jnp.dot(lhs, rhs, preferred_element_type=F32)
    for c in chains:
        r = c["r"]
        c["h"] = r[:, :d] / jnp.maximum(jnp.abs(r[:, d:]), jnp.exp(-c["mt"]))
    for c in chains:
        src = c["bl"] - c["b_full"] + c["i_full"]
        m_new = jnp.maximum(c["bl"] + c["m_old"], jnp.max(src, axis=0, keepdims=True))
        gk = (jnp.exp(src - m_new) * c["k"]).T.astype(BF16)
        decay = jnp.exp(c["bl"] + c["m_old"] - m_new)
        ct_ref[c["idx"]] = jnp.concatenate([decay, decay], axis=1) * c["ct"] + \
            jnp.dot(gk, c["v_aug"], preferred_element_type=F32)
        m_ref[c["idx"]] = m_new
    for bi in range(ML_NB):
        for di, h_ref in enumerate((hf_ref, hb_ref)):
            base = (bi * 2 + di) * ML_HEADS
            h_ref[bi] = jnp.concatenate([chains[base + h]["h"] for h in range(ML_HEADS)], axis=1)


def _mlstm(pl_all, pc_all, gl, gc, gate_bias):
    B, L, _ = pl_all.shape
    Lc = pc_all.shape[1] // B
    T = ML_CHUNK
    nl, ncx = L // T, Lc // T
    nsteps = nl + ncx
    pc3 = pc_all.reshape(B, Lc, pc_all.shape[2])
    gc3 = gc.reshape(B, Lc, gc.shape[2])

    def lat_f(j):
        return jnp.clip(j - ncx, 0, nl - 1)

    def lat_b(j):
        return jnp.clip(nsteps - 1 - j, 0, nl - 1)

    def ctx_f(j):
        return jnp.clip(j, 0, ncx - 1)

    def ctx_b(j):
        return jnp.clip(ncx - 1 - j, 0, ncx - 1)

    def lat_specs(fn):
        base = COL_ML // ML_W
        return [pl.BlockSpec((ML_NB, T, ML_W), lambda b, j, o=o: (b, fn(j), base + o)) for o in range(3)] + \
               [pl.BlockSpec((ML_NB, T, LANES), lambda b, j: (b, fn(j), 0))]

    def ctx_specs(fn):
        base = COL_ML // ML_W
        return [pl.BlockSpec((ML_NB, T, ML_W), lambda b, j, o=o: (b, fn(j), base + o)) for o in range(3)] + \
               [pl.BlockSpec((ML_NB, T, LANES), lambda b, j: (b, fn(j), 0))]

    def out_f(b, j):
        return (b, jnp.where(j < ncx, nl + j, j - ncx), 0)

    def out_b(b, j):
        return (b, jnp.where(j < ncx, nl + ncx - 1 - j, nsteps - 1 - j), 0)

    return pl.pallas_call(
        functools.partial(_mlstm_kernel, n_ctx_chunks=ncx),
        grid=(B // ML_NB, nsteps),
        in_specs=lat_specs(lat_f) + lat_specs(lat_b) + ctx_specs(ctx_f) + ctx_specs(ctx_b) +
        [pl.BlockSpec((1, LANES), lambda b, j: (0, 0))],
        out_specs=[pl.BlockSpec((ML_NB, T, ML_W), out_f), pl.BlockSpec((ML_NB, T, ML_W), out_b)],
        out_shape=[jax.ShapeDtypeStruct((B, L + Lc, ML_W), F32)] * 2,
        scratch_shapes=[pltpu.VMEM((ML_NB * 2 * ML_HEADS, ML_HEAD_DIM, 2 * ML_HEAD_DIM), F32),
                        pltpu.VMEM((ML_NB * 2 * ML_HEADS, 1, LANES), F32)],
        compiler_params=_cparams(("parallel", "arbitrary")),
        name="mlstm",
    )(*(([pl_all] * 3 + [gl]) * 2 + ([pc3] * 3 + [gc3]) * 2 + [gate_bias]))


def _short_conv_kernel(x_ref, w_ref, b_ref, o_ref):
    x = x_ref[0].astype(F32)
    L = x.shape[0]
    row = lax.broadcasted_iota(jnp.int32, x.shape, 0)
    prev = jnp.where(row == 0, 0.0, pltpu.roll(x, 1, axis=0))
    nxt = jnp.where(row == L - 1, 0.0, pltpu.roll(x, L - 1, axis=0))
    w = w_ref[...]
    o_ref[0] = b_ref[...] + prev * w[0:1] + x * w[1:2] + nxt * w[2:3]


def _short_conv(p_all, w, b, seq_len):
    Bx, T, _ = p_all.shape
    nseq = T // seq_len
    C = w.shape[1]
    cb = COL_HY // LANES
    return pl.pallas_call(
        _short_conv_kernel,
        grid=(Bx, nseq, C // LANES),
        in_specs=[
            pl.BlockSpec((1, seq_len, LANES), lambda b, s, j: (b, s, cb + j)),
            pl.BlockSpec((3, LANES), lambda b, s, j: (0, j)),
            pl.BlockSpec((1, LANES), lambda b, s, j: (0, j)),
        ],
        out_specs=pl.BlockSpec((1, seq_len, LANES), lambda b, s, j: (b, s, j)),
        out_shape=jax.ShapeDtypeStruct((Bx, T, C), F32),
        compiler_params=_cparams(("parallel", "parallel", "parallel")),
        name="hy_short_conv",
    )(p_all, w, b.reshape(1, C))


@functools.lru_cache(maxsize=None)
def _dft_consts(L):
    N2 = HY_N2
    N = 2 * L
    N1 = N // N2
    S = N1 // 2 + 1
    k1 = np.arange(S)[:, None]
    n2 = np.arange(N2)
    tw = np.exp(-2j * np.pi * k1 * n2[None, :] / N)
    tw_tab = np.stack([np.repeat(tw.real[:, :, None], LANES, 2), np.repeat(tw.imag[:, :, None], LANES, 2)], 1)
    F = np.exp(-2j * np.pi * np.outer(n2, n2) / N2)
    M2 = np.block([[F.real, -F.imag], [F.imag, F.real]])

    def stage1(nb):
        n1 = np.arange(nb)[None, :]
        ang = 2 * np.pi * k1 * n1 / N1
        return np.cos(ang), -np.sin(ang)

    n1o = np.arange(L // N2)[:, None]
    k1o = np.arange(S)[None, :]
    ang = 2 * np.pi * n1o * k1o / N1
    wgt = np.where((k1o == 0) | (k1o == N1 // 2), 1.0, 2.0) / N
    return dict(N=N, N1=N1, S=S, stage1=stage1, tw=tw_tab.astype(np.float32),
                m2=M2.astype(np.float32).astype(BF16), m2t=M2.T.astype(np.float32).astype(BF16),
                icr=np.cos(ang) * wgt, ici=-np.sin(ang) * wgt)


def _dft_mm(m_ref, x):
    return jnp.dot(m_ref[...], x.astype(BF16), preferred_element_type=F32)


def _coef_acc(acc, c, x):
    if x is None or abs(c) < 1e-12:
        return acc
    if abs(c - 1.0) < 1e-12:
        return x if acc is None else acc + x
    if abs(c + 1.0) < 1e-12:
        return -x if acc is None else acc - x
    return c * x if acc is None else acc + c * x


def _vadd(a, b):
    return b if a is None else a if b is None else a + b


def _vsub(a, b):
    return (None if b is None else -b) if a is None else a if b is None else a - b


def _slot_groups(N1):
    half = N1 // 2
    return [(k, half - k if half - k != k else None) for k in range(half // 2 + 1)]


def _stage1_group(load, nb, cr, ci, k, kp, tw_ref, a_ref):
    N2, CT = HY_N2, HY_CT
    for r0 in range(0, N2, HY_SUB):
        if kp is None:
            ar = ai = None
            for n1 in range(nb):
                xb = load(n1, r0)
                ar = _coef_acc(ar, cr[k, n1], xb)
                ai = _coef_acc(ai, ci[k, n1], xb)
            slots = [(k, ar, ai)]
        else:
            even = [None, None]
            odd = [None, None]
            for n1 in range(nb):
                xb = load(n1, r0)
                tgt = even if n1 % 2 == 0 else odd
                tgt[0] = _coef_acc(tgt[0], cr[k, n1], xb)
                tgt[1] = _coef_acc(tgt[1], ci[k, n1], xb)
            slots = [(k, _vadd(even[0], odd[0]), _vadd(even[1], odd[1])),
                     (kp, _vsub(even[0], odd[0]), _vsub(odd[1], even[1]))]
        for idx, (kk, ar, ai) in enumerate(slots):
            zero = jnp.zeros((HY_SUB, CT), F32)
            ar = zero if ar is None else ar
            if kk > 0:
                twr = tw_ref[kk, 0, r0:r0 + HY_SUB, :]
                twi = tw_ref[kk, 1, r0:r0 + HY_SUB, :]
                if ai is None:
                    ar, ai = ar * twr, ar * twi
                else:
                    ar, ai = ar * twr - ai * twi, ar * twi + ai * twr
            ai = zero if ai is None else ai
            a_ref[r0:r0 + HY_SUB, idx * CT:(idx + 1) * CT] = ar
            a_ref[N2 + r0:N2 + r0 + HY_SUB, idx * CT:(idx + 1) * CT] = ai


def _stage1_inv_group(b_ref, nb, icr, ici, k, kp, tw_ref, acc_ref, first):
    N2, CT = HY_N2, HY_CT

    def load(idx, kk, r0):
        br = b_ref[r0:r0 + HY_SUB, idx * CT:(idx + 1) * CT]
        bi = b_ref[N2 + r0:N2 + r0 + HY_SUB, idx * CT:(idx + 1) * CT]
        if kk > 0:
            twr = tw_ref[kk, 0, r0:r0 + HY_SUB, :]
            twi = tw_ref[kk, 1, r0:r0 + HY_SUB, :]
            br, bi = br * twr + bi * twi, bi * twr - br * twi
        return br, bi

    for r0 in range(0, N2, HY_SUB):
        br, bi = load(0, k, r0)
        if kp is None:
            q_even = q_odd = (br, bi)
        else:
            br2, bi2 = load(1, kp, r0)
            q_even = (br + br2, bi - bi2)
            q_odd = (br - br2, bi + bi2)
        for n1 in range(nb):
            qr, qi = q_even if n1 % 2 == 0 else q_odd
            contrib = _coef_acc(_coef_acc(None, icr[n1, k], qr), ici[n1, k], qi)
            rows = slice(n1 * N2 + r0, n1 * N2 + r0 + HY_SUB)
            if first:
                acc_ref[rows, :] = contrib
            elif contrib is not None:
                acc_ref[rows, :] += contrib


def _filter_spec_kernel(hf_ref, hb_ref, tw_ref, m2_ref, o_ref, a_ref, *, L):
    c = _dft_consts(L)
    N2, CT = HY_N2, HY_CT
    nb = L // N2
    cr, ci = c["stage1"](nb)
    for k, kp in _slot_groups(c["N1"]):
        w = CT if kp is None else 2 * CT
        specs = []
        for h_ref in (hf_ref, hb_ref):
            _stage1_group(lambda n1, r0: h_ref[n1 * N2 + r0:n1 * N2 + r0 + HY_SUB, :], nb, cr, ci, k, kp, tw_ref, a_ref)
            specs.append(_dft_mm(m2_ref, a_ref[:, :w]))
        xf, xb = specs
        g = jnp.concatenate([xf[:N2] + xb[:N2], xf[N2:] - xb[N2:]], axis=0)
        o_ref[k] = g[:, :CT]
        if kp is not None:
            o_ref[kp] = g[:, CT:]


def _filter_spectrum(hfb, L):
    c = _dft_consts(L)
    C = hfb.shape[2]
    S, R = c["S"], 2 * HY_N2
    return pl.pallas_call(
        functools.partial(_filter_spec_kernel, L=L),
        grid=(C // HY_CT,),
        in_specs=[
            pl.BlockSpec((None, L, HY_CT), lambda j: (0, 0, j)),
            pl.BlockSpec((None, L, HY_CT), lambda j: (1, 0, j)),
            pl.BlockSpec((S, 2, HY_N2, LANES), lambda j: (0, 0, 0, 0)),
            pl.BlockSpec((R, R), lambda j: (0, 0)),
        ],
        out_specs=pl.BlockSpec((S, R, HY_CT), lambda j: (0, 0, j)),
        out_shape=jax.ShapeDtypeStruct((S, R, C), F32),
        scratch_shapes=[pltpu.VMEM((R, 2 * HY_CT), F32)],
        compiler_params=_cparams(("parallel",)),
        name="hy_filter_spectrum",
    )(hfb, hfb, jnp.asarray(c["tw"]), jnp.asarray(c["m2"]))


def _long_conv_kernel(u_ref, gate_ref, spec_ref, skip_ref, tw_ref, m2_ref, m2t_ref, o_ref, a_ref, b_ref, acc_ref,
                      *, L):
    c = _dft_consts(L)
    N2, CT = HY_N2, HY_CT
    nb = L // N2
    cr, ci = c["stage1"](nb)
    icr, ici = c["icr"] * c["N"], c["ici"] * c["N"]
    for gi_, (k, kp) in enumerate(_slot_groups(c["N1"])):
        w = CT if kp is None else 2 * CT
        _stage1_group(lambda n1, r0: u_ref[0, n1 * N2 + r0:n1 * N2 + r0 + HY_SUB, :], nb, cr, ci, k, kp, tw_ref, a_ref)
        x = _dft_mm(m2_ref, a_ref[:, :w])
        g = spec_ref[k] if kp is None else jnp.concatenate([spec_ref[k], spec_ref[kp]], axis=1)
        xr, xi, gr, gi = x[:N2], x[N2:], g[:N2], g[N2:]
        y = jnp.concatenate([xr * gr - xi * gi, xr * gi + xi * gr], axis=0)
        b_ref[:, :w] = _dft_mm(m2t_ref, y)
        _stage1_inv_group(b_ref, nb, icr, ici, k, kp, tw_ref, acc_ref, gi_ == 0)
    o_ref[0] = (gate_ref[0] * (acc_ref[...] * (1.0 / c["N"]) + skip_ref[...] * u_ref[0])).astype(o_ref.dtype)


def _long_conv(u_arr, u_col, gate_arr, gate_col, spec, spec_col, skip, L, out_dtype):
    Bx, T, _ = u_arr.shape
    nseq = T // L
    c = _dft_consts(L)
    C = skip.shape[0]
    S, R = c["S"], 2 * HY_N2
    ub, gb, sb = u_col // HY_CT, gate_col // HY_CT, spec_col // HY_CT
    return pl.pallas_call(
        functools.partial(_long_conv_kernel, L=L),
        grid=(C // HY_CT, Bx, nseq),
        in_specs=[
            pl.BlockSpec((1, L, HY_CT), lambda j, b, s: (b, s, ub + j)),
            pl.BlockSpec((1, L, HY_CT), lambda j, b, s: (b, s, gb + j)),
            pl.BlockSpec((S, R, HY_CT), lambda j, b, s: (0, 0, sb + j)),
            pl.BlockSpec((1, HY_CT), lambda j, b, s: (0, j)),
            pl.BlockSpec((S, 2, HY_N2, LANES), lambda j, b, s: (0, 0, 0, 0)),
            pl.BlockSpec((R, R), lambda j, b, s: (0, 0)),
            pl.BlockSpec((R, R), lambda j, b, s: (0, 0)),
        ],
        out_specs=pl.BlockSpec((1, L, HY_CT), lambda j, b, s: (b, s, j)),
        out_shape=jax.ShapeDtypeStruct((Bx, T, C), out_dtype),
        scratch_shapes=[pltpu.VMEM((R, 2 * HY_CT), F32), pltpu.VMEM((R, 2 * HY_CT), F32),
                        pltpu.VMEM((L, HY_CT), F32)],
        compiler_params=_cparams(("parallel", "parallel", "parallel")),
        name="hy_long_conv",
    )(u_arr, gate_arr, spec, skip.reshape(1, C), jnp.asarray(c["tw"]), jnp.asarray(c["m2"]), jnp.asarray(c["m2t"]))


def _dot3(a, b):
    a_hi = a.astype(BF16)
    a_lo = (a - a_hi.astype(F32)).astype(BF16)
    b_hi = b.astype(BF16)
    b_lo = (b - b_hi.astype(F32)).astype(BF16)
    return jnp.dot(a_hi, b_hi, preferred_element_type=F32) + \
        (jnp.dot(a_hi, b_lo, preferred_element_type=F32) + jnp.dot(a_lo, b_hi, preferred_element_type=F32))


def _filter_gen_kernel(z_ref, w1_ref, b1_ref, w2_ref, b2_ref, freq_ref, w3_ref, delta_ref, o_ref, h_ref):
    L = z_ref.shape[0]

    @pl.when(pl.program_id(0) == 0)
    def _():
        h = jnp.sin(freq_ref[0:1, :] * (_dot3(z_ref[...], w1_ref[...]) + b1_ref[...]))
        h_ref[...] = jnp.sin(freq_ref[1:2, :] * (_dot3(h, w2_ref[...]) + b2_ref[...]))

    t_norm = lax.broadcasted_iota(jnp.int32, (L, LANES), 0).astype(F32) / max(L - 1, 1)
    h = _dot3(h_ref[...], w3_ref[...]) * jnp.exp(-t_norm * delta_ref[...])
    o_ref[...] = h / jnp.sum(jnp.abs(h), axis=0, keepdims=True)


def _hyena_filters(L, w1, b1, w2, b2, w3, freq):
    t = jnp.arange(L, dtype=F32)
    t_norm = t / max(L - 1, 1)
    w = 2.0 * math.pi * t / L
    bands = jnp.linspace(1e-4, HY_EMB_BANDS - 1, HY_EMB_BANDS, dtype=F32)
    z = jnp.concatenate([t_norm[:, None], jnp.cos(w[:, None] * bands), -jnp.sin(w[:, None] * bands)], axis=-1)
    deltas = jnp.abs(jnp.linspace(math.log(HY_DECAY_TARGET) / HY_DECAY_SLOW,
                                  math.log(HY_DECAY_TARGET) / HY_DECAY_FAST, HY_WIDTH, dtype=F32))
    E, FW = z.shape[1], w1.shape[1]
    nct = HY_WIDTH // LANES
    return pl.pallas_call(
        _filter_gen_kernel,
        grid=(HY_ORDER * 2 * nct,),
        in_specs=[
            pl.BlockSpec((L, E), lambda j: (0, 0)),
            pl.BlockSpec((E, FW), lambda j: (0, 0)),
            pl.BlockSpec((1, FW), lambda j: (0, 0)),
            pl.BlockSpec((FW, FW), lambda j: (0, 0)),
            pl.BlockSpec((1, FW), lambda j: (0, 0)),
            pl.BlockSpec((2, FW), lambda j: (0, 0)),
            pl.BlockSpec((FW, LANES), lambda j: (0, j)),
            pl.BlockSpec((1, LANES), lambda j: (0, j % nct)),
        ],
        out_specs=pl.BlockSpec((None, L, LANES), lambda j: ((j // nct) % 2, 0, (j // (2 * nct)) * nct + j % nct)),
        out_shape=jax.ShapeDtypeStruct((2, L, HY_ORDER * HY_WIDTH), F32),
        scratch_shapes=[pltpu.VMEM((L, FW), F32)],
        compiler_params=_cparams(("arbitrary",)),
        name="hy_filter_gen",
    )(z, w1, b1.reshape(1, FW), w2, b2.reshape(1, FW), freq, w3, deltas.reshape(1, HY_WIDTH))


def _hyena_spectra(L, hy_fp):
    return _filter_spectrum(_hyena_filters(L, *hy_fp), L)


def _hyena(p_all, sw, sb, spec, skip, L):
    u3 = _short_conv(p_all, sw, sb, L)
    W = HY_WIDTH
    z1 = _long_conv(u3, 2 * W, u3, 0, spec, 0, skip[0], L, F32)
    return _long_conv(z1, 0, u3, W, spec, W, skip[1], L, BF16)


def _merge_kernel(x_ref, gs_ref, sh_ref, att_ref, hf_ref, hb_ref, op_ref, hy_ref, mg_ref, wb_ref, wg_ref, wo_ref,
                  g_ref, o_ref):
    d = ML_HEAD_DIM
    D = D_MODEL
    x = x_ref[0]
    h1 = _rms_mod(x, gs_ref[0], sh_ref[0])
    hsum = hf_ref[0] + hb_ref[0]
    parts = []
    for hh in range(ML_HEADS):
        hs = hsum[:, hh * d:(hh + 1) * d]
        parts.append(hs * lax.rsqrt(jnp.mean(hs * hs, axis=-1, keepdims=True) + EPS))
    mls = jnp.concatenate(parts, axis=1) * mg_ref[...] * _sigmoid(op_ref[0].astype(F32))
    y = None
    for r, br in enumerate((att_ref[0], mls.astype(BF16), hy_ref[0])):
        gate = _sigmoid(jnp.dot(h1, wg_ref[:, r * D:(r + 1) * D], preferred_element_type=F32))
        term = gate * jnp.dot(br, wb_ref[r], preferred_element_type=F32)
        y = term if y is None else y + term
    o_ref[0] = x + g_ref[0] * jnp.dot(y.astype(BF16), wo_ref[...], preferred_element_type=F32)


def _merge(x, gs, sh, att, hf, hb, p_all, hy, ml_g, wb, wg, wo, g, tm, layer):
    Bx, T, D = x.shape
    W = ML_W
    row = lambda b, i: (b, i, 0)
    vec = lambda b, i: (b, 0, 0)
    return pl.pallas_call(
        _merge_kernel,
        grid=(Bx, T // tm),
        in_specs=[
            pl.BlockSpec((1, tm, D), row),
            pl.BlockSpec((1, 1, D), vec),
            pl.BlockSpec((1, 1, D), vec),
            pl.BlockSpec((1, tm, W), row),
            pl.BlockSpec((1, tm, W), row),
            pl.BlockSpec((1, tm, W), row),
            pl.BlockSpec((1, tm, W), lambda b, i: (b, i, COL_ML // W + 3)),
            pl.BlockSpec((1, tm, W), row),
            pl.BlockSpec((1, W), lambda b, i: (0, 0)),
            pl.BlockSpec((N_BRANCH, W, D), lambda b, i: (0, 0, 0)),
            pl.BlockSpec((D, N_BRANCH * D), lambda b, i: (0, 0)),
            pl.BlockSpec((None, D, D), lambda b, i: (layer, 0, 0)),
            pl.BlockSpec((1, 1, D), vec),
        ],
        out_specs=pl.BlockSpec((1, tm, D), row),
        out_shape=jax.ShapeDtypeStruct((Bx, T, D), F32),
        compiler_params=_cparams(("parallel", "parallel")),
        name="merge",
    )(x, gs, sh, att, hf, hb, p_all, hy, ml_g.reshape(1, W), wb, wg, wo, g)


FFN_TC = 1408
FFN_HALO = 8


def _ffn_kernel(x_ref, xp_ref, xn_ref, gs_ref, sh_ref, wu_ref, cw_ref, cb_ref, wd_ref, g_ref, o_ref, *, tiles_per_seq):
    i = pl.program_id(1)
    tm = x_ref.shape[1]
    x = x_ref[0]
    h = _rms_mod(x, gs_ref[0], sh_ref[0])
    hh = _rms_mod(jnp.concatenate([xp_ref[0], xn_ref[0]], axis=0), gs_ref[0], sh_ref[0])
    first = (i % tiles_per_seq) == 0
    last = (i % tiles_per_seq) == tiles_per_seq - 1
    row8 = lax.broadcasted_iota(jnp.int32, (8, FFN_TC), 0)

    def conv_cols(c0):
        cs = slice(c0, c0 + FFN_TC)
        u = jnp.dot(h, wu_ref[:, cs], preferred_element_type=F32)
        uh = jnp.dot(hh, wu_ref[:, cs], preferred_element_type=F32)
        pr = jnp.where(first, 0.0, uh[FFN_HALO - 1:FFN_HALO])
        nx = jnp.where(last, 0.0, uh[FFN_HALO:FFN_HALO + 1])
        prev = pltpu.roll(u, 1, axis=0)
        nxt = pltpu.roll(u, tm - 1, axis=0)
        prev = jnp.concatenate([jnp.where(row8 == 0, pr, prev[:8]), prev[8:]], axis=0)
        nxt = jnp.concatenate([nxt[:tm - 8], jnp.where(row8 == 7, nx, nxt[tm - 8:])], axis=0)
        cw = cw_ref[:, cs]
        return cb_ref[:, cs] + prev * cw[0:1] + u * cw[1:2] + nxt * cw[2:3]

    acc = None
    for c0 in range(0, D_FF, FFN_TC):
        gate = conv_cols(c0)
        val = conv_cols(D_FF + c0)
        act = (gate * _sigmoid(gate) * val).astype(BF16)
        part = jnp.dot(act, wd_ref[c0:c0 + FFN_TC, :], preferred_element_type=F32)
        acc = part if acc is None else acc + part
    o_ref[0] = x + g_ref[0] * acc


def _conv_ffn(x, gs, sh, wu, cw, cb, wd, g, seq_len, layer):
    Bx, T, D = x.shape
    tm = min(seq_len, 512)
    hb = tm // FFN_HALO
    nh = T // FFN_HALO
    vec = lambda b, i: (b, 0, 0)
    const = lambda b, i: (0, 0)
    lconst = lambda b, i: (layer, 0, 0)
    once = pl.Buffered(1)
    return pl.pallas_call(
        functools.partial(_ffn_kernel, tiles_per_seq=seq_len // tm),
        grid=(Bx, T // tm),
        in_specs=[
            pl.BlockSpec((1, tm, D), lambda b, i: (b, i, 0)),
            pl.BlockSpec((1, FFN_HALO, D), lambda b, i: (b, jnp.maximum(i * hb - 1, 0), 0)),
            pl.BlockSpec((1, FFN_HALO, D), lambda b, i: (b, jnp.minimum((i + 1) * hb, nh - 1), 0)),
            pl.BlockSpec((1, 1, D), vec),
            pl.BlockSpec((1, 1, D), vec),
            pl.BlockSpec((None, D, 2 * D_FF), lconst, pipeline_mode=once),
            pl.BlockSpec((3, 2 * D_FF), const, pipeline_mode=once),
            pl.BlockSpec((1, 2 * D_FF), const, pipeline_mode=once),
            pl.BlockSpec((None, D_FF, D), lconst, pipeline_mode=once),
            pl.BlockSpec((1, 1, D), vec),
        ],
        out_specs=pl.BlockSpec((1, tm, D), lambda b, i: (b, i, 0)),
        out_shape=jax.ShapeDtypeStruct((Bx, T, D), F32),
        compiler_params=_cparams(("parallel", "parallel")),
        name="conv_ffn",
    )(x, x, x, gs, sh, wu, cw, cb.reshape(1, -1), wd, g)


def _pair_perm():
    hd, half = ATT_HEAD_DIM, ATT_HEAD_DIM // 2
    n_pair = ATT_Q // LANES
    qperm = []
    for p in range(n_pair):
        for sub in range(4):
            head = p if sub % 2 == 0 else n_pair + p
            qperm += [head * hd + (sub // 2) * half + dd for dd in range(half)]
    kperm = []
    for sub in range(4):
        kperm += [(sub % 2) * hd + (sub // 2) * half + dd for dd in range(half)]
    return np.asarray(qperm), np.asarray(kperm)


def _rope_tables(L):
    rows = L // GRID_W
    row = jnp.repeat(jnp.arange(rows, dtype=F32), GRID_W)
    col = jnp.tile(jnp.arange(GRID_W, dtype=F32), rows)
    nf = ATT_HEAD_DIM // 4
    inv = ROPE_THETA ** (-jnp.arange(nf, dtype=F32) / nf)
    ang = jnp.concatenate([row[:, None] * inv, col[:, None] * inv], axis=-1)
    cos, sin = jnp.cos(ang), jnp.sin(ang)
    return jnp.concatenate([cos, cos, cos, cos, -sin, -sin, sin, sin], axis=1)


def _take_runs(t, idx, axis):
    runs, start = [], 0
    for p in range(1, len(idx) + 1):
        if p == len(idx) or idx[p] != idx[p - 1] + 1:
            runs.append(lax.slice_in_dim(t, int(idx[start]), int(idx[p - 1]) + 1, axis=axis))
            start = p
    return jnp.concatenate(runs, axis=axis)


def _prep_w_in(w):
    qperm, kperm = _pair_perm()
    o = IN_OFFSETS
    w = w.astype(BF16)
    cols = [_take_runs(w, qperm, 1), w[:, o[2]:o[6]], w[:, o[7]:o[8]],
            _take_runs(w, o[0] + kperm, 1), _take_runs(w, o[1] + kperm, 1)]
    wc = jnp.concatenate(cols, axis=1)
    w_gate = jnp.pad(w[:, o[6]:o[7]], ((0, 0), (0, LANES - ML_GATES)))
    return jnp.pad(wc, ((0, 0), (0, IN_PAD - wc.shape[1]))), w_gate, w[:, o[8]:]


def _sink_cols(sink, rows_per_pair):
    n_pair = ATT_Q // LANES
    lo = jnp.repeat(sink[:n_pair], rows_per_pair)
    hi = jnp.repeat(sink[n_pair:], rows_per_pair)
    return jnp.stack([lo, hi])[:, :, None]


def kernel(x, c, ctx, c_ctx, ada_w, ada_b, norm1_g, norm2_g, w_in, att_sink, ml_gate_b, ml_norm_g,
           hy_short_w, hy_short_b, hy_w1, hy_b1, hy_w2, hy_b2, hy_w3, hy_freq, hy_skip,
           w_branch, w_out, w_up, ffn_conv_w, ffn_conv_b, w_down, final_g):
    B, L, D = x.shape
    Lc = ctx.shape[1]
    qperm, _ = _pair_perm()
    tabs = _rope_tables(L)
    sc_rows = jnp.concatenate([jax.nn.silu(c), jax.nn.silu(c_ctx)[None], jnp.zeros((8 - B - 1, D), F32)], axis=0)
    xl = x
    xc = ctx.reshape(1, B * Lc, D)
    wo, wu, wd = w_out.astype(BF16), w_up.astype(BF16), w_down.astype(BF16)
    for l in range(DEPTH):
        need_ctx = l < DEPTH - 1
        mod = _ada_proj(sc_rows, ada_w, ada_b, l)
        sh1, sc1, g1, sh2, sc2, g2 = [m[:B, None, :] for m in jnp.split(mod, 6, axis=-1)]
        csh1, csc1, cg1, csh2, csc2, cg2 = [m[B:B + 1, None, :] for m in jnp.split(mod, 6, axis=-1)]
        n1 = norm1_g[l][None, None, :]
        n2 = norm2_g[l][None, None, :]
        w_in_b, w_gate, w_bg = _prep_w_in(w_in[l])
        wb_l = w_branch[l].astype(BF16)
        wb = jnp.concatenate([_take_runs(wb_l[:1], qperm, 1), wb_l[1:]], axis=0)
        gate_bias = jnp.pad(ml_gate_b[l].reshape(1, ML_GATES), ((0, 0), (0, LANES - ML_GATES)))
        hy_fp = (hy_w1[l], hy_b1[l], hy_w2[l], hy_b2[l], hy_w3[l], hy_freq[l])

        pl_all, gl = _norm_proj(xl, n1 * (1.0 + sc1), sh1, w_in_b, IN_TN, w_gate)
        pc_all, gc = _norm_proj(xc, n1 * (1.0 + csc1), csh1, w_in_b, IN_TN, w_gate)
        att_l = _win_attention(pl_all, pc_all, tabs, _sink_cols(att_sink[l], ATT_BLOCK))
        hf, hb = _mlstm(pl_all, pc_all, gl, gc, gate_bias)
        hy_l = _hyena(pl_all, hy_short_w[l], hy_short_b[l], _hyena_spectra(L, hy_fp), hy_skip[l], L)
        xl = _merge(xl, n1 * (1.0 + sc1), sh1, att_l, hf, hb, pl_all, hy_l, ml_norm_g[l], wb, w_bg, wo, g1, 512, l)
        xl = _conv_ffn(xl, n2 * (1.0 + sc2), sh2, wu, ffn_conv_w[l], ffn_conv_b[l], wd, g2, L, l)
        if need_ctx:
            att_c = _ctx_attention(pc_all, _sink_cols(att_sink[l], Lc), B)
            hy_c = _hyena(pc_all, hy_short_w[l], hy_short_b[l], _hyena_spectra(Lc, hy_fp), hy_skip[l], Lc)
            hfc = hf[:, L:].reshape(1, B * Lc, ML_W)
            hbc = hb[:, L:].reshape(1, B * Lc, ML_W)
            xc = _merge(xc, n1 * (1.0 + csc1), csh1, att_c, hfc, hbc, pc_all, hy_c, ml_norm_g[l], wb, w_bg, wo,
                        cg1, Lc, l)
            xc = _conv_ffn(xc, n2 * (1.0 + csc2), csh2, wu, ffn_conv_w[l], ffn_conv_b[l], wd, cg2, Lc, l)
    return _final_norm(xl, final_g)
```

```python
import functools
import math

import numpy as np
import jax
import jax.numpy as jnp
from jax import lax
from jax.experimental import pallas as pl
from jax.experimental.pallas import tpu as pltpu

F32 = jnp.float32
BF16 = jnp.bfloat16

D_MODEL = 1024
DEPTH = 4
GRID_W = 64
EPS = 1e-6
ATT_HEADS = 8
ATT_KV_HEADS = 2
ATT_HEAD_DIM = 64
ATT_BLOCK = 128
ROPE_THETA = 10000.0
ML_HEADS = 4
ML_HEAD_DIM = 128
ML_CHUNK = 128
ML_NB = 4
HY_WIDTH = 512
HY_ORDER = 2
HY_EMB_BANDS = 16
HY_DECAY_TARGET = 1e-2
HY_DECAY_FAST = 0.3
HY_DECAY_SLOW = 1.5
D_FF = 2816
N_BRANCH = 3
LANES = 128

ATT_Q = ATT_HEADS * ATT_HEAD_DIM
ATT_KV = ATT_KV_HEADS * ATT_HEAD_DIM
ML_W = ML_HEADS * ML_HEAD_DIM
ML_GATES = 2 * 2 * ML_HEADS
IN_SIZES = (ATT_Q, ATT_KV, ATT_KV, ML_W, ML_W, ML_W, ML_W, ML_GATES, 3 * HY_WIDTH, N_BRANCH * D_MODEL)
IN_OFFSETS = tuple(int(o) for o in np.cumsum(IN_SIZES)[:-1])

COL_Q = 0
COL_ML = COL_Q + ATT_Q
COL_HY = COL_ML + 4 * ML_W
COL_K = COL_HY + 3 * HY_WIDTH
COL_V = COL_K + ATT_KV
IN_PAD = 4608
IN_TN = 1536
IN_TM = 2048

HY_N2 = 256
HY_CT = 128
HY_SUB = 32
HY_SC_W = 256

VMEM_LIMIT = 56 * 1024 * 1024


def _cparams(sem):
    return pltpu.CompilerParams(dimension_semantics=sem, vmem_limit_bytes=VMEM_LIMIT)


def _sigmoid(x):
    return 0.5 * jnp.tanh(0.5 * x) + 0.5


def _log_sigmoid(x):
    return jnp.minimum(x, 0.0) - jnp.log(1.0 + jnp.exp(-jnp.abs(x)))


def _rms_mod(x, gs, sh):
    ms = jnp.mean(x * x, axis=-1, keepdims=True)
    return (x * lax.rsqrt(ms + EPS) * gs + sh).astype(BF16)


def _norm_proj_kernel(x_ref, gs_ref, sh_ref, w_ref, *rest, with_aux):
    if with_aux:
        wa_ref, o_ref, oa_ref, h_ref = rest
    else:
        o_ref, h_ref = rest

    @pl.when(pl.program_id(2) == 0)
    def _():
        h = _rms_mod(x_ref[0], gs_ref[0], sh_ref[0])
        h_ref[...] = h
        if with_aux:
            oa_ref[0] = jnp.dot(h, wa_ref[...], preferred_element_type=F32)

    o_ref[0] = jnp.dot(h_ref[...], w_ref[...], preferred_element_type=F32).astype(o_ref.dtype)


def _norm_proj(x, gs, sh, w, tn, w_aux=None):
    B, T, D = x.shape
    N = w.shape[1]
    tm = min(T, IN_TM)
    in_specs = [
        pl.BlockSpec((1, tm, D), lambda b, i, j: (b, i, 0)),
        pl.BlockSpec((1, 1, D), lambda b, i, j: (b, 0, 0)),
        pl.BlockSpec((1, 1, D), lambda b, i, j: (b, 0, 0)),
        pl.BlockSpec((D, tn), lambda b, i, j: (0, j)),
    ]
    out_specs = [pl.BlockSpec((1, tm, tn), lambda b, i, j: (b, i, j))]
    out_shape = [jax.ShapeDtypeStruct((B, T, N), BF16)]
    args = [x, gs, sh, w]
    if w_aux is not None:
        na = w_aux.shape[1]
        in_specs.append(pl.BlockSpec((D, na), lambda b, i, j: (0, 0)))
        out_specs.append(pl.BlockSpec((1, tm, na), lambda b, i, j: (b, i, 0)))
        out_shape.append(jax.ShapeDtypeStruct((B, T, na), F32))
        args.append(w_aux)
    outs = pl.pallas_call(
        functools.partial(_norm_proj_kernel, with_aux=w_aux is not None),
        grid=(B, T // tm, N // tn),
        in_specs=in_specs,
        out_specs=out_specs,
        out_shape=out_shape,
        scratch_shapes=[pltpu.VMEM((tm, D), BF16)],
        compiler_params=_cparams(("parallel", "parallel", "arbitrary")),
        name="norm_proj",
    )(*args)
    return outs if w_aux is not None else outs[0]


def _small_proj_kernel(a_ref, w_ref, b_ref, o_ref):
    a = a_ref[...]
    a_hi = a.astype(BF16)
    a_lo = (a - a_hi.astype(F32)).astype(BF16)
    w = w_ref[...]
    w_hi = w.astype(BF16)
    w_lo = (w - w_hi.astype(F32)).astype(BF16)
    acc = jnp.dot(a_hi, w_hi, preferred_element_type=F32)
    acc = acc + (jnp.dot(a_hi, w_lo, preferred_element_type=F32) + jnp.dot(a_lo, w_hi, preferred_element_type=F32))
    o_ref[...] = acc + b_ref[...]


def _ada_proj(a, w, b, layer):
    M, D = a.shape
    depth, _, N = w.shape
    tn = 1024
    return pl.pallas_call(
        _small_proj_kernel,
        grid=(N // tn,),
        in_specs=[
            pl.BlockSpec((M, D), lambda j: (0, 0)),
            pl.BlockSpec((None, D, tn), lambda j: (layer, 0, j)),
            pl.BlockSpec((None, 1, tn), lambda j: (layer, 0, j)),
        ],
        out_specs=pl.BlockSpec((M, tn), lambda j: (0, j)),
        out_shape=jax.ShapeDtypeStruct((M, N), F32),
        compiler_params=_cparams(("parallel",)),
        name="ada_proj",
    )(a, w, b.reshape(depth, 1, N))


def _final_norm_kernel(x_ref, g_ref, o_ref):
    x = x_ref[0]
    ms = jnp.mean(x * x, axis=-1, keepdims=True)
    o_ref[0] = x * lax.rsqrt(ms + EPS) * g_ref[...]


def _final_norm(x, g):
    B, T, D = x.shape
    tm = 512
    return pl.pallas_call(
        _final_norm_kernel,
        grid=(B, T // tm),
        in_specs=[
            pl.BlockSpec((1, tm, D), lambda b, i: (b, i, 0)),
            pl.BlockSpec((1, D), lambda b, i: (0, 0)),
        ],
        out_specs=pl.BlockSpec((1, tm, D), lambda b, i: (b, i, 0)),
        out_shape=jax.ShapeDtypeStruct((B, T, D), F32),
        compiler_params=_cparams(("parallel", "parallel")),
        name="final_norm",
    )(x, g.reshape(1, D))


def _lane_lo_mask(shape):
    lane = lax.broadcasted_iota(jnp.int32, shape, len(shape) - 1)
    return (lane % 64) < 32


def _rope(x, tab):
    c = tab[:, :LANES]
    s = tab[:, LANES:]
    outs = []
    for g in range(x.shape[1] // LANES):
        xg = x[:, g * LANES:(g + 1) * LANES]
        outs.append(xg * c + pltpu.roll(xg, 64, axis=1) * s)
    return outs[0] if len(outs) == 1 else jnp.concatenate(outs, axis=1)


def _attend(qs, kcat, vcat, bias, sink_lo, sink_hi):
    lo = _lane_lo_mask(kcat.shape)
    T = ATT_BLOCK
    outs = []
    for msk, sink in ((lo, sink_lo), (jnp.logical_not(lo), sink_hi)):
        kh = jnp.where(msk, kcat, 0.0).astype(BF16)
        s = lax.dot_general(qs, kh, (((1,), (1,)), ((), ())), preferred_element_type=F32)
        if bias is not None:
            c0, bias_prev, bias_next = bias
            s = jnp.concatenate([s[:, :c0], s[:, c0:c0 + T] + bias_prev, s[:, c0 + T:c0 + 2 * T],
                                 s[:, c0 + 2 * T:] + bias_next], axis=1)
        m = jnp.maximum(jnp.max(s, axis=-1, keepdims=True), sink)
        p = jnp.exp(s - m)
        den = jnp.sum(p, axis=-1, keepdims=True) + jnp.exp(sink - m)
        outs.append(jnp.dot(p.astype(BF16), vcat, preferred_element_type=F32) * (1.0 / den))
    return jnp.where(_lane_lo_mask(outs[0].shape), outs[0], outs[1])


ATT_QB = 4


def _win_attn_kernel(*refs):
    nkb = ATT_QB + 2
    q_ref = refs[0]
    k_refs = refs[1:1 + nkb]
    v_refs = refs[1 + nkb:1 + 2 * nkb]
    kx_ref, vx_ref = refs[1 + 2 * nkb:3 + 2 * nkb]
    t_refs = refs[3 + 2 * nkb:3 + 3 * nkb]
    sink_ref, o_ref = refs[3 + 3 * nkb:]
    i = pl.program_id(1)
    nb = pl.num_programs(1) * ATT_QB
    T = ATT_BLOCK
    n_pair = ATT_Q // LANES
    Lc = kx_ref.shape[1]
    kx = kx_ref[0].astype(F32)
    k_rot = [_rope(k_refs[j][0].astype(F32), t_refs[j][...]) for j in range(nkb)]
    t = lax.broadcasted_iota(jnp.int32, (T, T), 0)
    s = lax.broadcasted_iota(jnp.int32, (T, T), 1)
    for sb in range(ATT_QB):
        blk = i * ATT_QB + sb
        q = _rope(q_ref[0, sb * T:(sb + 1) * T, :].astype(F32), t_refs[sb + 1][...]) * (ATT_HEAD_DIM ** -0.5)
        qs = jnp.concatenate([q[:, g * LANES:(g + 1) * LANES] for g in range(n_pair)], axis=0).astype(BF16)
        kcat = jnp.concatenate([kx, k_rot[sb], k_rot[sb + 1], k_rot[sb + 2]], axis=0)
        vcat = jnp.concatenate([vx_ref[0], v_refs[sb][0], v_refs[sb + 1][0], v_refs[sb + 2][0]], axis=0)
        bias_prev = jnp.where((s >= t) & (blk > 0), 0.0, -1e30).astype(F32)
        bias_next = jnp.where((s <= t) & (blk < nb - 1), 0.0, -1e30).astype(F32)
        bias = (Lc, jnp.concatenate([bias_prev] * n_pair, axis=0), jnp.concatenate([bias_next] * n_pair, axis=0))
        o = _attend(qs, kcat, vcat, bias, sink_ref[0], sink_ref[1])
        o_ref[0, sb * T:(sb + 1) * T, :] = jnp.concatenate(
            [o[g * T:(g + 1) * T] for g in range(n_pair)], axis=1).astype(o_ref.dtype)


def _win_attention(pl_all, pc_all, tabs, sinks):
    B, L, _ = pl_all.shape
    Lc = pc_all.shape[1] // B
    T = ATT_BLOCK
    nb = L // T
    kb, vb = COL_K // LANES, COL_V // LANES
    offs = range(-1, ATT_QB + 1)

    def blk(col, d):
        return pl.BlockSpec((1, T, LANES), lambda b, i: (b, jnp.clip(i * ATT_QB + d, 0, nb - 1), col))

    def tab(d):
        return pl.BlockSpec((T, 2 * LANES), lambda b, i: (jnp.clip(i * ATT_QB + d, 0, nb - 1), 0))

    nkb = ATT_QB + 2
    return pl.pallas_call(
        _win_attn_kernel,
        grid=(B, nb // ATT_QB),
        in_specs=[pl.BlockSpec((1, ATT_QB * T, ATT_Q), lambda b, i: (b, i, 0))] +
        [blk(kb, d) for d in offs] + [blk(vb, d) for d in offs] +
        [pl.BlockSpec((1, Lc, LANES), lambda b, i: (0, b, kb)),
         pl.BlockSpec((1, Lc, LANES), lambda b, i: (0, b, vb))] +
        [tab(d) for d in offs] +
        [pl.BlockSpec((2, ATT_Q, 1), lambda b, i: (0, 0, 0))],
        out_specs=pl.BlockSpec((1, ATT_QB * T, ATT_Q), lambda b, i: (b, i, 0)),
        out_shape=jax.ShapeDtypeStruct((B, L, ATT_Q), BF16),
        compiler_params=_cparams(("parallel", "parallel")),
        name="win_attention",
    )(*([pl_all] * (1 + 2 * nkb) + [pc_all] * 2 + [tabs] * nkb + [sinks]))


def _ctx_attn_kernel(q_ref, kx_ref, vx_ref, sink_ref, o_ref):
    Lc = q_ref.shape[1]
    n_pair = ATT_Q // LANES
    q = q_ref[0].astype(F32) * (ATT_HEAD_DIM ** -0.5)
    qs = jnp.concatenate([q[:, g * LANES:(g + 1) * LANES] for g in range(n_pair)], axis=0).astype(BF16)
    o = _attend(qs, kx_ref[0].astype(F32), vx_ref[0], None, sink_ref[0], sink_ref[1])
    o_ref[0] = jnp.concatenate([o[g * Lc:(g + 1) * Lc] for g in range(n_pair)], axis=1).astype(o_ref.dtype)


def _ctx_attention(pc_all, sinks, B):
    Lc = pc_all.shape[1] // B
    kb, vb = COL_K // LANES, COL_V // LANES
    return pl.pallas_call(
        _ctx_attn_kernel,
        grid=(B,),
        in_specs=[
            pl.BlockSpec((1, Lc, ATT_Q), lambda b: (0, b, 0)),
            pl.BlockSpec((1, Lc, LANES), lambda b: (0, b, kb)),
            pl.BlockSpec((1, Lc, LANES), lambda b: (0, b, vb)),
            pl.BlockSpec((2, ATT_Q // LANES * Lc, 1), lambda b: (0, 0, 0)),
        ],
        out_specs=pl.BlockSpec((1, Lc, ATT_Q), lambda b: (0, b, 0)),
        out_shape=jax.ShapeDtypeStruct((1, B * Lc, ATT_Q), BF16),
        compiler_params=_cparams(("parallel",)),
        name="ctx_attention",
    )(pc_all, pc_all, pc_all, sinks)


def _dot_hl(a_exact, x):
    x_hi = x.astype(BF16)
    x_lo = (x - x_hi.astype(F32)).astype(BF16)
    return jnp.dot(a_exact, x_hi, preferred_element_type=F32) + jnp.dot(a_exact, x_lo, preferred_element_type=F32)


def _mlstm_kernel(qf_l, kf_l, vf_l, gf_l, qb_l, kb_l, vb_l, gb_l,
                  qf_c, kf_c, vf_c, gf_c, qb_c, kb_c, vb_c, gb_c, bias_ref,
                  hf_ref, hb_ref, ct_ref, m_ref, *, n_ctx_chunks):
    j = pl.program_id(1)
    T = ML_CHUNK
    d = ML_HEAD_DIM
    is_ctx = j < n_ctx_chunks

    @pl.when(j == 0)
    def _():
        ct_ref[...] = jnp.zeros_like(ct_ref)
        m_ref[...] = jnp.zeros_like(m_ref)

    row = lax.broadcasted_iota(jnp.int32, (T, T), 0)
    col = lax.broadcasted_iota(jnp.int32, (T, T), 1)
    ones_td = jnp.ones((T, d), BF16)

    nt = (((1,), (1,)), ((), ()))
    sel_row = lax.broadcasted_iota(jnp.int32, (LANES, 2 * ML_HEADS * LANES), 0)
    sel_blk = lax.broadcasted_iota(jnp.int32, (LANES, 2 * ML_HEADS * LANES), 1) // LANES
    chains = []
    for bi in range(ML_NB):
        for di, (q_l, k_l, v_l, g_l, q_c, k_c, v_c, g_c) in enumerate(
                ((qf_l, kf_l, vf_l, gf_l, qf_c, kf_c, vf_c, gf_c),
                 (qb_l, kb_l, vb_l, gb_l, qb_c, kb_c, vb_c, gb_c))):
            keep = (col <= row) if di == 0 else (col >= row)
            keep_b = keep.astype(BF16)
            g = jnp.where(is_ctx, g_c[bi], g_l[bi]) + bias_ref[...]
            lane = lax.broadcasted_iota(jnp.int32, g.shape, 1)
            is_f = (lane % 8) >= 4
            gv = jnp.where(is_f, _log_sigmoid(g), g)
            gt = gv.T
            cum_c = _dot_hl(keep_b, gv)
            gt_hi = gt.astype(BF16)
            gt_lo = (gt - gt_hi.astype(F32)).astype(BF16)
            cum_r = lax.dot_general(gt_hi, keep_b, nt, preferred_element_type=F32) + \
                lax.dot_general(gt_lo, keep_b, nt, preferred_element_type=F32)
            src_col = jnp.where(sel_blk < ML_HEADS, di * 8 + ML_HEADS + sel_blk, di * 8 + sel_blk - ML_HEADS)
            sel = jnp.where(sel_row == src_col, 1.0, 0.0).astype(BF16)
            x = jnp.where(is_f, cum_c, gv)
            x_hi = x.astype(BF16)
            x_lo = (x - x_hi.astype(F32)).astype(BF16)
            full = jnp.dot(x_hi, sel, preferred_element_type=F32) + jnp.dot(x_lo, sel, preferred_element_type=F32)
            q_all = jnp.where(is_ctx, q_c[bi], q_l[bi])
            k_all = jnp.where(is_ctx, k_c[bi], k_l[bi]).astype(F32) * (d ** -0.5)
            v_all = jnp.where(is_ctx, v_c[bi], v_l[bi])
            for h in range(ML_HEADS):
                ci = di * 8 + h
                cf = di * 8 + 4 + h
                sl = slice(h * d, (h + 1) * d)
                b_full = full[:, h * LANES:(h + 1) * LANES]
                chains.append(dict(
                    idx=(bi * 2 + di) * ML_HEADS + h, keep=keep, q=q_all[:, sl], k=k_all[:, sl], v=v_all[:, sl],
                    b_full=b_full, i_full=full[:, (ML_HEADS + h) * LANES:(ML_HEADS + h + 1) * LANES],
                    b_r=cum_r[cf:cf + 1, :], i_r=gt[ci:ci + 1, :],
                    bl=b_full[T - 1:T, :] if di == 0 else b_full[0:1, :]))
    for c in chains:
        c["m_old"] = m_ref[c["idx"]]
        c["a"] = c["b_full"] + c["m_old"]
        c["dmat"] = jnp.where(c["keep"], c["b_full"] - c["b_r"] + c["i_r"], -1e30)
    for c in chains:
        c["mt"] = jnp.maximum(c["a"], jnp.broadcast_to(jnp.max(c["dmat"], axis=-1, keepdims=True), (T, T)))
    for c in chains:
        c["qk"] = lax.dot_general(c["q"], c["k"].astype(BF16), nt, preferred_element_type=F32)
    for c in chains:
        s = (c["qk"] * jnp.exp(c["dmat"] - c["mt"])).astype(BF16)
        wq = (c["q"].astype(F32) * jnp.exp(c["a"] - c["mt"])).astype(BF16)
        c["v_aug"] = jnp.concatenate([c["v"], ones_td], axis=1)
        c["ct"] = ct_ref[c["idx"]]
        lhs = jnp.concatenate([wq, s], axis=1)
        rhs = jnp.concatenate([c["ct"].astype(BF16), c["v_aug"]], axis=0)
        c["r"] = jnp.dot(lhs, rhs, preferred_element_type=F32)
    for c in chains:
        r = c["r"]
        c["h"] = r[:, :d] / jnp.maximum(jnp.abs(r[:, d:]), jnp.exp(-c["mt"]))
    for c in chains:
        src = c["bl"] - c["b_full"] + c["i_full"]
        m_new = jnp.maximum(c["bl"] + c["m_old"], jnp.max(src, axis=0, keepdims=True))
        gk = (jnp.exp(src - m_new) * c["k"]).T.astype(BF16)
        decay = jnp.exp(c["bl"] + c["m_old"] - m_new)
        ct_ref[c["idx"]] = jnp.concatenate([decay, decay], axis=1) * c["ct"] + \
            jnp.dot(gk, c["v_aug"], preferred_element_type=F32)
        m_ref[c["idx"]] = m_new
    for bi in range(ML_NB):
        for di, h_ref in enumerate((hf_ref, hb_ref)):
            base = (bi * 2 + di) * ML_HEADS
            h_ref[bi] = jnp.concatenate([chains[base + h]["h"] for h in range(ML_HEADS)], axis=1)


def _mlstm(pl_all, pc_all, gl, gc, gate_bias):
    B, L, _ = pl_all.shape
    Lc = pc_all.shape[1] // B
    T = ML_CHUNK
    nl, ncx = L // T, Lc // T
    nsteps = nl + ncx
    pc3 = pc_all.reshape(B, Lc, pc_all.shape[2])
    gc3 = gc.reshape(B, Lc, gc.shape[2])

    def lat_f(j):
        return jnp.clip(j - ncx, 0, nl - 1)

    def lat_b(j):
        return jnp.clip(nsteps - 1 - j, 0, nl - 1)

    def ctx_f(j):
        return jnp.clip(j, 0, ncx - 1)

    def ctx_b(j):
        return jnp.clip(ncx - 1 - j, 0, ncx - 1)

    def lat_specs(fn):
        base = COL_ML // ML_W
        return [pl.BlockSpec((ML_NB, T, ML_W), lambda b, j, o=o: (b, fn(j), base + o)) for o in range(3)] + \
               [pl.BlockSpec((ML_NB, T, LANES), lambda b, j: (b, fn(j), 0))]

    def ctx_specs(fn):
        base = COL_ML // ML_W
        return [pl.BlockSpec((ML_NB, T, ML_W), lambda b, j, o=o: (b, fn(j), base + o)) for o in range(3)] + \
               [pl.BlockSpec((ML_NB, T, LANES), lambda b, j: (b, fn(j), 0))]

    def out_f(b, j):
        return (b, jnp.where(j < ncx, nl + j, j - ncx), 0)

    def out_b(b, j):
        return (b, jnp.where(j < ncx, nl + ncx - 1 - j, nsteps - 1 - j), 0)

    return pl.pallas_call(
        functools.partial(_mlstm_kernel, n_ctx_chunks=ncx),
        grid=(B // ML_NB, nsteps),
        in_specs=lat_specs(lat_f) + lat_specs(lat_b) + ctx_specs(ctx_f) + ctx_specs(ctx_b) +
        [pl.BlockSpec((1, LANES), lambda b, j: (0, 0))],
        out_specs=[pl.BlockSpec((ML_NB, T, ML_W), out_f), pl.BlockSpec((ML_NB, T, ML_W), out_b)],
        out_shape=[jax.ShapeDtypeStruct((B, L + Lc, ML_W), F32)] * 2,
        scratch_shapes=[pltpu.VMEM((ML_NB * 2 * ML_HEADS, ML_HEAD_DIM, 2 * ML_HEAD_DIM), F32),
                        pltpu.VMEM((ML_NB * 2 * ML_HEADS, 1, LANES), F32)],
        compiler_params=_cparams(("parallel", "arbitrary")),
        name="mlstm",
    )(*(([pl_all] * 3 + [gl]) * 2 + ([pc3] * 3 + [gc3]) * 2 + [gate_bias]))


def _short_conv_kernel(x_ref, w_ref, b_ref, o_ref):
    x = x_ref[0].astype(F32)
    L = x.shape[0]
    row = lax.broadcasted_iota(jnp.int32, x.shape, 0)
    prev = jnp.where(row == 0, 0.0, pltpu.roll(x, 1, axis=0))
    nxt = jnp.where(row == L - 1, 0.0, pltpu.roll(x, L - 1, axis=0))
    w = w_ref[...]
    o_ref[0] = b_ref[...] + prev * w[0:1] + x * w[1:2] + nxt * w[2:3]


def _short_conv(p_all, w, b, seq_len):
    Bx, T, _ = p_all.shape
    nseq = T // seq_len
    C = w.shape[1]
    cb = COL_HY // HY_SC_W
    return pl.pallas_call(
        _short_conv_kernel,
        grid=(Bx, nseq, C // HY_SC_W),
        in_specs=[
            pl.BlockSpec((1, seq_len, HY_SC_W), lambda b, s, j: (b, s, cb + j)),
            pl.BlockSpec((3, HY_SC_W), lambda b, s, j: (0, j)),
            pl.BlockSpec((1, HY_SC_W), lambda b, s, j: (0, j)),
        ],
        out_specs=pl.BlockSpec((1, seq_len, HY_SC_W), lambda b, s, j: (b, s, j)),
        out_shape=jax.ShapeDtypeStruct((Bx, T, C), F32),
        compiler_params=_cparams(("parallel", "parallel", "parallel")),
        name="hy_short_conv",
    )(p_all, w, b.reshape(1, C))


@functools.lru_cache(maxsize=None)
def _dft_consts(L):
    N2 = HY_N2
    N = 2 * L
    N1 = N // N2
    S = N1 // 2 + 1
    k1 = np.arange(S)[:, None]
    n2 = np.arange(N2)
    tw = np.exp(-2j * np.pi * k1 * n2[None, :] / N)
    tw_tab = np.stack([np.repeat(tw.real[:, :, None], LANES, 2), np.repeat(tw.imag[:, :, None], LANES, 2)], 1)
    F = np.exp(-2j * np.pi * np.outer(n2, n2) / N2)
    M2 = np.block([[F.real, -F.imag], [F.imag, F.real]])

    def stage1(nb):
        n1 = np.arange(nb)[None, :]
        ang = 2 * np.pi * k1 * n1 / N1
        return np.cos(ang), -np.sin(ang)

    n1o = np.arange(L // N2)[:, None]
    k1o = np.arange(S)[None, :]
    ang = 2 * np.pi * n1o * k1o / N1
    wgt = np.where((k1o == 0) | (k1o == N1 // 2), 1.0, 2.0) / N
    return dict(N=N, N1=N1, S=S, stage1=stage1, tw=tw_tab.astype(np.float32),
                m2=M2.astype(np.float32).astype(BF16), m2t=M2.T.astype(np.float32).astype(BF16),
                icr=np.cos(ang) * wgt, ici=-np.sin(ang) * wgt)


def _dft_mm(m_ref, x):
    return jnp.dot(m_ref[...], x.astype(BF16), preferred_element_type=F32)


def _coef_acc(acc, c, x):
    if x is None or abs(c) < 1e-12:
        return acc
    if abs(c - 1.0) < 1e-12:
        return x if acc is None else acc + x
    if abs(c + 1.0) < 1e-12:
        return -x if acc is None else acc - x
    return c * x if acc is None else acc + c * x


def _vadd(a, b):
    return b if a is None else a if b is None else a + b


def _vsub(a, b):
    return (None if b is None else -b) if a is None else a if b is None else a - b


def _slot_groups(N1):
    half = N1 // 2
    return [(k, half - k if half - k != k else None) for k in range(half // 2 + 1)]


def _stage1_group(load, nb, cr, ci, k, kp, tw_ref, a_ref):
    N2, CT = HY_N2, HY_CT
    for r0 in range(0, N2, HY_SUB):
        if kp is None:
            ar = ai = None
            for n1 in range(nb):
                xb = load(n1, r0)
                ar = _coef_acc(ar, cr[k, n1], xb)
                ai = _coef_acc(ai, ci[k, n1], xb)
            slots = [(k, ar, ai)]
        else:
            even = [None, None]
            odd = [None, None]
            for n1 in range(nb):
                xb = load(n1, r0)
                tgt = even if n1 % 2 == 0 else odd
                tgt[0] = _coef_acc(tgt[0], cr[k, n1], xb)
                tgt[1] = _coef_acc(tgt[1], ci[k, n1], xb)
            slots = [(k, _vadd(even[0], odd[0]), _vadd(even[1], odd[1])),
                     (kp, _vsub(even[0], odd[0]), _vsub(odd[1], even[1]))]
        for idx, (kk, ar, ai) in enumerate(slots):
            zero = jnp.zeros((HY_SUB, CT), F32)
            ar = zero if ar is None else ar
            if kk > 0:
                twr = tw_ref[kk, 0, r0:r0 + HY_SUB, :]
                twi = tw_ref[kk, 1, r0:r0 + HY_SUB, :]
                if ai is None:
                    ar, ai = ar * twr, ar * twi
                else:
                    ar, ai = ar * twr - ai * twi, ar * twi + ai * twr
            ai = zero if ai is None else ai
            a_ref[r0:r0 + HY_SUB, idx * CT:(idx + 1) * CT] = ar
            a_ref[N2 + r0:N2 + r0 + HY_SUB, idx * CT:(idx + 1) * CT] = ai


def _stage1_inv_group(b_ref, nb, icr, ici, k, kp, tw_ref, acc_ref, first):
    N2, CT = HY_N2, HY_CT

    def load(idx, kk, r0):
        br = b_ref[r0:r0 + HY_SUB, idx * CT:(idx + 1) * CT]
        bi = b_ref[N2 + r0:N2 + r0 + HY_SUB, idx * CT:(idx + 1) * CT]
        if kk > 0:
            twr = tw_ref[kk, 0, r0:r0 + HY_SUB, :]
            twi = tw_ref[kk, 1, r0:r0 + HY_SUB, :]
            br, bi = br * twr + bi * twi, bi * twr - br * twi
        return br, bi

    for r0 in range(0, N2, HY_SUB):
        br, bi = load(0, k, r0)
        if kp is None:
            q_even = q_odd = (br, bi)
        else:
            br2, bi2 = load(1, kp, r0)
            q_even = (br + br2, bi - bi2)
            q_odd = (br - br2, bi + bi2)
        for n1 in range(nb):
            qr, qi = q_even if n1 % 2 == 0 else q_odd
            contrib = _coef_acc(_coef_acc(None, icr[n1, k], qr), ici[n1, k], qi)
            rows = slice(n1 * N2 + r0, n1 * N2 + r0 + HY_SUB)
            if first:
                acc_ref[rows, :] = contrib
            elif contrib is not None:
                acc_ref[rows, :] += contrib


def _filter_spec_kernel(hf_ref, hb_ref, tw_ref, m2_ref, o_ref, a_ref, *, L):
    c = _dft_consts(L)
    N2, CT = HY_N2, HY_CT
    nb = L // N2
    cr, ci = c["stage1"](nb)
    for k, kp in _slot_groups(c["N1"]):
        w = CT if kp is None else 2 * CT
        specs = []
        for h_ref in (hf_ref, hb_ref):
            _stage1_group(lambda n1, r0: h_ref[n1 * N2 + r0:n1 * N2 + r0 + HY_SUB, :], nb, cr, ci, k, kp, tw_ref, a_ref)
            specs.append(_dft_mm(m2_ref, a_ref[:, :w]))
        xf, xb = specs
        g = jnp.concatenate([xf[:N2] + xb[:N2], xf[N2:] - xb[N2:]], axis=0)
        o_ref[k] = g[:, :CT]
        if kp is not None:
            o_ref[kp] = g[:, CT:]


def _filter_spectrum(hfb, L):
    c = _dft_consts(L)
    C = hfb.shape[2]
    S, R = c["S"], 2 * HY_N2
    return pl.pallas_call(
        functools.partial(_filter_spec_kernel, L=L),
        grid=(C // HY_CT,),
        in_specs=[
            pl.BlockSpec((None, L, HY_CT), lambda j: (0, 0, j)),
            pl.BlockSpec((None, L, HY_CT), lambda j: (1, 0, j)),
            pl.BlockSpec((S, 2, HY_N2, LANES), lambda j: (0, 0, 0, 0)),
            pl.BlockSpec((R, R), lambda j: (0, 0)),
        ],
        out_specs=pl.BlockSpec((S, R, HY_CT), lambda j: (0, 0, j)),
        out_shape=jax.ShapeDtypeStruct((S, R, C), F32),
        scratch_shapes=[pltpu.VMEM((R, 2 * HY_CT), F32)],
        compiler_params=_cparams(("parallel",)),
        name="hy_filter_spectrum",
    )(hfb, hfb, jnp.asarray(c["tw"]), jnp.asarray(c["m2"]))


def _long_conv_kernel(u_ref, gate_ref, spec_ref, skip_ref, tw_ref, m2_ref, m2t_ref, o_ref, a_ref, b_ref, acc_ref,
                      *, L):
    c = _dft_consts(L)
    N2, CT = HY_N2, HY_CT
    nb = L // N2
    cr, ci = c["stage1"](nb)
    icr, ici = c["icr"] * c["N"], c["ici"] * c["N"]
    for gi_, (k, kp) in enumerate(_slot_groups(c["N1"])):
        w = CT if kp is None else 2 * CT
        _stage1_group(lambda n1, r0: u_ref[0, n1 * N2 + r0:n1 * N2 + r0 + HY_SUB, :], nb, cr, ci, k, kp, tw_ref, a_ref)
        x = _dft_mm(m2_ref, a_ref[:, :w])
        g = spec_ref[k] if kp is None else jnp.concatenate([spec_ref[k], spec_ref[kp]], axis=1)
        xr, xi, gr, gi = x[:N2], x[N2:], g[:N2], g[N2:]
        y = jnp.concatenate([xr * gr - xi * gi, xr * gi + xi * gr], axis=0)
        b_ref[:, :w] = _dft_mm(m2t_ref, y)
        _stage1_inv_group(b_ref, nb, icr, ici, k, kp, tw_ref, acc_ref, gi_ == 0)
    o_ref[0] = (gate_ref[0] * (acc_ref[...] * (1.0 / c["N"]) + skip_ref[...] * u_ref[0])).astype(o_ref.dtype)


def _long_conv(u_arr, u_col, gate_arr, gate_col, spec, spec_col, skip, L, out_dtype):
    Bx, T, _ = u_arr.shape
    nseq = T // L
    c = _dft_consts(L)
    C = skip.shape[0]
    S, R = c["S"], 2 * HY_N2
    ub, gb, sb = u_col // HY_CT, gate_col // HY_CT, spec_col // HY_CT
    return pl.pallas_call(
        functools.partial(_long_conv_kernel, L=L),
        grid=(C // HY_CT, Bx, nseq),
        in_specs=[
            pl.BlockSpec((1, L, HY_CT), lambda j, b, s: (b, s, ub + j)),
            pl.BlockSpec((1, L, HY_CT), lambda j, b, s: (b, s, gb + j)),
            pl.BlockSpec((S, R, HY_CT), lambda j, b, s: (0, 0, sb + j)),
            pl.BlockSpec((1, HY_CT), lambda j, b, s: (0, j)),
            pl.BlockSpec((S, 2, HY_N2, LANES), lambda j, b, s: (0, 0, 0, 0)),
            pl.BlockSpec((R, R), lambda j, b, s: (0, 0)),
            pl.BlockSpec((R, R), lambda j, b, s: (0, 0)),
        ],
        out_specs=pl.BlockSpec((1, L, HY_CT), lambda j, b, s: (b, s, j)),
        out_shape=jax.ShapeDtypeStruct((Bx, T, C), out_dtype),
        scratch_shapes=[pltpu.VMEM((R, 2 * HY_CT), F32), pltpu.VMEM((R, 2 * HY_CT), F32),
                        pltpu.VMEM((L, HY_CT), F32)],
        compiler_params=_cparams(("parallel", "parallel", "parallel")),
        name="hy_long_conv",
    )(u_arr, gate_arr, spec, skip.reshape(1, C), jnp.asarray(c["tw"]), jnp.asarray(c["m2"]), jnp.asarray(c["m2t"]))


def _dot3(a, b):
    a_hi = a.astype(BF16)
    a_lo = (a - a_hi.astype(F32)).astype(BF16)
    b_hi = b.astype(BF16)
    b_lo = (b - b_hi.astype(F32)).astype(BF16)
    return jnp.dot(a_hi, b_hi, preferred_element_type=F32) + \
        (jnp.dot(a_hi, b_lo, preferred_element_type=F32) + jnp.dot(a_lo, b_hi, preferred_element_type=F32))


def _filter_gen_kernel(z_ref, w1_ref, b1_ref, w2_ref, b2_ref, freq_ref, w3_ref, delta_ref, o_ref, h_ref):
    L = z_ref.shape[0]

    @pl.when(pl.program_id(0) == 0)
    def _():
        h = jnp.sin(freq_ref[0:1, :] * (_dot3(z_ref[...], w1_ref[...]) + b1_ref[...]))
        h_ref[...] = jnp.sin(freq_ref[1:2, :] * (_dot3(h, w2_ref[...]) + b2_ref[...]))

    t_norm = lax.broadcasted_iota(jnp.int32, (L, LANES), 0).astype(F32) / max(L - 1, 1)
    h = _dot3(h_ref[...], w3_ref[...]) * jnp.exp(-t_norm * delta_ref[...])
    o_ref[...] = h / jnp.sum(jnp.abs(h), axis=0, keepdims=True)


def _hyena_filters(L, w1, b1, w2, b2, w3, freq):
    t = jnp.arange(L, dtype=F32)
    t_norm = t / max(L - 1, 1)
    w = 2.0 * math.pi * t / L
    bands = jnp.linspace(1e-4, HY_EMB_BANDS - 1, HY_EMB_BANDS, dtype=F32)
    z = jnp.concatenate([t_norm[:, None], jnp.cos(w[:, None] * bands), -jnp.sin(w[:, None] * bands)], axis=-1)
    deltas = jnp.abs(jnp.linspace(math.log(HY_DECAY_TARGET) / HY_DECAY_SLOW,
                                  math.log(HY_DECAY_TARGET) / HY_DECAY_FAST, HY_WIDTH, dtype=F32))
    E, FW = z.shape[1], w1.shape[1]
    nct = HY_WIDTH // LANES
    return pl.pallas_call(
        _filter_gen_kernel,
        grid=(HY_ORDER * 2 * nct,),
        in_specs=[
            pl.BlockSpec((L, E), lambda j: (0, 0)),
            pl.BlockSpec((E, FW), lambda j: (0, 0)),
            pl.BlockSpec((1, FW), lambda j: (0, 0)),
            pl.BlockSpec((FW, FW), lambda j: (0, 0)),
            pl.BlockSpec((1, FW), lambda j: (0, 0)),
            pl.BlockSpec((2, FW), lambda j: (0, 0)),
            pl.BlockSpec((FW, LANES), lambda j: (0, j)),
            pl.BlockSpec((1, LANES), lambda j: (0, j % nct)),
        ],
        out_specs=pl.BlockSpec((None, L, LANES), lambda j: ((j // nct) % 2, 0, (j // (2 * nct)) * nct + j % nct)),
        out_shape=jax.ShapeDtypeStruct((2, L, HY_ORDER * HY_WIDTH), F32),
        scratch_shapes=[pltpu.VMEM((L, FW), F32)],
        compiler_params=_cparams(("arbitrary",)),
        name="hy_filter_gen",
    )(z, w1, b1.reshape(1, FW), w2, b2.reshape(1, FW), freq, w3, deltas.reshape(1, HY_WIDTH))


def _hyena_spectra(L, hy_fp):
    return _filter_spectrum(_hyena_filters(L, *hy_fp), L)


def _hyena(p_all, sw, sb, spec, skip, L):
    u3 = _short_conv(p_all, sw, sb, L)
    W = HY_WIDTH
    z1 = _long_conv(u3, 2 * W, u3, 0, spec, 0, skip[0], L, F32)
    return _long_conv(z1, 0, u3, W, spec, W, skip[1], L, BF16)


def _merge_kernel(x_ref, gs_ref, sh_ref, att_ref, hf_ref, hb_ref, op_ref, hy_ref, mg_ref, wb_ref, wg_ref, wo_ref,
                  g_ref, o_ref):
    d = ML_HEAD_DIM
    D = D_MODEL
    x = x_ref[0]
    h1 = _rms_mod(x, gs_ref[0], sh_ref[0])
    hsum = hf_ref[0] + hb_ref[0]
    parts = []
    for hh in range(ML_HEADS):
        hs = hsum[:, hh * d:(hh + 1) * d]
        parts.append(hs * lax.rsqrt(jnp.mean(hs * hs, axis=-1, keepdims=True) + EPS))
    mls = jnp.concatenate(parts, axis=1) * mg_ref[...] * _sigmoid(op_ref[0].astype(F32))
    y = None
    for r, br in enumerate((att_ref[0], mls.astype(BF16), hy_ref[0])):
        gate = _sigmoid(jnp.dot(h1, wg_ref[:, r * D:(r + 1) * D], preferred_element_type=F32))
        term = gate * jnp.dot(br, wb_ref[r], preferred_element_type=F32)
        y = term if y is None else y + term
    o_ref[0] = x + g_ref[0] * jnp.dot(y.astype(BF16), wo_ref[...], preferred_element_type=F32)


def _merge(x, gs, sh, att, hf, hb, p_all, hy, ml_g, wb, wg, wo, g, tm, layer):
    Bx, T, D = x.shape
    W = ML_W
    row = lambda b, i: (b, i, 0)
    vec = lambda b, i: (b, 0, 0)
    return pl.pallas_call(
        _merge_kernel,
        grid=(Bx, T // tm),
        in_specs=[
            pl.BlockSpec((1, tm, D), row),
            pl.BlockSpec((1, 1, D), vec),
            pl.BlockSpec((1, 1, D), vec),
            pl.BlockSpec((1, tm, W), row),
            pl.BlockSpec((1, tm, W), row),
            pl.BlockSpec((1, tm, W), row),
            pl.BlockSpec((1, tm, W), lambda b, i: (b, i, COL_ML // W + 3)),
            pl.BlockSpec((1, tm, W), row),
            pl.BlockSpec((1, W), lambda b, i: (0, 0)),
            pl.BlockSpec((N_BRANCH, W, D), lambda b, i: (0, 0, 0)),
            pl.BlockSpec((D, N_BRANCH * D), lambda b, i: (0, 0)),
            pl.BlockSpec((None, D, D), lambda b, i: (layer, 0, 0)),
            pl.BlockSpec((1, 1, D), vec),
        ],
        out_specs=pl.BlockSpec((1, tm, D), row),
        out_shape=jax.ShapeDtypeStruct((Bx, T, D), F32),
        compiler_params=_cparams(("parallel", "parallel")),
        name="merge",
    )(x, gs, sh, att, hf, hb, p_all, hy, ml_g.reshape(1, W), wb, wg, wo, g)


FFN_TC = 1408
FFN_HALO = 8


def _ffn_kernel(x_ref, xp_ref, xn_ref, gs_ref, sh_ref, wu_ref, cw_ref, cb_ref, wd_ref, g_ref, o_ref, *, tiles_per_seq):
    i = pl.program_id(1)
    tm = x_ref.shape[1]
    x = x_ref[0]
    h = _rms_mod(x, gs_ref[0], sh_ref[0])
    hh = _rms_mod(jnp.concatenate([xp_ref[0], xn_ref[0]], axis=0), gs_ref[0], sh_ref[0])
    first = (i % tiles_per_seq) == 0
    last = (i % tiles_per_seq) == tiles_per_seq - 1
    row8 = lax.broadcasted_iota(jnp.int32, (8, FFN_TC), 0)

    def conv_cols(c0):
        cs = slice(c0, c0 + FFN_TC)
        u = jnp.dot(h, wu_ref[:, cs], preferred_element_type=F32)
        uh = jnp.dot(hh, wu_ref[:, cs], preferred_element_type=F32)
        pr = jnp.where(first, 0.0, uh[FFN_HALO - 1:FFN_HALO])
        nx = jnp.where(last, 0.0, uh[FFN_HALO:FFN_HALO + 1])
        prev = pltpu.roll(u, 1, axis=0)
        nxt = pltpu.roll(u, tm - 1, axis=0)
        prev = jnp.concatenate([jnp.where(row8 == 0, pr, prev[:8]), prev[8:]], axis=0)
        nxt = jnp.concatenate([nxt[:tm - 8], jnp.where(row8 == 7, nx, nxt[tm - 8:])], axis=0)
        cw = cw_ref[:, cs]
        return cb_ref[:, cs] + prev * cw[0:1] + u * cw[1:2] + nxt * cw[2:3]

    acc = None
    for c0 in range(0, D_FF, FFN_TC):
        gate = conv_cols(c0)
        val = conv_cols(D_FF + c0)
        act = (gate * _sigmoid(gate) * val).astype(BF16)
        part = jnp.dot(act, wd_ref[c0:c0 + FFN_TC, :], preferred_element_type=F32)
        acc = part if acc is None else acc + part
    o_ref[0] = x + g_ref[0] * acc


def _conv_ffn(x, gs, sh, wu, cw, cb, wd, g, seq_len, layer):
    Bx, T, D = x.shape
    tm = min(seq_len, 512)
    hb = tm // FFN_HALO
    nh = T // FFN_HALO
    vec = lambda b, i: (b, 0, 0)
    const = lambda b, i: (0, 0)
    lconst = lambda b, i: (layer, 0, 0)
    once = pl.Buffered(1)
    return pl.pallas_call(
        functools.partial(_ffn_kernel, tiles_per_seq=seq_len // tm),
        grid=(Bx, T // tm),
        in_specs=[
            pl.BlockSpec((1, tm, D), lambda b, i: (b, i, 0)),
            pl.BlockSpec((1, FFN_HALO, D), lambda b, i: (b, jnp.maximum(i * hb - 1, 0), 0)),
            pl.BlockSpec((1, FFN_HALO, D), lambda b, i: (b, jnp.minimum((i + 1) * hb, nh - 1), 0)),
            pl.BlockSpec((1, 1, D), vec),
            pl.BlockSpec((1, 1, D), vec),
            pl.BlockSpec((None, D, 2 * D_FF), lconst, pipeline_mode=once),
            pl.BlockSpec((3, 2 * D_FF), const, pipeline_mode=once),
            pl.BlockSpec((1, 2 * D_FF), const, pipeline_mode=once),
            pl.BlockSpec((None, D_FF, D), lconst, pipeline_mode=once),
            pl.BlockSpec((1, 1, D), vec),
        ],
        out_specs=pl.BlockSpec((1, tm, D), lambda b, i: (b, i, 0)),
        out_shape=jax.ShapeDtypeStruct((Bx, T, D), F32),
        compiler_params=_cparams(("parallel", "parallel")),
        name="conv_ffn",
    )(x, x, x, gs, sh, wu, cw, cb.reshape(1, -1), wd, g)


def _pair_perm():
    hd, half = ATT_HEAD_DIM, ATT_HEAD_DIM // 2
    n_pair = ATT_Q // LANES
    qperm = []
    for p in range(n_pair):
        for sub in range(4):
            head = p if sub % 2 == 0 else n_pair + p
            qperm += [head * hd + (sub // 2) * half + dd for dd in range(half)]
    kperm = []
    for sub in range(4):
        kperm += [(sub % 2) * hd + (sub // 2) * half + dd for dd in range(half)]
    return np.asarray(qperm), np.asarray(kperm)


def _rope_tables(L):
    rows = L // GRID_W
    row = jnp.repeat(jnp.arange(rows, dtype=F32), GRID_W)
    col = jnp.tile(jnp.arange(GRID_W, dtype=F32), rows)
    nf = ATT_HEAD_DIM // 4
    inv = ROPE_THETA ** (-jnp.arange(nf, dtype=F32) / nf)
    ang = jnp.concatenate([row[:, None] * inv, col[:, None] * inv], axis=-1)
    cos, sin = jnp.cos(ang), jnp.sin(ang)
    return jnp.concatenate([cos, cos, cos, cos, -sin, -sin, sin, sin], axis=1)


def _take_runs(t, idx, axis):
    runs, start = [], 0
    for p in range(1, len(idx) + 1):
        if p == len(idx) or idx[p] != idx[p - 1] + 1:
            runs.append(lax.slice_in_dim(t, int(idx[start]), int(idx[p - 1]) + 1, axis=axis))
            start = p
    return jnp.concatenate(runs, axis=axis)


def _prep_w_in(w):
    qperm, kperm = _pair_perm()
    o = IN_OFFSETS
    w = w.astype(BF16)
    cols = [_take_runs(w, qperm, 1), w[:, o[2]:o[6]], w[:, o[7]:o[8]],
            _take_runs(w, o[0] + kperm, 1), _take_runs(w, o[1] + kperm, 1)]
    wc = jnp.concatenate(cols, axis=1)
    w_gate = jnp.pad(w[:, o[6]:o[7]], ((0, 0), (0, LANES - ML_GATES)))
    return jnp.pad(wc, ((0, 0), (0, IN_PAD - wc.shape[1]))), w_gate, w[:, o[8]:]


def _sink_cols(sink, rows_per_pair):
    n_pair = ATT_Q // LANES
    lo = jnp.repeat(sink[:n_pair], rows_per_pair)
    hi = jnp.repeat(sink[n_pair:], rows_per_pair)
    return jnp.stack([lo, hi])[:, :, None]


def kernel(x, c, ctx, c_ctx, ada_w, ada_b, norm1_g, norm2_g, w_in, att_sink, ml_gate_b, ml_norm_g,
           hy_short_w, hy_short_b, hy_w1, hy_b1, hy_w2, hy_b2, hy_w3, hy_freq, hy_skip,
           w_branch, w_out, w_up, ffn_conv_w, ffn_conv_b, w_down, final_g):
    B, L, D = x.shape
    Lc = ctx.shape[1]
    qperm, _ = _pair_perm()
    tabs = _rope_tables(L)
    sc_rows = jnp.concatenate([jax.nn.silu(c), jax.nn.silu(c_ctx)[None], jnp.zeros((8 - B - 1, D), F32)], axis=0)
    xl = x
    xc = ctx.reshape(1, B * Lc, D)
    wo, wu, wd = w_out.astype(BF16), w_up.astype(BF16), w_down.astype(BF16)
    for l in range(DEPTH):
        need_ctx = l < DEPTH - 1
        mod = _ada_proj(sc_rows, ada_w, ada_b, l)
        sh1, sc1, g1, sh2, sc2, g2 = [m[:B, None, :] for m in jnp.split(mod, 6, axis=-1)]
        csh1, csc1, cg1, csh2, csc2, cg2 = [m[B:B + 1, None, :] for m in jnp.split(mod, 6, axis=-1)]
        n1 = norm1_g[l][None, None, :]
        n2 = norm2_g[l][None, None, :]
        w_in_b, w_gate, w_bg = _prep_w_in(w_in[l])
        wb_l = w_branch[l].astype(BF16)
        wb = jnp.concatenate([_take_runs(wb_l[:1], qperm, 1), wb_l[1:]], axis=0)
        gate_bias = jnp.pad(ml_gate_b[l].reshape(1, ML_GATES), ((0, 0), (0, LANES - ML_GATES)))
        hy_fp = (hy_w1[l], hy_b1[l], hy_w2[l], hy_b2[l], hy_w3[l], hy_freq[l])

        pl_all, gl = _norm_proj(xl, n1 * (1.0 + sc1), sh1, w_in_b, IN_TN, w_gate)
        pc_all, gc = _norm_proj(xc, n1 * (1.0 + csc1), csh1, w_in_b, IN_TN, w_gate)
        att_l = _win_attention(pl_all, pc_all, tabs, _sink_cols(att_sink[l], ATT_BLOCK))
        hf, hb = _mlstm(pl_all, pc_all, gl, gc, gate_bias)
        hy_l = _hyena(pl_all, hy_short_w[l], hy_short_b[l], _hyena_spectra(L, hy_fp), hy_skip[l], L)
        xl = _merge(xl, n1 * (1.0 + sc1), sh1, att_l, hf, hb, pl_all, hy_l, ml_norm_g[l], wb, w_bg, wo, g1, 512, l)
        xl = _conv_ffn(xl, n2 * (1.0 + sc2), sh2, wu, ffn_conv_w[l], ffn_conv_b[l], wd, g2, L, l)
        if need_ctx:
            att_c = _ctx_attention(pc_all, _sink_cols(att_sink[l], Lc), B)
            hy_c = _hyena(pc_all, hy_short_w[l], hy_short_b[l], _hyena_spectra(Lc, hy_fp), hy_skip[l], Lc)
            hfc = hf[:, L:].reshape(1, B * Lc, ML_W)
            hbc = hb[:, L:].reshape(1, B * Lc, ML_W)
            xc = _merge(xc, n1 * (1.0 + csc1), csh1, att_c, hfc, hbc, pc_all, hy_c, ml_norm_g[l], wb, w_bg, wo,
                        cg1, Lc, l)
            xc = _conv_ffn(xc, n2 * (1.0 + csc2), csh2, wu, ffn_conv_w[l], ffn_conv_b[l], wd, cg2, Lc, l)
    return _final_norm(xl, final_g)
```
